```python
import math
import jax
import jax.numpy as jnp
from jax import lax
import numpy as np

D_MODEL = 1024
BATCH = 16
SEQ = 256
DEPTH = 2
DEC_BATCH = 2
DEC_SEQ = 1024
PAST_LEN = 256

GRID_W = 64
N_MIXERS = 4
GROUP_WIDTH = D_MODEL // N_MIXERS
MIX_WIDTH = N_MIXERS * GROUP_WIDTH
HG_HEADS = 4
HG_KEY_DIM = GROUP_WIDTH // HG_HEADS
HG_VAL_DIM = GROUP_WIDTH // HG_HEADS
HG_CHUNK = 64
MLA_HEADS = 4
MLA_NOPE_DIM = 64
MLA_ROPE_DIM = 32
MLA_V_DIM = GROUP_WIDTH // MLA_HEADS
MLA_Q_RANK = 192
MLA_KV_RANK = 128
DIFF_HEADS = 4
DIFF_HEAD_DIM = GROUP_WIDTH // (2 * DIFF_HEADS)
SSD_HEADS = 4
SSD_HEAD_DIM = GROUP_WIDTH // SSD_HEADS
SSD_GROUPS = 2
SSD_STATE = 64
SSD_CONV = 3
SSD_CHUNK = 64
SSD_CONV_CH = GROUP_WIDTH + 2 * SSD_GROUPS * SSD_STATE
PEER_HEADS = 8
PEER_N_KEYS = 128
PEER_N_EXPERTS = PEER_N_KEYS * PEER_N_KEYS
PEER_TOPK = 16
PEER_QUERY_DIM = 128
PEER_HALF = PEER_QUERY_DIM // 2
PEER_BLOCK = 128
Q_BLOCK = 128
ROPE_BASE = 10000.0
ROPE_PAIRS = 8
ALPHA = (2.0 * DEPTH) ** 0.25
BETA = (8.0 * DEPTH) ** -0.25
NORM_EPS = 1e-6

IN_SIZES = (
    GROUP_WIDTH, GROUP_WIDTH, GROUP_WIDTH, GROUP_WIDTH, GROUP_WIDTH,
    MLA_Q_RANK, MLA_KV_RANK, MLA_ROPE_DIM,
    GROUP_WIDTH, GROUP_WIDTH, GROUP_WIDTH,
    GROUP_WIDTH, SSD_CONV_CH, SSD_HEADS, SSD_HEADS,
)
IN_WIDTH = sum(IN_SIZES)
IN_OFFSETS = tuple(sum(IN_SIZES[:i + 1]) for i in range(len(IN_SIZES) - 1))

kernel_name = "hybrid_diffusion_prefix_step"

F32 = jnp.float32


def _rmsnorm(x, w):
    xf = x.astype(F32)
    y = xf * lax.rsqrt(jnp.mean(xf * xf, axis=-1, keepdims=True) + NORM_EPS)
    return (y * w.astype(F32)).astype(x.dtype)


def _layernorm(x, g, b):
    xf = x.astype(F32)
    mu = jnp.mean(xf, axis=-1, keepdims=True)
    var = jnp.mean(jnp.square(xf - mu), axis=-1, keepdims=True)
    return ((xf - mu) * lax.rsqrt(var + 1e-5) * g.astype(F32) + b.astype(F32)).astype(x.dtype)


def _axial_rope_tables(rows):
    row = jnp.repeat(jnp.arange(rows, dtype=F32), GRID_W)
    col = jnp.tile(jnp.arange(GRID_W, dtype=F32), rows)
    freqs = ROPE_BASE ** (-jnp.arange(ROPE_PAIRS, dtype=F32) / ROPE_PAIRS)
    ang = jnp.stack([row[:, None] * freqs, col[:, None] * freqs], axis=1)
    return jnp.cos(ang), jnp.sin(ang)


def _rope2d(x, cos, sin):
    xr = x.astype(F32).reshape(*x.shape[:-1], 2, 2, ROPE_PAIRS)
    x1, x2 = xr[..., 0, :], xr[..., 1, :]
    out = jnp.stack([x1 * cos - x2 * sin, x2 * cos + x1 * sin], axis=-2)
    return out.reshape(x.shape).astype(x.dtype)


def _query_blocked(fn, q):
    b, s = q.shape[:2]
    qb = jnp.moveaxis(q.reshape(b, s // Q_BLOCK, Q_BLOCK, *q.shape[2:]), 1, 0)
    out = lax.map(fn, qb)
    return jnp.moveaxis(out, 0, 1).reshape(b, s, out.shape[-1])


def _to_chunks(t, c):
    b, s, h = t.shape[:3]
    t = t.reshape(b, s // c, c, h, *t.shape[3:])
    return jnp.moveaxis(t, (1, 3), (0, 2))


def _from_chunks(o):
    n, b, h, c, d = o.shape
    return jnp.moveaxis(o, (0, 2), (1, 3)).reshape(b, n * c, h, d)


def _hgrn_scan(q, k, v, log_f, s0):
    causal = jnp.tril(jnp.ones((HG_CHUNK, HG_CHUNK), bool))

    def step(state, inp):
        qc, kc, vc, lc = inp
        bc = jnp.cumsum(lc, axis=2)
        diff = bc[:, :, :, None, :] - bc[:, :, None, :, :]
        decay = jnp.exp(jnp.where(causal[:, :, None], diff, -jnp.inf))
        scores = jnp.einsum('bhid,bhjd,bhijd->bhij', qc, kc, decay)
        o = (jnp.einsum('bhij,bhjv->bhiv', scores, vc)
             + jnp.einsum('bhid,bhdv->bhiv', qc * jnp.exp(bc), state))
        b_last = bc[:, :, -1:, :]
        state = (jnp.exp(b_last[:, :, 0, :, None]) * state
                 + jnp.einsum('bhjd,bhjv->bhdv', kc * jnp.exp(b_last - bc), vc))
        return state, o

    xs = tuple(_to_chunks(t, HG_CHUNK) for t in (q, k, v, log_f))
    s_fin, o = lax.scan(step, s0, xs)
    return _from_chunks(o), s_fin


def _ssd_scan(x, bm, cm, a, s0):
    causal = jnp.tril(jnp.ones((SSD_CHUNK, SSD_CHUNK), bool))

    def step(state, inp):
        xc, bc, cc, ac = inp
        acum = jnp.cumsum(ac, axis=-1)
        seg = jnp.exp(jnp.where(causal, acum[..., :, None] - acum[..., None, :], -jnp.inf))
        scores = jnp.einsum('bhin,bhjn->bhij', cc, bc) * seg
        y = (jnp.einsum('bhij,bhjp->bhip', scores, xc)
             + jnp.einsum('bhin,bhpn->bhip', cc, state) * jnp.exp(acum)[..., None])
        state = (jnp.exp(acum[..., -1])[..., None, None] * state
                 + jnp.einsum('bhjp,bhjn->bhpn', xc * jnp.exp(acum[..., -1:] - acum)[..., None], bc))
        return state, y

    xs = tuple(_to_chunks(t, SSD_CHUNK) for t in (x, bm, cm, a))
    s_fin, y = lax.scan(step, s0, xs)
    return _from_chunks(y), s_fin


def _dwconv_centred(x, w, b):
    pad = SSD_CONV // 2
    y = lax.conv_general_dilated(x, w[:, None, :].astype(x.dtype), (1,), [(pad, pad)],
                                 dimension_numbers=("NWC", "WIO", "NWC"),
                                 feature_group_count=x.shape[-1])
    return y + b.astype(x.dtype)


def _mixer(h, p, ctx, rope):
    bsz, s, _ = h.shape
    odt = h.dtype
    latent = ctx is not None
    (hq, hff, hfb, hi, hg, mcq, mckv, mkpe, dq, dk, dv, sz, sxbc, sdtf, sdtb) = jnp.split(
        h @ p["w_in"], IN_OFFSETS, axis=-1)

    q = jax.nn.silu(hq.astype(F32)).reshape(bsz, s, HG_HEADS, HG_KEY_DIM)
    iv = hi.astype(F32).reshape(bsz, s, HG_HEADS, HG_VAL_DIM)
    s0 = (ctx["hgrn"].astype(F32) if latent
          else jnp.zeros((bsz, 2, HG_HEADS, HG_KEY_DIM, HG_VAL_DIM), F32))
    hg_out, hg_fin = [], []
    for d, (fpre, rev) in enumerate(((hff, False), (hfb, True))):
        lb = p["hg_lb"][d]
        f = (lb + (1.0 - lb) * jax.nn.sigmoid(fpre.astype(F32))).reshape(bsz, s, HG_HEADS, HG_KEY_DIM)
        args = (q, 1.0 - f, iv, jnp.log(f))
        if rev:
            args = tuple(jnp.flip(t, 1) for t in args)
        o, sf = _hgrn_scan(*args, s0[:, d])
        hg_out.append(jnp.flip(o, 1) if rev else o)
        hg_fin.append(sf)
    gate = jax.nn.silu(hg.astype(F32)).reshape(bsz, s, HG_HEADS, HG_VAL_DIM)
    o_hg = (_rmsnorm(hg_out[0] + hg_out[1], p["hg_norm"]) * gate).reshape(bsz, s, GROUP_WIDTH).astype(odt)

    cq = _rmsnorm(mcq, p["mla_q_norm"])
    qf = (cq @ p["mla_w_qb"]).reshape(bsz, s, MLA_HEADS, MLA_NOPE_DIM + MLA_ROPE_DIM)
    ckv = _rmsnorm(mckv, p["mla_kv_norm"])
    kpe = mkpe
    if latent:
        cos, sin = rope
        qf = jnp.concatenate([qf[..., :MLA_NOPE_DIM],
                              _rope2d(qf[..., MLA_NOPE_DIM:], cos[:, None], sin[:, None])], axis=-1)
        ckv_all = jnp.concatenate([ctx["mla_ckv"], ckv], axis=1)
        kpe_all = jnp.concatenate([ctx["mla_kpe"], _rope2d(kpe, cos, sin)], axis=1)
    else:
        ckv_all, kpe_all = ckv, kpe
    kv = (ckv_all @ p["mla_w_kvb"]).reshape(bsz, -1, MLA_HEADS, MLA_NOPE_DIM + MLA_V_DIM)
    k_nope, v_m = kv[..., :MLA_NOPE_DIM], kv[..., MLA_NOPE_DIM:]
    mla_scale = (MLA_NOPE_DIM + MLA_ROPE_DIM) ** -0.5

    def mla_block(qb):
        sc = (jnp.einsum('bqhd,bkhd->bhqk', qb[..., :MLA_NOPE_DIM], k_nope)
              + jnp.einsum('bqhr,bkr->bhqk', qb[..., MLA_NOPE_DIM:], kpe_all))
        pr = jax.nn.softmax(sc.astype(F32) * mla_scale, axis=-1).astype(v_m.dtype)
        return jnp.einsum('bhqk,bkhv->bqhv', pr, v_m).reshape(bsz, -1, MLA_HEADS * MLA_V_DIM)

    o_mla = _query_blocked(mla_block, qf).astype(odt)

    dq = dq.reshape(bsz, s, DIFF_HEADS, 2, DIFF_HEAD_DIM)
    dk = dk.reshape(bsz, s, DIFF_HEADS, 2, DIFF_HEAD_DIM)
    dv = dv.reshape(bsz, s, DIFF_HEADS, 2 * DIFF_HEAD_DIM)
    if latent:
        cos, sin = rope
        dq = _rope2d(dq, cos[:, None, None], sin[:, None, None])
        dk_all = jnp.concatenate([ctx["diff_k"], _rope2d(dk, cos[:, None, None], sin[:, None, None])], axis=1)
        dv_all = jnp.concatenate([ctx["diff_v"], dv], axis=1)
    else:
        dk_all, dv_all = dk, dv
    lam_init = p["diff_lambda_init"]
    lv = p["diff_lambda"].astype(F32)
    lam = jnp.exp(jnp.sum(lv[0] * lv[1])) - jnp.exp(jnp.sum(lv[2] * lv[3])) + lam_init
    diff_scale = DIFF_HEAD_DIM ** -0.5

    def diff_block(qb):
        sc = jnp.einsum('bqhcd,bkhcd->bhcqk', qb, dk_all).astype(F32) * diff_scale
        pr = jax.nn.softmax(sc, axis=-1)
        w = (pr[:, :, 0] - lam * pr[:, :, 1]).astype(dv_all.dtype)
        o = jnp.einsum('bhqk,bkhv->bqhv', w, dv_all)
        o = _rmsnorm(o, p["diff_norm"]) * (1.0 - lam_init)
        return o.reshape(bsz, -1, DIFF_HEADS * 2 * DIFF_HEAD_DIM)

    o_diff = _query_blocked(diff_block, dq).astype(odt)

    xbc = jax.nn.silu(_dwconv_centred(sxbc, p["ssd_conv_w"], p["ssd_conv_b"]).astype(F32))
    xs, bm, cm = jnp.split(xbc, [GROUP_WIDTH, GROUP_WIDTH + SSD_GROUPS * SSD_STATE], axis=-1)
    xs = xs.reshape(bsz, s, SSD_HEADS, SSD_HEAD_DIM)
    rep = SSD_HEADS // SSD_GROUPS
    bm = jnp.repeat(bm.reshape(bsz, s, SSD_GROUPS, SSD_STATE), rep, axis=2)
    cm = jnp.repeat(cm.reshape(bsz, s, SSD_GROUPS, SSD_STATE), rep, axis=2)
    s0 = (ctx["ssd"].astype(F32) if latent
          else jnp.zeros((bsz, 2, SSD_HEADS, SSD_HEAD_DIM, SSD_STATE), F32))
    ssd_out, ssd_fin = [], []
    for d, (dtr, rev) in enumerate(((sdtf, False), (sdtb, True))):
        dt = jax.nn.softplus(dtr.astype(F32) + p["ssd_dt_bias"][d].astype(F32))
        a = dt * -jnp.exp(p["ssd_a_log"][d].astype(F32))
        args = (xs * dt[..., None], bm, cm, a)
        if rev:
            args = tuple(jnp.flip(t, 1) for t in args)
        y, sf = _ssd_scan(*args, s0[:, d])
        ssd_out.append(jnp.flip(y, 1) if rev else y)
        ssd_fin.append(sf)
    y = ssd_out[0] + ssd_out[1] + p["ssd_d"].astype(F32)[:, None] * xs
    o_ssd = _rmsnorm(y.reshape(bsz, s, GROUP_WIDTH) * jax.nn.silu(sz.astype(F32)), p["ssd_norm"]).astype(odt)

    mixed = jnp.concatenate([o_hg, o_mla, o_diff, o_ssd], axis=-1) @ p["w_out"]
    if latent:
        return mixed, None
    produced = (ckv, kpe, dk, dv,
                jnp.stack(hg_fin, axis=1).astype(odt), jnp.stack(ssd_fin, axis=1).astype(odt))
    return mixed, produced


def _peer(h, wq, keys, u_tab, v_tab):
    bsz, s, d = h.shape
    xt = h.reshape(-1, PEER_BLOCK, d)

    def block(xb):
        q = (xb @ wq).reshape(PEER_BLOCK, PEER_HEADS, 2, PEER_HALF)
        sc = jnp.einsum('thcd,hcnd->thcn', q, keys).astype(F32)
        s_top, i_top = lax.top_k(sc, PEER_TOPK)
        cand_s = (s_top[:, :, 0, :, None] + s_top[:, :, 1, None, :]).reshape(PEER_BLOCK, PEER_HEADS, -1)
        cand_i = (i_top[:, :, 0, :, None] * PEER_N_KEYS + i_top[:, :, 1, None, :]).reshape(PEER_BLOCK, PEER_HEADS, -1)
        best_s, pos = lax.top_k(cand_s, PEER_TOPK)
        idx = jnp.take_along_axis(cand_i, pos, axis=-1)
        gate = jax.nn.softmax(best_s, axis=-1)
        act = jax.nn.gelu(jnp.einsum('td,thkd->thk', xb, jnp.take(u_tab, idx, axis=0)).astype(F32))
        return jnp.einsum('thk,thkd->td', (gate * act).astype(xb.dtype), jnp.take(v_tab, idx, axis=0))

    return lax.map(block, xt).reshape(bsz, s, d)


def _layer(x, cond, p, ctx, rope):
    mods = jax.nn.silu(cond.astype(F32)) @ p["w_mod"].astype(F32) + p["b_mod"].astype(F32)
    sh1, sc1, g1, sh2, sc2, g2 = [m[:, None, :].astype(x.dtype) for m in jnp.split(mods, 6, axis=-1)]
    mixed, produced = _mixer(x * (1 + sc1) + sh1, p, ctx, rope)
    x = _layernorm(ALPHA * x + g1 * mixed, p["ln1_g"], p["ln1_b"])
    ffn = _peer(x * (1 + sc2) + sh2, p["peer_wq"], p["peer_keys"], p["peer_u"], p["peer_v"])
    x = _layernorm(ALPHA * x + g2 * ffn, p["ln2_g"], p["ln2_b"])
    return x, produced


def setup_inputs(seed: int = 0) -> dict:
    key = jax.random.key(seed)
    ks = iter(jax.random.split(key, 48))

    def nrm(shape, scale=1.0):
        return jax.random.normal(next(ks), shape, F32) * scale

    D = D_MODEL
    dt0 = jnp.exp(jax.random.uniform(next(ks), (DEPTH, 2, SSD_HEADS), F32, math.log(1e-3), math.log(1e-1)))
    a0 = jax.random.uniform(next(ks), (DEPTH, 2, SSD_HEADS), F32, 1.0, 16.0)
    return {
        "x_prompt": nrm((BATCH, SEQ, D)),
        "x_sample": nrm((DEC_BATCH, DEC_SEQ, D)),
        "cache_mla_ckv": nrm((DEC_BATCH, DEPTH, PAST_LEN, MLA_KV_RANK)),
        "cache_mla_kpe": nrm((DEC_BATCH, DEPTH, PAST_LEN, MLA_ROPE_DIM)),
        "cache_diff_k": nrm((DEC_BATCH, DEPTH, PAST_LEN, DIFF_HEADS, 2, DIFF_HEAD_DIM)),
        "cache_diff_v": nrm((DEC_BATCH, DEPTH, PAST_LEN, DIFF_HEADS, 2 * DIFF_HEAD_DIM)),
        "state_hgrn": nrm((DEC_BATCH, DEPTH, 2, HG_HEADS, HG_KEY_DIM, HG_VAL_DIM), 0.5),
        "state_ssd": nrm((DEC_BATCH, DEPTH, 2, SSD_HEADS, SSD_HEAD_DIM, SSD_STATE), 0.5),
        "c": nrm((DEC_BATCH, D)),
        "c_ctx": nrm((D,)),
        "w_mod": nrm((DEPTH, D, 6 * D), 0.5 * D ** -0.5),
        "b_mod": nrm((DEPTH, 6 * D), 0.02),
        "w_in": nrm((DEPTH, D, IN_WIDTH), D ** -0.5),
        "hgrn_lb": nrm((DEPTH, 2, HG_HEADS * HG_KEY_DIM)),
        "hgrn_norm": 1.0 + nrm((DEPTH, HG_VAL_DIM), 0.02),
        "mla_q_norm": 1.0 + nrm((DEPTH, MLA_Q_RANK), 0.02),
        "mla_w_qb": nrm((DEPTH, MLA_Q_RANK, MLA_HEADS * (MLA_NOPE_DIM + MLA_ROPE_DIM)), MLA_Q_RANK ** -0.5),
        "mla_kv_norm": 1.0 + nrm((DEPTH, MLA_KV_RANK), 0.02),
        "mla_w_kvb": nrm((DEPTH, MLA_KV_RANK, MLA_HEADS * (MLA_NOPE_DIM + MLA_V_DIM)), MLA_KV_RANK ** -0.5),
        "diff_lambda": nrm((DEPTH, 4, DIFF_HEAD_DIM), 0.1),
        "diff_norm": 1.0 + nrm((DEPTH, 2 * DIFF_HEAD_DIM), 0.02),
        "ssd_conv_w": nrm((DEPTH, SSD_CONV, SSD_CONV_CH), SSD_CONV ** -0.5),
        "ssd_conv_b": nrm((DEPTH, SSD_CONV_CH), 0.02),
        "ssd_dt_bias": dt0 + jnp.log(-jnp.expm1(-dt0)),
        "ssd_a_log": jnp.log(a0),
        "ssd_d": 1.0 + nrm((DEPTH, SSD_HEADS), 0.1),
        "ssd_norm": 1.0 + nrm((DEPTH, GROUP_WIDTH), 0.02),
        "w_out": nrm((DEPTH, MIX_WIDTH, D), BETA * MIX_WIDTH ** -0.5),
        "ln1_g": 1.0 + nrm((DEPTH, D), 0.02),
        "ln1_b": nrm((DEPTH, D), 0.02),
        "peer_wq": nrm((DEPTH, D, PEER_HEADS * PEER_QUERY_DIM), D ** -0.5),
        "peer_keys": nrm((DEPTH, PEER_HEADS, 2, PEER_N_KEYS, PEER_HALF), PEER_HALF ** -0.5),
        "peer_u": nrm((DEPTH, PEER_N_EXPERTS, D), D ** -0.5),
        "peer_v": nrm((DEPTH, PEER_N_EXPERTS, D), BETA),
        "ln2_g": 1.0 + nrm((DEPTH, D), 0.02),
        "ln2_b": nrm((DEPTH, D), 0.02),
    }


def reference(x_prompt, x_sample, cache_mla_ckv, cache_mla_kpe, cache_diff_k, cache_diff_v,
              state_hgrn, state_ssd, c, c_ctx, w_mod, b_mod, w_in, hgrn_lb, hgrn_norm,
              mla_q_norm, mla_w_qb, mla_kv_norm, mla_w_kvb, diff_lambda, diff_norm,
              ssd_conv_w, ssd_conv_b, ssd_dt_bias, ssd_a_log, ssd_d, ssd_norm, w_out,
              ln1_g, ln1_b, peer_wq, peer_keys, peer_u, peer_v, ln2_g, ln2_b):
    lb_p = jax.nn.softmax(hgrn_lb.astype(F32), axis=0)
    lower_bounds = jnp.cumsum(lb_p, axis=0) - lb_p[0]

    def layer_params(l):
        return {
            "w_mod": w_mod[l], "b_mod": b_mod[l], "w_in": w_in[l],
            "hg_lb": lower_bounds[l], "hg_norm": hgrn_norm[l],
            "mla_q_norm": mla_q_norm[l], "mla_w_qb": mla_w_qb[l],
            "mla_kv_norm": mla_kv_norm[l], "mla_w_kvb": mla_w_kvb[l],
            "diff_lambda": diff_lambda[l], "diff_lambda_init": 0.8 - 0.6 * math.exp(-0.3 * l),
            "diff_norm": diff_norm[l],
            "ssd_conv_w": ssd_conv_w[l], "ssd_conv_b": ssd_conv_b[l], "ssd_dt_bias": ssd_dt_bias[l],
            "ssd_a_log": ssd_a_log[l], "ssd_d": ssd_d[l], "ssd_norm": ssd_norm[l],
            "w_out": w_out[l], "ln1_g": ln1_g[l], "ln1_b": ln1_b[l],
            "peer_wq": peer_wq[l], "peer_keys": peer_keys[l], "peer_u": peer_u[l], "peer_v": peer_v[l],
            "ln2_g": ln2_g[l], "ln2_b": ln2_b[l],
        }

    y_prompt = x_prompt
    produced = []
    for l in range(DEPTH):
        y_prompt, prod = _layer(y_prompt, c_ctx[None, :], layer_params(l), None, None)
        produced.append(prod)

    rows = x_sample.shape[1] // GRID_W
    rope = _axial_rope_tables(rows)
    y_sample = x_sample
    for l in range(DEPTH):
        ctx = {"mla_ckv": cache_mla_ckv[:, l], "mla_kpe": cache_mla_kpe[:, l],
               "diff_k": cache_diff_k[:, l], "diff_v": cache_diff_v[:, l],
               "hgrn": state_hgrn[:, l], "ssd": state_ssd[:, l]}
        y_sample, _ = _layer(y_sample, c, layer_params(l), ctx, rope)

    new_mla_ckv = jnp.stack([pr[0] for pr in produced], axis=1)
    new_mla_kpe = jnp.stack([pr[1] for pr in produced], axis=1)
    new_diff_k = jnp.stack([pr[2] for pr in produced], axis=1)
    new_diff_v = jnp.stack([pr[3] for pr in produced], axis=1)
    new_hgrn_state = jnp.stack([pr[4] for pr in produced], axis=1)
    new_ssd_state = jnp.stack([pr[5] for pr in produced], axis=1)
    return (y_prompt, y_sample, new_mla_ckv, new_mla_kpe, new_diff_k, new_diff_v, new_hgrn_state, new_ssd_state)
```

```python
import functools
import math

import jax
import jax.numpy as jnp
from jax import lax
from jax.experimental import pallas as pl
from jax.experimental.pallas import tpu as pltpu

F32 = jnp.float32
BF16 = jnp.bfloat16
I32 = jnp.int32
HIGHEST = lax.Precision.HIGHEST

D_MODEL = 1024
GROUP = 256
N_HEADS = 4
HEAD_DIM = 64
HG_BLOCK = 16
SSD_CHUNK = 128
SSD_STATE = 64
MLA_Q_RANK = 192
MLA_KV_RANK = 128
MLA_NOPE = 64
MLA_ROPE = 32
DIFF_DIM = 32
GRID_W = 64
ROPE_PAIRS = 8
ROPE_BASE = 10000.0
PEER_HEADS = 8
PEER_KEYS = 128
PEER_TOPK = 16
PEER_HALF = 64
N_EXPERTS = PEER_KEYS * PEER_KEYS
NORM_EPS = 1e-6
LN_EPS = 1e-5
DEPTH = 2
ALPHA = (2.0 * DEPTH) ** 0.25

PROJ_A = 5 * GROUP
PROJ_B = 512
PROJ_C = 3 * GROUP
PROJ_D = 896
VMEM_LIMIT = 56 * 1024 * 1024


def _cparams(*sem):
    return pltpu.CompilerParams(dimension_semantics=sem, vmem_limit_bytes=VMEM_LIMIT)


def _sigmoid(x):
    return 1.0 / (1.0 + jnp.exp(-x))


def _silu(x):
    return x * _sigmoid(x)


def _softplus(x):
    return jnp.maximum(x, 0.0) + jnp.log(1.0 + jnp.exp(-jnp.abs(x)))


def _gelu_tanh(x):
    return 0.5 * x * (1.0 + jnp.tanh(math.sqrt(2.0 / math.pi) * (x + 0.044715 * (x * x * x))))


def _dot(a, b, precision=None):
    return jnp.dot(a, b, preferred_element_type=F32, precision=precision)


def _dot_nt(a, b, precision=None):
    return lax.dot_general(a, b, (((1,), (1,)), ((), ())), preferred_element_type=F32,
                           precision=precision)


def _dot_tn(a, b, precision=None):
    return lax.dot_general(a, b, (((0,), (0,)), ((), ())), preferred_element_type=F32,
                           precision=precision)


def _iota(shape, dim):
    return lax.broadcasted_iota(I32, shape, dim)


def _block_mask(rows, cols, rblk, cblk):
    return (_iota((rows, cols), 0) // rblk) == (_iota((rows, cols), 1) // cblk)


def _lane_group_mask(width, start, size):
    lane = _iota((1, width), 1)
    return (lane >= start) & (lane < start + size)


def _layernorm(v, g, b):
    mu = jnp.mean(v, axis=-1, keepdims=True)
    d = v - mu
    var = jnp.mean(d * d, axis=-1, keepdims=True)
    return d * lax.rsqrt(var + LN_EPS) * g + b


def _swap_halves16(x):
    width = x.shape[-1]
    lane = _iota(x.shape, x.ndim - 1)
    up = pltpu.roll(x, width - 8, x.ndim - 1)
    down = pltpu.roll(x, 8, x.ndim - 1)
    return jnp.where((lane % 16) < 8, up, down)


def _rope(x, cos, sin_signed):
    return x * cos + _swap_halves16(x) * sin_signed


def _mods_kernel(c_ref, w_ref, b_ref, o_ref):
    s = _silu(c_ref[...]).astype(BF16)
    o_ref[...] = _dot(s, w_ref[...].astype(BF16)) + b_ref[...]


def _mods(cond8, w_mod, b_mod):
    depth, d, n = w_mod.shape
    tn = 1536
    return pl.pallas_call(
        _mods_kernel,
        grid=(depth, n // tn),
        in_specs=[pl.BlockSpec((8, d), lambda l, j: (0, 0)),
                  pl.BlockSpec((None, d, tn), lambda l, j: (l, 0, j)),
                  pl.BlockSpec((None, 1, tn), lambda l, j: (l, 0, j))],
        out_specs=pl.BlockSpec((None, 8, tn), lambda l, j: (l, 0, j)),
        out_shape=jax.ShapeDtypeStruct((depth, 8, n), F32),
        compiler_params=_cparams("arbitrary", "arbitrary"),
        name="mods",
    )(cond8, w_mod, b_mod.reshape(depth, 1, n))


def _mod_row_map(tm, t_ctx, s_lat):
    def index_map(i, *_):
        start = i * tm
        return (jnp.where(start < t_ctx, 0, 1 + (start - t_ctx) // s_lat), 0, 0, 0)
    return index_map


def _inproj_kernel(x_ref, m_ref, w_ref, oa_ref, ob_ref, oc_ref, od_ref):
    h = (x_ref[...] * (1.0 + m_ref[1]) + m_ref[0]).astype(BF16)
    start = 0
    for o_ref in (oa_ref, ob_ref, oc_ref, od_ref):
        width = o_ref.shape[-1]
        o_ref[...] = _dot(h, w_ref[:, start:start + width])
        start += width


def _inproj(x, mods_l, w_in_p, t_ctx, s_lat):
    t, d = x.shape
    tm = 256
    widths = (PROJ_A, PROJ_B, PROJ_C, PROJ_D)
    return pl.pallas_call(
        _inproj_kernel,
        grid=(t // tm,),
        in_specs=[pl.BlockSpec((tm, d), lambda i: (i, 0)),
                  pl.BlockSpec((None, 6, 1, d), _mod_row_map(tm, t_ctx, s_lat)),
                  pl.BlockSpec(w_in_p.shape, lambda i: (0, 0))],
        out_specs=[pl.BlockSpec((tm, w), lambda i: (i, 0)) for w in widths],
        out_shape=[jax.ShapeDtypeStruct((t, w), F32) for w in widths],
        compiler_params=_cparams("arbitrary"),
        name="inproj",
    )(x, mods_l, w_in_p)


def _hgrn_kernel(layer, has_state, *refs):
    if has_state:
        (a_ref, lb_ref, norm_ref, s0_ref, o_ref, sfin_ref,
         q_scr, k_scr, lf_scr, st_scr, of_scr, ob_scr) = refs
    else:
        (a_ref, lb_ref, norm_ref, o_ref, sfin_ref,
         q_scr, k_scr, lf_scr, st_scr, of_scr, ob_scr) = refs
        s0_ref = None
    seq = a_ref.shape[0]
    nblk = seq // HG_BLOCK
    c = HG_BLOCK

    lbp = lb_ref[...]
    e = jnp.exp(lbp - jnp.max(lbp, axis=0, keepdims=True))
    p = e / jnp.sum(e, axis=0, keepdims=True)
    lower = jnp.sum(p[1:layer + 1], axis=0) if layer > 0 else jnp.zeros_like(p[0])

    q_scr[...] = _silu(a_ref[:, 0:GROUP])
    for d in range(2):
        lb = lower[d:d + 1]
        f = lb + (1.0 - lb) * _sigmoid(a_ref[:, (1 + d) * GROUP:(2 + d) * GROUP])
        k_scr[d] = 1.0 - f
        lf_scr[d] = jnp.log(f)
    if has_state:
        st_scr[...] = s0_ref[...]
    else:
        st_scr[...] = jnp.zeros_like(st_scr)

    row = _iota((c, c), 0)
    col = _iota((c, c), 1)
    tri = ((col <= row).astype(F32), (col >= row).astype(F32))
    rowi = _iota((c, GROUP), 0)
    bd_ones = _block_mask(GROUP, GROUP, HEAD_DIM, HEAD_DIM).astype(BF16)
    bd_mask = _block_mask(GROUP, GROUP, HEAD_DIM, HEAD_DIM)

    def block_step(d, r0, out_scr):
        qb = q_scr[pl.ds(r0, c), :]
        kb = k_scr[d, pl.ds(r0, c), :]
        vb = a_ref[pl.ds(r0, c), 3 * GROUP:4 * GROUP]
        lfb = lf_scr[d, pl.ds(r0, c), :]
        bc = _dot(tri[d], lfb, precision=HIGHEST)
        parts = []
        for j in range(c):
            keep = (rowi >= j) if d == 0 else (rowi <= j)
            dec = jnp.exp(jnp.where(keep, bc - bc[j:j + 1], -jnp.inf))
            parts.append((dec * qb * kb[j:j + 1]).astype(BF16))
        srep = _dot(jnp.concatenate(parts, axis=0), bd_ones)
        o = srep[0:c] * vb[0:1]
        for j in range(1, c):
            o = o + srep[j * c:(j + 1) * c] * vb[j:j + 1]
        st = st_scr[d]
        o = o + _dot_nt((qb * jnp.exp(bc)).astype(BF16), st.astype(BF16))
        out_scr[pl.ds(r0, c), :] = o
        edge = bc[c - 1:c] if d == 0 else bc[0:1]
        kt = (kb * jnp.exp(edge - bc)).astype(BF16)
        upd = _dot_tn(vb.astype(BF16), kt)
        st_scr[d] = st * jnp.exp(edge) + jnp.where(bd_mask, upd, 0.0)

    def body(n, carry):
        block_step(0, pl.multiple_of(n * c, c), of_scr)
        block_step(1, pl.multiple_of((nblk - 1 - n) * c, c), ob_scr)
        return carry

    lax.fori_loop(0, nblk, body, 0)

    o = of_scr[...] + ob_scr[...]
    mean_op = jnp.where(_block_mask(GROUP, GROUP, HEAD_DIM, HEAD_DIM), 1.0 / HEAD_DIM, 0.0)
    ms = _dot(o * o, mean_op, precision=HIGHEST)
    y = o * lax.rsqrt(ms + NORM_EPS) * norm_ref[...]
    o_ref[...] = y * _silu(a_ref[:, 4 * GROUP:5 * GROUP])
    sfin_ref[...] = st_scr[...]


def _hgrn(proj_a, row0, nseq, seq, hgrn_lb, norm_t, s0, layer):
    has_state = s0 is not None
    blk0 = row0 // seq
    in_specs = [pl.BlockSpec((seq, PROJ_A), lambda b: (blk0 + b, 0)),
                pl.BlockSpec(hgrn_lb.shape, lambda b: (0, 0, 0)),
                pl.BlockSpec((1, GROUP), lambda b: (0, 0))]
    args = [proj_a, hgrn_lb, norm_t]
    if has_state:
        in_specs.append(pl.BlockSpec((None, 2, GROUP, GROUP), lambda b: (b, 0, 0, 0)))
        args.append(s0)
    return pl.pallas_call(
        functools.partial(_hgrn_kernel, layer, has_state),
        grid=(nseq,),
        in_specs=in_specs,
        out_specs=[pl.BlockSpec((None, seq, GROUP), lambda b: (b, 0, 0)),
                   pl.BlockSpec((None, 2, GROUP, GROUP), lambda b: (b, 0, 0, 0))],
        out_shape=[jax.ShapeDtypeStruct((nseq, seq, GROUP), F32),
                   jax.ShapeDtypeStruct((nseq, 2, GROUP, GROUP), F32)],
        scratch_shapes=[pltpu.VMEM((seq, GROUP), F32),
                        pltpu.VMEM((2, seq, GROUP), F32),
                        pltpu.VMEM((2, seq, GROUP), F32),
                        pltpu.VMEM((2, GROUP, GROUP), F32),
                        pltpu.VMEM((seq, GROUP), F32),
                        pltpu.VMEM((seq, GROUP), F32)],
        compiler_params=_cparams("arbitrary"),
        name="hgrn",
    )(*args)


def _mla_kernel(latent, *refs):
    if latent:
        (b_ref, qn_ref, wq_ref, kvn_ref, wkv_ref, cckv_ref, ckpe_ref, cos_ref, sin_ref,
         o_ref) = refs
    else:
        (b_ref, qn_ref, wq_ref, kvn_ref, wkv_ref, o_ref, ckv_ref, kpe_ref) = refs
    seq = b_ref.shape[0]
    mckv = b_ref[:, 0:MLA_KV_RANK]
    mcq = b_ref[:, MLA_KV_RANK:MLA_KV_RANK + MLA_Q_RANK]
    kpe_t = b_ref[:, 384:512]

    cq = mcq * lax.rsqrt(jnp.mean(mcq * mcq, axis=-1, keepdims=True) + NORM_EPS) * qn_ref[...]
    qf = _dot(cq.astype(BF16), wq_ref[...])
    ckv = mckv * lax.rsqrt(jnp.mean(mckv * mckv, axis=-1, keepdims=True) + NORM_EPS) * kvn_ref[...]
    q_nope = qf[:, 0:N_HEADS * MLA_NOPE]
    q_rope = qf[:, N_HEADS * MLA_NOPE:]
    if latent:
        cos = cos_ref[...]
        sin = sin_ref[...]
        q_rope = _rope(q_rope, cos, sin)
        ckv_all = jnp.concatenate([cckv_ref[...], ckv], axis=0)
        kpe_all = jnp.concatenate([ckpe_ref[...], _rope(kpe_t, cos, sin)], axis=0)
    else:
        ckv_ref[...] = ckv
        kpe_ref[...] = kpe_t[:, 0:MLA_ROPE]
        ckv_all = ckv
        kpe_all = kpe_t
    kv = _dot(ckv_all.astype(BF16), wkv_ref[...])
    kcat = jnp.concatenate([kv[:, 0:GROUP], kpe_all], axis=1).astype(BF16)
    v = kv[:, GROUP:].astype(BF16)
    qcat = jnp.concatenate([q_nope, q_rope], axis=1)
    scale = (MLA_NOPE + MLA_ROPE) ** -0.5
    qb = min(seq, 256)
    width = qcat.shape[1]
    for r0 in range(0, seq, qb):
        qblk = qcat[r0:r0 + qb]
        acc = jnp.zeros((qb, GROUP), F32)
        for h in range(N_HEADS):
            hm = (_lane_group_mask(width, h * MLA_NOPE, MLA_NOPE)
                  | _lane_group_mask(width, N_HEADS * MLA_NOPE + h * MLA_ROPE, MLA_ROPE))
            s = _dot_nt(jnp.where(hm, qblk, 0.0).astype(BF16), kcat) * scale
            e = jnp.exp(s - jnp.max(s, axis=-1, keepdims=True))
            z = jnp.sum(e, axis=-1, keepdims=True)
            oh = _dot(e.astype(BF16), v) / z
            acc = acc + jnp.where(_lane_group_mask(GROUP, h * HEAD_DIM, HEAD_DIM), oh, 0.0)
        o_ref[r0:r0 + qb, :] = acc


def _mla(proj_b, row0, nseq, seq, q_norm, w_qb_p, kv_norm, w_kvb_p, latent_args):
    latent = latent_args is not None
    blk0 = row0 // seq
    in_specs = [pl.BlockSpec((seq, PROJ_B), lambda b: (blk0 + b, 0)),
                pl.BlockSpec(q_norm.shape, lambda b: (0, 0)),
                pl.BlockSpec(w_qb_p.shape, lambda b: (0, 0)),
                pl.BlockSpec(kv_norm.shape, lambda b: (0, 0)),
                pl.BlockSpec(w_kvb_p.shape, lambda b: (0, 0))]
    args = [proj_b, q_norm, w_qb_p, kv_norm, w_kvb_p]
    out_specs = [pl.BlockSpec((None, seq, GROUP), lambda b: (b, 0, 0))]
    out_shape = [jax.ShapeDtypeStruct((nseq, seq, GROUP), F32)]
    if latent:
        cckv, ckpe_t, cos, sin = latent_args
        past = cckv.shape[1]
        in_specs += [pl.BlockSpec((None, past, MLA_KV_RANK), lambda b: (b, 0, 0)),
                     pl.BlockSpec((None, past, 128), lambda b: (b, 0, 0)),
                     pl.BlockSpec(cos.shape, lambda b: (0, 0)),
                     pl.BlockSpec(sin.shape, lambda b: (0, 0))]
        args += [cckv, ckpe_t, cos, sin]
    else:
        out_specs += [pl.BlockSpec((None, seq, MLA_KV_RANK), lambda b: (b, 0, 0)),
                      pl.BlockSpec((None, seq, MLA_ROPE), lambda b: (b, 0, 0))]
        out_shape += [jax.ShapeDtypeStruct((nseq, seq, MLA_KV_RANK), F32),
                      jax.ShapeDtypeStruct((nseq, seq, MLA_ROPE), F32)]
    return pl.pallas_call(
        functools.partial(_mla_kernel, latent),
        grid=(nseq,),
        in_specs=in_specs, out_specs=out_specs, out_shape=out_shape,
        compiler_params=_cparams("arbitrary"),
        name="mla",
    )(*args)


def _diff_kernel(latent, lam_init, *refs):
    if latent:
        (c_ref, lam_ref, norm_ref, ck_ref, cv_ref, cos_ref, sin_ref, o_ref) = refs
    else:
        (c_ref, lam_ref, norm_ref, o_ref, k_ref, v_ref) = refs
    seq = c_ref.shape[0]
    dq = c_ref[:, 0:GROUP]
    dk = c_ref[:, GROUP:2 * GROUP]
    dv = c_ref[:, 2 * GROUP:3 * GROUP]
    if latent:
        cos = cos_ref[...]
        sin = sin_ref[...]
        dq = _rope(dq, cos, sin)
        k_all = jnp.concatenate([ck_ref[...], _rope(dk, cos, sin)], axis=0)
        v_all = jnp.concatenate([cv_ref[...], dv], axis=0)
    else:
        k_ref[...] = dk
        v_ref[...] = dv
        k_all = dk
        v_all = dv
    lv = lam_ref[...]
    lam = (jnp.exp(jnp.sum(lv[0:1] * lv[1:2], axis=-1, keepdims=True))
           - jnp.exp(jnp.sum(lv[2:3] * lv[3:4], axis=-1, keepdims=True)) + lam_init)
    k_bf = k_all.astype(BF16)
    v_bf = v_all.astype(BF16)
    scale = DIFF_DIM ** -0.5
    mean_op = jnp.where(_block_mask(GROUP, GROUP, HEAD_DIM, HEAD_DIM), 1.0 / HEAD_DIM, 0.0)
    qb = min(seq, 256)
    for r0 in range(0, seq, qb):
        qblk = dq[r0:r0 + qb]
        acc = jnp.zeros((qb, GROUP), F32)
        for h in range(N_HEADS):
            probs = []
            for comp in range(2):
                cm = _lane_group_mask(GROUP, h * HEAD_DIM + comp * DIFF_DIM, DIFF_DIM)
                s = _dot_nt(jnp.where(cm, qblk, 0.0).astype(BF16), k_bf) * scale
                e = jnp.exp(s - jnp.max(s, axis=-1, keepdims=True))
                probs.append(e / jnp.sum(e, axis=-1, keepdims=True))
            w = (probs[0] - lam * probs[1]).astype(BF16)
            acc = acc + jnp.where(_lane_group_mask(GROUP, h * HEAD_DIM, HEAD_DIM), _dot(w, v_bf), 0.0)
        ms = _dot(acc * acc, mean_op, precision=HIGHEST)
        o_ref[r0:r0 + qb, :] = acc * lax.rsqrt(ms + NORM_EPS) * norm_ref[...] * (1.0 - lam_init)


def _diff(proj_c, row0, nseq, seq, lam_p, norm_t, lam_init, latent_args):
    latent = latent_args is not None
    blk0 = row0 // seq
    in_specs = [pl.BlockSpec((seq, PROJ_C), lambda b: (blk0 + b, 0)),
                pl.BlockSpec(lam_p.shape, lambda b: (0, 0)),
                pl.BlockSpec(norm_t.shape, lambda b: (0, 0))]
    args = [proj_c, lam_p, norm_t]
    out_specs = [pl.BlockSpec((None, seq, GROUP), lambda b: (b, 0, 0))]
    out_shape = [jax.ShapeDtypeStruct((nseq, seq, GROUP), F32)]
    if latent:
        ck, cv, cos, sin = latent_args
        past = ck.shape[1]
        in_specs += [pl.BlockSpec((None, past, GROUP), lambda b: (b, 0, 0)),
                     pl.BlockSpec((None, past, GROUP), lambda b: (b, 0, 0)),
                     pl.BlockSpec(cos.shape, lambda b: (0, 0)),
                     pl.BlockSpec(sin.shape, lambda b: (0, 0))]
        args += [ck, cv, cos, sin]
    else:
        out_specs += [pl.BlockSpec((None, seq, GROUP), lambda b: (b, 0, 0))] * 2
        out_shape += [jax.ShapeDtypeStruct((nseq, seq, GROUP), F32)] * 2
    return pl.pallas_call(
        functools.partial(_diff_kernel, latent, lam_init),
        grid=(nseq,),
        in_specs=in_specs, out_specs=out_specs, out_shape=out_shape,
        compiler_params=_cparams("arbitrary"),
        name="diffattn",
    )(*args)


def _ssd_kernel(has_state, *refs):
    if has_state:
        (d_ref, cw_ref, cb_ref, dtb_ref, alog_ref, dskip_ref, norm_ref, s0_ref,
         o_ref, sfin_ref, xs_scr, bm_scr, cm_scr, xdt_scr, a_scr, st_scr, yf_scr, yb_scr) = refs
    else:
        (d_ref, cw_ref, cb_ref, dtb_ref, alog_ref, dskip_ref, norm_ref,
         o_ref, sfin_ref, xs_scr, bm_scr, cm_scr, xdt_scr, a_scr, st_scr, yf_scr, yb_scr) = refs
    seq = d_ref.shape[0]
    c = SSD_CHUNK
    nchunk = seq // c
    ngrp = 2 * SSD_STATE

    xin = d_ref[:, GROUP:GROUP + 512]
    rows = _iota(xin.shape, 0)
    prev = jnp.where(rows == 0, 0.0, pltpu.roll(xin, 1, 0))
    nxt = jnp.where(rows == seq - 1, 0.0, pltpu.roll(xin, seq - 1, 0))
    cw = cw_ref[...]
    xbc = _silu(cw[0:1] * prev + cw[1:2] * xin + cw[2:3] * nxt + cb_ref[...])
    xs = xbc[:, 0:GROUP]
    xs_scr[...] = xs
    bm_scr[...] = xbc[:, GROUP:GROUP + ngrp]
    cm_scr[...] = xbc[:, GROUP + ngrp:GROUP + 2 * ngrp]
    dt = _softplus(d_ref[:, GROUP + 512:GROUP + 640] + dtb_ref[...])
    a_scr[...] = dt * (-jnp.exp(alog_ref[...]))
    erow = _iota((128, GROUP), 0)
    ehead = _iota((128, GROUP), 1) // HEAD_DIM
    expand = tuple((erow == 4 * d + ehead).astype(F32) for d in range(2))
    for d in range(2):
        xdt_scr[d] = xs * _dot(dt, expand[d], precision=HIGHEST)
    if has_state:
        st_scr[...] = s0_ref[...]
    else:
        st_scr[...] = jnp.zeros_like(st_scr)

    row = _iota((c, c), 0)
    col = _iota((c, c), 1)
    tri = ((col <= row).astype(F32), (col >= row).astype(F32))
    keep = (col <= row, col >= row)
    grp_lane = _iota((1, ngrp), 1) // SSD_STATE
    valid = (_iota((ngrp, GROUP), 0) // SSD_STATE) == (_iota((ngrp, GROUP), 1) // (2 * HEAD_DIM))

    def chunk_step(d, r0, out_scr):
        a_c = a_scr[pl.ds(r0, c), :]
        bm_c = bm_scr[pl.ds(r0, c), :]
        cm_c = cm_scr[pl.ds(r0, c), :].astype(BF16)
        xdt_c = xdt_scr[d, pl.ds(r0, c), :]
        acum = _dot(tri[d], a_c, precision=HIGHEST)
        acum_t = acum.T
        acum_rep = _dot(acum, expand[d], precision=HIGHEST)
        bm2 = jnp.concatenate([jnp.where(grp_lane == g, bm_c, 0.0) for g in range(2)], axis=0)
        cb = _dot_nt(cm_c, bm2.astype(BF16))
        scores = []
        xparts = []
        for h in range(N_HEADS):
            lane = 4 * d + h
            seg = jnp.exp(jnp.where(keep[d], acum[:, lane:lane + 1] - acum_t[lane:lane + 1, :], -jnp.inf))
            g = h // 2
            scores.append((cb[:, g * c:(g + 1) * c] * seg).astype(BF16))
            xparts.append(jnp.where(_lane_group_mask(GROUP, h * HEAD_DIM, HEAD_DIM), xdt_c, 0.0))
        y = _dot(jnp.concatenate(scores, axis=1), jnp.concatenate(xparts, axis=0).astype(BF16))
        st = st_scr[d]
        y = y + _dot(cm_c, st.astype(BF16)) * jnp.exp(acum_rep)
        out_scr[pl.ds(r0, c), :] = y
        edge = acum_rep[c - 1:c] if d == 0 else acum_rep[0:1]
        xt = (xdt_c * jnp.exp(edge - acum_rep)).astype(BF16)
        upd = _dot_tn(bm_c.astype(BF16), xt)
        st_scr[d] = st * jnp.exp(edge) + jnp.where(valid, upd, 0.0)

    def body(n, carry):
        chunk_step(0, pl.multiple_of(n * c, c), yf_scr)
        chunk_step(1, pl.multiple_of((nchunk - 1 - n) * c, c), yb_scr)
        return carry

    lax.fori_loop(0, nchunk, body, 0)

    y = yf_scr[...] + yb_scr[...] + dskip_ref[...] * xs_scr[...]
    y = y * _silu(d_ref[:, 0:GROUP])
    o_ref[...] = y * lax.rsqrt(jnp.mean(y * y, axis=-1, keepdims=True) + NORM_EPS) * norm_ref[...]
    sfin_ref[...] = st_scr[...]


def _ssd(proj_d, row0, nseq, seq, conv_w, conv_b, dt_bias_p, a_log_p, d_rep, norm, s0):
    has_state = s0 is not None
    blk0 = row0 // seq
    ngrp = 2 * SSD_STATE
    small = [conv_w, conv_b, dt_bias_p, a_log_p, d_rep, norm]
    in_specs = ([pl.BlockSpec((seq, PROJ_D), lambda b: (blk0 + b, 0))]
                + [pl.BlockSpec(s.shape, lambda b: (0, 0)) for s in small])
    args = [proj_d] + small
    if has_state:
        in_specs.append(pl.BlockSpec((None, 2, ngrp, GROUP), lambda b: (b, 0, 0, 0)))
        args.append(s0)
    return pl.pallas_call(
        functools.partial(_ssd_kernel, has_state),
        grid=(nseq,),
        in_specs=in_specs,
        out_specs=[pl.BlockSpec((None, seq, GROUP), lambda b: (b, 0, 0)),
                   pl.BlockSpec((None, 2, ngrp, GROUP), lambda b: (b, 0, 0, 0))],
        out_shape=[jax.ShapeDtypeStruct((nseq, seq, GROUP), F32),
                   jax.ShapeDtypeStruct((nseq, 2, ngrp, GROUP), F32)],
        scratch_shapes=[pltpu.VMEM((seq, GROUP), F32),
                        pltpu.VMEM((seq, ngrp), F32),
                        pltpu.VMEM((seq, ngrp), F32),
                        pltpu.VMEM((2, seq, GROUP), F32),
                        pltpu.VMEM((seq, 128), F32),
                        pltpu.VMEM((2, ngrp, GROUP), F32),
                        pltpu.VMEM((seq, GROUP), F32),
                        pltpu.VMEM((seq, GROUP), F32)],
        compiler_params=_cparams("arbitrary"),
        name="ssd",
    )(*args)


def _outproj_kernel(hg_ref, mla_ref, df_ref, ssd_ref, w_ref, x_ref, m_ref, g_ref, b_ref, o_ref):
    mixed = _dot(hg_ref[...].astype(BF16), w_ref[0:GROUP, :])
    for i, r in enumerate((mla_ref, df_ref, ssd_ref), start=1):
        mixed = mixed + _dot(r[...].astype(BF16), w_ref[i * GROUP:(i + 1) * GROUP, :])
    o_ref[...] = _layernorm(ALPHA * x_ref[...] + m_ref[2] * mixed, g_ref[...], b_ref[...])


def _outproj(parts, w_out_bf, x, mods_l, ln_g, ln_b, t_ctx, s_lat):
    t, d = x.shape
    tm = 256
    return pl.pallas_call(
        _outproj_kernel,
        grid=(t // tm,),
        in_specs=[pl.BlockSpec((tm, GROUP), lambda i: (i, 0))] * 4
        + [pl.BlockSpec(w_out_bf.shape, lambda i: (0, 0)),
           pl.BlockSpec((tm, d), lambda i: (i, 0)),
           pl.BlockSpec((None, 6, 1, d), _mod_row_map(tm, t_ctx, s_lat)),
           pl.BlockSpec((1, d), lambda i: (0, 0)),
           pl.BlockSpec((1, d), lambda i: (0, 0))],
        out_specs=pl.BlockSpec((tm, d), lambda i: (i, 0)),
        out_shape=jax.ShapeDtypeStruct((t, d), F32),
        compiler_params=_cparams("arbitrary"),
        name="outproj_ln",
    )(*parts, w_out_bf, x, mods_l, ln_g, ln_b)


def _top_rows(s, k, extra=()):
    r = s.shape[0]
    rid = _iota(s.shape, 0)
    vals, ids = [], []
    picked = [[] for _ in extra]
    for _ in range(k):
        m = jnp.max(s, axis=0, keepdims=True)
        i = jnp.min(jnp.where(s == m, rid, r), axis=0, keepdims=True)
        hit = rid == i
        vals.append(m)
        ids.append(i)
        for lst, arr in zip(picked, extra):
            lst.append(jnp.max(jnp.where(hit, arr, -1), axis=0, keepdims=True))
        s = jnp.where(hit, -jnp.inf, s)
    cat = lambda xs: jnp.concatenate(xs, axis=0)
    return cat(vals), cat(ids), [cat(p) for p in picked]


def _router_kernel(x_ref, m_ref, wqt_ref, keys_ref, h_ref, a_ref, b_ref, g_ref):
    tm = x_ref.shape[0]
    hb = (x_ref[...] * (1.0 + m_ref[4]) + m_ref[3]).astype(BF16)
    h_ref[...] = hb
    qt = _dot_nt(wqt_ref[...], hb).astype(BF16)
    k = PEER_TOPK
    a_rows, b_rows, g_rows = [], [], []
    for head in range(PEER_HEADS):
        tv, ti = [], []
        for half in range(2):
            g = 2 * head + half
            sc = _dot(keys_ref[g], qt[g * PEER_HALF:(g + 1) * PEER_HALF])
            v, i, _ = _top_rows(sc, k)
            tv.append(v)
            ti.append(i)
        cs = [tv[0][0:1] + tv[1]]
        ca = [jnp.broadcast_to(ti[0][0:1], (k, tm))]
        cb = [ti[1]]
        for k1 in range(1, 8):
            cs.append(tv[0][k1:k1 + 1] + tv[1][0:8])
            ca.append(jnp.broadcast_to(ti[0][k1:k1 + 1], (8, tm)))
            cb.append(ti[1][0:8])
        cs.append(tv[0][8:16] + tv[1][0:1])
        ca.append(ti[0][8:16])
        cb.append(jnp.broadcast_to(ti[1][0:1], (8, tm)))
        best, _, (sel_a, sel_b) = _top_rows(jnp.concatenate(cs, axis=0), k,
                                            extra=(jnp.concatenate(ca, axis=0), jnp.concatenate(cb, axis=0)))
        e = jnp.exp(best - best[0:1])
        g_rows.append(e / jnp.sum(e, axis=0, keepdims=True))
        a_rows.append(sel_a)
        b_rows.append(sel_b)
    a_ref[...] = jnp.concatenate(a_rows, axis=0).astype(F32).T.astype(I32)
    b_ref[...] = jnp.concatenate(b_rows, axis=0).astype(F32).T.astype(I32)
    g_ref[...] = jnp.concatenate(g_rows, axis=0).T


def _router(x, mods_l, wq_t, keys, t_ctx, s_lat):
    t, d = x.shape
    tm = 128
    nslot = PEER_HEADS * PEER_TOPK
    return pl.pallas_call(
        _router_kernel,
        grid=(t // tm,),
        in_specs=[pl.BlockSpec((tm, d), lambda i: (i, 0)),
                  pl.BlockSpec((None, 6, 1, d), _mod_row_map(tm, t_ctx, s_lat)),
                  pl.BlockSpec(wq_t.shape, lambda i: (0, 0)),
                  pl.BlockSpec(keys.shape, lambda i: (0, 0, 0))],
        out_specs=[pl.BlockSpec((tm, d), lambda i: (i, 0)),
                   pl.BlockSpec((tm, nslot), lambda i: (i, 0)),
                   pl.BlockSpec((tm, nslot), lambda i: (i, 0)),
                   pl.BlockSpec((tm, nslot), lambda i: (i, 0))],
        out_shape=[jax.ShapeDtypeStruct((t, d), BF16),
                   jax.ShapeDtypeStruct((t, nslot), I32),
                   jax.ShapeDtypeStruct((t, nslot), I32),
                   jax.ShapeDtypeStruct((t, nslot), F32)],
        compiler_params=_cparams("arbitrary"),
        name="peer_router",
    )(x, mods_l, wq_t, keys)


def _gates_kernel(a_ref, b_ref, g_ref, o_ref, w_scr):
    tm = a_ref.shape[0]
    n = PEER_KEYS
    key = _iota((tm, n, a_ref.shape[2]), 1)
    onehot_a = jnp.where(key == a_ref[...], 1.0, 0.0).astype(BF16)
    gated_b = jnp.where(key == b_ref[...], g_ref[...], 0.0).astype(BF16)
    w_scr[...] = lax.dot_general(onehot_a, gated_b, (((2,), (2,)), ((0,), (0,))),
                                 preferred_element_type=F32)
    for r in range(n):
        o_ref[:, r * n:(r + 1) * n] = w_scr[:, r, :].astype(BF16)


def _gates(a_idx, b_idx, gate):
    t, nslot = a_idx.shape
    tm = 64
    spec = pl.BlockSpec((tm, 1, nslot), lambda i: (i, 0, 0))
    return pl.pallas_call(
        _gates_kernel,
        grid=(t // tm,),
        in_specs=[spec, spec, spec],
        out_specs=pl.BlockSpec((tm, N_EXPERTS), lambda i: (i, 0)),
        out_shape=jax.ShapeDtypeStruct((t, N_EXPERTS), BF16),
        scratch_shapes=[pltpu.VMEM((tm, PEER_KEYS, PEER_KEYS), F32)],
        compiler_params=_cparams("arbitrary"),
        name="peer_gates",
    )(a_idx.reshape(t, 1, nslot), b_idx.reshape(t, 1, nslot), gate.reshape(t, 1, nslot))


def _experts_kernel(h_ref, u_ref, v_ref, w_ref, x_ref, m_ref, g_ref, b_ref, o_ref, acc_ref):
    j = pl.program_id(1)

    @pl.when(j == 0)
    def _():
        acc_ref[...] = jnp.zeros_like(acc_ref)

    act = _gelu_tanh(_dot_nt(h_ref[...], u_ref[...]))
    acc_ref[...] += _dot((act * w_ref[...].astype(F32)).astype(BF16), v_ref[...])

    @pl.when(j == pl.num_programs(1) - 1)
    def _():
        o_ref[...] = _layernorm(ALPHA * x_ref[...] + m_ref[5] * acc_ref[...], g_ref[...], b_ref[...])


def _experts(h_bf, u_bf, v_bf, w_gate, x, mods_l, ln_g, ln_b, t_ctx, s_lat):
    t, d = x.shape
    tm, te = 512, 1024
    return pl.pallas_call(
        _experts_kernel,
        grid=(t // tm, N_EXPERTS // te),
        in_specs=[pl.BlockSpec((tm, d), lambda i, j: (i, 0)),
                  pl.BlockSpec((te, d), lambda i, j: (j, 0)),
                  pl.BlockSpec((te, d), lambda i, j: (j, 0)),
                  pl.BlockSpec((tm, te), lambda i, j: (i, j)),
                  pl.BlockSpec((tm, d), lambda i, j: (i, 0)),
                  pl.BlockSpec((None, 6, 1, d), _mod_row_map(tm, t_ctx, s_lat)),
                  pl.BlockSpec((1, d), lambda i, j: (0, 0)),
                  pl.BlockSpec((1, d), lambda i, j: (0, 0))],
        out_specs=pl.BlockSpec((tm, d), lambda i, j: (i, 0)),
        out_shape=jax.ShapeDtypeStruct((t, d), F32),
        scratch_shapes=[pltpu.VMEM((tm, d), F32)],
        compiler_params=_cparams("arbitrary", "arbitrary"),
        name="peer_experts",
    )(h_bf, u_bf, v_bf, w_gate, x, mods_l, ln_g, ln_b)


def _pad_cols(w, n):
    return jnp.concatenate([w, jnp.zeros((w.shape[0], n), w.dtype)], axis=1) if n else w


def _layout_w_in(w):
    a = w[:, 0:1280]
    mcq, mckv, mkpe = w[:, 1280:1472], w[:, 1472:1600], w[:, 1600:1632]
    b = jnp.concatenate([mckv, _pad_cols(mcq, 64), mkpe, mkpe, mkpe, mkpe], axis=1)
    c = w[:, 1632:2400]
    d = _pad_cols(w[:, 2400:3176], PROJ_D - 776)
    return jnp.concatenate([a, b, c, d], axis=1).astype(BF16)


def _layout_w_qb(w):
    w4 = w.reshape(MLA_Q_RANK, N_HEADS, MLA_NOPE + MLA_ROPE)
    return jnp.concatenate([w4[:, :, :MLA_NOPE].reshape(MLA_Q_RANK, -1),
                            w4[:, :, MLA_NOPE:].reshape(MLA_Q_RANK, -1)], axis=1).astype(BF16)


def _layout_w_kvb(w):
    w4 = w.reshape(MLA_KV_RANK, N_HEADS, MLA_NOPE + HEAD_DIM)
    return jnp.concatenate([w4[:, :, :MLA_NOPE].reshape(MLA_KV_RANK, -1),
                            w4[:, :, MLA_NOPE:].reshape(MLA_KV_RANK, -1)], axis=1).astype(BF16)


def _rope_tables(seq):
    rows = seq // GRID_W
    row = jnp.repeat(jnp.arange(rows, dtype=F32), GRID_W)
    col = jnp.tile(jnp.arange(GRID_W, dtype=F32), rows)
    freqs = ROPE_BASE ** (-jnp.arange(ROPE_PAIRS, dtype=F32) / ROPE_PAIRS)
    cos_l, sin_l = [], []
    for pos in (row, col):
        ang = pos[:, None] * freqs
        cos_l += [jnp.cos(ang), jnp.cos(ang)]
        sin_l += [-jnp.sin(ang), jnp.sin(ang)]
    return jnp.concatenate(cos_l, axis=1), jnp.concatenate(sin_l, axis=1)


def _hgrn_state_pack(st):
    b = st.shape[0]
    st_t = jnp.swapaxes(st, -1, -2)
    zero = jnp.zeros_like(st_t[:, :, 0])
    rows = [jnp.concatenate([st_t[:, :, h] if g == h else zero for g in range(N_HEADS)], axis=-1)
            for h in range(N_HEADS)]
    return jnp.concatenate(rows, axis=-2).reshape(b, 2, GROUP, GROUP)


def _hgrn_state_unpack(sb):
    blocks = [sb[:, :, h * HEAD_DIM:(h + 1) * HEAD_DIM, h * HEAD_DIM:(h + 1) * HEAD_DIM]
              for h in range(N_HEADS)]
    return jnp.swapaxes(jnp.stack(blocks, axis=2), -1, -2)


def _ssd_state_pack(st):
    st_t = jnp.swapaxes(st, -1, -2)
    zero = jnp.zeros_like(st_t[:, :, 0])
    rows = [jnp.concatenate([st_t[:, :, h] if h // 2 == g else zero for h in range(N_HEADS)], axis=-1)
            for g in range(2)]
    return jnp.concatenate(rows, axis=-2)


def _ssd_state_unpack(sb):
    blocks = [sb[:, :, (h // 2) * SSD_STATE:(h // 2 + 1) * SSD_STATE, h * HEAD_DIM:(h + 1) * HEAD_DIM]
              for h in range(N_HEADS)]
    return jnp.swapaxes(jnp.stack(blocks, axis=2), -1, -2)


def _tile_lanes(v, n):
    return jnp.tile(v.reshape(1, -1), (1, n))


def kernel(x_prompt, x_sample, cache_mla_ckv, cache_mla_kpe, cache_diff_k, cache_diff_v, state_hgrn, state_ssd, c, c_ctx, w_mod, b_mod, w_in, hgrn_lb, hgrn_norm, mla_q_norm, mla_w_qb, mla_kv_norm, mla_w_kvb, diff_lambda, diff_norm, ssd_conv_w, ssd_conv_b, ssd_dt_bias, ssd_a_log, ssd_d, ssd_norm, w_out, ln1_g, ln1_b, peer_wq, peer_keys, peer_u, peer_v, ln2_g, ln2_b):
    nb, seq, d = x_prompt.shape
    nlat, lseq, _ = x_sample.shape
    depth = w_in.shape[0]
    t_ctx = nb * seq
    x = jnp.concatenate([x_prompt.reshape(t_ctx, d), x_sample.reshape(nlat * lseq, d)], axis=0)

    cond8 = jnp.concatenate([c_ctx.reshape(1, d), c, jnp.zeros((8 - 1 - nlat, d), F32)], axis=0)
    mods = _mods(cond8, w_mod, b_mod)
    mods = mods[:, :1 + nlat].reshape(depth, 1 + nlat, 6, 1, d)

    cos32, sin32 = _rope_tables(lseq)
    cos128, sin128 = jnp.tile(cos32, (1, 4)), jnp.tile(sin32, (1, 4))
    cos256, sin256 = jnp.tile(cos32, (1, 8)), jnp.tile(sin32, (1, 8))

    produced = []
    for l in range(depth):
        mods_l = mods[l]
        pa, pb, pc, pd = _inproj(x, mods_l, _layout_w_in(w_in[l]), t_ctx, lseq)

        norm_hg = _tile_lanes(hgrn_norm[l], N_HEADS)
        hg_ctx, hg_fin = _hgrn(pa, 0, nb, seq, hgrn_lb, norm_hg, None, l)
        hg_lat, _ = _hgrn(pa, t_ctx, nlat, lseq, hgrn_lb, norm_hg, _hgrn_state_pack(state_hgrn[:, l]), l)

        mla_w = (mla_q_norm[l].reshape(1, -1), _layout_w_qb(mla_w_qb[l]),
                 mla_kv_norm[l].reshape(1, -1), _layout_w_kvb(mla_w_kvb[l]))
        mla_ctx, new_ckv, new_kpe = _mla(pb, 0, nb, seq, *mla_w, None)
        (mla_lat,) = _mla(pb, t_ctx, nlat, lseq, *mla_w,
                          (cache_mla_ckv[:, l], jnp.tile(cache_mla_kpe[:, l], (1, 1, 4)), cos128, sin128))

        lam_init = 0.8 - 0.6 * math.exp(-0.3 * l)
        norm_df = _tile_lanes(diff_norm[l], N_HEADS)
        df_ctx, new_dk, new_dv = _diff(pc, 0, nb, seq, diff_lambda[l], norm_df, lam_init, None)
        past = cache_diff_k.shape[2]
        (df_lat,) = _diff(pc, t_ctx, nlat, lseq, diff_lambda[l], norm_df, lam_init,
                          (cache_diff_k[:, l].reshape(nlat, past, GROUP),
                           cache_diff_v[:, l].reshape(nlat, past, GROUP), cos256, sin256))

        ssd_w = (ssd_conv_w[l], ssd_conv_b[l].reshape(1, -1),
                 _pad_cols(ssd_dt_bias[l].reshape(1, -1), 120), _pad_cols(ssd_a_log[l].reshape(1, -1), 120),
                 jnp.repeat(ssd_d[l], HEAD_DIM).reshape(1, -1), ssd_norm[l].reshape(1, -1))
        ssd_ctx, ssd_fin = _ssd(pd, 0, nb, seq, *ssd_w, None)
        ssd_lat, _ = _ssd(pd, t_ctx, nlat, lseq, *ssd_w, _ssd_state_pack(state_ssd[:, l]))

        parts = [jnp.concatenate([a.reshape(t_ctx, GROUP), b.reshape(nlat * lseq, GROUP)], axis=0)
                 for a, b in ((hg_ctx, hg_lat), (mla_ctx, mla_lat), (df_ctx, df_lat), (ssd_ctx, ssd_lat))]
        x = _outproj(parts, w_out[l].astype(BF16), x, mods_l, ln1_g[l].reshape(1, d), ln1_b[l].reshape(1, d),
                     t_ctx, lseq)

        keys = peer_keys[l].reshape(2 * PEER_HEADS, PEER_KEYS, PEER_HALF).astype(BF16)
        h_bf, a_idx, b_idx, gate = _router(x, mods_l, peer_wq[l].T.astype(BF16), keys, t_ctx, lseq)
        w_gate = _gates(a_idx, b_idx, gate)
        x = _experts(h_bf, peer_u[l].astype(BF16), peer_v[l].astype(BF16), w_gate, x, mods_l,
                     ln2_g[l].reshape(1, d), ln2_b[l].reshape(1, d), t_ctx, lseq)

        produced.append((new_ckv, new_kpe,
                         new_dk.reshape(nb, seq, N_HEADS, 2, DIFF_DIM),
                         new_dv.reshape(nb, seq, N_HEADS, 2 * DIFF_DIM),
                         _hgrn_state_unpack(hg_fin), _ssd_state_unpack(ssd_fin)))

    y_prompt = x[:t_ctx].reshape(nb, seq, d)
    y_sample = x[t_ctx:].reshape(nlat, lseq, d)
    stacked = tuple(jnp.stack([p[i] for p in produced], axis=1) for i in range(6))
    return (y_prompt, y_sample) + stacked
```

```python
import functools
import math

import jax
import jax.numpy as jnp
from jax import lax
from jax.experimental import pallas as pl
from jax.experimental.pallas import tpu as pltpu

F32 = jnp.float32
BF16 = jnp.bfloat16
I32 = jnp.int32
HIGHEST = lax.Precision.HIGHEST

D_MODEL = 1024
GROUP = 256
N_HEADS = 4
HEAD_DIM = 64
HG_BLOCK = 16
SSD_CHUNK = 128
SSD_STATE = 64
MLA_Q_RANK = 192
MLA_KV_RANK = 128
MLA_NOPE = 64
MLA_ROPE = 32
DIFF_DIM = 32
GRID_W = 64
ROPE_PAIRS = 8
ROPE_BASE = 10000.0
PEER_HEADS = 8
PEER_KEYS = 128
PEER_TOPK = 16
PEER_HALF = 64
N_EXPERTS = PEER_KEYS * PEER_KEYS
NORM_EPS = 1e-6
LN_EPS = 1e-5
DEPTH = 2
ALPHA = (2.0 * DEPTH) ** 0.25

PROJ_A = 5 * GROUP
PROJ_B = 512
PROJ_C = 3 * GROUP
PROJ_D = 896
VMEM_LIMIT = 56 * 1024 * 1024


def _cparams(*sem):
    return pltpu.CompilerParams(dimension_semantics=sem, vmem_limit_bytes=VMEM_LIMIT)


def _sigmoid(x):
    return 1.0 / (1.0 + jnp.exp(-x))


def _silu(x):
    return x * _sigmoid(x)


def _softplus(x):
    return jnp.maximum(x, 0.0) + jnp.log(1.0 + jnp.exp(-jnp.abs(x)))


def _gelu_tanh(x):
    return 0.5 * x * (1.0 + jnp.tanh(math.sqrt(2.0 / math.pi) * (x + 0.044715 * (x * x * x))))


def _dot(a, b, precision=None):
    return jnp.dot(a, b, preferred_element_type=F32, precision=precision)


def _dot_nt(a, b, precision=None):
    return lax.dot_general(a, b, (((1,), (1,)), ((), ())), preferred_element_type=F32,
                           precision=precision)


def _dot_tn(a, b, precision=None):
    return lax.dot_general(a, b, (((0,), (0,)), ((), ())), preferred_element_type=F32,
                           precision=precision)


def _iota(shape, dim):
    return lax.broadcasted_iota(I32, shape, dim)


def _block_mask(rows, cols, rblk, cblk):
    return (_iota((rows, cols), 0) // rblk) == (_iota((rows, cols), 1) // cblk)


def _lane_group_mask(width, start, size):
    lane = _iota((1, width), 1)
    return (lane >= start) & (lane < start + size)


def _layernorm(v, g, b):
    mu = jnp.mean(v, axis=-1, keepdims=True)
    d = v - mu
    var = jnp.mean(d * d, axis=-1, keepdims=True)
    return d * lax.rsqrt(var + LN_EPS) * g + b


def _swap_halves16(x):
    width = x.shape[-1]
    lane = _iota(x.shape, x.ndim - 1)
    up = pltpu.roll(x, width - 8, x.ndim - 1)
    down = pltpu.roll(x, 8, x.ndim - 1)
    return jnp.where((lane % 16) < 8, up, down)


def _rope(x, cos, sin_signed):
    return x * cos + _swap_halves16(x) * sin_signed


def _mods_kernel(c_ref, w_ref, b_ref, o_ref):
    s = _silu(c_ref[...]).astype(BF16)
    o_ref[...] = _dot(s, w_ref[...].astype(BF16)) + b_ref[...]


def _mods(cond8, w_mod, b_mod):
    depth, d, n = w_mod.shape
    tn = 1536
    return pl.pallas_call(
        _mods_kernel,
        grid=(depth, n // tn),
        in_specs=[pl.BlockSpec((8, d), lambda l, j: (0, 0)),
                  pl.BlockSpec((None, d, tn), lambda l, j: (l, 0, j)),
                  pl.BlockSpec((None, 1, tn), lambda l, j: (l, 0, j))],
        out_specs=pl.BlockSpec((None, 8, tn), lambda l, j: (l, 0, j)),
        out_shape=jax.ShapeDtypeStruct((depth, 8, n), F32),
        compiler_params=_cparams("arbitrary", "arbitrary"),
        name="mods",
    )(cond8, w_mod, b_mod.reshape(depth, 1, n))


def _mod_row_map(tm, t_ctx, s_lat):
    def index_map(i, *_):
        start = i * tm
        return (jnp.where(start < t_ctx, 0, 1 + (start - t_ctx) // s_lat), 0, 0, 0)
    return index_map


def _inproj_kernel(x_ref, m_ref, w_ref, oa_ref, ob_ref, oc_ref, od_ref):
    h = (x_ref[...] * (1.0 + m_ref[1]) + m_ref[0]).astype(BF16)
    start = 0
    for o_ref in (oa_ref, ob_ref, oc_ref, od_ref):
        width = o_ref.shape[-1]
        o_ref[...] = _dot(h, w_ref[:, start:start + width])
        start += width


def _inproj(x, mods_l, w_in_p, t_ctx, s_lat):
    t, d = x.shape
    tm = 256
    widths = (PROJ_A, PROJ_B, PROJ_C, PROJ_D)
    return pl.pallas_call(
        _inproj_kernel,
        grid=(t // tm,),
        in_specs=[pl.BlockSpec((tm, d), lambda i: (i, 0)),
                  pl.BlockSpec((None, 6, 1, d), _mod_row_map(tm, t_ctx, s_lat)),
                  pl.BlockSpec(w_in_p.shape, lambda i: (0, 0))],
        out_specs=[pl.BlockSpec((tm, w), lambda i: (i, 0)) for w in widths],
        out_shape=[jax.ShapeDtypeStruct((t, w), F32) for w in widths],
        compiler_params=_cparams("arbitrary"),
        name="inproj",
    )(x, mods_l, w_in_p)


def _hgrn_kernel(layer, has_state, *refs):
    if has_state:
        (a_ref, lb_ref, norm_ref, s0_ref, o_ref, sfin_ref,
         q_scr, k_scr, lf_scr, st_scr, of_scr, ob_scr) = refs
    else:
        (a_ref, lb_ref, norm_ref, o_ref, sfin_ref,
         q_scr, k_scr, lf_scr, st_scr, of_scr, ob_scr) = refs
        s0_ref = None
    seq = a_ref.shape[0]
    nblk = seq // HG_BLOCK
    c = HG_BLOCK

    lbp = lb_ref[...]
    e = jnp.exp(lbp - jnp.max(lbp, axis=0, keepdims=True))
    p = e / jnp.sum(e, axis=0, keepdims=True)
    lower = jnp.sum(p[1:layer + 1], axis=0) if layer > 0 else jnp.zeros_like(p[0])

    q_scr[...] = _silu(a_ref[:, 0:GROUP])
    for d in range(2):
        lb = lower[d:d + 1]
        f = lb + (1.0 - lb) * _sigmoid(a_ref[:, (1 + d) * GROUP:(2 + d) * GROUP])
        k_scr[d] = 1.0 - f
        lf_scr[d] = jnp.log(f)
    if has_state:
        st_scr[...] = s0_ref[...]
    else:
        st_scr[...] = jnp.zeros_like(st_scr)

    row = _iota((c, c), 0)
    col = _iota((c, c), 1)
    tri = ((col <= row).astype(F32), (col >= row).astype(F32))
    rowi = _iota((c, GROUP), 0)
    bd_ones = _block_mask(GROUP, GROUP, HEAD_DIM, HEAD_DIM).astype(BF16)
    bd_mask = _block_mask(GROUP, GROUP, HEAD_DIM, HEAD_DIM)

    def block_step(d, r0, out_scr):
        qb = q_scr[pl.ds(r0, c), :]
        kb = k_scr[d, pl.ds(r0, c), :]
        vb = a_ref[pl.ds(r0, c), 3 * GROUP:4 * GROUP]
        lfb = lf_scr[d, pl.ds(r0, c), :]
        bc = _dot(tri[d], lfb, precision=HIGHEST)
        parts = []
        for j in range(c):
            keep = (rowi >= j) if d == 0 else (rowi <= j)
            dec = jnp.exp(jnp.where(keep, bc - bc[j:j + 1], -jnp.inf))
            parts.append((dec * qb * kb[j:j + 1]).astype(BF16))
        srep = _dot(jnp.concatenate(parts, axis=0), bd_ones)
        o = srep[0:c] * vb[0:1]
        for j in range(1, c):
            o = o + srep[j * c:(j + 1) * c] * vb[j:j + 1]
        st = st_scr[d]
        o = o + _dot_nt((qb * jnp.exp(bc)).astype(BF16), st.astype(BF16))
        out_scr[pl.ds(r0, c), :] = o
        edge = bc[c - 1:c] if d == 0 else bc[0:1]
        kt = (kb * jnp.exp(edge - bc)).astype(BF16)
        upd = _dot_tn(vb.astype(BF16), kt)
        st_scr[d] = st * jnp.exp(edge) + jnp.where(bd_mask, upd, 0.0)

    def body(n, carry):
        block_step(0, pl.multiple_of(n * c, c), of_scr)
        block_step(1, pl.multiple_of((nblk - 1 - n) * c, c), ob_scr)
        return carry

    lax.fori_loop(0, nblk, body, 0)

    o = of_scr[...] + ob_scr[...]
    mean_op = jnp.where(_block_mask(GROUP, GROUP, HEAD_DIM, HEAD_DIM), 1.0 / HEAD_DIM, 0.0)
    ms = _dot(o * o, mean_op, precision=HIGHEST)
    y = o * lax.rsqrt(ms + NORM_EPS) * norm_ref[...]
    o_ref[...] = y * _silu(a_ref[:, 4 * GROUP:5 * GROUP])
    sfin_ref[...] = st_scr[...]


def _hgrn(proj_a, row0, nseq, seq, hgrn_lb, norm_t, s0, layer):
    has_state = s0 is not None
    blk0 = row0 // seq
    in_specs = [pl.BlockSpec((seq, PROJ_A), lambda b: (blk0 + b, 0)),
                pl.BlockSpec(hgrn_lb.shape, lambda b: (0, 0, 0)),
                pl.BlockSpec((1, GROUP), lambda b: (0, 0))]
    args = [proj_a, hgrn_lb, norm_t]
    if has_state:
        in_specs.append(pl.BlockSpec((None, 2, GROUP, GROUP), lambda b: (b, 0, 0, 0)))
        args.append(s0)
    return pl.pallas_call(
        functools.partial(_hgrn_kernel, layer, has_state),
        grid=(nseq,),
        in_specs=in_specs,
        out_specs=[pl.BlockSpec((None, seq, GROUP), lambda b: (b, 0, 0)),
                   pl.BlockSpec((None, 2, GROUP, GROUP), lambda b: (b, 0, 0, 0))],
        out_shape=[jax.ShapeDtypeStruct((nseq, seq, GROUP), F32),
                   jax.ShapeDtypeStruct((nseq, 2, GROUP, GROUP), F32)],
        scratch_shapes=[pltpu.VMEM((seq, GROUP), F32),
                        pltpu.VMEM((2, seq, GROUP), F32),
                        pltpu.VMEM((2, seq, GROUP), F32),
                        pltpu.VMEM((2, GROUP, GROUP), F32),
                        pltpu.VMEM((seq, GROUP), F32),
                        pltpu.VMEM((seq, GROUP), F32)],
        compiler_params=_cparams("arbitrary"),
        name="hgrn",
    )(*args)


def _mla_kernel(latent, *refs):
    if latent:
        (b_ref, qn_ref, wq_ref, kvn_ref, wkv_ref, cckv_ref, ckpe_ref, cos_ref, sin_ref,
         o_ref) = refs
    else:
        (b_ref, qn_ref, wq_ref, kvn_ref, wkv_ref, o_ref, ckv_ref, kpe_ref) = refs
    seq = b_ref.shape[0]
    mckv = b_ref[:, 0:MLA_KV_RANK]
    mcq = b_ref[:, MLA_KV_RANK:MLA_KV_RANK + MLA_Q_RANK]
    kpe_t = b_ref[:, 384:512]

    cq = mcq * lax.rsqrt(jnp.mean(mcq * mcq, axis=-1, keepdims=True) + NORM_EPS) * qn_ref[...]
    qf = _dot(cq.astype(BF16), wq_ref[...])
    ckv = mckv * lax.rsqrt(jnp.mean(mckv * mckv, axis=-1, keepdims=True) + NORM_EPS) * kvn_ref[...]
    q_nope = qf[:, 0:N_HEADS * MLA_NOPE]
    q_rope = qf[:, N_HEADS * MLA_NOPE:]
    if latent:
        cos = cos_ref[...]
        sin = sin_ref[...]
        q_rope = _rope(q_rope, cos, sin)
        ckv_all = jnp.concatenate([cckv_ref[...], ckv], axis=0)
        kpe_all = jnp.concatenate([ckpe_ref[...], _rope(kpe_t, cos, sin)], axis=0)
    else:
        ckv_ref[...] = ckv
        kpe_ref[...] = kpe_t[:, 0:MLA_ROPE]
        ckv_all = ckv
        kpe_all = kpe_t
    kv = _dot(ckv_all.astype(BF16), wkv_ref[...])
    kcat = jnp.concatenate([kv[:, 0:GROUP], kpe_all], axis=1).astype(BF16)
    v = kv[:, GROUP:].astype(BF16)
    qcat = jnp.concatenate([q_nope, q_rope], axis=1)
    scale = (MLA_NOPE + MLA_ROPE) ** -0.5
    qb = min(seq, 256)
    width = qcat.shape[1]
    for r0 in range(0, seq, qb):
        qblk = qcat[r0:r0 + qb]
        acc = jnp.zeros((qb, GROUP), F32)
        for h in range(N_HEADS):
            hm = (_lane_group_mask(width, h * MLA_NOPE, MLA_NOPE)
                  | _lane_group_mask(width, N_HEADS * MLA_NOPE + h * MLA_ROPE, MLA_ROPE))
            s = _dot_nt(jnp.where(hm, qblk, 0.0).astype(BF16), kcat) * scale
            e = jnp.exp(s - jnp.max(s, axis=-1, keepdims=True))
            z = jnp.sum(e, axis=-1, keepdims=True)
            oh = _dot(e.astype(BF16), v) / z
            acc = acc + jnp.where(_lane_group_mask(GROUP, h * HEAD_DIM, HEAD_DIM), oh, 0.0)
        o_ref[r0:r0 + qb, :] = acc


def _mla(proj_b, row0, nseq, seq, q_norm, w_qb_p, kv_norm, w_kvb_p, latent_args):
    latent = latent_args is not None
    blk0 = row0 // seq
    in_specs = [pl.BlockSpec((seq, PROJ_B), lambda b: (blk0 + b, 0)),
                pl.BlockSpec(q_norm.shape, lambda b: (0, 0)),
                pl.BlockSpec(w_qb_p.shape, lambda b: (0, 0)),
                pl.BlockSpec(kv_norm.shape, lambda b: (0, 0)),
                pl.BlockSpec(w_kvb_p.shape, lambda b: (0, 0))]
    args = [proj_b, q_norm, w_qb_p, kv_norm, w_kvb_p]
    out_specs = [pl.BlockSpec((None, seq, GROUP), lambda b: (b, 0, 0))]
    out_shape = [jax.ShapeDtypeStruct((nseq, seq, GROUP), F32)]
    if latent:
        cckv, ckpe_t, cos, sin = latent_args
        past = cckv.shape[1]
        in_specs += [pl.BlockSpec((None, past, MLA_KV_RANK), lambda b: (b, 0, 0)),
                     pl.BlockSpec((None, past, 128), lambda b: (b, 0, 0)),
                     pl.BlockSpec(cos.shape, lambda b: (0, 0)),
                     pl.BlockSpec(sin.shape, lambda b: (0, 0))]
        args += [cckv, ckpe_t, cos, sin]
    else:
        out_specs += [pl.BlockSpec((None, seq, MLA_KV_RANK), lambda b: (b, 0, 0)),
                      pl.BlockSpec((None, seq, MLA_ROPE), lambda b: (b, 0, 0))]
        out_shape += [jax.ShapeDtypeStruct((nseq, seq, MLA_KV_RANK), F32),
                      jax.ShapeDtypeStruct((nseq, seq, MLA_ROPE), F32)]
    return pl.pallas_call(
        functools.partial(_mla_kernel, latent),
        grid=(nseq,),
        in_specs=in_specs, out_specs=out_specs, out_shape=out_shape,
        compiler_params=_cparams("arbitrary"),
        name="mla",
    )(*args)


def _diff_kernel(latent, lam_init, *refs):
    if latent:
        (c_ref, lam_ref, norm_ref, ck_ref, cv_ref, cos_ref, sin_ref, o_ref) = refs
    else:
        (c_ref, lam_ref, norm_ref, o_ref, k_ref, v_ref) = refs
    seq = c_ref.shape[0]
    dq = c_ref[:, 0:GROUP]
    dk = c_ref[:, GROUP:2 * GROUP]
    dv = c_ref[:, 2 * GROUP:3 * GROUP]
    if latent:
        cos = cos_ref[...]
        sin = sin_ref[...]
        dq = _rope(dq, cos, sin)
        k_all = jnp.concatenate([ck_ref[...], _rope(dk, cos, sin)], axis=0)
        v_all = jnp.concatenate([cv_ref[...], dv], axis=0)
    else:
        k_ref[...] = dk
        v_ref[...] = dv
        k_all = dk
        v_all = dv
    lv = lam_ref[...]
    lam = (jnp.exp(jnp.sum(lv[0:1] * lv[1:2], axis=-1, keepdims=True))
           - jnp.exp(jnp.sum(lv[2:3] * lv[3:4], axis=-1, keepdims=True)) + lam_init)
    k_bf = k_all.astype(BF16)
    v_bf = v_all.astype(BF16)
    scale = DIFF_DIM ** -0.5
    mean_op = jnp.where(_block_mask(GROUP, GROUP, HEAD_DIM, HEAD_DIM), 1.0 / HEAD_DIM, 0.0)
    qb = min(seq, 256)
    for r0 in range(0, seq, qb):
        qblk = dq[r0:r0 + qb]
        acc = jnp.zeros((qb, GROUP), F32)
        for h in range(N_HEADS):
            probs = []
            for comp in range(2):
                cm = _lane_group_mask(GROUP, h * HEAD_DIM + comp * DIFF_DIM, DIFF_DIM)
                s = _dot_nt(jnp.where(cm, qblk, 0.0).astype(BF16), k_bf) * scale
                e = jnp.exp(s - jnp.max(s, axis=-1, keepdims=True))
                probs.append(e / jnp.sum(e, axis=-1, keepdims=True))
            w = (probs[0] - lam * probs[1]).astype(BF16)
            acc = acc + jnp.where(_lane_group_mask(GROUP, h * HEAD_DIM, HEAD_DIM), _dot(w, v_bf), 0.0)
        ms = _dot(acc * acc, mean_op, precision=HIGHEST)
        o_ref[r0:r0 + qb, :] = acc * lax.rsqrt(ms + NORM_EPS) * norm_ref[...] * (1.0 - lam_init)


def _diff(proj_c, row0, nseq, seq, lam_p, norm_t, lam_init, latent_args):
    latent = latent_args is not None
    blk0 = row0 // seq
    in_specs = [pl.BlockSpec((seq, PROJ_C), lambda b: (blk0 + b, 0)),
                pl.BlockSpec(lam_p.shape, lambda b: (0, 0)),
                pl.BlockSpec(norm_t.shape, lambda b: (0, 0))]
    args = [proj_c, lam_p, norm_t]
    out_specs = [pl.BlockSpec((None, seq, GROUP), lambda b: (b, 0, 0))]
    out_shape = [jax.ShapeDtypeStruct((nseq, seq, GROUP), F32)]
    if latent:
        ck, cv, cos, sin = latent_args
        past = ck.shape[1]
        in_specs += [pl.BlockSpec((None, past, GROUP), lambda b: (b, 0, 0)),
                     pl.BlockSpec((None, past, GROUP), lambda b: (b, 0, 0)),
                     pl.BlockSpec(cos.shape, lambda b: (0, 0)),
                     pl.BlockSpec(sin.shape, lambda b: (0, 0))]
        args += [ck, cv, cos, sin]
    else:
        out_specs += [pl.BlockSpec((None, seq, GROUP), lambda b: (b, 0, 0))] * 2
        out_shape += [jax.ShapeDtypeStruct((nseq, seq, GROUP), F32)] * 2
    return pl.pallas_call(
        functools.partial(_diff_kernel, latent, lam_init),
        grid=(nseq,),
        in_specs=in_specs, out_specs=out_specs, out_shape=out_shape,
        compiler_params=_cparams("arbitrary"),
        name="diffattn",
    )(*args)


def _ssd_kernel(has_state, *refs):
    if has_state:
        (d_ref, cw_ref, cb_ref, dtb_ref, alog_ref, dskip_ref, norm_ref, s0_ref,
         o_ref, sfin_ref, xs_scr, bm_scr, cm_scr, xdt_scr, a_scr, st_scr, yf_scr, yb_scr) = refs
    else:
        (d_ref, cw_ref, cb_ref, dtb_ref, alog_ref, dskip_ref, norm_ref,
         o_ref, sfin_ref, xs_scr, bm_scr, cm_scr, xdt_scr, a_scr, st_scr, yf_scr, yb_scr) = refs
    seq = d_ref.shape[0]
    c = SSD_CHUNK
    nchunk = seq // c
    ngrp = 2 * SSD_STATE

    xin = d_ref[:, GROUP:GROUP + 512]
    rows = _iota(xin.shape, 0)
    prev = jnp.where(rows == 0, 0.0, pltpu.roll(xin, 1, 0))
    nxt = jnp.where(rows == seq - 1, 0.0, pltpu.roll(xin, seq - 1, 0))
    cw = cw_ref[...]
    xbc = _silu(cw[0:1] * prev + cw[1:2] * xin + cw[2:3] * nxt + cb_ref[...])
    xs = xbc[:, 0:GROUP]
    xs_scr[...] = xs
    bm_scr[...] = xbc[:, GROUP:GROUP + ngrp]
    cm_scr[...] = xbc[:, GROUP + ngrp:GROUP + 2 * ngrp]
    dt = _softplus(d_ref[:, GROUP + 512:GROUP + 640] + dtb_ref[...])
    a_scr[...] = dt * (-jnp.exp(alog_ref[...]))
    erow = _iota((128, GROUP), 0)
    ehead = _iota((128, GROUP), 1) // HEAD_DIM
    expand = tuple((erow == 4 * d + ehead).astype(F32) for d in range(2))
    for d in range(2):
        xdt_scr[d] = xs * _dot(dt, expand[d], precision=HIGHEST)
    if has_state:
        st_scr[...] = s0_ref[...]
    else:
        st_scr[...] = jnp.zeros_like(st_scr)

    row = _iota((c, c), 0)
    col = _iota((c, c), 1)
    tri = ((col <= row).astype(F32), (col >= row).astype(F32))
    keep = (col <= row, col >= row)
    grp_lane = _iota((1, ngrp), 1) // SSD_STATE
    valid = (_iota((ngrp, GROUP), 0) // SSD_STATE) == (_iota((ngrp, GROUP), 1) // (2 * HEAD_DIM))

    def chunk_step(d, r0, out_scr):
        a_c = a_scr[pl.ds(r0, c), :]
        bm_c = bm_scr[pl.ds(r0, c), :]
        cm_c = cm_scr[pl.ds(r0, c), :].astype(BF16)
        xdt_c = xdt_scr[d, pl.ds(r0, c), :]
        acum = _dot(tri[d], a_c, precision=HIGHEST)
        acum_t = acum.T
        acum_rep = _dot(acum, expand[d], precision=HIGHEST)
        bm2 = jnp.concatenate([jnp.where(grp_lane == g, bm_c, 0.0) for g in range(2)], axis=0)
        cb = _dot_nt(cm_c, bm2.astype(BF16))
        scores = []
        xparts = []
        for h in range(N_HEADS):
            lane = 4 * d + h
            seg = jnp.exp(jnp.where(keep[d], acum[:, lane:lane + 1] - acum_t[lane:lane + 1, :], -jnp.inf))
            g = h // 2
            scores.append((cb[:, g * c:(g + 1) * c] * seg).astype(BF16))
            xparts.append(jnp.where(_lane_group_mask(GROUP, h * HEAD_DIM, HEAD_DIM), xdt_c, 0.0))
        y = _dot(jnp.concatenate(scores, axis=1), jnp.concatenate(xparts, axis=0).astype(BF16))
        st = st_scr[d]
        y = y + _dot(cm_c, st.astype(BF16)) * jnp.exp(acum_rep)
        out_scr[pl.ds(r0, c), :] = y
        edge = acum_rep[c - 1:c] if d == 0 else acum_rep[0:1]
        xt = (xdt_c * jnp.exp(edge - acum_rep)).astype(BF16)
        upd = _dot_tn(bm_c.astype(BF16), xt)
        st_scr[d] = st * jnp.exp(edge) + jnp.where(valid, upd, 0.0)

    def body(n, carry):
        chunk_step(0, pl.multiple_of(n * c, c), yf_scr)
        chunk_step(1, pl.multiple_of((nchunk - 1 - n) * c, c), yb_scr)
        return carry

    lax.fori_loop(0, nchunk, body, 0)

    y = yf_scr[...] + yb_scr[...] + dskip_ref[...] * xs_scr[...]
    y = y * _silu(d_ref[:, 0:GROUP])
    o_ref[...] = y * lax.rsqrt(jnp.mean(y * y, axis=-1, keepdims=True) + NORM_EPS) * norm_ref[...]
    sfin_ref[...] = st_scr[...]


def _ssd(proj_d, row0, nseq, seq, conv_w, conv_b, dt_bias_p, a_log_p, d_rep, norm, s0):
    has_state = s0 is not None
    blk0 = row0 // seq
    ngrp = 2 * SSD_STATE
    small = [conv_w, conv_b, dt_bias_p, a_log_p, d_rep, norm]
    in_specs = ([pl.BlockSpec((seq, PROJ_D), lambda b: (blk0 + b, 0))]
                + [pl.BlockSpec(s.shape, lambda b: (0, 0)) for s in small])
    args = [proj_d] + small
    if has_state:
        in_specs.append(pl.BlockSpec((None, 2, ngrp, GROUP), lambda b: (b, 0, 0, 0)))
        args.append(s0)
    return pl.pallas_call(
        functools.partial(_ssd_kernel, has_state),
        grid=(nseq,),
        in_specs=in_specs,
        out_specs=[pl.BlockSpec((None, seq, GROUP), lambda b: (b, 0, 0)),
                   pl.BlockSpec((None, 2, ngrp, GROUP), lambda b: (b, 0, 0, 0))],
        out_shape=[jax.ShapeDtypeStruct((nseq, seq, GROUP), F32),
                   jax.ShapeDtypeStruct((nseq, 2, ngrp, GROUP), F32)],
        scratch_shapes=[pltpu.VMEM((seq, GROUP), F32),
                        pltpu.VMEM((seq, ngrp), F32),
                        pltpu.VMEM((seq, ngrp), F32),
                        pltpu.VMEM((2, seq, GROUP), F32),
                        pltpu.VMEM((seq, 128), F32),
                        pltpu.VMEM((2, ngrp, GROUP), F32),
                        pltpu.VMEM((seq, GROUP), F32),
                        pltpu.VMEM((seq, GROUP), F32)],
        compiler_params=_cparams("arbitrary"),
        name="ssd",
    )(*args)


def _outproj_kernel(n_ctx_tiles, *refs):
    ctx_refs, lat_refs = refs[0:4], refs[4:8]
    w_ref, x_ref, m_ref, g_ref, b_ref, o_ref = refs[8:]
    is_ctx = pl.program_id(0) < n_ctx_tiles
    mixed = None
    for i, (c_ref, l_ref) in enumerate(zip(ctx_refs, lat_refs)):
        part = jnp.where(is_ctx, c_ref[...], l_ref[...]).astype(BF16)
        term = _dot(part, w_ref[i * GROUP:(i + 1) * GROUP, :])
        mixed = term if mixed is None else mixed + term
    o_ref[...] = _layernorm(ALPHA * x_ref[...] + m_ref[2] * mixed, g_ref[...], b_ref[...])


def _outproj(parts_ctx, parts_lat, w_out_bf, x, mods_l, ln_g, ln_b, t_ctx, s_lat):
    t, d = x.shape
    tm = 256
    n_ctx = t_ctx // tm
    n_lat = (t - t_ctx) // tm
    ctx_spec = pl.BlockSpec((tm, GROUP), lambda i: (jnp.minimum(i, n_ctx - 1), 0))
    lat_spec = pl.BlockSpec((tm, GROUP), lambda i: (jnp.clip(i - n_ctx, 0, n_lat - 1), 0))
    return pl.pallas_call(
        functools.partial(_outproj_kernel, n_ctx),
        grid=(t // tm,),
        in_specs=[ctx_spec] * 4 + [lat_spec] * 4
        + [pl.BlockSpec(w_out_bf.shape, lambda i: (0, 0)),
           pl.BlockSpec((tm, d), lambda i: (i, 0)),
           pl.BlockSpec((None, 6, 1, d), _mod_row_map(tm, t_ctx, s_lat)),
           pl.BlockSpec((1, d), lambda i: (0, 0)),
           pl.BlockSpec((1, d), lambda i: (0, 0))],
        out_specs=pl.BlockSpec((tm, d), lambda i: (i, 0)),
        out_shape=jax.ShapeDtypeStruct((t, d), F32),
        compiler_params=_cparams("arbitrary"),
        name="outproj_ln",
    )(*parts_ctx, *parts_lat, w_out_bf, x, mods_l, ln_g, ln_b)


def _top_rows(s, k, extra=()):
    r = s.shape[0]
    rid = _iota(s.shape, 0).astype(F32)
    vals, ids = [], []
    picked = [[] for _ in extra]
    for _ in range(k):
        m = jnp.max(s, axis=0, keepdims=True)
        cand = jnp.where(s == m, rid, float(r))
        i = jnp.min(cand, axis=0, keepdims=True)
        hit = cand == i
        vals.append(m)
        ids.append(i)
        for lst, arr in zip(picked, extra):
            lst.append(jnp.max(jnp.where(hit, arr, -1.0), axis=0, keepdims=True))
        s = jnp.where(hit, -jnp.inf, s)
    cat = lambda xs: jnp.concatenate(xs, axis=0)
    return cat(vals), cat(ids), [cat(p) for p in picked]


def _router_kernel(x_ref, m_ref, wqt_ref, keys_ref, h_ref, a_ref, b_ref, g_ref):
    tm = x_ref.shape[0]
    hb = (x_ref[...] * (1.0 + m_ref[4]) + m_ref[3]).astype(BF16)
    h_ref[...] = hb
    qt = _dot_nt(wqt_ref[...], hb).astype(BF16)
    k = PEER_TOPK
    a_rows, b_rows, g_rows = [], [], []
    for head in range(PEER_HEADS):
        tv, ti = [], []
        for half in range(2):
            g = 2 * head + half
            sc = _dot(keys_ref[g], qt[g * PEER_HALF:(g + 1) * PEER_HALF])
            v, i, _ = _top_rows(sc, k)
            tv.append(v)
            ti.append(i)
        cs = [tv[0][0:1] + tv[1]]
        ca = [jnp.broadcast_to(ti[0][0:1], (k, tm))]
        cb = [ti[1]]
        for k1 in range(1, 4):
            cs.append(tv[0][k1:k1 + 1] + tv[1][0:8])
            ca.append(jnp.broadcast_to(ti[0][k1:k1 + 1], (8, tm)))
            cb.append(ti[1][0:8])
        low = _iota((8, tm), 0) < 4
        v2_dup = jnp.where(low, tv[1][0:8], pltpu.roll(tv[1][0:8], 4, 0))
        i2_dup = jnp.where(low, ti[1][0:8], pltpu.roll(ti[1][0:8], 4, 0))
        for k1 in (4, 6):
            cs.append(jnp.where(low, tv[0][k1:k1 + 1], tv[0][k1 + 1:k1 + 2]) + v2_dup)
            ca.append(jnp.where(low, ti[0][k1:k1 + 1], ti[0][k1 + 1:k1 + 2]))
            cb.append(i2_dup)
        cs.append(tv[0][8:16] + tv[1][0:1])
        ca.append(ti[0][8:16])
        cb.append(jnp.broadcast_to(ti[1][0:1], (8, tm)))
        best, _, (sel_a, sel_b) = _top_rows(jnp.concatenate(cs, axis=0), k,
                                            extra=(jnp.concatenate(ca, axis=0), jnp.concatenate(cb, axis=0)))
        e = jnp.exp(best - best[0:1])
        g_rows.append(e / jnp.sum(e, axis=0, keepdims=True))
        a_rows.append(sel_a)
        b_rows.append(sel_b)
    a_ref[...] = jnp.concatenate(a_rows, axis=0).T.astype(I32)
    b_ref[...] = jnp.concatenate(b_rows, axis=0).T.astype(I32)
    g_ref[...] = jnp.concatenate(g_rows, axis=0).T


def _router(x, mods_l, wq_t, keys, t_ctx, s_lat):
    t, d = x.shape
    tm = 128
    nslot = PEER_HEADS * PEER_TOPK
    return pl.pallas_call(
        _router_kernel,
        grid=(t // tm,),
        in_specs=[pl.BlockSpec((tm, d), lambda i: (i, 0)),
                  pl.BlockSpec((None, 6, 1, d), _mod_row_map(tm, t_ctx, s_lat)),
                  pl.BlockSpec(wq_t.shape, lambda i: (0, 0)),
                  pl.BlockSpec(keys.shape, lambda i: (0, 0, 0))],
        out_specs=[pl.BlockSpec((tm, d), lambda i: (i, 0)),
                   pl.BlockSpec((tm, nslot), lambda i: (i, 0)),
                   pl.BlockSpec((tm, nslot), lambda i: (i, 0)),
                   pl.BlockSpec((tm, nslot), lambda i: (i, 0))],
        out_shape=[jax.ShapeDtypeStruct((t, d), BF16),
                   jax.ShapeDtypeStruct((t, nslot), I32),
                   jax.ShapeDtypeStruct((t, nslot), I32),
                   jax.ShapeDtypeStruct((t, nslot), F32)],
        compiler_params=_cparams("arbitrary"),
        name="peer_router",
    )(x, mods_l, wq_t, keys)


def _gates_kernel(a_ref, b_ref, g_ref, o_ref, w_scr):
    tm = a_ref.shape[0]
    n = PEER_KEYS
    key = _iota((tm, n, a_ref.shape[2]), 1)
    onehot_a = jnp.where(key == a_ref[...], 1.0, 0.0).astype(BF16)
    gated_b = jnp.where(key == b_ref[...], g_ref[...], 0.0).astype(BF16)
    w = lax.dot_general(onehot_a, gated_b, (((2,), (2,)), ((0,), (0,))),
                        preferred_element_type=F32)
    w_t = jnp.swapaxes(w, 0, 1)
    for r in range(n):
        o_ref[:, r * n:(r + 1) * n] = w_t[r].astype(BF16)


def _gates(a_idx, b_idx, gate):
    t, nslot = a_idx.shape
    tm = 64
    spec = pl.BlockSpec((tm, 1, nslot), lambda i: (i, 0, 0))
    return pl.pallas_call(
        _gates_kernel,
        grid=(t // tm,),
        in_specs=[spec, spec, spec],
        out_specs=pl.BlockSpec((tm, N_EXPERTS), lambda i: (i, 0)),
        out_shape=jax.ShapeDtypeStruct((t, N_EXPERTS), BF16),
        scratch_shapes=[pltpu.VMEM((tm * PEER_KEYS, PEER_KEYS), F32)],
        compiler_params=_cparams("arbitrary"),
        name="peer_gates",
    )(a_idx.reshape(t, 1, nslot), b_idx.reshape(t, 1, nslot), gate.reshape(t, 1, nslot))


def _experts_kernel(h_ref, u_ref, v_ref, w_ref, x_ref, m_ref, g_ref, b_ref, o_ref, acc_ref):
    j = pl.program_id(1)

    @pl.when(j == 0)
    def _():
        acc_ref[...] = jnp.zeros_like(acc_ref)

    act = _gelu_tanh(_dot_nt(h_ref[...], u_ref[...].astype(BF16)))
    acc_ref[...] += _dot((act * w_ref[...].astype(F32)).astype(BF16), v_ref[...].astype(BF16))

    @pl.when(j == pl.num_programs(1) - 1)
    def _():
        o_ref[...] = _layernorm(ALPHA * x_ref[...] + m_ref[5] * acc_ref[...], g_ref[...], b_ref[...])


def _experts(h_bf, peer_u, peer_v, layer, w_gate, x, mods_l, ln_g, ln_b, t_ctx, s_lat):
    t, d = x.shape
    tm, te = 1024, 512
    return pl.pallas_call(
        _experts_kernel,
        grid=(t // tm, N_EXPERTS // te),
        in_specs=[pl.BlockSpec((tm, d), lambda i, j: (i, 0)),
                  pl.BlockSpec((None, te, d), lambda i, j: (layer, j, 0)),
                  pl.BlockSpec((None, te, d), lambda i, j: (layer, j, 0)),
                  pl.BlockSpec((tm, te), lambda i, j: (i, j)),
                  pl.BlockSpec((tm, d), lambda i, j: (i, 0)),
                  pl.BlockSpec((None, 6, 1, d), _mod_row_map(tm, t_ctx, s_lat)),
                  pl.BlockSpec((1, d), lambda i, j: (0, 0)),
                  pl.BlockSpec((1, d), lambda i, j: (0, 0))],
        out_specs=pl.BlockSpec((tm, d), lambda i, j: (i, 0)),
        out_shape=jax.ShapeDtypeStruct((t, d), F32),
        scratch_shapes=[pltpu.VMEM((tm, d), F32)],
        compiler_params=_cparams("arbitrary", "arbitrary"),
        name="peer_experts",
    )(h_bf, peer_u, peer_v, w_gate, x, mods_l, ln_g, ln_b)


def _pad_cols(w, n):
    return jnp.concatenate([w, jnp.zeros((w.shape[0], n), w.dtype)], axis=1) if n else w


def _layout_w_in(w):
    a = w[:, 0:1280]
    mcq, mckv, mkpe = w[:, 1280:1472], w[:, 1472:1600], w[:, 1600:1632]
    b = jnp.concatenate([mckv, _pad_cols(mcq, 64), mkpe, mkpe, mkpe, mkpe], axis=1)
    c = w[:, 1632:2400]
    d = _pad_cols(w[:, 2400:3176], PROJ_D - 776)
    return jnp.concatenate([a, b, c, d], axis=1).astype(BF16)


def _layout_w_qb(w):
    w4 = w.reshape(MLA_Q_RANK, N_HEADS, MLA_NOPE + MLA_ROPE)
    return jnp.concatenate([w4[:, :, :MLA_NOPE].reshape(MLA_Q_RANK, -1),
                            w4[:, :, MLA_NOPE:].reshape(MLA_Q_RANK, -1)], axis=1).astype(BF16)


def _layout_w_kvb(w):
    w4 = w.reshape(MLA_KV_RANK, N_HEADS, MLA_NOPE + HEAD_DIM)
    return jnp.concatenate([w4[:, :, :MLA_NOPE].reshape(MLA_KV_RANK, -1),
                            w4[:, :, MLA_NOPE:].reshape(MLA_KV_RANK, -1)], axis=1).astype(BF16)


def _rope_tables(seq):
    rows = seq // GRID_W
    row = jnp.repeat(jnp.arange(rows, dtype=F32), GRID_W)
    col = jnp.tile(jnp.arange(GRID_W, dtype=F32), rows)
    freqs = ROPE_BASE ** (-jnp.arange(ROPE_PAIRS, dtype=F32) / ROPE_PAIRS)
    cos_l, sin_l = [], []
    for pos in (row, col):
        ang = pos[:, None] * freqs
        cos_l += [jnp.cos(ang), jnp.cos(ang)]
        sin_l += [-jnp.sin(ang), jnp.sin(ang)]
    return jnp.concatenate(cos_l, axis=1), jnp.concatenate(sin_l, axis=1)


def _hgrn_state_pack(st):
    b = st.shape[0]
    st_t = jnp.swapaxes(st, -1, -2)
    zero = jnp.zeros_like(st_t[:, :, 0])
    rows = [jnp.concatenate([st_t[:, :, h] if g == h else zero for g in range(N_HEADS)], axis=-1)
            for h in range(N_HEADS)]
    return jnp.concatenate(rows, axis=-2).reshape(b, 2, GROUP, GROUP)


def _hgrn_state_unpack(sb):
    blocks = [sb[:, :, h * HEAD_DIM:(h + 1) * HEAD_DIM, h * HEAD_DIM:(h + 1) * HEAD_DIM]
              for h in range(N_HEADS)]
    return jnp.swapaxes(jnp.stack(blocks, axis=2), -1, -2)


def _ssd_state_pack(st):
    st_t = jnp.swapaxes(st, -1, -2)
    zero = jnp.zeros_like(st_t[:, :, 0])
    rows = [jnp.concatenate([st_t[:, :, h] if h // 2 == g else zero for h in range(N_HEADS)], axis=-1)
            for g in range(2)]
    return jnp.concatenate(rows, axis=-2)


def _ssd_state_unpack(sb):
    blocks = [sb[:, :, (h // 2) * SSD_STATE:(h // 2 + 1) * SSD_STATE, h * HEAD_DIM:(h + 1) * HEAD_DIM]
              for h in range(N_HEADS)]
    return jnp.swapaxes(jnp.stack(blocks, axis=2), -1, -2)


def _tile_lanes(v, n):
    return jnp.tile(v.reshape(1, -1), (1, n))


def kernel(x_prompt, x_sample, cache_mla_ckv, cache_mla_kpe, cache_diff_k, cache_diff_v, state_hgrn, state_ssd, c, c_ctx, w_mod, b_mod, w_in, hgrn_lb, hgrn_norm, mla_q_norm, mla_w_qb, mla_kv_norm, mla_w_kvb, diff_lambda, diff_norm, ssd_conv_w, ssd_conv_b, ssd_dt_bias, ssd_a_log, ssd_d, ssd_norm, w_out, ln1_g, ln1_b, peer_wq, peer_keys, peer_u, peer_v, ln2_g, ln2_b):
    nb, seq, d = x_prompt.shape
    nlat, lseq, _ = x_sample.shape
    depth = w_in.shape[0]
    t_ctx = nb * seq
    x = jnp.concatenate([x_prompt.reshape(t_ctx, d), x_sample.reshape(nlat * lseq, d)], axis=0)

    cond8 = jnp.concatenate([c_ctx.reshape(1, d), c, jnp.zeros((8 - 1 - nlat, d), F32)], axis=0)
    mods = _mods(cond8, w_mod, b_mod)
    mods = mods[:, :1 + nlat].reshape(depth, 1 + nlat, 6, 1, d)

    cos32, sin32 = _rope_tables(lseq)
    cos128, sin128 = jnp.tile(cos32, (1, 4)), jnp.tile(sin32, (1, 4))
    cos256, sin256 = jnp.tile(cos32, (1, 8)), jnp.tile(sin32, (1, 8))

    produced = []
    for l in range(depth):
        mods_l = mods[l]
        pa, pb, pc, pd = _inproj(x, mods_l, _layout_w_in(w_in[l]), t_ctx, lseq)

        norm_hg = _tile_lanes(hgrn_norm[l], N_HEADS)
        hg_ctx, hg_fin = _hgrn(pa, 0, nb, seq, hgrn_lb, norm_hg, None, l)
        hg_lat, _ = _hgrn(pa, t_ctx, nlat, lseq, hgrn_lb, norm_hg, _hgrn_state_pack(state_hgrn[:, l]), l)

        mla_w = (mla_q_norm[l].reshape(1, -1), _layout_w_qb(mla_w_qb[l]),
                 mla_kv_norm[l].reshape(1, -1), _layout_w_kvb(mla_w_kvb[l]))
        mla_ctx, new_ckv, new_kpe = _mla(pb, 0, nb, seq, *mla_w, None)
        (mla_lat,) = _mla(pb, t_ctx, nlat, lseq, *mla_w,
                          (cache_mla_ckv[:, l], jnp.tile(cache_mla_kpe[:, l], (1, 1, 4)), cos128, sin128))

        lam_init = 0.8 - 0.6 * math.exp(-0.3 * l)
        norm_df = _tile_lanes(diff_norm[l], N_HEADS)
        df_ctx, new_dk, new_dv = _diff(pc, 0, nb, seq, diff_lambda[l], norm_df, lam_init, None)
        past = cache_diff_k.shape[2]
        (df_lat,) = _diff(pc, t_ctx, nlat, lseq, diff_lambda[l], norm_df, lam_init,
                          (cache_diff_k[:, l].reshape(nlat, past, GROUP),
                           cache_diff_v[:, l].reshape(nlat, past, GROUP), cos256, sin256))

        ssd_w = (ssd_conv_w[l], ssd_conv_b[l].reshape(1, -1),
                 _pad_cols(ssd_dt_bias[l].reshape(1, -1), 120), _pad_cols(ssd_a_log[l].reshape(1, -1), 120),
                 jnp.repeat(ssd_d[l], HEAD_DIM).reshape(1, -1), ssd_norm[l].reshape(1, -1))
        ssd_ctx, ssd_fin = _ssd(pd, 0, nb, seq, *ssd_w, None)
        ssd_lat, _ = _ssd(pd, t_ctx, nlat, lseq, *ssd_w, _ssd_state_pack(state_ssd[:, l]))

        parts_ctx = [a.reshape(t_ctx, GROUP) for a in (hg_ctx, mla_ctx, df_ctx, ssd_ctx)]
        parts_lat = [a.reshape(nlat * lseq, GROUP) for a in (hg_lat, mla_lat, df_lat, ssd_lat)]
        x = _outproj(parts_ctx, parts_lat, w_out[l].astype(BF16), x, mods_l,
                     ln1_g[l].reshape(1, d), ln1_b[l].reshape(1, d), t_ctx, lseq)

        keys = peer_keys[l].reshape(2 * PEER_HEADS, PEER_KEYS, PEER_HALF).astype(BF16)
        h_bf, a_idx, b_idx, gate = _router(x, mods_l, peer_wq[l].T.astype(BF16), keys, t_ctx, lseq)
        w_gate = _gates(a_idx, b_idx, gate)
        x = _experts(h_bf, peer_u, peer_v, l, w_gate, x, mods_l,
                     ln2_g[l].reshape(1, d), ln2_b[l].reshape(1, d), t_ctx, lseq)

        produced.append((new_ckv, new_kpe,
                         new_dk.reshape(nb, seq, N_HEADS, 2, DIFF_DIM),
                         new_dv.reshape(nb, seq, N_HEADS, 2 * DIFF_DIM),
                         _hgrn_state_unpack(hg_fin), _ssd_state_unpack(ssd_fin)))

    y_prompt = x[:t_ctx].reshape(nb, seq, d)
    y_sample = x[t_ctx:].reshape(nlat, lseq, d)
    stacked = tuple(jnp.stack([p[i] for p in produced], axis=1) for i in range(6))
    return (y_prompt, y_sample) + stacked
```

```python
import functools
import math

import jax
import jax.numpy as jnp
from jax import lax
from jax.experimental import pallas as pl
from jax.experimental.pallas import tpu as pltpu

F32 = jnp.float32
BF16 = jnp.bfloat16
I32 = jnp.int32
HIGHEST = lax.Precision.HIGHEST

D_MODEL = 1024
GROUP = 256
N_HEADS = 4
HEAD_DIM = 64
HG_BLOCK = 16
HG_SLAB = 256
HG_UNROLL = 8
SSD_CHUNK = 128
SSD_STATE = 64
MLA_Q_RANK = 192
MLA_KV_RANK = 128
MLA_NOPE = 64
MLA_ROPE = 32
DIFF_DIM = 32
GRID_W = 64
ROPE_PAIRS = 8
ROPE_BASE = 10000.0
PEER_HEADS = 8
PEER_KEYS = 128
PEER_TOPK = 16
PEER_HALF = 64
N_EXPERTS = PEER_KEYS * PEER_KEYS
NORM_EPS = 1e-6
LN_EPS = 1e-5
DEPTH = 2
ALPHA = (2.0 * DEPTH) ** 0.25
LOG2_E = 1.4426950408889634

PROJ_A = 5 * GROUP
PROJ_B = 512
PROJ_C = 3 * GROUP
PROJ_D = 896
VMEM_LIMIT = 56 * 1024 * 1024


def _cparams(*sem):
    return pltpu.CompilerParams(dimension_semantics=sem, vmem_limit_bytes=VMEM_LIMIT)


def _sigmoid(x):
    return 1.0 / (1.0 + jnp.exp(-x))


def _silu(x):
    return x * _sigmoid(x)


def _softplus(x):
    return jnp.maximum(x, 0.0) + jnp.log(1.0 + jnp.exp(-jnp.abs(x)))


def _gelu_tanh(x):
    return 0.5 * x * (1.0 + jnp.tanh(math.sqrt(2.0 / math.pi) * (x + 0.044715 * (x * x * x))))


def _dot(a, b, precision=None):
    return jnp.dot(a, b, preferred_element_type=F32, precision=precision)


def _dot_nt(a, b, precision=None):
    return lax.dot_general(a, b, (((1,), (1,)), ((), ())), preferred_element_type=F32,
                           precision=precision)


def _dot_tn(a, b, precision=None):
    return lax.dot_general(a, b, (((0,), (0,)), ((), ())), preferred_element_type=F32,
                           precision=precision)


def _iota(shape, dim):
    return lax.broadcasted_iota(I32, shape, dim)


def _block_mask(rows, cols, rblk, cblk):
    return (_iota((rows, cols), 0) // rblk) == (_iota((rows, cols), 1) // cblk)


def _lane_group_mask(width, start, size):
    lane = _iota((1, width), 1)
    return (lane >= start) & (lane < start + size)


def _layernorm(v, g, b):
    mu = jnp.mean(v, axis=-1, keepdims=True)
    d = v - mu
    var = jnp.mean(d * d, axis=-1, keepdims=True)
    return d * lax.rsqrt(var + LN_EPS) * g + b


def _swap_halves16(x):
    width = x.shape[-1]
    lane = _iota(x.shape, x.ndim - 1)
    up = pltpu.roll(x, width - 8, x.ndim - 1)
    down = pltpu.roll(x, 8, x.ndim - 1)
    return jnp.where((lane % 16) < 8, up, down)


def _rope(x, cos, sin_signed):
    return x * cos + _swap_halves16(x) * sin_signed


def _mods_kernel(c_ref, w_ref, b_ref, o_ref):
    s = _silu(c_ref[...]).astype(BF16)
    o_ref[...] = _dot(s, w_ref[...].astype(BF16)) + b_ref[...]


def _mods(cond8, w_mod, b_mod):
    depth, d, n = w_mod.shape
    tn = 1536
    return pl.pallas_call(
        _mods_kernel,
        grid=(depth, n // tn),
        in_specs=[pl.BlockSpec((8, d), lambda l, j: (0, 0)),
                  pl.BlockSpec((None, d, tn), lambda l, j: (l, 0, j)),
                  pl.BlockSpec((None, 1, tn), lambda l, j: (l, 0, j))],
        out_specs=pl.BlockSpec((None, 8, tn), lambda l, j: (l, 0, j)),
        out_shape=jax.ShapeDtypeStruct((depth, 8, n), F32),
        compiler_params=_cparams("arbitrary", "arbitrary"),
        name="mods",
    )(cond8, w_mod, b_mod.reshape(depth, 1, n))


def _mod_row_map(tm, t_ctx, s_lat):
    def index_map(i, *_):
        start = i * tm
        return (jnp.where(start < t_ctx, 0, 1 + (start - t_ctx) // s_lat), 0, 0, 0)
    return index_map


def _inproj_kernel(x_ref, m_ref, w_ref, oa_ref, ob_ref, oc_ref, od_ref):
    h = (x_ref[...] * (1.0 + m_ref[1]) + m_ref[0]).astype(BF16)
    start = 0
    for o_ref in (oa_ref, ob_ref, oc_ref, od_ref):
        width = o_ref.shape[-1]
        o_ref[...] = _dot(h, w_ref[:, start:start + width])
        start += width


def _inproj(x, mods_l, w_in_p, t_ctx, s_lat):
    t, d = x.shape
    tm = 256
    widths = (PROJ_A, PROJ_B, PROJ_C, PROJ_D)
    return pl.pallas_call(
        _inproj_kernel,
        grid=(t // tm,),
        in_specs=[pl.BlockSpec((tm, d), lambda i: (i, 0)),
                  pl.BlockSpec((None, 6, 1, d), _mod_row_map(tm, t_ctx, s_lat)),
                  pl.BlockSpec(w_in_p.shape, lambda i: (0, 0))],
        out_specs=[pl.BlockSpec((tm, w), lambda i: (i, 0)) for w in widths],
        out_shape=[jax.ShapeDtypeStruct((t, w), F32) for w in widths],
        compiler_params=_cparams("arbitrary"),
        name="inproj",
    )(x, mods_l, w_in_p)


def _hgrn_kernel(layer, has_state, *refs):
    if has_state:
        (a_ref, lb_ref, norm_ref, s0_ref, o_ref, sfin_ref,
         q_scr, k_scr, bc_scr, dec_scr, qt_scr, kt_scr, st_scr, o_scr) = refs
    else:
        (a_ref, lb_ref, norm_ref, o_ref, sfin_ref,
         q_scr, k_scr, bc_scr, dec_scr, qt_scr, kt_scr, st_scr, o_scr) = refs
        s0_ref = None
    seq = a_ref.shape[0]
    c = HG_BLOCK
    nblk = seq // c
    slab = HG_SLAB
    nb = slab // c

    lbp = lb_ref[...]
    e = jnp.exp(lbp - jnp.max(lbp, axis=0, keepdims=True))
    p = e / jnp.sum(e, axis=0, keepdims=True)
    lower = jnp.sum(p[1:layer + 1], axis=0) if layer > 0 else jnp.zeros_like(p[0])

    q = _silu(a_ref[:, 0:GROUP])
    q_scr[...] = q
    srow = _iota((slab, slab), 0)
    scol = _iota((slab, slab), 1)
    same = (srow // c) == (scol // c)
    cum_op = (jnp.where(same & (scol <= srow), 1.0, 0.0), jnp.where(same & (scol >= srow), 1.0, 0.0))
    for d in range(2):
        lb = lower[d:d + 1]
        f = lb + (1.0 - lb) * _sigmoid(a_ref[:, (1 + d) * GROUP:(2 + d) * GROUP])
        k = 1.0 - f
        lf = jnp.log(f)
        k_scr[d] = k
        for s0 in range(0, seq, slab):
            bc = _dot(cum_op[d], lf[s0:s0 + slab], precision=HIGHEST)
            bc3 = bc.reshape(nb, c, GROUP)
            edge = bc3[:, c - 1:c, :] if d == 0 else bc3[:, 0:1, :]
            tot = jnp.broadcast_to(edge, (nb, c, GROUP)).reshape(slab, GROUP)
            bc_scr[d, s0:s0 + slab, :] = bc * LOG2_E
            dec_scr[d, s0:s0 + slab, :] = jnp.exp(tot)
            qt_scr[d, s0:s0 + slab, :] = (q[s0:s0 + slab] * jnp.exp(bc)).astype(BF16)
            kt_scr[d, s0:s0 + slab, :] = (k[s0:s0 + slab] * jnp.exp(tot - bc)).astype(BF16)
    if has_state:
        st_scr[...] = s0_ref[...]
    else:
        st_scr[...] = jnp.zeros_like(st_scr)

    bd_ones = _block_mask(GROUP, GROUP, HEAD_DIM, HEAD_DIM).astype(BF16)
    rib = _iota((1, c, GROUP), 1)

    def slab_step(i, carry):
        r0 = pl.multiple_of(i * slab, slab)
        q3 = q_scr[pl.ds(r0, slab), :].reshape(nb, c, GROUP)
        v3 = a_ref[pl.ds(r0, slab), 3 * GROUP:4 * GROUP].reshape(nb, c, GROUP)
        o3 = jnp.zeros((nb, c, GROUP), F32)
        for d in range(2):
            bc3 = bc_scr[d, pl.ds(r0, slab), :].reshape(nb, c, GROUP)
            k3 = k_scr[d, pl.ds(r0, slab), :].reshape(nb, c, GROUP)
            for j in range(c):
                keep = (rib >= j) if d == 0 else (rib <= j)
                dec = jnp.exp2(jnp.where(keep, bc3 - bc3[:, j:j + 1, :], -jnp.inf))
                pj = (dec * q3 * k3[:, j:j + 1, :]).astype(BF16).reshape(slab, GROUP)
                srep = _dot(pj, bd_ones).reshape(nb, c, GROUP)
                o3 = o3 + srep * v3[:, j:j + 1, :]
        o_scr[0, pl.ds(r0, slab), :] = o3.reshape(slab, GROUP)
        return carry

    lax.fori_loop(0, seq // slab, slab_step, 0)

    bd_mask = _block_mask(GROUP, GROUP, HEAD_DIM, HEAD_DIM)

    def body(n, carry):
        rows = [[pl.multiple_of(((n * HG_UNROLL + u) if d == 0 else nblk - 1 - (n * HG_UNROLL + u)) * c, c)
                 for u in range(HG_UNROLL)] for d in range(2)]
        upd = [[_dot_tn(a_ref[pl.ds(r0, c), 3 * GROUP:4 * GROUP].astype(BF16), kt_scr[d, pl.ds(r0, c), :])
                for r0 in rows[d]] for d in range(2)]
        for d in range(2):
            st = st_scr[d]
            for u, r0 in enumerate(rows[d]):
                o_scr[1 + d, pl.ds(r0, c), :] = _dot_nt(qt_scr[d, pl.ds(r0, c), :], st.astype(BF16))
                st = st * dec_scr[d, pl.ds(r0, 1), :] + jnp.where(bd_mask, upd[d][u], 0.0)
            st_scr[d] = st
        return carry

    lax.fori_loop(0, nblk // HG_UNROLL, body, 0)

    o = o_scr[0] + o_scr[1] + o_scr[2]
    mean_op = jnp.where(_block_mask(GROUP, GROUP, HEAD_DIM, HEAD_DIM), 1.0 / HEAD_DIM, 0.0)
    ms = _dot(o * o, mean_op, precision=HIGHEST)
    y = o * lax.rsqrt(ms + NORM_EPS) * norm_ref[...]
    o_ref[...] = y * _silu(a_ref[:, 4 * GROUP:5 * GROUP])
    sfin_ref[...] = st_scr[...]


def _hgrn(proj_a, row0, nseq, seq, hgrn_lb, norm_t, s0, layer):
    has_state = s0 is not None
    blk0 = row0 // seq
    in_specs = [pl.BlockSpec((seq, PROJ_A), lambda b: (blk0 + b, 0)),
                pl.BlockSpec(hgrn_lb.shape, lambda b: (0, 0, 0)),
                pl.BlockSpec((1, GROUP), lambda b: (0, 0))]
    args = [proj_a, hgrn_lb, norm_t]
    if has_state:
        in_specs.append(pl.BlockSpec((None, 2, GROUP, GROUP), lambda b: (b, 0, 0, 0)))
        args.append(s0)
    return pl.pallas_call(
        functools.partial(_hgrn_kernel, layer, has_state),
        grid=(nseq,),
        in_specs=in_specs,
        out_specs=[pl.BlockSpec((None, seq, GROUP), lambda b: (b, 0, 0)),
                   pl.BlockSpec((None, 2, GROUP, GROUP), lambda b: (b, 0, 0, 0))],
        out_shape=[jax.ShapeDtypeStruct((nseq, seq, GROUP), F32),
                   jax.ShapeDtypeStruct((nseq, 2, GROUP, GROUP), F32)],
        scratch_shapes=[pltpu.VMEM((seq, GROUP), F32),
                        pltpu.VMEM((2, seq, GROUP), F32),
                        pltpu.VMEM((2, seq, GROUP), F32),
                        pltpu.VMEM((2, seq, GROUP), F32),
                        pltpu.VMEM((2, seq, GROUP), BF16),
                        pltpu.VMEM((2, seq, GROUP), BF16),
                        pltpu.VMEM((2, GROUP, GROUP), F32),
                        pltpu.VMEM((3, seq, GROUP), F32)],
        compiler_params=_cparams("arbitrary"),
        name="hgrn",
    )(*args)


def _mla_kernel(latent, *refs):
    if latent:
        (b_ref, qn_ref, wq_ref, kvn_ref, wkv_ref, cckv_ref, ckpe_ref, cos_ref, sin_ref,
         o_ref) = refs
    else:
        (b_ref, qn_ref, wq_ref, kvn_ref, wkv_ref, o_ref, ckv_ref, kpe_ref) = refs
    seq = b_ref.shape[0]
    mckv = b_ref[:, 0:MLA_KV_RANK]
    mcq = b_ref[:, MLA_KV_RANK:MLA_KV_RANK + MLA_Q_RANK]
    kpe_t = b_ref[:, 384:512]

    cq = mcq * lax.rsqrt(jnp.mean(mcq * mcq, axis=-1, keepdims=True) + NORM_EPS) * qn_ref[...]
    qf = _dot(cq.astype(BF16), wq_ref[...])
    ckv = mckv * lax.rsqrt(jnp.mean(mckv * mckv, axis=-1, keepdims=True) + NORM_EPS) * kvn_ref[...]
    q_nope = qf[:, 0:N_HEADS * MLA_NOPE]
    q_rope = qf[:, N_HEADS * MLA_NOPE:]
    if latent:
        cos = cos_ref[...]
        sin = sin_ref[...]
        q_rope = _rope(q_rope, cos, sin)
        ckv_all = jnp.concatenate([cckv_ref[...], ckv], axis=0)
        kpe_all = jnp.concatenate([ckpe_ref[...], _rope(kpe_t, cos, sin)], axis=0)
    else:
        ckv_ref[...] = ckv
        kpe_ref[...] = kpe_t[:, 0:MLA_ROPE]
        ckv_all = ckv
        kpe_all = kpe_t
    kv = _dot(ckv_all.astype(BF16), wkv_ref[...])
    kcat = jnp.concatenate([kv[:, 0:GROUP], kpe_all], axis=1).astype(BF16)
    v = kv[:, GROUP:].astype(BF16)
    qcat = jnp.concatenate([q_nope, q_rope], axis=1)
    scale = (MLA_NOPE + MLA_ROPE) ** -0.5
    qb = min(seq, 256)
    width = qcat.shape[1]
    for r0 in range(0, seq, qb):
        qblk = qcat[r0:r0 + qb]
        acc = jnp.zeros((qb, GROUP), F32)
        for h in range(N_HEADS):
            hm = (_lane_group_mask(width, h * MLA_NOPE, MLA_NOPE)
                  | _lane_group_mask(width, N_HEADS * MLA_NOPE + h * MLA_ROPE, MLA_ROPE))
            s = _dot_nt(jnp.where(hm, qblk, 0.0).astype(BF16), kcat) * scale
            e = jnp.exp(s - jnp.max(s, axis=-1, keepdims=True))
            z = jnp.sum(e, axis=-1, keepdims=True)
            oh = _dot(e.astype(BF16), v) / z
            acc = acc + jnp.where(_lane_group_mask(GROUP, h * HEAD_DIM, HEAD_DIM), oh, 0.0)
        o_ref[r0:r0 + qb, :] = acc


def _mla(proj_b, row0, nseq, seq, q_norm, w_qb_p, kv_norm, w_kvb_p, latent_args):
    latent = latent_args is not None
    blk0 = row0 // seq
    in_specs = [pl.BlockSpec((seq, PROJ_B), lambda b: (blk0 + b, 0)),
                pl.BlockSpec(q_norm.shape, lambda b: (0, 0)),
                pl.BlockSpec(w_qb_p.shape, lambda b: (0, 0)),
                pl.BlockSpec(kv_norm.shape, lambda b: (0, 0)),
                pl.BlockSpec(w_kvb_p.shape, lambda b: (0, 0))]
    args = [proj_b, q_norm, w_qb_p, kv_norm, w_kvb_p]
    out_specs = [pl.BlockSpec((None, seq, GROUP), lambda b: (b, 0, 0))]
    out_shape = [jax.ShapeDtypeStruct((nseq, seq, GROUP), F32)]
    if latent:
        cckv, ckpe_t, cos, sin = latent_args
        past = cckv.shape[1]
        in_specs += [pl.BlockSpec((None, past, MLA_KV_RANK), lambda b: (b, 0, 0)),
                     pl.BlockSpec((None, past, 128), lambda b: (b, 0, 0)),
                     pl.BlockSpec(cos.shape, lambda b: (0, 0)),
                     pl.BlockSpec(sin.shape, lambda b: (0, 0))]
        args += [cckv, ckpe_t, cos, sin]
    else:
        out_specs += [pl.BlockSpec((None, seq, MLA_KV_RANK), lambda b: (b, 0, 0)),
                      pl.BlockSpec((None, seq, MLA_ROPE), lambda b: (b, 0, 0))]
        out_shape += [jax.ShapeDtypeStruct((nseq, seq, MLA_KV_RANK), F32),
                      jax.ShapeDtypeStruct((nseq, seq, MLA_ROPE), F32)]
    return pl.pallas_call(
        functools.partial(_mla_kernel, latent),
        grid=(nseq,),
        in_specs=in_specs, out_specs=out_specs, out_shape=out_shape,
        compiler_params=_cparams("arbitrary"),
        name="mla",
    )(*args)


def _diff_kernel(latent, lam_init, *refs):
    if latent:
        (c_ref, lam_ref, norm_ref, ck_ref, cv_ref, cos_ref, sin_ref, o_ref) = refs
    else:
        (c_ref, lam_ref, norm_ref, o_ref, k_ref, v_ref) = refs
    seq = c_ref.shape[0]
    dq = c_ref[:, 0:GROUP]
    dk = c_ref[:, GROUP:2 * GROUP]
    dv = c_ref[:, 2 * GROUP:3 * GROUP]
    if latent:
        cos = cos_ref[...]
        sin = sin_ref[...]
        dq = _rope(dq, cos, sin)
        k_all = jnp.concatenate([ck_ref[...], _rope(dk, cos, sin)], axis=0)
        v_all = jnp.concatenate([cv_ref[...], dv], axis=0)
    else:
        k_ref[...] = dk
        v_ref[...] = dv
        k_all = dk
        v_all = dv
    lv = lam_ref[...]
    lam = (jnp.exp(jnp.sum(lv[0:1] * lv[1:2], axis=-1, keepdims=True))
           - jnp.exp(jnp.sum(lv[2:3] * lv[3:4], axis=-1, keepdims=True)) + lam_init)
    k_bf = k_all.astype(BF16)
    v_bf = v_all.astype(BF16)
    scale = DIFF_DIM ** -0.5
    mean_op = jnp.where(_block_mask(GROUP, GROUP, HEAD_DIM, HEAD_DIM), 1.0 / HEAD_DIM, 0.0)
    qb = min(seq, 256)
    for r0 in range(0, seq, qb):
        qblk = dq[r0:r0 + qb]
        acc = jnp.zeros((qb, GROUP), F32)
        for h in range(N_HEADS):
            probs = []
            for comp in range(2):
                cm = _lane_group_mask(GROUP, h * HEAD_DIM + comp * DIFF_DIM, DIFF_DIM)
                s = _dot_nt(jnp.where(cm, qblk, 0.0).astype(BF16), k_bf) * scale
                e = jnp.exp(s - jnp.max(s, axis=-1, keepdims=True))
                probs.append(e / jnp.sum(e, axis=-1, keepdims=True))
            w = (probs[0] - lam * probs[1]).astype(BF16)
            acc = acc + jnp.where(_lane_group_mask(GROUP, h * HEAD_DIM, HEAD_DIM), _dot(w, v_bf), 0.0)
        ms = _dot(acc * acc, mean_op, precision=HIGHEST)
        o_ref[r0:r0 + qb, :] = acc * lax.rsqrt(ms + NORM_EPS) * norm_ref[...] * (1.0 - lam_init)


def _diff(proj_c, row0, nseq, seq, lam_p, norm_t, lam_init, latent_args):
    latent = latent_args is not None
    blk0 = row0 // seq
    in_specs = [pl.BlockSpec((seq, PROJ_C), lambda b: (blk0 + b, 0)),
                pl.BlockSpec(lam_p.shape, lambda b: (0, 0)),
                pl.BlockSpec(norm_t.shape, lambda b: (0, 0))]
    args = [proj_c, lam_p, norm_t]
    out_specs = [pl.BlockSpec((None, seq, GROUP), lambda b: (b, 0, 0))]
    out_shape = [jax.ShapeDtypeStruct((nseq, seq, GROUP), F32)]
    if latent:
        ck, cv, cos, sin = latent_args
        past = ck.shape[1]
        in_specs += [pl.BlockSpec((None, past, GROUP), lambda b: (b, 0, 0)),
                     pl.BlockSpec((None, past, GROUP), lambda b: (b, 0, 0)),
                     pl.BlockSpec(cos.shape, lambda b: (0, 0)),
                     pl.BlockSpec(sin.shape, lambda b: (0, 0))]
        args += [ck, cv, cos, sin]
    else:
        out_specs += [pl.BlockSpec((None, seq, GROUP), lambda b: (b, 0, 0))] * 2
        out_shape += [jax.ShapeDtypeStruct((nseq, seq, GROUP), F32)] * 2
    return pl.pallas_call(
        functools.partial(_diff_kernel, latent, lam_init),
        grid=(nseq,),
        in_specs=in_specs, out_specs=out_specs, out_shape=out_shape,
        compiler_params=_cparams("arbitrary"),
        name="diffattn",
    )(*args)


def _ssd_kernel(has_state, *refs):
    if has_state:
        (d_ref, cw_ref, cb_ref, dtb_ref, alog_ref, dskip_ref, norm_ref, s0_ref,
         o_ref, sfin_ref, xs_scr, bm_scr, cm_scr, xdt_scr, a_scr, st_scr, yf_scr, yb_scr) = refs
    else:
        (d_ref, cw_ref, cb_ref, dtb_ref, alog_ref, dskip_ref, norm_ref,
         o_ref, sfin_ref, xs_scr, bm_scr, cm_scr, xdt_scr, a_scr, st_scr, yf_scr, yb_scr) = refs
    seq = d_ref.shape[0]
    c = SSD_CHUNK
    nchunk = seq // c
    ngrp = 2 * SSD_STATE

    xin = d_ref[:, GROUP:GROUP + 512]
    rows = _iota(xin.shape, 0)
    prev = jnp.where(rows == 0, 0.0, pltpu.roll(xin, 1, 0))
    nxt = jnp.where(rows == seq - 1, 0.0, pltpu.roll(xin, seq - 1, 0))
    cw = cw_ref[...]
    xbc = _silu(cw[0:1] * prev + cw[1:2] * xin + cw[2:3] * nxt + cb_ref[...])
    xs = xbc[:, 0:GROUP]
    xs_scr[...] = xs
    bm_scr[...] = xbc[:, GROUP:GROUP + ngrp]
    cm_scr[...] = xbc[:, GROUP + ngrp:GROUP + 2 * ngrp]
    dt = _softplus(d_ref[:, GROUP + 512:GROUP + 640] + dtb_ref[...])
    a_scr[...] = dt * (-jnp.exp(alog_ref[...]))
    erow = _iota((128, GROUP), 0)
    ehead = _iota((128, GROUP), 1) // HEAD_DIM
    expand = tuple((erow == 4 * d + ehead).astype(F32) for d in range(2))
    for d in range(2):
        xdt_scr[d] = xs * _dot(dt, expand[d], precision=HIGHEST)
    if has_state:
        st_scr[...] = s0_ref[...]
    else:
        st_scr[...] = jnp.zeros_like(st_scr)

    row = _iota((c, c), 0)
    col = _iota((c, c), 1)
    tri = ((col <= row).astype(F32), (col >= row).astype(F32))
    keep = (col <= row, col >= row)
    grp_lane = _iota((1, ngrp), 1) // SSD_STATE
    valid = (_iota((ngrp, GROUP), 0) // SSD_STATE) == (_iota((ngrp, GROUP), 1) // (2 * HEAD_DIM))

    def chunk_step(d, r0, out_scr):
        a_c = a_scr[pl.ds(r0, c), :]
        bm_c = bm_scr[pl.ds(r0, c), :]
        cm_c = cm_scr[pl.ds(r0, c), :].astype(BF16)
        xdt_c = xdt_scr[d, pl.ds(r0, c), :]
        acum = _dot(tri[d], a_c, precision=HIGHEST)
        acum_t = acum.T
        acum_rep = _dot(acum, expand[d], precision=HIGHEST)
        bm2 = jnp.concatenate([jnp.where(grp_lane == g, bm_c, 0.0) for g in range(2)], axis=0)
        cb = _dot_nt(cm_c, bm2.astype(BF16))
        scores = []
        xparts = []
        for h in range(N_HEADS):
            lane = 4 * d + h
            seg = jnp.exp(jnp.where(keep[d], acum[:, lane:lane + 1] - acum_t[lane:lane + 1, :], -jnp.inf))
            g = h // 2
            scores.append((cb[:, g * c:(g + 1) * c] * seg).astype(BF16))
            xparts.append(jnp.where(_lane_group_mask(GROUP, h * HEAD_DIM, HEAD_DIM), xdt_c, 0.0))
        y = _dot(jnp.concatenate(scores, axis=1), jnp.concatenate(xparts, axis=0).astype(BF16))
        st = st_scr[d]
        y = y + _dot(cm_c, st.astype(BF16)) * jnp.exp(acum_rep)
        out_scr[pl.ds(r0, c), :] = y
        edge = acum_rep[c - 1:c] if d == 0 else acum_rep[0:1]
        xt = (xdt_c * jnp.exp(edge - acum_rep)).astype(BF16)
        upd = _dot_tn(bm_c.astype(BF16), xt)
        st_scr[d] = st * jnp.exp(edge) + jnp.where(valid, upd, 0.0)

    def body(n, carry):
        chunk_step(0, pl.multiple_of(n * c, c), yf_scr)
        chunk_step(1, pl.multiple_of((nchunk - 1 - n) * c, c), yb_scr)
        return carry

    lax.fori_loop(0, nchunk, body, 0)

    y = yf_scr[...] + yb_scr[...] + dskip_ref[...] * xs_scr[...]
    y = y * _silu(d_ref[:, 0:GROUP])
    o_ref[...] = y * lax.rsqrt(jnp.mean(y * y, axis=-1, keepdims=True) + NORM_EPS) * norm_ref[...]
    sfin_ref[...] = st_scr[...]


def _ssd(proj_d, row0, nseq, seq, conv_w, conv_b, dt_bias_p, a_log_p, d_rep, norm, s0):
    has_state = s0 is not None
    blk0 = row0 // seq
    ngrp = 2 * SSD_STATE
    small = [conv_w, conv_b, dt_bias_p, a_log_p, d_rep, norm]
    in_specs = ([pl.BlockSpec((seq, PROJ_D), lambda b: (blk0 + b, 0))]
                + [pl.BlockSpec(s.shape, lambda b: (0, 0)) for s in small])
    args = [proj_d] + small
    if has_state:
        in_specs.append(pl.BlockSpec((None, 2, ngrp, GROUP), lambda b: (b, 0, 0, 0)))
        args.append(s0)
    return pl.pallas_call(
        functools.partial(_ssd_kernel, has_state),
        grid=(nseq,),
        in_specs=in_specs,
        out_specs=[pl.BlockSpec((None, seq, GROUP), lambda b: (b, 0, 0)),
                   pl.BlockSpec((None, 2, ngrp, GROUP), lambda b: (b, 0, 0, 0))],
        out_shape=[jax.ShapeDtypeStruct((nseq, seq, GROUP), F32),
                   jax.ShapeDtypeStruct((nseq, 2, ngrp, GROUP), F32)],
        scratch_shapes=[pltpu.VMEM((seq, GROUP), F32),
                        pltpu.VMEM((seq, ngrp), F32),
                        pltpu.VMEM((seq, ngrp), F32),
                        pltpu.VMEM((2, seq, GROUP), F32),
                        pltpu.VMEM((seq, 128), F32),
                        pltpu.VMEM((2, ngrp, GROUP), F32),
                        pltpu.VMEM((seq, GROUP), F32),
                        pltpu.VMEM((seq, GROUP), F32)],
        compiler_params=_cparams("arbitrary"),
        name="ssd",
    )(*args)


def _outproj_kernel(n_ctx_tiles, *refs):
    ctx_refs, lat_refs = refs[0:4], refs[4:8]
    w_ref, x_ref, m_ref, g_ref, b_ref, o_ref = refs[8:]
    is_ctx = pl.program_id(0) < n_ctx_tiles
    mixed = None
    for i, (c_ref, l_ref) in enumerate(zip(ctx_refs, lat_refs)):
        part = jnp.where(is_ctx, c_ref[...], l_ref[...]).astype(BF16)
        term = _dot(part, w_ref[i * GROUP:(i + 1) * GROUP, :])
        mixed = term if mixed is None else mixed + term
    o_ref[...] = _layernorm(ALPHA * x_ref[...] + m_ref[2] * mixed, g_ref[...], b_ref[...])


def _outproj(parts_ctx, parts_lat, w_out_bf, x, mods_l, ln_g, ln_b, t_ctx, s_lat):
    t, d = x.shape
    tm = 256
    n_ctx = t_ctx // tm
    n_lat = (t - t_ctx) // tm
    ctx_spec = pl.BlockSpec((tm, GROUP), lambda i: (jnp.minimum(i, n_ctx - 1), 0))
    lat_spec = pl.BlockSpec((tm, GROUP), lambda i: (jnp.clip(i - n_ctx, 0, n_lat - 1), 0))
    return pl.pallas_call(
        functools.partial(_outproj_kernel, n_ctx),
        grid=(t // tm,),
        in_specs=[ctx_spec] * 4 + [lat_spec] * 4
        + [pl.BlockSpec(w_out_bf.shape, lambda i: (0, 0)),
           pl.BlockSpec((tm, d), lambda i: (i, 0)),
           pl.BlockSpec((None, 6, 1, d), _mod_row_map(tm, t_ctx, s_lat)),
           pl.BlockSpec((1, d), lambda i: (0, 0)),
           pl.BlockSpec((1, d), lambda i: (0, 0))],
        out_specs=pl.BlockSpec((tm, d), lambda i: (i, 0)),
        out_shape=jax.ShapeDtypeStruct((t, d), F32),
        compiler_params=_cparams("arbitrary"),
        name="outproj_ln",
    )(*parts_ctx, *parts_lat, w_out_bf, x, mods_l, ln_g, ln_b)


def _top_rows(s, k, extra=()):
    r = s.shape[0]
    rid = _iota(s.shape, 0).astype(F32)
    vals, ids = [], []
    picked = [[] for _ in extra]
    for _ in range(k):
        m = jnp.max(s, axis=0, keepdims=True)
        cand = jnp.where(s == m, rid, float(r))
        i = jnp.min(cand, axis=0, keepdims=True)
        hit = cand == i
        vals.append(m)
        ids.append(i)
        for lst, arr in zip(picked, extra):
            lst.append(jnp.max(jnp.where(hit, arr, -1.0), axis=0, keepdims=True))
        s = jnp.where(hit, -jnp.inf, s)
    cat = lambda xs: jnp.concatenate(xs, axis=0)
    return cat(vals), cat(ids), [cat(p) for p in picked]


def _router_kernel(x_ref, m_ref, wqt_ref, keys_ref, h_ref, a_ref, b_ref, g_ref):
    tm = x_ref.shape[0]
    hb = (x_ref[...] * (1.0 + m_ref[4]) + m_ref[3]).astype(BF16)
    h_ref[...] = hb
    qt = _dot_nt(wqt_ref[...], hb).astype(BF16)
    k = PEER_TOPK
    a_rows, b_rows, g_rows = [], [], []
    for head in range(PEER_HEADS):
        tv, ti = [], []
        for half in range(2):
            g = 2 * head + half
            sc = _dot(keys_ref[g], qt[g * PEER_HALF:(g + 1) * PEER_HALF])
            v, i, _ = _top_rows(sc, k)
            tv.append(v)
            ti.append(i)
        cs = [tv[0][0:1] + tv[1]]
        ca = [jnp.broadcast_to(ti[0][0:1], (k, tm))]
        cb = [ti[1]]
        for k1 in range(1, 4):
            cs.append(tv[0][k1:k1 + 1] + tv[1][0:8])
            ca.append(jnp.broadcast_to(ti[0][k1:k1 + 1], (8, tm)))
            cb.append(ti[1][0:8])
        low = _iota((8, tm), 0) < 4
        v2_dup = jnp.where(low, tv[1][0:8], pltpu.roll(tv[1][0:8], 4, 0))
        i2_dup = jnp.where(low, ti[1][0:8], pltpu.roll(ti[1][0:8], 4, 0))
        for k1 in (4, 6):
            cs.append(jnp.where(low, tv[0][k1:k1 + 1], tv[0][k1 + 1:k1 + 2]) + v2_dup)
            ca.append(jnp.where(low, ti[0][k1:k1 + 1], ti[0][k1 + 1:k1 + 2]))
            cb.append(i2_dup)
        cs.append(tv[0][8:16] + tv[1][0:1])
        ca.append(ti[0][8:16])
        cb.append(jnp.broadcast_to(ti[1][0:1], (8, tm)))
        best, _, (sel_a, sel_b) = _top_rows(jnp.concatenate(cs, axis=0), k,
                                            extra=(jnp.concatenate(ca, axis=0), jnp.concatenate(cb, axis=0)))
        e = jnp.exp(best - best[0:1])
        g_rows.append(e / jnp.sum(e, axis=0, keepdims=True))
        a_rows.append(sel_a)
        b_rows.append(sel_b)
    a_ref[...] = jnp.concatenate(a_rows, axis=0).T.astype(I32)
    b_ref[...] = jnp.concatenate(b_rows, axis=0).T.astype(I32)
    g_ref[...] = jnp.concatenate(g_rows, axis=0).T


def _router(x, mods_l, wq_t, keys, t_ctx, s_lat):
    t, d = x.shape
    tm = 128
    nslot = PEER_HEADS * PEER_TOPK
    return pl.pallas_call(
        _router_kernel,
        grid=(t // tm,),
        in_specs=[pl.BlockSpec((tm, d), lambda i: (i, 0)),
                  pl.BlockSpec((None, 6, 1, d), _mod_row_map(tm, t_ctx, s_lat)),
                  pl.BlockSpec(wq_t.shape, lambda i: (0, 0)),
                  pl.BlockSpec(keys.shape, lambda i: (0, 0, 0))],
        out_specs=[pl.BlockSpec((tm, d), lambda i: (i, 0)),
                   pl.BlockSpec((tm, nslot), lambda i: (i, 0)),
                   pl.BlockSpec((tm, nslot), lambda i: (i, 0)),
                   pl.BlockSpec((tm, nslot), lambda i: (i, 0))],
        out_shape=[jax.ShapeDtypeStruct((t, d), BF16),
                   jax.ShapeDtypeStruct((t, nslot), I32),
                   jax.ShapeDtypeStruct((t, nslot), I32),
                   jax.ShapeDtypeStruct((t, nslot), F32)],
        compiler_params=_cparams("arbitrary"),
        name="peer_router",
    )(x, mods_l, wq_t, keys)


def _gates_kernel(a_ref, b_ref, g_ref, o_ref):
    tm = a_ref.shape[0]
    n = PEER_KEYS
    key = _iota((tm, n, a_ref.shape[2]), 1).astype(F32).astype(BF16)
    a = a_ref[...].astype(F32).astype(BF16)
    b = b_ref[...].astype(F32).astype(BF16)
    g = g_ref[...].astype(BF16)
    zero = jnp.zeros((), BF16)
    onehot_a = jnp.where(key == a, jnp.ones((), BF16), zero)
    gated_b = jnp.where(key == b, g, zero)
    w = lax.dot_general(onehot_a, gated_b, (((2,), (2,)), ((0,), (0,))),
                        preferred_element_type=F32)
    w_t = jnp.swapaxes(w, 0, 1)
    for r in range(n):
        o_ref[:, r * n:(r + 1) * n] = w_t[r].astype(BF16)


def _gates(a_idx, b_idx, gate):
    t, nslot = a_idx.shape
    tm = 64
    spec = pl.BlockSpec((tm, 1, nslot), lambda i: (i, 0, 0))
    return pl.pallas_call(
        _gates_kernel,
        grid=(t // tm,),
        in_specs=[spec, spec, spec],
        out_specs=pl.BlockSpec((tm, N_EXPERTS), lambda i: (i, 0)),
        out_shape=jax.ShapeDtypeStruct((t, N_EXPERTS), BF16),
        compiler_params=_cparams("arbitrary"),
        name="peer_gates",
    )(a_idx.reshape(t, 1, nslot), b_idx.reshape(t, 1, nslot), gate.reshape(t, 1, nslot))


def _experts_kernel(h_ref, u_ref, v_ref, w_ref, x_ref, m_ref, g_ref, b_ref, o_ref, acc_ref):
    j = pl.program_id(1)

    @pl.when(j == 0)
    def _():
        acc_ref[...] = jnp.zeros_like(acc_ref)

    act = _gelu_tanh(_dot_nt(h_ref[...], u_ref[...].astype(BF16)))
    acc_ref[...] += _dot((act * w_ref[...].astype(F32)).astype(BF16), v_ref[...].astype(BF16))

    @pl.when(j == pl.num_programs(1) - 1)
    def _():
        o_ref[...] = _layernorm(ALPHA * x_ref[...] + m_ref[5] * acc_ref[...], g_ref[...], b_ref[...])


def _experts(h_bf, peer_u, peer_v, layer, w_gate, x, mods_l, ln_g, ln_b, t_ctx, s_lat):
    t, d = x.shape
    tm, te = 1024, 512
    return pl.pallas_call(
        _experts_kernel,
        grid=(t // tm, N_EXPERTS // te),
        in_specs=[pl.BlockSpec((tm, d), lambda i, j: (i, 0)),
                  pl.BlockSpec((None, te, d), lambda i, j: (layer, j, 0)),
                  pl.BlockSpec((None, te, d), lambda i, j: (layer, j, 0)),
                  pl.BlockSpec((tm, te), lambda i, j: (i, j)),
                  pl.BlockSpec((tm, d), lambda i, j: (i, 0)),
                  pl.BlockSpec((None, 6, 1, d), _mod_row_map(tm, t_ctx, s_lat)),
                  pl.BlockSpec((1, d), lambda i, j: (0, 0)),
                  pl.BlockSpec((1, d), lambda i, j: (0, 0))],
        out_specs=pl.BlockSpec((tm, d), lambda i, j: (i, 0)),
        out_shape=jax.ShapeDtypeStruct((t, d), F32),
        scratch_shapes=[pltpu.VMEM((tm, d), F32)],
        compiler_params=_cparams("arbitrary", "arbitrary"),
        name="peer_experts",
    )(h_bf, peer_u, peer_v, w_gate, x, mods_l, ln_g, ln_b)


def _pad_cols(w, n):
    return jnp.concatenate([w, jnp.zeros((w.shape[0], n), w.dtype)], axis=1) if n else w


def _layout_w_in(w):
    a = w[:, 0:1280]
    mcq, mckv, mkpe = w[:, 1280:1472], w[:, 1472:1600], w[:, 1600:1632]
    b = jnp.concatenate([mckv, _pad_cols(mcq, 64), mkpe, mkpe, mkpe, mkpe], axis=1)
    c = w[:, 1632:2400]
    d = _pad_cols(w[:, 2400:3176], PROJ_D - 776)
    return jnp.concatenate([a, b, c, d], axis=1).astype(BF16)


def _layout_w_qb(w):
    w4 = w.reshape(MLA_Q_RANK, N_HEADS, MLA_NOPE + MLA_ROPE)
    return jnp.concatenate([w4[:, :, :MLA_NOPE].reshape(MLA_Q_RANK, -1),
                            w4[:, :, MLA_NOPE:].reshape(MLA_Q_RANK, -1)], axis=1).astype(BF16)


def _layout_w_kvb(w):
    w4 = w.reshape(MLA_KV_RANK, N_HEADS, MLA_NOPE + HEAD_DIM)
    return jnp.concatenate([w4[:, :, :MLA_NOPE].reshape(MLA_KV_RANK, -1),
                            w4[:, :, MLA_NOPE:].reshape(MLA_KV_RANK, -1)], axis=1).astype(BF16)


def _rope_tables(seq):
    rows = seq // GRID_W
    row = jnp.repeat(jnp.arange(rows, dtype=F32), GRID_W)
    col = jnp.tile(jnp.arange(GRID_W, dtype=F32), rows)
    freqs = ROPE_BASE ** (-jnp.arange(ROPE_PAIRS, dtype=F32) / ROPE_PAIRS)
    cos_l, sin_l = [], []
    for pos in (row, col):
        ang = pos[:, None] * freqs
        cos_l += [jnp.cos(ang), jnp.cos(ang)]
        sin_l += [-jnp.sin(ang), jnp.sin(ang)]
    return jnp.concatenate(cos_l, axis=1), jnp.concatenate(sin_l, axis=1)


def _hgrn_state_pack(st):
    b = st.shape[0]
    st_t = jnp.swapaxes(st, -1, -2)
    zero = jnp.zeros_like(st_t[:, :, 0])
    rows = [jnp.concatenate([st_t[:, :, h] if g == h else zero for g in range(N_HEADS)], axis=-1)
            for h in range(N_HEADS)]
    return jnp.concatenate(rows, axis=-2).reshape(b, 2, GROUP, GROUP)


def _hgrn_state_unpack(sb):
    blocks = [sb[:, :, h * HEAD_DIM:(h + 1) * HEAD_DIM, h * HEAD_DIM:(h + 1) * HEAD_DIM]
              for h in range(N_HEADS)]
    return jnp.swapaxes(jnp.stack(blocks, axis=2), -1, -2)


def _ssd_state_pack(st):
    st_t = jnp.swapaxes(st, -1, -2)
    zero = jnp.zeros_like(st_t[:, :, 0])
    rows = [jnp.concatenate([st_t[:, :, h] if h // 2 == g else zero for h in range(N_HEADS)], axis=-1)
            for g in range(2)]
    return jnp.concatenate(rows, axis=-2)


def _ssd_state_unpack(sb):
    blocks = [sb[:, :, (h // 2) * SSD_STATE:(h // 2 + 1) * SSD_STATE, h * HEAD_DIM:(h + 1) * HEAD_DIM]
              for h in range(N_HEADS)]
    return jnp.swapaxes(jnp.stack(blocks, axis=2), -1, -2)


def _tile_lanes(v, n):
    return jnp.tile(v.reshape(1, -1), (1, n))


def kernel(x_prompt, x_sample, cache_mla_ckv, cache_mla_kpe, cache_diff_k, cache_diff_v, state_hgrn, state_ssd, c, c_ctx, w_mod, b_mod, w_in, hgrn_lb, hgrn_norm, mla_q_norm, mla_w_qb, mla_kv_norm, mla_w_kvb, diff_lambda, diff_norm, ssd_conv_w, ssd_conv_b, ssd_dt_bias, ssd_a_log, ssd_d, ssd_norm, w_out, ln1_g, ln1_b, peer_wq, peer_keys, peer_u, peer_v, ln2_g, ln2_b):
    nb, seq, d = x_prompt.shape
    nlat, lseq, _ = x_sample.shape
    depth = w_in.shape[0]
    t_ctx = nb * seq
    x = jnp.concatenate([x_prompt.reshape(t_ctx, d), x_sample.reshape(nlat * lseq, d)], axis=0)

    cond8 = jnp.concatenate([c_ctx.reshape(1, d), c, jnp.zeros((8 - 1 - nlat, d), F32)], axis=0)
    mods = _mods(cond8, w_mod, b_mod)
    mods = mods[:, :1 + nlat].reshape(depth, 1 + nlat, 6, 1, d)

    cos32, sin32 = _rope_tables(lseq)
    cos128, sin128 = jnp.tile(cos32, (1, 4)), jnp.tile(sin32, (1, 4))
    cos256, sin256 = jnp.tile(cos32, (1, 8)), jnp.tile(sin32, (1, 8))

    produced = []
    for l in range(depth):
        mods_l = mods[l]
        pa, pb, pc, pd = _inproj(x, mods_l, _layout_w_in(w_in[l]), t_ctx, lseq)

        norm_hg = _tile_lanes(hgrn_norm[l], N_HEADS)
        hg_ctx, hg_fin = _hgrn(pa, 0, nb, seq, hgrn_lb, norm_hg, None, l)
        hg_lat, _ = _hgrn(pa, t_ctx, nlat, lseq, hgrn_lb, norm_hg, _hgrn_state_pack(state_hgrn[:, l]), l)

        mla_w = (mla_q_norm[l].reshape(1, -1), _layout_w_qb(mla_w_qb[l]),
                 mla_kv_norm[l].reshape(1, -1), _layout_w_kvb(mla_w_kvb[l]))
        mla_ctx, new_ckv, new_kpe = _mla(pb, 0, nb, seq, *mla_w, None)
        (mla_lat,) = _mla(pb, t_ctx, nlat, lseq, *mla_w,
                          (cache_mla_ckv[:, l], jnp.tile(cache_mla_kpe[:, l], (1, 1, 4)), cos128, sin128))

        lam_init = 0.8 - 0.6 * math.exp(-0.3 * l)
        norm_df = _tile_lanes(diff_norm[l], N_HEADS)
        df_ctx, new_dk, new_dv = _diff(pc, 0, nb, seq, diff_lambda[l], norm_df, lam_init, None)
        past = cache_diff_k.shape[2]
        (df_lat,) = _diff(pc, t_ctx, nlat, lseq, diff_lambda[l], norm_df, lam_init,
                          (cache_diff_k[:, l].reshape(nlat, past, GROUP),
                           cache_diff_v[:, l].reshape(nlat, past, GROUP), cos256, sin256))

        ssd_w = (ssd_conv_w[l], ssd_conv_b[l].reshape(1, -1),
                 _pad_cols(ssd_dt_bias[l].reshape(1, -1), 120), _pad_cols(ssd_a_log[l].reshape(1, -1), 120),
                 jnp.repeat(ssd_d[l], HEAD_DIM).reshape(1, -1), ssd_norm[l].reshape(1, -1))
        ssd_ctx, ssd_fin = _ssd(pd, 0, nb, seq, *ssd_w, None)
        ssd_lat, _ = _ssd(pd, t_ctx, nlat, lseq, *ssd_w, _ssd_state_pack(state_ssd[:, l]))

        parts_ctx = [a.reshape(t_ctx, GROUP) for a in (hg_ctx, mla_ctx, df_ctx, ssd_ctx)]
        parts_lat = [a.reshape(nlat * lseq, GROUP) for a in (hg_lat, mla_lat, df_lat, ssd_lat)]
        x = _outproj(parts_ctx, parts_lat, w_out[l].astype(BF16), x, mods_l,
                     ln1_g[l].reshape(1, d), ln1_b[l].reshape(1, d), t_ctx, lseq)

        keys = peer_keys[l].reshape(2 * PEER_HEADS, PEER_KEYS, PEER_HALF).astype(BF16)
        h_bf, a_idx, b_idx, gate = _router(x, mods_l, peer_wq[l].T.astype(BF16), keys, t_ctx, lseq)
        w_gate = _gates(a_idx, b_idx, gate)
        x = _experts(h_bf, peer_u, peer_v, l, w_gate, x, mods_l,
                     ln2_g[l].reshape(1, d), ln2_b[l].reshape(1, d), t_ctx, lseq)

        produced.append((new_ckv, new_kpe,
                         new_dk.reshape(nb, seq, N_HEADS, 2, DIFF_DIM),
                         new_dv.reshape(nb, seq, N_HEADS, 2 * DIFF_DIM),
                         _hgrn_state_unpack(hg_fin), _ssd_state_unpack(ssd_fin)))

    y_prompt = x[:t_ctx].reshape(nb, seq, d)
    y_sample = x[t_ctx:].reshape(nlat, lseq, d)
    stacked = tuple(jnp.stack([p[i] for p in produced], axis=1) for i in range(6))
    return (y_prompt, y_sample) + stacked
```

```python
import functools
import math

import jax
import jax.numpy as jnp
from jax import lax
from jax.experimental import pallas as pl
from jax.experimental.pallas import tpu as pltpu

F32 = jnp.float32
BF16 = jnp.bfloat16
I32 = jnp.int32
HIGHEST = lax.Precision.HIGHEST

D_MODEL = 1024
GROUP = 256
N_HEADS = 4
HEAD_DIM = 64
HG_BLOCK = 16
HG_SLAB = 256
HG_UNROLL = 8
SSD_CHUNK = 128
SSD_STATE = 64
MLA_Q_RANK = 192
MLA_KV_RANK = 128
MLA_NOPE = 64
MLA_ROPE = 32
DIFF_DIM = 32
GRID_W = 64
ROPE_PAIRS = 8
ROPE_BASE = 10000.0
PEER_HEADS = 8
PEER_KEYS = 128
PEER_TOPK = 16
PEER_HALF = 64
N_EXPERTS = PEER_KEYS * PEER_KEYS
NORM_EPS = 1e-6
LN_EPS = 1e-5
DEPTH = 2
ALPHA = (2.0 * DEPTH) ** 0.25
LOG2_E = 1.4426950408889634

PROJ_A = 5 * GROUP
PROJ_B = 512
PROJ_C = 3 * GROUP
PROJ_D = 896
VMEM_LIMIT = 56 * 1024 * 1024


def _cparams(*sem):
    return pltpu.CompilerParams(dimension_semantics=sem, vmem_limit_bytes=VMEM_LIMIT)


def _sigmoid(x):
    return 1.0 / (1.0 + jnp.exp(-x))


def _silu(x):
    return x * _sigmoid(x)


def _softplus(x):
    return jnp.maximum(x, 0.0) + jnp.log(1.0 + jnp.exp(-jnp.abs(x)))


def _gelu_tanh(x):
    return 0.5 * x * (1.0 + jnp.tanh(math.sqrt(2.0 / math.pi) * (x + 0.044715 * (x * x * x))))


def _dot(a, b, precision=None):
    return jnp.dot(a, b, preferred_element_type=F32, precision=precision)


def _dot_nt(a, b, precision=None):
    return lax.dot_general(a, b, (((1,), (1,)), ((), ())), preferred_element_type=F32,
                           precision=precision)


def _dot_tn(a, b, precision=None):
    return lax.dot_general(a, b, (((0,), (0,)), ((), ())), preferred_element_type=F32,
                           precision=precision)


def _iota(shape, dim):
    return lax.broadcasted_iota(I32, shape, dim)


def _block_mask(rows, cols, rblk, cblk):
    return (_iota((rows, cols), 0) // rblk) == (_iota((rows, cols), 1) // cblk)


def _lane_group_mask(width, start, size):
    lane = _iota((1, width), 1)
    return (lane >= start) & (lane < start + size)


def _layernorm(v, g, b):
    mu = jnp.mean(v, axis=-1, keepdims=True)
    d = v - mu
    var = jnp.mean(d * d, axis=-1, keepdims=True)
    return d * lax.rsqrt(var + LN_EPS) * g + b


def _swap_halves16(x):
    width = x.shape[-1]
    lane = _iota(x.shape, x.ndim - 1)
    up = pltpu.roll(x, width - 8, x.ndim - 1)
    down = pltpu.roll(x, 8, x.ndim - 1)
    return jnp.where((lane % 16) < 8, up, down)


def _rope(x, cos, sin_signed):
    return x * cos + _swap_halves16(x) * sin_signed


def _mods_kernel(c_ref, w_ref, b_ref, o_ref):
    s = _silu(c_ref[...]).astype(BF16)
    o_ref[...] = _dot(s, w_ref[...].astype(BF16)) + b_ref[...]


def _mods(cond8, w_mod, b_mod):
    depth, d, n = w_mod.shape
    tn = 1536
    return pl.pallas_call(
        _mods_kernel,
        grid=(depth, n // tn),
        in_specs=[pl.BlockSpec((8, d), lambda l, j: (0, 0)),
                  pl.BlockSpec((None, d, tn), lambda l, j: (l, 0, j)),
                  pl.BlockSpec((None, 1, tn), lambda l, j: (l, 0, j))],
        out_specs=pl.BlockSpec((None, 8, tn), lambda l, j: (l, 0, j)),
        out_shape=jax.ShapeDtypeStruct((depth, 8, n), F32),
        compiler_params=_cparams("arbitrary", "arbitrary"),
        name="mods",
    )(cond8, w_mod, b_mod.reshape(depth, 1, n))


def _mod_row_map(tm, t_ctx, s_lat):
    def index_map(i, *_):
        start = i * tm
        return (jnp.where(start < t_ctx, 0, 1 + (start - t_ctx) // s_lat), 0, 0, 0)
    return index_map


def _inproj_kernel(x_ref, m_ref, w_ref, oa_ref, ob_ref, oc_ref, od_ref):
    h = (x_ref[...] * (1.0 + m_ref[1]) + m_ref[0]).astype(BF16)
    start = 0
    for o_ref in (oa_ref, ob_ref, oc_ref, od_ref):
        width = o_ref.shape[-1]
        o_ref[...] = _dot(h, w_ref[:, start:start + width])
        start += width


def _inproj(x, mods_l, w_in_p, t_ctx, s_lat):
    t, d = x.shape
    tm = 256
    widths = (PROJ_A, PROJ_B, PROJ_C, PROJ_D)
    return pl.pallas_call(
        _inproj_kernel,
        grid=(t // tm,),
        in_specs=[pl.BlockSpec((tm, d), lambda i: (i, 0)),
                  pl.BlockSpec((None, 6, 1, d), _mod_row_map(tm, t_ctx, s_lat)),
                  pl.BlockSpec(w_in_p.shape, lambda i: (0, 0))],
        out_specs=[pl.BlockSpec((tm, w), lambda i: (i, 0)) for w in widths],
        out_shape=[jax.ShapeDtypeStruct((t, w), F32) for w in widths],
        compiler_params=_cparams("arbitrary"),
        name="inproj",
    )(x, mods_l, w_in_p)


def _hgrn_kernel(layer, has_state, *refs):
    if has_state:
        (a_ref, lb_ref, norm_ref, s0_ref, o_ref, sfin_ref,
         q_scr, k_scr, bc_scr, dec_scr, qt_scr, kt_scr, st_scr, o_scr) = refs
    else:
        (a_ref, lb_ref, norm_ref, o_ref, sfin_ref,
         q_scr, k_scr, bc_scr, dec_scr, qt_scr, kt_scr, st_scr, o_scr) = refs
        s0_ref = None
    seq = a_ref.shape[0]
    c = HG_BLOCK
    nblk = seq // c
    slab = HG_SLAB
    nb = slab // c

    lbp = lb_ref[...]
    e = jnp.exp(lbp - jnp.max(lbp, axis=0, keepdims=True))
    p = e / jnp.sum(e, axis=0, keepdims=True)
    lower = jnp.sum(p[1:layer + 1], axis=0) if layer > 0 else jnp.zeros_like(p[0])

    q = _silu(a_ref[:, 0:GROUP])
    q_scr[...] = q
    srow = _iota((slab, slab), 0)
    scol = _iota((slab, slab), 1)
    same = (srow // c) == (scol // c)
    cum_op = (jnp.where(same & (scol <= srow), 1.0, 0.0), jnp.where(same & (scol >= srow), 1.0, 0.0))
    for d in range(2):
        lb = lower[d:d + 1]
        f = lb + (1.0 - lb) * _sigmoid(a_ref[:, (1 + d) * GROUP:(2 + d) * GROUP])
        k = 1.0 - f
        lf = jnp.log(f)
        k_scr[d] = k
        for s0 in range(0, seq, slab):
            bc = _dot(cum_op[d], lf[s0:s0 + slab], precision=HIGHEST)
            bc3 = bc.reshape(nb, c, GROUP)
            edge = bc3[:, c - 1:c, :] if d == 0 else bc3[:, 0:1, :]
            tot = jnp.broadcast_to(edge, (nb, c, GROUP)).reshape(slab, GROUP)
            bc_scr[d, s0:s0 + slab, :] = bc * LOG2_E
            dec_scr[d, s0:s0 + slab, :] = jnp.exp(tot)
            qt_scr[d, s0:s0 + slab, :] = (q[s0:s0 + slab] * jnp.exp(bc)).astype(BF16)
            kt_scr[d, s0:s0 + slab, :] = (k[s0:s0 + slab] * jnp.exp(tot - bc)).astype(BF16)
    if has_state:
        st_scr[...] = s0_ref[...]
    else:
        st_scr[...] = jnp.zeros_like(st_scr)

    bd_ones = _block_mask(GROUP, GROUP, HEAD_DIM, HEAD_DIM).astype(BF16)
    rib = _iota((1, c, GROUP), 1)

    def slab_step(i, carry):
        r0 = pl.multiple_of(i * slab, slab)
        q3 = q_scr[pl.ds(r0, slab), :].reshape(nb, c, GROUP)
        v3 = a_ref[pl.ds(r0, slab), 3 * GROUP:4 * GROUP].reshape(nb, c, GROUP)
        o3 = jnp.zeros((nb, c, GROUP), F32)
        for d in range(2):
            bc3 = bc_scr[d, pl.ds(r0, slab), :].reshape(nb, c, GROUP)
            k3 = k_scr[d, pl.ds(r0, slab), :].reshape(nb, c, GROUP)
            for j in range(c):
                keep = (rib >= j) if d == 0 else (rib <= j)
                dec = jnp.exp2(jnp.where(keep, bc3 - bc3[:, j:j + 1, :], -jnp.inf))
                pj = (dec * q3 * k3[:, j:j + 1, :]).astype(BF16).reshape(slab, GROUP)
                srep = _dot(pj, bd_ones).reshape(nb, c, GROUP)
                o3 = o3 + srep * v3[:, j:j + 1, :]
        o_scr[0, pl.ds(r0, slab), :] = o3.reshape(slab, GROUP)
        return carry

    lax.fori_loop(0, seq // slab, slab_step, 0)

    bd_mask = _block_mask(GROUP, GROUP, HEAD_DIM, HEAD_DIM)

    def body(n, carry):
        rows = [[pl.multiple_of(((n * HG_UNROLL + u) if d == 0 else nblk - 1 - (n * HG_UNROLL + u)) * c, c)
                 for u in range(HG_UNROLL)] for d in range(2)]
        upd = [[_dot_tn(a_ref[pl.ds(r0, c), 3 * GROUP:4 * GROUP].astype(BF16), kt_scr[d, pl.ds(r0, c), :])
                for r0 in rows[d]] for d in range(2)]
        for d in range(2):
            st = st_scr[d]
            for u, r0 in enumerate(rows[d]):
                o_scr[1 + d, pl.ds(r0, c), :] = _dot_nt(qt_scr[d, pl.ds(r0, c), :], st.astype(BF16))
                st = st * dec_scr[d, pl.ds(r0, 1), :] + jnp.where(bd_mask, upd[d][u], 0.0)
            st_scr[d] = st
        return carry

    lax.fori_loop(0, nblk // HG_UNROLL, body, 0)

    o = o_scr[0] + o_scr[1] + o_scr[2]
    mean_op = jnp.where(_block_mask(GROUP, GROUP, HEAD_DIM, HEAD_DIM), 1.0 / HEAD_DIM, 0.0)
    ms = _dot(o * o, mean_op, precision=HIGHEST)
    y = o * lax.rsqrt(ms + NORM_EPS) * norm_ref[...]
    o_ref[...] = y * _silu(a_ref[:, 4 * GROUP:5 * GROUP])
    sfin_ref[...] = st_scr[...]


def _hgrn(proj_a, row0, nseq, seq, hgrn_lb, norm_t, s0, layer):
    has_state = s0 is not None
    blk0 = row0 // seq
    in_specs = [pl.BlockSpec((seq, PROJ_A), lambda b: (blk0 + b, 0)),
                pl.BlockSpec(hgrn_lb.shape, lambda b: (0, 0, 0)),
                pl.BlockSpec((1, GROUP), lambda b: (0, 0))]
    args = [proj_a, hgrn_lb, norm_t]
    if has_state:
        in_specs.append(pl.BlockSpec((None, 2, GROUP, GROUP), lambda b: (b, 0, 0, 0)))
        args.append(s0)
    return pl.pallas_call(
        functools.partial(_hgrn_kernel, layer, has_state),
        grid=(nseq,),
        in_specs=in_specs,
        out_specs=[pl.BlockSpec((None, seq, GROUP), lambda b: (b, 0, 0)),
                   pl.BlockSpec((None, 2, GROUP, GROUP), lambda b: (b, 0, 0, 0))],
        out_shape=[jax.ShapeDtypeStruct((nseq, seq, GROUP), F32),
                   jax.ShapeDtypeStruct((nseq, 2, GROUP, GROUP), F32)],
        scratch_shapes=[pltpu.VMEM((seq, GROUP), F32),
                        pltpu.VMEM((2, seq, GROUP), F32),
                        pltpu.VMEM((2, seq, GROUP), F32),
                        pltpu.VMEM((2, seq, GROUP), F32),
                        pltpu.VMEM((2, seq, GROUP), BF16),
                        pltpu.VMEM((2, seq, GROUP), BF16),
                        pltpu.VMEM((2, GROUP, GROUP), F32),
                        pltpu.VMEM((3, seq, GROUP), F32)],
        compiler_params=_cparams("arbitrary"),
        name="hgrn",
    )(*args)


def _mla_kernel(latent, *refs):
    if latent:
        (b_ref, qn_ref, wq_ref, kvn_ref, wkv_ref, cckv_ref, ckpe_ref, cos_ref, sin_ref,
         o_ref) = refs
    else:
        (b_ref, qn_ref, wq_ref, kvn_ref, wkv_ref, o_ref, ckv_ref, kpe_ref) = refs
    seq = b_ref.shape[0]
    mckv = b_ref[:, 0:MLA_KV_RANK]
    mcq = b_ref[:, MLA_KV_RANK:MLA_KV_RANK + MLA_Q_RANK]
    kpe_t = b_ref[:, 384:512]

    cq = mcq * lax.rsqrt(jnp.mean(mcq * mcq, axis=-1, keepdims=True) + NORM_EPS) * qn_ref[...]
    qf = _dot(cq.astype(BF16), wq_ref[...])
    ckv = mckv * lax.rsqrt(jnp.mean(mckv * mckv, axis=-1, keepdims=True) + NORM_EPS) * kvn_ref[...]
    q_nope = qf[:, 0:N_HEADS * MLA_NOPE]
    q_rope = qf[:, N_HEADS * MLA_NOPE:]
    if latent:
        cos = cos_ref[...]
        sin = sin_ref[...]
        q_rope = _rope(q_rope, cos, sin)
        ckv_all = jnp.concatenate([cckv_ref[...], ckv], axis=0)
        kpe_all = jnp.concatenate([ckpe_ref[...], _rope(kpe_t, cos, sin)], axis=0)
    else:
        ckv_ref[...] = ckv
        kpe_ref[...] = kpe_t[:, 0:MLA_ROPE]
        ckv_all = ckv
        kpe_all = kpe_t
    kv = _dot(ckv_all.astype(BF16), wkv_ref[...])
    kcat = jnp.concatenate([kv[:, 0:GROUP], kpe_all], axis=1).astype(BF16)
    v = kv[:, GROUP:].astype(BF16)
    qcat = jnp.concatenate([q_nope, q_rope], axis=1)
    scale = (MLA_NOPE + MLA_ROPE) ** -0.5
    qb = min(seq, 256)
    width = qcat.shape[1]
    for r0 in range(0, seq, qb):
        qblk = qcat[r0:r0 + qb]
        acc = jnp.zeros((qb, GROUP), F32)
        for h in range(N_HEADS):
            hm = (_lane_group_mask(width, h * MLA_NOPE, MLA_NOPE)
                  | _lane_group_mask(width, N_HEADS * MLA_NOPE + h * MLA_ROPE, MLA_ROPE))
            s = _dot_nt(jnp.where(hm, qblk, 0.0).astype(BF16), kcat) * scale
            e = jnp.exp(s - jnp.max(s, axis=-1, keepdims=True))
            z = jnp.sum(e, axis=-1, keepdims=True)
            oh = _dot(e.astype(BF16), v) / z
            acc = acc + jnp.where(_lane_group_mask(GROUP, h * HEAD_DIM, HEAD_DIM), oh, 0.0)
        o_ref[r0:r0 + qb, :] = acc


def _mla(proj_b, row0, nseq, seq, q_norm, w_qb_p, kv_norm, w_kvb_p, latent_args):
    latent = latent_args is not None
    blk0 = row0 // seq
    in_specs = [pl.BlockSpec((seq, PROJ_B), lambda b: (blk0 + b, 0)),
                pl.BlockSpec(q_norm.shape, lambda b: (0, 0)),
                pl.BlockSpec(w_qb_p.shape, lambda b: (0, 0)),
                pl.BlockSpec(kv_norm.shape, lambda b: (0, 0)),
                pl.BlockSpec(w_kvb_p.shape, lambda b: (0, 0))]
    args = [proj_b, q_norm, w_qb_p, kv_norm, w_kvb_p]
    out_specs = [pl.BlockSpec((None, seq, GROUP), lambda b: (b, 0, 0))]
    out_shape = [jax.ShapeDtypeStruct((nseq, seq, GROUP), F32)]
    if latent:
        cckv, ckpe_t, cos, sin = latent_args
        past = cckv.shape[1]
        in_specs += [pl.BlockSpec((None, past, MLA_KV_RANK), lambda b: (b, 0, 0)),
                     pl.BlockSpec((None, past, 128), lambda b: (b, 0, 0)),
                     pl.BlockSpec(cos.shape, lambda b: (0, 0)),
                     pl.BlockSpec(sin.shape, lambda b: (0, 0))]
        args += [cckv, ckpe_t, cos, sin]
    else:
        out_specs += [pl.BlockSpec((None, seq, MLA_KV_RANK), lambda b: (b, 0, 0)),
                      pl.BlockSpec((None, seq, MLA_ROPE), lambda b: (b, 0, 0))]
        out_shape += [jax.ShapeDtypeStruct((nseq, seq, MLA_KV_RANK), F32),
                      jax.ShapeDtypeStruct((nseq, seq, MLA_ROPE), F32)]
    return pl.pallas_call(
        functools.partial(_mla_kernel, latent),
        grid=(nseq,),
        in_specs=in_specs, out_specs=out_specs, out_shape=out_shape,
        compiler_params=_cparams("arbitrary"),
        name="mla",
    )(*args)


def _diff_kernel(latent, lam_init, *refs):
    if latent:
        (c_ref, lam_ref, norm_ref, ck_ref, cv_ref, cos_ref, sin_ref, o_ref) = refs
    else:
        (c_ref, lam_ref, norm_ref, o_ref, k_ref, v_ref) = refs
    seq = c_ref.shape[0]
    dq = c_ref[:, 0:GROUP]
    dk = c_ref[:, GROUP:2 * GROUP]
    dv = c_ref[:, 2 * GROUP:3 * GROUP]
    if latent:
        cos = cos_ref[...]
        sin = sin_ref[...]
        dq = _rope(dq, cos, sin)
        k_all = jnp.concatenate([ck_ref[...], _rope(dk, cos, sin)], axis=0)
        v_all = jnp.concatenate([cv_ref[...], dv], axis=0)
    else:
        k_ref[...] = dk
        v_ref[...] = dv
        k_all = dk
        v_all = dv
    lv = lam_ref[...]
    lam = (jnp.exp(jnp.sum(lv[0:1] * lv[1:2], axis=-1, keepdims=True))
           - jnp.exp(jnp.sum(lv[2:3] * lv[3:4], axis=-1, keepdims=True)) + lam_init)
    k_bf = k_all.astype(BF16)
    v_bf = v_all.astype(BF16)
    scale = DIFF_DIM ** -0.5
    mean_op = jnp.where(_block_mask(GROUP, GROUP, HEAD_DIM, HEAD_DIM), 1.0 / HEAD_DIM, 0.0)
    qb = min(seq, 256)
    for r0 in range(0, seq, qb):
        qblk = dq[r0:r0 + qb]
        acc = jnp.zeros((qb, GROUP), F32)
        for h in range(N_HEADS):
            probs = []
            for comp in range(2):
                cm = _lane_group_mask(GROUP, h * HEAD_DIM + comp * DIFF_DIM, DIFF_DIM)
                s = _dot_nt(jnp.where(cm, qblk, 0.0).astype(BF16), k_bf) * scale
                e = jnp.exp(s - jnp.max(s, axis=-1, keepdims=True))
                probs.append(e / jnp.sum(e, axis=-1, keepdims=True))
            w = (probs[0] - lam * probs[1]).astype(BF16)
            acc = acc + jnp.where(_lane_group_mask(GROUP, h * HEAD_DIM, HEAD_DIM), _dot(w, v_bf), 0.0)
        ms = _dot(acc * acc, mean_op, precision=HIGHEST)
        o_ref[r0:r0 + qb, :] = acc * lax.rsqrt(ms + NORM_EPS) * norm_ref[...] * (1.0 - lam_init)


def _diff(proj_c, row0, nseq, seq, lam_p, norm_t, lam_init, latent_args):
    latent = latent_args is not None
    blk0 = row0 // seq
    in_specs = [pl.BlockSpec((seq, PROJ_C), lambda b: (blk0 + b, 0)),
                pl.BlockSpec(lam_p.shape, lambda b: (0, 0)),
                pl.BlockSpec(norm_t.shape, lambda b: (0, 0))]
    args = [proj_c, lam_p, norm_t]
    out_specs = [pl.BlockSpec((None, seq, GROUP), lambda b: (b, 0, 0))]
    out_shape = [jax.ShapeDtypeStruct((nseq, seq, GROUP), F32)]
    if latent:
        ck, cv, cos, sin = latent_args
        past = ck.shape[1]
        in_specs += [pl.BlockSpec((None, past, GROUP), lambda b: (b, 0, 0)),
                     pl.BlockSpec((None, past, GROUP), lambda b: (b, 0, 0)),
                     pl.BlockSpec(cos.shape, lambda b: (0, 0)),
                     pl.BlockSpec(sin.shape, lambda b: (0, 0))]
        args += [ck, cv, cos, sin]
    else:
        out_specs += [pl.BlockSpec((None, seq, GROUP), lambda b: (b, 0, 0))] * 2
        out_shape += [jax.ShapeDtypeStruct((nseq, seq, GROUP), F32)] * 2
    return pl.pallas_call(
        functools.partial(_diff_kernel, latent, lam_init),
        grid=(nseq,),
        in_specs=in_specs, out_specs=out_specs, out_shape=out_shape,
        compiler_params=_cparams("arbitrary"),
        name="diffattn",
    )(*args)


def _ssd_kernel(has_state, *refs):
    if has_state:
        (d_ref, cw_ref, cb_ref, dtb_ref, alog_ref, dskip_ref, norm_ref, s0_ref,
         o_ref, sfin_ref, xs_scr, bm_scr, cm_scr, xdt_scr, a_scr, st_scr, yf_scr, yb_scr) = refs
    else:
        (d_ref, cw_ref, cb_ref, dtb_ref, alog_ref, dskip_ref, norm_ref,
         o_ref, sfin_ref, xs_scr, bm_scr, cm_scr, xdt_scr, a_scr, st_scr, yf_scr, yb_scr) = refs
    seq = d_ref.shape[0]
    c = SSD_CHUNK
    nchunk = seq // c
    ngrp = 2 * SSD_STATE

    xin = d_ref[:, GROUP:GROUP + 512]
    rows = _iota(xin.shape, 0)
    prev = jnp.where(rows == 0, 0.0, pltpu.roll(xin, 1, 0))
    nxt = jnp.where(rows == seq - 1, 0.0, pltpu.roll(xin, seq - 1, 0))
    cw = cw_ref[...]
    xbc = _silu(cw[0:1] * prev + cw[1:2] * xin + cw[2:3] * nxt + cb_ref[...])
    xs = xbc[:, 0:GROUP]
    xs_scr[...] = xs
    bm_scr[...] = xbc[:, GROUP:GROUP + ngrp]
    cm_scr[...] = xbc[:, GROUP + ngrp:GROUP + 2 * ngrp]
    dt = _softplus(d_ref[:, GROUP + 512:GROUP + 640] + dtb_ref[...])
    a_scr[...] = dt * (-jnp.exp(alog_ref[...]))
    erow = _iota((128, GROUP), 0)
    ehead = _iota((128, GROUP), 1) // HEAD_DIM
    expand = tuple((erow == 4 * d + ehead).astype(F32) for d in range(2))
    for d in range(2):
        xdt_scr[d] = xs * _dot(dt, expand[d], precision=HIGHEST)
    if has_state:
        st_scr[...] = s0_ref[...]
    else:
        st_scr[...] = jnp.zeros_like(st_scr)

    row = _iota((c, c), 0)
    col = _iota((c, c), 1)
    tri = ((col <= row).astype(F32), (col >= row).astype(F32))
    keep = (col <= row, col >= row)
    grp_lane = _iota((1, ngrp), 1) // SSD_STATE
    valid = (_iota((ngrp, GROUP), 0) // SSD_STATE) == (_iota((ngrp, GROUP), 1) // (2 * HEAD_DIM))

    def chunk_step(d, r0, out_scr):
        a_c = a_scr[pl.ds(r0, c), :]
        bm_c = bm_scr[pl.ds(r0, c), :]
        cm_c = cm_scr[pl.ds(r0, c), :].astype(BF16)
        xdt_c = xdt_scr[d, pl.ds(r0, c), :]
        acum = _dot(tri[d], a_c, precision=HIGHEST)
        acum_t = acum.T
        acum_rep = _dot(acum, expand[d], precision=HIGHEST)
        bm2 = jnp.concatenate([jnp.where(grp_lane == g, bm_c, 0.0) for g in range(2)], axis=0)
        cb = _dot_nt(cm_c, bm2.astype(BF16))
        scores = []
        xparts = []
        for h in range(N_HEADS):
            lane = 4 * d + h
            seg = jnp.exp(jnp.where(keep[d], acum[:, lane:lane + 1] - acum_t[lane:lane + 1, :], -jnp.inf))
            g = h // 2
            scores.append((cb[:, g * c:(g + 1) * c] * seg).astype(BF16))
            xparts.append(jnp.where(_lane_group_mask(GROUP, h * HEAD_DIM, HEAD_DIM), xdt_c, 0.0))
        y = _dot(jnp.concatenate(scores, axis=1), jnp.concatenate(xparts, axis=0).astype(BF16))
        st = st_scr[d]
        y = y + _dot(cm_c, st.astype(BF16)) * jnp.exp(acum_rep)
        out_scr[pl.ds(r0, c), :] = y
        edge = acum_rep[c - 1:c] if d == 0 else acum_rep[0:1]
        xt = (xdt_c * jnp.exp(edge - acum_rep)).astype(BF16)
        upd = _dot_tn(bm_c.astype(BF16), xt)
        st_scr[d] = st * jnp.exp(edge) + jnp.where(valid, upd, 0.0)

    def body(n, carry):
        chunk_step(0, pl.multiple_of(n * c, c), yf_scr)
        chunk_step(1, pl.multiple_of((nchunk - 1 - n) * c, c), yb_scr)
        return carry

    lax.fori_loop(0, nchunk, body, 0)

    y = yf_scr[...] + yb_scr[...] + dskip_ref[...] * xs_scr[...]
    y = y * _silu(d_ref[:, 0:GROUP])
    o_ref[...] = y * lax.rsqrt(jnp.mean(y * y, axis=-1, keepdims=True) + NORM_EPS) * norm_ref[...]
    sfin_ref[...] = st_scr[...]


def _ssd(proj_d, row0, nseq, seq, conv_w, conv_b, dt_bias_p, a_log_p, d_rep, norm, s0):
    has_state = s0 is not None
    blk0 = row0 // seq
    ngrp = 2 * SSD_STATE
    small = [conv_w, conv_b, dt_bias_p, a_log_p, d_rep, norm]
    in_specs = ([pl.BlockSpec((seq, PROJ_D), lambda b: (blk0 + b, 0))]
                + [pl.BlockSpec(s.shape, lambda b: (0, 0)) for s in small])
    args = [proj_d] + small
    if has_state:
        in_specs.append(pl.BlockSpec((None, 2, ngrp, GROUP), lambda b: (b, 0, 0, 0)))
        args.append(s0)
    return pl.pallas_call(
        functools.partial(_ssd_kernel, has_state),
        grid=(nseq,),
        in_specs=in_specs,
        out_specs=[pl.BlockSpec((None, seq, GROUP), lambda b: (b, 0, 0)),
                   pl.BlockSpec((None, 2, ngrp, GROUP), lambda b: (b, 0, 0, 0))],
        out_shape=[jax.ShapeDtypeStruct((nseq, seq, GROUP), F32),
                   jax.ShapeDtypeStruct((nseq, 2, ngrp, GROUP), F32)],
        scratch_shapes=[pltpu.VMEM((seq, GROUP), F32),
                        pltpu.VMEM((seq, ngrp), F32),
                        pltpu.VMEM((seq, ngrp), F32),
                        pltpu.VMEM((2, seq, GROUP), F32),
                        pltpu.VMEM((seq, 128), F32),
                        pltpu.VMEM((2, ngrp, GROUP), F32),
                        pltpu.VMEM((seq, GROUP), F32),
                        pltpu.VMEM((seq, GROUP), F32)],
        compiler_params=_cparams("arbitrary"),
        name="ssd",
    )(*args)


def _outproj_kernel(n_ctx_tiles, *refs):
    ctx_refs, lat_refs = refs[0:4], refs[4:8]
    w_ref, x_ref, m_ref, g_ref, b_ref, o_ref = refs[8:]
    is_ctx = pl.program_id(0) < n_ctx_tiles
    mixed = None
    for i, (c_ref, l_ref) in enumerate(zip(ctx_refs, lat_refs)):
        part = jnp.where(is_ctx, c_ref[...], l_ref[...]).astype(BF16)
        term = _dot(part, w_ref[i * GROUP:(i + 1) * GROUP, :])
        mixed = term if mixed is None else mixed + term
    o_ref[...] = _layernorm(ALPHA * x_ref[...] + m_ref[2] * mixed, g_ref[...], b_ref[...])


def _outproj(parts_ctx, parts_lat, w_out_bf, x, mods_l, ln_g, ln_b, t_ctx, s_lat):
    t, d = x.shape
    tm = 256
    n_ctx = t_ctx // tm
    n_lat = (t - t_ctx) // tm
    ctx_spec = pl.BlockSpec((tm, GROUP), lambda i: (jnp.minimum(i, n_ctx - 1), 0))
    lat_spec = pl.BlockSpec((tm, GROUP), lambda i: (jnp.clip(i - n_ctx, 0, n_lat - 1), 0))
    return pl.pallas_call(
        functools.partial(_outproj_kernel, n_ctx),
        grid=(t // tm,),
        in_specs=[ctx_spec] * 4 + [lat_spec] * 4
        + [pl.BlockSpec(w_out_bf.shape, lambda i: (0, 0)),
           pl.BlockSpec((tm, d), lambda i: (i, 0)),
           pl.BlockSpec((None, 6, 1, d), _mod_row_map(tm, t_ctx, s_lat)),
           pl.BlockSpec((1, d), lambda i: (0, 0)),
           pl.BlockSpec((1, d), lambda i: (0, 0))],
        out_specs=pl.BlockSpec((tm, d), lambda i: (i, 0)),
        out_shape=jax.ShapeDtypeStruct((t, d), F32),
        compiler_params=_cparams("arbitrary"),
        name="outproj_ln",
    )(*parts_ctx, *parts_lat, w_out_bf, x, mods_l, ln_g, ln_b)


def _top_rows(s, k, extra=()):
    r = s.shape[0]
    rid = _iota(s.shape, 0).astype(F32)
    vals, ids = [], []
    picked = [[] for _ in extra]
    for _ in range(k):
        m = jnp.max(s, axis=0, keepdims=True)
        cand = jnp.where(s == m, rid, float(r))
        i = jnp.min(cand, axis=0, keepdims=True)
        hit = cand == i
        vals.append(m)
        ids.append(i)
        for lst, arr in zip(picked, extra):
            lst.append(jnp.max(jnp.where(hit, arr, -1.0), axis=0, keepdims=True))
        s = jnp.where(hit, -jnp.inf, s)
    cat = lambda xs: jnp.concatenate(xs, axis=0)
    return cat(vals), cat(ids), [cat(p) for p in picked]


def _router_kernel(x_ref, m_ref, wqt_ref, keys_ref, h_ref, a_ref, b_ref, g_ref):
    tm = x_ref.shape[0]
    hb = (x_ref[...] * (1.0 + m_ref[4]) + m_ref[3]).astype(BF16)
    h_ref[...] = hb
    qt = _dot_nt(wqt_ref[...], hb).astype(BF16)
    k = PEER_TOPK
    a_rows, b_rows, g_rows = [], [], []
    for head in range(PEER_HEADS):
        tv, ti = [], []
        for half in range(2):
            g = 2 * head + half
            sc = _dot(keys_ref[g], qt[g * PEER_HALF:(g + 1) * PEER_HALF])
            v, i, _ = _top_rows(sc, k)
            tv.append(v)
            ti.append(i)
        cs = [tv[0][0:1] + tv[1]]
        ca = [jnp.broadcast_to(ti[0][0:1], (k, tm))]
        cb = [ti[1]]
        for k1 in range(1, 4):
            cs.append(tv[0][k1:k1 + 1] + tv[1][0:8])
            ca.append(jnp.broadcast_to(ti[0][k1:k1 + 1], (8, tm)))
            cb.append(ti[1][0:8])
        low = _iota((8, tm), 0) < 4
        v2_dup = jnp.where(low, tv[1][0:8], pltpu.roll(tv[1][0:8], 4, 0))
        i2_dup = jnp.where(low, ti[1][0:8], pltpu.roll(ti[1][0:8], 4, 0))
        for k1 in (4, 6):
            cs.append(jnp.where(low, tv[0][k1:k1 + 1], tv[0][k1 + 1:k1 + 2]) + v2_dup)
            ca.append(jnp.where(low, ti[0][k1:k1 + 1], ti[0][k1 + 1:k1 + 2]))
            cb.append(i2_dup)
        cs.append(tv[0][8:16] + tv[1][0:1])
        ca.append(ti[0][8:16])
        cb.append(jnp.broadcast_to(ti[1][0:1], (8, tm)))
        best, _, (sel_a, sel_b) = _top_rows(jnp.concatenate(cs, axis=0), k,
                                            extra=(jnp.concatenate(ca, axis=0), jnp.concatenate(cb, axis=0)))
        e = jnp.exp(best - best[0:1])
        g_rows.append(e / jnp.sum(e, axis=0, keepdims=True))
        a_rows.append(sel_a)
        b_rows.append(sel_b)
    a_ref[...] = jnp.concatenate(a_rows, axis=0).T.astype(I32)
    b_ref[...] = jnp.concatenate(b_rows, axis=0).T.astype(I32)
    g_ref[...] = jnp.concatenate(g_rows, axis=0).T


def _router(x, mods_l, wq_t, keys, t_ctx, s_lat):
    t, d = x.shape
    tm = 128
    nslot = PEER_HEADS * PEER_TOPK
    return pl.pallas_call(
        _router_kernel,
        grid=(t // tm,),
        in_specs=[pl.BlockSpec((tm, d), lambda i: (i, 0)),
                  pl.BlockSpec((None, 6, 1, d), _mod_row_map(tm, t_ctx, s_lat)),
                  pl.BlockSpec(wq_t.shape, lambda i: (0, 0)),
                  pl.BlockSpec(keys.shape, lambda i: (0, 0, 0))],
        out_specs=[pl.BlockSpec((tm, d), lambda i: (i, 0)),
                   pl.BlockSpec((tm, nslot), lambda i: (i, 0)),
                   pl.BlockSpec((tm, nslot), lambda i: (i, 0)),
                   pl.BlockSpec((tm, nslot), lambda i: (i, 0))],
        out_shape=[jax.ShapeDtypeStruct((t, d), BF16),
                   jax.ShapeDtypeStruct((t, nslot), I32),
                   jax.ShapeDtypeStruct((t, nslot), I32),
                   jax.ShapeDtypeStruct((t, nslot), F32)],
        compiler_params=_cparams("arbitrary"),
        name="peer_router",
    )(x, mods_l, wq_t, keys)


def _gates_kernel(a_ref, b_ref, g_ref, u_ref, v_ref, o_ref, ub_ref, vb_ref):
    tm = a_ref.shape[0]
    n = PEER_KEYS
    sub = 16
    ub_ref[...] = u_ref[...].astype(BF16)
    vb_ref[...] = v_ref[...].astype(BF16)
    key = _iota((sub, n, a_ref.shape[2]), 1).astype(F32).astype(BF16)
    zero = jnp.zeros((), BF16)
    for t0 in range(0, tm, sub):
        a = a_ref[t0:t0 + sub].astype(F32).astype(BF16)
        b = b_ref[t0:t0 + sub].astype(F32).astype(BF16)
        g = g_ref[t0:t0 + sub].astype(BF16)
        onehot_a = jnp.where(key == a, jnp.ones((), BF16), zero)
        gated_b = jnp.where(key == b, g, zero)
        w = lax.dot_general(onehot_a, gated_b, (((2,), (2,)), ((0,), (0,))),
                            preferred_element_type=F32)
        w_t = jnp.swapaxes(w.astype(BF16), 0, 1)
        for r in range(n):
            o_ref[t0:t0 + sub, r * n:(r + 1) * n] = w_t[r]


def _gates(a_idx, b_idx, gate, peer_u, peer_v, layer):
    t, nslot = a_idx.shape
    d = peer_u.shape[-1]
    tm = 96
    steps = t // tm
    te = N_EXPERTS // steps
    spec = pl.BlockSpec((tm, 1, nslot), lambda i: (i, 0, 0))
    tab_in = pl.BlockSpec((None, te, d), lambda i: (layer, i, 0))
    tab_out = pl.BlockSpec((te, d), lambda i: (i, 0))
    return pl.pallas_call(
        _gates_kernel,
        grid=(steps,),
        in_specs=[spec, spec, spec, tab_in, tab_in],
        out_specs=[pl.BlockSpec((tm, N_EXPERTS), lambda i: (i, 0)), tab_out, tab_out],
        out_shape=[jax.ShapeDtypeStruct((t, N_EXPERTS), BF16),
                   jax.ShapeDtypeStruct((N_EXPERTS, d), BF16),
                   jax.ShapeDtypeStruct((N_EXPERTS, d), BF16)],
        compiler_params=_cparams("arbitrary"),
        name="peer_gates",
    )(a_idx.reshape(t, 1, nslot), b_idx.reshape(t, 1, nslot), gate.reshape(t, 1, nslot), peer_u, peer_v)


def _experts_kernel(h_ref, u_ref, v_ref, w_ref, x_ref, m_ref, g_ref, b_ref, o_ref, acc_ref):
    j = pl.program_id(1)

    @pl.when(j == 0)
    def _():
        acc_ref[...] = jnp.zeros_like(acc_ref)

    act = _gelu_tanh(_dot_nt(h_ref[...], u_ref[...]))
    acc_ref[...] += _dot((act * w_ref[...].astype(F32)).astype(BF16), v_ref[...])

    @pl.when(j == pl.num_programs(1) - 1)
    def _():
        o_ref[...] = _layernorm(ALPHA * x_ref[...] + m_ref[5] * acc_ref[...], g_ref[...], b_ref[...])


def _experts(h_bf, u_bf, v_bf, w_gate, x, mods_l, ln_g, ln_b, t_ctx, s_lat):
    t, d = x.shape
    tm, te = 1024, 1024
    return pl.pallas_call(
        _experts_kernel,
        grid=(t // tm, N_EXPERTS // te),
        in_specs=[pl.BlockSpec((tm, d), lambda i, j: (i, 0)),
                  pl.BlockSpec((te, d), lambda i, j: (j, 0)),
                  pl.BlockSpec((te, d), lambda i, j: (j, 0)),
                  pl.BlockSpec((tm, te), lambda i, j: (i, j)),
                  pl.BlockSpec((tm, d), lambda i, j: (i, 0)),
                  pl.BlockSpec((None, 6, 1, d), _mod_row_map(tm, t_ctx, s_lat)),
                  pl.BlockSpec((1, d), lambda i, j: (0, 0)),
                  pl.BlockSpec((1, d), lambda i, j: (0, 0))],
        out_specs=pl.BlockSpec((tm, d), lambda i, j: (i, 0)),
        out_shape=jax.ShapeDtypeStruct((t, d), F32),
        scratch_shapes=[pltpu.VMEM((tm, d), F32)],
        compiler_params=_cparams("arbitrary", "arbitrary"),
        name="peer_experts",
    )(h_bf, u_bf, v_bf, w_gate, x, mods_l, ln_g, ln_b)


def _pad_cols(w, n):
    return jnp.concatenate([w, jnp.zeros((w.shape[0], n), w.dtype)], axis=1) if n else w


def _layout_w_in(w):
    a = w[:, 0:1280]
    mcq, mckv, mkpe = w[:, 1280:1472], w[:, 1472:1600], w[:, 1600:1632]
    b = jnp.concatenate([mckv, _pad_cols(mcq, 64), mkpe, mkpe, mkpe, mkpe], axis=1)
    c = w[:, 1632:2400]
    d = _pad_cols(w[:, 2400:3176], PROJ_D - 776)
    return jnp.concatenate([a, b, c, d], axis=1).astype(BF16)


def _layout_w_qb(w):
    w4 = w.reshape(MLA_Q_RANK, N_HEADS, MLA_NOPE + MLA_ROPE)
    return jnp.concatenate([w4[:, :, :MLA_NOPE].reshape(MLA_Q_RANK, -1),
                            w4[:, :, MLA_NOPE:].reshape(MLA_Q_RANK, -1)], axis=1).astype(BF16)


def _layout_w_kvb(w):
    w4 = w.reshape(MLA_KV_RANK, N_HEADS, MLA_NOPE + HEAD_DIM)
    return jnp.concatenate([w4[:, :, :MLA_NOPE].reshape(MLA_KV_RANK, -1),
                            w4[:, :, MLA_NOPE:].reshape(MLA_KV_RANK, -1)], axis=1).astype(BF16)


def _rope_tables(seq):
    rows = seq // GRID_W
    row = jnp.repeat(jnp.arange(rows, dtype=F32), GRID_W)
    col = jnp.tile(jnp.arange(GRID_W, dtype=F32), rows)
    freqs = ROPE_BASE ** (-jnp.arange(ROPE_PAIRS, dtype=F32) / ROPE_PAIRS)
    cos_l, sin_l = [], []
    for pos in (row, col):
        ang = pos[:, None] * freqs
        cos_l += [jnp.cos(ang), jnp.cos(ang)]
        sin_l += [-jnp.sin(ang), jnp.sin(ang)]
    return jnp.concatenate(cos_l, axis=1), jnp.concatenate(sin_l, axis=1)


def _hgrn_state_pack(st):
    b = st.shape[0]
    st_t = jnp.swapaxes(st, -1, -2)
    zero = jnp.zeros_like(st_t[:, :, 0])
    rows = [jnp.concatenate([st_t[:, :, h] if g == h else zero for g in range(N_HEADS)], axis=-1)
            for h in range(N_HEADS)]
    return jnp.concatenate(rows, axis=-2).reshape(b, 2, GROUP, GROUP)


def _hgrn_state_unpack(sb):
    blocks = [sb[:, :, h * HEAD_DIM:(h + 1) * HEAD_DIM, h * HEAD_DIM:(h + 1) * HEAD_DIM]
              for h in range(N_HEADS)]
    return jnp.swapaxes(jnp.stack(blocks, axis=2), -1, -2)


def _ssd_state_pack(st):
    st_t = jnp.swapaxes(st, -1, -2)
    zero = jnp.zeros_like(st_t[:, :, 0])
    rows = [jnp.concatenate([st_t[:, :, h] if h // 2 == g else zero for h in range(N_HEADS)], axis=-1)
            for g in range(2)]
    return jnp.concatenate(rows, axis=-2)


def _ssd_state_unpack(sb):
    blocks = [sb[:, :, (h // 2) * SSD_STATE:(h // 2 + 1) * SSD_STATE, h * HEAD_DIM:(h + 1) * HEAD_DIM]
              for h in range(N_HEADS)]
    return jnp.swapaxes(jnp.stack(blocks, axis=2), -1, -2)


def _tile_lanes(v, n):
    return jnp.tile(v.reshape(1, -1), (1, n))


def kernel(x_prompt, x_sample, cache_mla_ckv, cache_mla_kpe, cache_diff_k, cache_diff_v, state_hgrn, state_ssd, c, c_ctx, w_mod, b_mod, w_in, hgrn_lb, hgrn_norm, mla_q_norm, mla_w_qb, mla_kv_norm, mla_w_kvb, diff_lambda, diff_norm, ssd_conv_w, ssd_conv_b, ssd_dt_bias, ssd_a_log, ssd_d, ssd_norm, w_out, ln1_g, ln1_b, peer_wq, peer_keys, peer_u, peer_v, ln2_g, ln2_b):
    nb, seq, d = x_prompt.shape
    nlat, lseq, _ = x_sample.shape
    depth = w_in.shape[0]
    t_ctx = nb * seq
    x = jnp.concatenate([x_prompt.reshape(t_ctx, d), x_sample.reshape(nlat * lseq, d)], axis=0)

    cond8 = jnp.concatenate([c_ctx.reshape(1, d), c, jnp.zeros((8 - 1 - nlat, d), F32)], axis=0)
    mods = _mods(cond8, w_mod, b_mod)
    mods = mods[:, :1 + nlat].reshape(depth, 1 + nlat, 6, 1, d)

    cos32, sin32 = _rope_tables(lseq)
    cos128, sin128 = jnp.tile(cos32, (1, 4)), jnp.tile(sin32, (1, 4))
    cos256, sin256 = jnp.tile(cos32, (1, 8)), jnp.tile(sin32, (1, 8))

    produced = []
    for l in range(depth):
        mods_l = mods[l]
        pa, pb, pc, pd = _inproj(x, mods_l, _layout_w_in(w_in[l]), t_ctx, lseq)

        norm_hg = _tile_lanes(hgrn_norm[l], N_HEADS)
        hg_ctx, hg_fin = _hgrn(pa, 0, nb, seq, hgrn_lb, norm_hg, None, l)
        hg_lat, _ = _hgrn(pa, t_ctx, nlat, lseq, hgrn_lb, norm_hg, _hgrn_state_pack(state_hgrn[:, l]), l)

        mla_w = (mla_q_norm[l].reshape(1, -1), _layout_w_qb(mla_w_qb[l]),
                 mla_kv_norm[l].reshape(1, -1), _layout_w_kvb(mla_w_kvb[l]))
        mla_ctx, new_ckv, new_kpe = _mla(pb, 0, nb, seq, *mla_w, None)
        (mla_lat,) = _mla(pb, t_ctx, nlat, lseq, *mla_w,
                          (cache_mla_ckv[:, l], jnp.tile(cache_mla_kpe[:, l], (1, 1, 4)), cos128, sin128))

        lam_init = 0.8 - 0.6 * math.exp(-0.3 * l)
        norm_df = _tile_lanes(diff_norm[l], N_HEADS)
        df_ctx, new_dk, new_dv = _diff(pc, 0, nb, seq, diff_lambda[l], norm_df, lam_init, None)
        past = cache_diff_k.shape[2]
        (df_lat,) = _diff(pc, t_ctx, nlat, lseq, diff_lambda[l], norm_df, lam_init,
                          (cache_diff_k[:, l].reshape(nlat, past, GROUP),
                           cache_diff_v[:, l].reshape(nlat, past, GROUP), cos256, sin256))

        ssd_w = (ssd_conv_w[l], ssd_conv_b[l].reshape(1, -1),
                 _pad_cols(ssd_dt_bias[l].reshape(1, -1), 120), _pad_cols(ssd_a_log[l].reshape(1, -1), 120),
                 jnp.repeat(ssd_d[l], HEAD_DIM).reshape(1, -1), ssd_norm[l].reshape(1, -1))
        ssd_ctx, ssd_fin = _ssd(pd, 0, nb, seq, *ssd_w, None)
        ssd_lat, _ = _ssd(pd, t_ctx, nlat, lseq, *ssd_w, _ssd_state_pack(state_ssd[:, l]))

        parts_ctx = [a.reshape(t_ctx, GROUP) for a in (hg_ctx, mla_ctx, df_ctx, ssd_ctx)]
        parts_lat = [a.reshape(nlat * lseq, GROUP) for a in (hg_lat, mla_lat, df_lat, ssd_lat)]
        x = _outproj(parts_ctx, parts_lat, w_out[l].astype(BF16), x, mods_l,
                     ln1_g[l].reshape(1, d), ln1_b[l].reshape(1, d), t_ctx, lseq)

        keys = peer_keys[l].reshape(2 * PEER_HEADS, PEER_KEYS, PEER_HALF).astype(BF16)
        h_bf, a_idx, b_idx, gate = _router(x, mods_l, peer_wq[l].T.astype(BF16), keys, t_ctx, lseq)
        w_gate, u_bf, v_bf = _gates(a_idx, b_idx, gate, peer_u, peer_v, l)
        x = _experts(h_bf, u_bf, v_bf, w_gate, x, mods_l,
                     ln2_g[l].reshape(1, d), ln2_b[l].reshape(1, d), t_ctx, lseq)

        produced.append((new_ckv, new_kpe,
                         new_dk.reshape(nb, seq, N_HEADS, 2, DIFF_DIM),
                         new_dv.reshape(nb, seq, N_HEADS, 2 * DIFF_DIM),
                         _hgrn_state_unpack(hg_fin), _ssd_state_unpack(ssd_fin)))

    y_prompt = x[:t_ctx].reshape(nb, seq, d)
    y_sample = x[t_ctx:].reshape(nlat, lseq, d)
    stacked = tuple(jnp.stack([p[i] for p in produced], axis=1) for i in range(6))
    return (y_prompt, y_sample) + stacked
```

```python
import functools
import math

import jax
import jax.numpy as jnp
from jax import lax
from jax.experimental import pallas as pl
from jax.experimental.pallas import tpu as pltpu

F32 = jnp.float32
BF16 = jnp.bfloat16
I32 = jnp.int32
HIGHEST = lax.Precision.HIGHEST

D_MODEL = 1024
GROUP = 256
N_HEADS = 4
HEAD_DIM = 64
HG_BLOCK = 16
HG_SLAB = 256
HG_UNROLL = 8
SSD_CHUNK = 128
SSD_STATE = 64
MLA_Q_RANK = 192
MLA_KV_RANK = 128
MLA_NOPE = 64
MLA_ROPE = 32
DIFF_DIM = 32
GRID_W = 64
ROPE_PAIRS = 8
ROPE_BASE = 10000.0
PEER_HEADS = 8
PEER_KEYS = 128
PEER_TOPK = 16
PEER_HALF = 64
N_EXPERTS = PEER_KEYS * PEER_KEYS
NORM_EPS = 1e-6
LN_EPS = 1e-5
DEPTH = 2
ALPHA = (2.0 * DEPTH) ** 0.25
LOG2_E = 1.4426950408889634

PROJ_A = 5 * GROUP
PROJ_B = 512
PROJ_C = 3 * GROUP
PROJ_D = 896
VMEM_LIMIT = 56 * 1024 * 1024


def _cparams(*sem):
    return pltpu.CompilerParams(dimension_semantics=sem, vmem_limit_bytes=VMEM_LIMIT)


def _sigmoid(x):
    return 1.0 / (1.0 + jnp.exp(-x))


def _silu(x):
    return x * _sigmoid(x)


def _softplus(x):
    return jnp.maximum(x, 0.0) + jnp.log(1.0 + jnp.exp(-jnp.abs(x)))


def _gelu_tanh(x):
    return 0.5 * x * (1.0 + jnp.tanh(math.sqrt(2.0 / math.pi) * (x + 0.044715 * (x * x * x))))


def _dot(a, b, precision=None):
    return jnp.dot(a, b, preferred_element_type=F32, precision=precision)


def _dot_nt(a, b, precision=None):
    return lax.dot_general(a, b, (((1,), (1,)), ((), ())), preferred_element_type=F32,
                           precision=precision)


def _dot_tn(a, b, precision=None):
    return lax.dot_general(a, b, (((0,), (0,)), ((), ())), preferred_element_type=F32,
                           precision=precision)


def _iota(shape, dim):
    return lax.broadcasted_iota(I32, shape, dim)


def _block_mask(rows, cols, rblk, cblk):
    return (_iota((rows, cols), 0) // rblk) == (_iota((rows, cols), 1) // cblk)


def _lane_group_mask(width, start, size):
    lane = _iota((1, width), 1)
    return (lane >= start) & (lane < start + size)


def _layernorm(v, g, b):
    mu = jnp.mean(v, axis=-1, keepdims=True)
    d = v - mu
    var = jnp.mean(d * d, axis=-1, keepdims=True)
    return d * lax.rsqrt(var + LN_EPS) * g + b


def _swap_halves16(x):
    width = x.shape[-1]
    lane = _iota(x.shape, x.ndim - 1)
    up = pltpu.roll(x, width - 8, x.ndim - 1)
    down = pltpu.roll(x, 8, x.ndim - 1)
    return jnp.where((lane % 16) < 8, up, down)


def _rope(x, cos, sin_signed):
    return x * cos + _swap_halves16(x) * sin_signed


def _mods_kernel(c_ref, w_ref, b_ref, o_ref):
    s = _silu(c_ref[...]).astype(BF16)
    o_ref[...] = _dot(s, w_ref[...].astype(BF16)) + b_ref[...]


def _mods(cond8, w_mod, b_mod):
    depth, d, n = w_mod.shape
    tn = 1536
    return pl.pallas_call(
        _mods_kernel,
        grid=(depth, n // tn),
        in_specs=[pl.BlockSpec((8, d), lambda l, j: (0, 0)),
                  pl.BlockSpec((None, d, tn), lambda l, j: (l, 0, j)),
                  pl.BlockSpec((None, 1, tn), lambda l, j: (l, 0, j))],
        out_specs=pl.BlockSpec((None, 8, tn), lambda l, j: (l, 0, j)),
        out_shape=jax.ShapeDtypeStruct((depth, 8, n), F32),
        compiler_params=_cparams("arbitrary", "arbitrary"),
        name="mods",
    )(cond8, w_mod, b_mod.reshape(depth, 1, n))


def _mod_row_map(tm, t_ctx, s_lat):
    def index_map(i, *_):
        start = i * tm
        return (jnp.where(start < t_ctx, 0, 1 + (start - t_ctx) // s_lat), 0, 0, 0)
    return index_map


def _inproj_kernel(x_ref, m_ref, w_ref, oa_ref, ob_ref, oc_ref, od_ref):
    h = (x_ref[...] * (1.0 + m_ref[1]) + m_ref[0]).astype(BF16)
    start = 0
    for o_ref in (oa_ref, ob_ref, oc_ref, od_ref):
        width = o_ref.shape[-1]
        o_ref[...] = _dot(h, w_ref[:, start:start + width])
        start += width


def _inproj(x, mods_l, w_in_p, t_ctx, s_lat):
    t, d = x.shape
    tm = 512
    widths = (PROJ_A, PROJ_B, PROJ_C, PROJ_D)
    return pl.pallas_call(
        _inproj_kernel,
        grid=(t // tm,),
        in_specs=[pl.BlockSpec((tm, d), lambda i: (i, 0)),
                  pl.BlockSpec((None, 6, 1, d), _mod_row_map(tm, t_ctx, s_lat)),
                  pl.BlockSpec(w_in_p.shape, lambda i: (0, 0))],
        out_specs=[pl.BlockSpec((tm, w), lambda i: (i, 0)) for w in widths],
        out_shape=[jax.ShapeDtypeStruct((t, w), F32) for w in widths],
        compiler_params=_cparams("arbitrary"),
        name="inproj",
    )(x, mods_l, w_in_p)


def _hgrn_kernel(layer, has_state, *refs):
    if has_state:
        (a_ref, lb_ref, norm_ref, s0_ref, o_ref, sfin_ref,
         q_scr, k_scr, bc_scr, dec_scr, qt_scr, kt_scr, st_scr, o_scr) = refs
    else:
        (a_ref, lb_ref, norm_ref, o_ref, sfin_ref,
         q_scr, k_scr, bc_scr, dec_scr, qt_scr, kt_scr, st_scr, o_scr) = refs
        s0_ref = None
    seq = a_ref.shape[0]
    c = HG_BLOCK
    nblk = seq // c
    slab = HG_SLAB
    nb = slab // c

    lbp = lb_ref[...]
    e = jnp.exp(lbp - jnp.max(lbp, axis=0, keepdims=True))
    p = e / jnp.sum(e, axis=0, keepdims=True)
    lower = jnp.sum(p[1:layer + 1], axis=0) if layer > 0 else jnp.zeros_like(p[0])

    q = _silu(a_ref[:, 0:GROUP])
    q_scr[...] = q
    srow = _iota((slab, slab), 0)
    scol = _iota((slab, slab), 1)
    same = (srow // c) == (scol // c)
    cum_op = (jnp.where(same & (scol <= srow), 1.0, 0.0), jnp.where(same & (scol >= srow), 1.0, 0.0))
    for d in range(2):
        lb = lower[d:d + 1]
        f = lb + (1.0 - lb) * _sigmoid(a_ref[:, (1 + d) * GROUP:(2 + d) * GROUP])
        k = 1.0 - f
        lf = jnp.log(f)
        k_scr[d] = k
        for s0 in range(0, seq, slab):
            bc = _dot(cum_op[d], lf[s0:s0 + slab], precision=HIGHEST)
            bc3 = bc.reshape(nb, c, GROUP)
            edge = bc3[:, c - 1:c, :] if d == 0 else bc3[:, 0:1, :]
            tot = jnp.broadcast_to(edge, (nb, c, GROUP)).reshape(slab, GROUP)
            bc_scr[d, s0:s0 + slab, :] = bc * LOG2_E
            dec_scr[d, s0:s0 + slab, :] = jnp.exp(tot)
            qt_scr[d, s0:s0 + slab, :] = (q[s0:s0 + slab] * jnp.exp(bc)).astype(BF16)
            kt_scr[d, s0:s0 + slab, :] = (k[s0:s0 + slab] * jnp.exp(tot - bc)).astype(BF16)
    if has_state:
        st_scr[...] = s0_ref[...]
    else:
        st_scr[...] = jnp.zeros_like(st_scr)

    bd_ones = _block_mask(GROUP, GROUP, HEAD_DIM, HEAD_DIM).astype(BF16)
    rib = _iota((1, c, GROUP), 1)

    def slab_step(i, carry):
        r0 = pl.multiple_of(i * slab, slab)
        q3 = q_scr[pl.ds(r0, slab), :].reshape(nb, c, GROUP)
        v3 = a_ref[pl.ds(r0, slab), 3 * GROUP:4 * GROUP].reshape(nb, c, GROUP)
        o3 = jnp.zeros((nb, c, GROUP), F32)
        for d in range(2):
            bc3 = bc_scr[d, pl.ds(r0, slab), :].reshape(nb, c, GROUP)
            k3 = k_scr[d, pl.ds(r0, slab), :].reshape(nb, c, GROUP)
            for j in range(c):
                keep = (rib >= j) if d == 0 else (rib <= j)
                dec = jnp.exp2(jnp.where(keep, bc3 - bc3[:, j:j + 1, :], -jnp.inf))
                pj = (dec * q3 * k3[:, j:j + 1, :]).astype(BF16).reshape(slab, GROUP)
                srep = _dot(pj, bd_ones).reshape(nb, c, GROUP)
                o3 = o3 + srep * v3[:, j:j + 1, :]
        o_scr[0, pl.ds(r0, slab), :] = o3.reshape(slab, GROUP)
        return carry

    lax.fori_loop(0, seq // slab, slab_step, 0)

    bd_mask = _block_mask(GROUP, GROUP, HEAD_DIM, HEAD_DIM)

    def body(n, carry):
        rows = [[pl.multiple_of(((n * HG_UNROLL + u) if d == 0 else nblk - 1 - (n * HG_UNROLL + u)) * c, c)
                 for u in range(HG_UNROLL)] for d in range(2)]
        upd = [[_dot_tn(a_ref[pl.ds(r0, c), 3 * GROUP:4 * GROUP].astype(BF16), kt_scr[d, pl.ds(r0, c), :])
                for r0 in rows[d]] for d in range(2)]
        for d in range(2):
            st = st_scr[d]
            for u, r0 in enumerate(rows[d]):
                o_scr[1 + d, pl.ds(r0, c), :] = _dot_nt(qt_scr[d, pl.ds(r0, c), :], st.astype(BF16))
                st = st * dec_scr[d, pl.ds(r0, 1), :] + jnp.where(bd_mask, upd[d][u], 0.0)
            st_scr[d] = st
        return carry

    lax.fori_loop(0, nblk // HG_UNROLL, body, 0)

    o = o_scr[0] + o_scr[1] + o_scr[2]
    mean_op = jnp.where(_block_mask(GROUP, GROUP, HEAD_DIM, HEAD_DIM), 1.0 / HEAD_DIM, 0.0)
    ms = _dot(o * o, mean_op, precision=HIGHEST)
    y = o * lax.rsqrt(ms + NORM_EPS) * norm_ref[...]
    o_ref[...] = y * _silu(a_ref[:, 4 * GROUP:5 * GROUP])
    sfin_ref[...] = st_scr[...]


def _hgrn_parts(proj_a, row0, nseq, seq, hgrn_lb, norm_t, s0, layer):
    has_state = s0 is not None
    blk0 = row0 // seq
    in_specs = [pl.BlockSpec((seq, PROJ_A), lambda b: (blk0 + b, 0)),
                pl.BlockSpec(hgrn_lb.shape, lambda b: (0, 0, 0)),
                pl.BlockSpec((1, GROUP), lambda b: (0, 0))]
    args = [proj_a, hgrn_lb, norm_t]
    if has_state:
        in_specs.append(pl.BlockSpec((None, 2, GROUP, GROUP), lambda b: (b, 0, 0, 0)))
        args.append(s0)
    out_specs = [pl.BlockSpec((None, seq, GROUP), lambda b: (b, 0, 0)),
                 pl.BlockSpec((None, 2, GROUP, GROUP), lambda b: (b, 0, 0, 0))]
    out_shape = [jax.ShapeDtypeStruct((nseq, seq, GROUP), F32),
                 jax.ShapeDtypeStruct((nseq, 2, GROUP, GROUP), F32)]
    scratch = [pltpu.VMEM((seq, GROUP), F32),
               pltpu.VMEM((2, seq, GROUP), F32),
               pltpu.VMEM((2, seq, GROUP), F32),
               pltpu.VMEM((2, seq, GROUP), F32),
               pltpu.VMEM((2, seq, GROUP), BF16),
               pltpu.VMEM((2, seq, GROUP), BF16),
               pltpu.VMEM((2, GROUP, GROUP), F32),
               pltpu.VMEM((3, seq, GROUP), F32)]
    return functools.partial(_hgrn_kernel, layer, has_state), in_specs, args, out_specs, out_shape, scratch


def _mla_kernel(latent, *refs):
    if latent:
        (b_ref, qn_ref, wq_ref, kvn_ref, wkv_ref, cckv_ref, ckpe_ref, cos_ref, sin_ref,
         o_ref) = refs
    else:
        (b_ref, qn_ref, wq_ref, kvn_ref, wkv_ref, o_ref, ckv_ref, kpe_ref) = refs
    seq = b_ref.shape[0]
    mckv = b_ref[:, 0:MLA_KV_RANK]
    mcq = b_ref[:, MLA_KV_RANK:MLA_KV_RANK + MLA_Q_RANK]
    kpe_t = b_ref[:, 384:512]

    cq = mcq * lax.rsqrt(jnp.mean(mcq * mcq, axis=-1, keepdims=True) + NORM_EPS) * qn_ref[...]
    qf = _dot(cq.astype(BF16), wq_ref[...])
    ckv = mckv * lax.rsqrt(jnp.mean(mckv * mckv, axis=-1, keepdims=True) + NORM_EPS) * kvn_ref[...]
    q_nope = qf[:, 0:N_HEADS * MLA_NOPE]
    q_rope = qf[:, N_HEADS * MLA_NOPE:]
    if latent:
        cos = cos_ref[...]
        sin = sin_ref[...]
        q_rope = _rope(q_rope, cos, sin)
        ckv_all = jnp.concatenate([cckv_ref[...], ckv], axis=0)
        kpe_all = jnp.concatenate([ckpe_ref[...], _rope(kpe_t, cos, sin)], axis=0)
    else:
        ckv_ref[...] = ckv
        kpe_ref[...] = kpe_t[:, 0:MLA_ROPE]
        ckv_all = ckv
        kpe_all = kpe_t
    kv = _dot(ckv_all.astype(BF16), wkv_ref[...])
    kcat = jnp.concatenate([kv[:, 0:GROUP], kpe_all], axis=1).astype(BF16)
    v = kv[:, GROUP:].astype(BF16)
    qcat = jnp.concatenate([q_nope, q_rope], axis=1)
    scale = (MLA_NOPE + MLA_ROPE) ** -0.5
    qb = min(seq, 256)
    width = qcat.shape[1]
    for r0 in range(0, seq, qb):
        qblk = qcat[r0:r0 + qb]
        acc = jnp.zeros((qb, GROUP), F32)
        for h in range(N_HEADS):
            hm = (_lane_group_mask(width, h * MLA_NOPE, MLA_NOPE)
                  | _lane_group_mask(width, N_HEADS * MLA_NOPE + h * MLA_ROPE, MLA_ROPE))
            s = _dot_nt(jnp.where(hm, qblk, 0.0).astype(BF16), kcat) * scale
            e = jnp.exp(s - jnp.max(s, axis=-1, keepdims=True))
            z = jnp.sum(e, axis=-1, keepdims=True)
            oh = _dot(e.astype(BF16), v) / z
            acc = acc + jnp.where(_lane_group_mask(GROUP, h * HEAD_DIM, HEAD_DIM), oh, 0.0)
        o_ref[r0:r0 + qb, :] = acc


def _mla_parts(proj_b, row0, nseq, seq, q_norm, w_qb_p, kv_norm, w_kvb_p, latent_args):
    latent = latent_args is not None
    blk0 = row0 // seq
    in_specs = [pl.BlockSpec((seq, PROJ_B), lambda b: (blk0 + b, 0)),
                pl.BlockSpec(q_norm.shape, lambda b: (0, 0)),
                pl.BlockSpec(w_qb_p.shape, lambda b: (0, 0)),
                pl.BlockSpec(kv_norm.shape, lambda b: (0, 0)),
                pl.BlockSpec(w_kvb_p.shape, lambda b: (0, 0))]
    args = [proj_b, q_norm, w_qb_p, kv_norm, w_kvb_p]
    out_specs = [pl.BlockSpec((None, seq, GROUP), lambda b: (b, 0, 0))]
    out_shape = [jax.ShapeDtypeStruct((nseq, seq, GROUP), F32)]
    if latent:
        cckv, ckpe_t, cos, sin = latent_args
        past = cckv.shape[1]
        in_specs += [pl.BlockSpec((None, past, MLA_KV_RANK), lambda b: (b, 0, 0)),
                     pl.BlockSpec((None, past, 128), lambda b: (b, 0, 0)),
                     pl.BlockSpec(cos.shape, lambda b: (0, 0)),
                     pl.BlockSpec(sin.shape, lambda b: (0, 0))]
        args += [cckv, ckpe_t, cos, sin]
    else:
        out_specs += [pl.BlockSpec((None, seq, MLA_KV_RANK), lambda b: (b, 0, 0)),
                      pl.BlockSpec((None, seq, MLA_ROPE), lambda b: (b, 0, 0))]
        out_shape += [jax.ShapeDtypeStruct((nseq, seq, MLA_KV_RANK), F32),
                      jax.ShapeDtypeStruct((nseq, seq, MLA_ROPE), F32)]
    return functools.partial(_mla_kernel, latent), in_specs, args, out_specs, out_shape, []


def _diff_kernel(latent, lam_init, *refs):
    if latent:
        (c_ref, lam_ref, norm_ref, ck_ref, cv_ref, cos_ref, sin_ref, o_ref) = refs
    else:
        (c_ref, lam_ref, norm_ref, o_ref, k_ref, v_ref) = refs
    seq = c_ref.shape[0]
    dq = c_ref[:, 0:GROUP]
    dk = c_ref[:, GROUP:2 * GROUP]
    dv = c_ref[:, 2 * GROUP:3 * GROUP]
    if latent:
        cos = cos_ref[...]
        sin = sin_ref[...]
        dq = _rope(dq, cos, sin)
        k_all = jnp.concatenate([ck_ref[...], _rope(dk, cos, sin)], axis=0)
        v_all = jnp.concatenate([cv_ref[...], dv], axis=0)
    else:
        k_ref[...] = dk
        v_ref[...] = dv
        k_all = dk
        v_all = dv
    lv = lam_ref[...]
    lam = (jnp.exp(jnp.sum(lv[0:1] * lv[1:2], axis=-1, keepdims=True))
           - jnp.exp(jnp.sum(lv[2:3] * lv[3:4], axis=-1, keepdims=True)) + lam_init)
    k_bf = k_all.astype(BF16)
    v_bf = v_all.astype(BF16)
    scale = DIFF_DIM ** -0.5
    mean_op = jnp.where(_block_mask(GROUP, GROUP, HEAD_DIM, HEAD_DIM), 1.0 / HEAD_DIM, 0.0)
    qb = min(seq, 256)
    for r0 in range(0, seq, qb):
        qblk = dq[r0:r0 + qb]
        acc = jnp.zeros((qb, GROUP), F32)
        for h in range(N_HEADS):
            probs = []
            for comp in range(2):
                cm = _lane_group_mask(GROUP, h * HEAD_DIM + comp * DIFF_DIM, DIFF_DIM)
                s = _dot_nt(jnp.where(cm, qblk, 0.0).astype(BF16), k_bf) * scale
                e = jnp.exp(s - jnp.max(s, axis=-1, keepdims=True))
                probs.append(e / jnp.sum(e, axis=-1, keepdims=True))
            w = (probs[0] - lam * probs[1]).astype(BF16)
            acc = acc + jnp.where(_lane_group_mask(GROUP, h * HEAD_DIM, HEAD_DIM), _dot(w, v_bf), 0.0)
        ms = _dot(acc * acc, mean_op, precision=HIGHEST)
        o_ref[r0:r0 + qb, :] = acc * lax.rsqrt(ms + NORM_EPS) * norm_ref[...] * (1.0 - lam_init)


def _diff_parts(proj_c, row0, nseq, seq, lam_p, norm_t, lam_init, latent_args):
    latent = latent_args is not None
    blk0 = row0 // seq
    in_specs = [pl.BlockSpec((seq, PROJ_C), lambda b: (blk0 + b, 0)),
                pl.BlockSpec(lam_p.shape, lambda b: (0, 0)),
                pl.BlockSpec(norm_t.shape, lambda b: (0, 0))]
    args = [proj_c, lam_p, norm_t]
    out_specs = [pl.BlockSpec((None, seq, GROUP), lambda b: (b, 0, 0))]
    out_shape = [jax.ShapeDtypeStruct((nseq, seq, GROUP), F32)]
    if latent:
        ck, cv, cos, sin = latent_args
        past = ck.shape[1]
        in_specs += [pl.BlockSpec((None, past, GROUP), lambda b: (b, 0, 0)),
                     pl.BlockSpec((None, past, GROUP), lambda b: (b, 0, 0)),
                     pl.BlockSpec(cos.shape, lambda b: (0, 0)),
                     pl.BlockSpec(sin.shape, lambda b: (0, 0))]
        args += [ck, cv, cos, sin]
    else:
        out_specs += [pl.BlockSpec((None, seq, GROUP), lambda b: (b, 0, 0))] * 2
        out_shape += [jax.ShapeDtypeStruct((nseq, seq, GROUP), F32)] * 2
    return functools.partial(_diff_kernel, latent, lam_init), in_specs, args, out_specs, out_shape, []


def _mixers(nseq, name, *parts):
    n_in = [len(p[1]) for p in parts]
    n_out = [len(p[3]) for p in parts]
    n_scr = [len(p[5]) for p in parts]

    def body(*refs):
        ins = refs[:sum(n_in)]
        outs = refs[sum(n_in):sum(n_in) + sum(n_out)]
        scr = refs[sum(n_in) + sum(n_out):]
        i0 = o0 = s0 = 0
        for part, ni, no, ns in zip(parts, n_in, n_out, n_scr):
            part[0](*ins[i0:i0 + ni], *outs[o0:o0 + no], *scr[s0:s0 + ns])
            i0, o0, s0 = i0 + ni, o0 + no, s0 + ns

    res = pl.pallas_call(
        body,
        grid=(nseq,),
        in_specs=[x for p in parts for x in p[1]],
        out_specs=[x for p in parts for x in p[3]],
        out_shape=[x for p in parts for x in p[4]],
        scratch_shapes=[x for p in parts for x in p[5]],
        compiler_params=_cparams("arbitrary"),
        name=name,
    )(*[x for p in parts for x in p[2]])
    out, o0 = [], 0
    for no in n_out:
        out.append(tuple(res[o0:o0 + no]))
        o0 += no
    return out


def _ssd_kernel(has_state, *refs):
    if has_state:
        (d_ref, cw_ref, cb_ref, dtb_ref, alog_ref, dskip_ref, norm_ref, s0_ref,
         o_ref, sfin_ref, xs_scr, bm_scr, cm_scr, xdt_scr, a_scr, st_scr, yf_scr, yb_scr) = refs
    else:
        (d_ref, cw_ref, cb_ref, dtb_ref, alog_ref, dskip_ref, norm_ref,
         o_ref, sfin_ref, xs_scr, bm_scr, cm_scr, xdt_scr, a_scr, st_scr, yf_scr, yb_scr) = refs
    seq = d_ref.shape[0]
    c = SSD_CHUNK
    nchunk = seq // c
    ngrp = 2 * SSD_STATE

    xin = d_ref[:, GROUP:GROUP + 512]
    rows = _iota(xin.shape, 0)
    prev = jnp.where(rows == 0, 0.0, pltpu.roll(xin, 1, 0))
    nxt = jnp.where(rows == seq - 1, 0.0, pltpu.roll(xin, seq - 1, 0))
    cw = cw_ref[...]
    xbc = _silu(cw[0:1] * prev + cw[1:2] * xin + cw[2:3] * nxt + cb_ref[...])
    xs = xbc[:, 0:GROUP]
    xs_scr[...] = xs
    bm_scr[...] = xbc[:, GROUP:GROUP + ngrp]
    cm_scr[...] = xbc[:, GROUP + ngrp:GROUP + 2 * ngrp]
    dt = _softplus(d_ref[:, GROUP + 512:GROUP + 640] + dtb_ref[...])
    a_scr[...] = dt * (-jnp.exp(alog_ref[...]))
    erow = _iota((128, GROUP), 0)
    ehead = _iota((128, GROUP), 1) // HEAD_DIM
    expand = tuple((erow == 4 * d + ehead).astype(F32) for d in range(2))
    for d in range(2):
        xdt_scr[d] = xs * _dot(dt, expand[d], precision=HIGHEST)
    if has_state:
        st_scr[...] = s0_ref[...]
    else:
        st_scr[...] = jnp.zeros_like(st_scr)

    row = _iota((c, c), 0)
    col = _iota((c, c), 1)
    tri = ((col <= row).astype(F32), (col >= row).astype(F32))
    keep = (col <= row, col >= row)
    grp_lane = _iota((1, ngrp), 1) // SSD_STATE
    valid = (_iota((ngrp, GROUP), 0) // SSD_STATE) == (_iota((ngrp, GROUP), 1) // (2 * HEAD_DIM))

    def chunk_step(d, r0, out_scr):
        a_c = a_scr[pl.ds(r0, c), :]
        bm_c = bm_scr[pl.ds(r0, c), :]
        cm_c = cm_scr[pl.ds(r0, c), :].astype(BF16)
        xdt_c = xdt_scr[d, pl.ds(r0, c), :]
        acum = _dot(tri[d], a_c, precision=HIGHEST)
        acum_t = acum.T
        acum_rep = _dot(acum, expand[d], precision=HIGHEST)
        bm2 = jnp.concatenate([jnp.where(grp_lane == g, bm_c, 0.0) for g in range(2)], axis=0)
        cb = _dot_nt(cm_c, bm2.astype(BF16))
        scores = []
        xparts = []
        for h in range(N_HEADS):
            lane = 4 * d + h
            seg = jnp.exp(jnp.where(keep[d], acum[:, lane:lane + 1] - acum_t[lane:lane + 1, :], -jnp.inf))
            g = h // 2
            scores.append((cb[:, g * c:(g + 1) * c] * seg).astype(BF16))
            xparts.append(jnp.where(_lane_group_mask(GROUP, h * HEAD_DIM, HEAD_DIM), xdt_c, 0.0))
        y = _dot(jnp.concatenate(scores, axis=1), jnp.concatenate(xparts, axis=0).astype(BF16))
        st = st_scr[d]
        y = y + _dot(cm_c, st.astype(BF16)) * jnp.exp(acum_rep)
        out_scr[pl.ds(r0, c), :] = y
        edge = acum_rep[c - 1:c] if d == 0 else acum_rep[0:1]
        xt = (xdt_c * jnp.exp(edge - acum_rep)).astype(BF16)
        upd = _dot_tn(bm_c.astype(BF16), xt)
        st_scr[d] = st * jnp.exp(edge) + jnp.where(valid, upd, 0.0)

    def body(n, carry):
        chunk_step(0, pl.multiple_of(n * c, c), yf_scr)
        chunk_step(1, pl.multiple_of((nchunk - 1 - n) * c, c), yb_scr)
        return carry

    lax.fori_loop(0, nchunk, body, 0)

    y = yf_scr[...] + yb_scr[...] + dskip_ref[...] * xs_scr[...]
    y = y * _silu(d_ref[:, 0:GROUP])
    o_ref[...] = y * lax.rsqrt(jnp.mean(y * y, axis=-1, keepdims=True) + NORM_EPS) * norm_ref[...]
    sfin_ref[...] = st_scr[...]


def _ssd_parts(proj_d, row0, nseq, seq, conv_w, conv_b, dt_bias_p, a_log_p, d_rep, norm, s0):
    has_state = s0 is not None
    blk0 = row0 // seq
    ngrp = 2 * SSD_STATE
    small = [conv_w, conv_b, dt_bias_p, a_log_p, d_rep, norm]
    in_specs = ([pl.BlockSpec((seq, PROJ_D), lambda b: (blk0 + b, 0))]
                + [pl.BlockSpec(s.shape, lambda b: (0, 0)) for s in small])
    args = [proj_d] + small
    if has_state:
        in_specs.append(pl.BlockSpec((None, 2, ngrp, GROUP), lambda b: (b, 0, 0, 0)))
        args.append(s0)
    out_specs = [pl.BlockSpec((None, seq, GROUP), lambda b: (b, 0, 0)),
                 pl.BlockSpec((None, 2, ngrp, GROUP), lambda b: (b, 0, 0, 0))]
    out_shape = [jax.ShapeDtypeStruct((nseq, seq, GROUP), F32),
                 jax.ShapeDtypeStruct((nseq, 2, ngrp, GROUP), F32)]
    scratch = [pltpu.VMEM((seq, GROUP), F32),
               pltpu.VMEM((seq, ngrp), F32),
               pltpu.VMEM((seq, ngrp), F32),
               pltpu.VMEM((2, seq, GROUP), F32),
               pltpu.VMEM((seq, 128), F32),
               pltpu.VMEM((2, ngrp, GROUP), F32),
               pltpu.VMEM((seq, GROUP), F32),
               pltpu.VMEM((seq, GROUP), F32)]
    return functools.partial(_ssd_kernel, has_state), in_specs, args, out_specs, out_shape, scratch


def _outproj_kernel(n_ctx_tiles, *refs):
    ctx_refs, lat_refs = refs[0:4], refs[4:8]
    w_ref, x_ref, m_ref, g_ref, b_ref, o_ref = refs[8:]
    is_ctx = pl.program_id(0) < n_ctx_tiles
    mixed = None
    for i, (c_ref, l_ref) in enumerate(zip(ctx_refs, lat_refs)):
        part = jnp.where(is_ctx, c_ref[...], l_ref[...]).astype(BF16)
        term = _dot(part, w_ref[i * GROUP:(i + 1) * GROUP, :])
        mixed = term if mixed is None else mixed + term
    o_ref[...] = _layernorm(ALPHA * x_ref[...] + m_ref[2] * mixed, g_ref[...], b_ref[...])


def _outproj(parts_ctx, parts_lat, w_out_bf, x, mods_l, ln_g, ln_b, t_ctx, s_lat):
    t, d = x.shape
    tm = 512
    n_ctx = t_ctx // tm
    n_lat = (t - t_ctx) // tm
    ctx_spec = pl.BlockSpec((tm, GROUP), lambda i: (jnp.minimum(i, n_ctx - 1), 0))
    lat_spec = pl.BlockSpec((tm, GROUP), lambda i: (jnp.clip(i - n_ctx, 0, n_lat - 1), 0))
    return pl.pallas_call(
        functools.partial(_outproj_kernel, n_ctx),
        grid=(t // tm,),
        in_specs=[ctx_spec] * 4 + [lat_spec] * 4
        + [pl.BlockSpec(w_out_bf.shape, lambda i: (0, 0)),
           pl.BlockSpec((tm, d), lambda i: (i, 0)),
           pl.BlockSpec((None, 6, 1, d), _mod_row_map(tm, t_ctx, s_lat)),
           pl.BlockSpec((1, d), lambda i: (0, 0)),
           pl.BlockSpec((1, d), lambda i: (0, 0))],
        out_specs=pl.BlockSpec((tm, d), lambda i: (i, 0)),
        out_shape=jax.ShapeDtypeStruct((t, d), F32),
        compiler_params=_cparams("arbitrary"),
        name="outproj_ln",
    )(*parts_ctx, *parts_lat, w_out_bf, x, mods_l, ln_g, ln_b)


def _top_rows(s, k, extra=()):
    r = s.shape[0]
    rid = _iota(s.shape, 0).astype(F32)
    vals, ids = [], []
    picked = [[] for _ in extra]
    for _ in range(k):
        m = jnp.max(s, axis=0, keepdims=True)
        cand = jnp.where(s == m, rid, float(r))
        i = jnp.min(cand, axis=0, keepdims=True)
        hit = cand == i
        vals.append(m)
        ids.append(i)
        for lst, arr in zip(picked, extra):
            lst.append(jnp.max(jnp.where(hit, arr, -1.0), axis=0, keepdims=True))
        s = jnp.where(hit, -jnp.inf, s)
    cat = lambda xs: jnp.concatenate(xs, axis=0)
    return cat(vals), cat(ids), [cat(p) for p in picked]


def _router_kernel(x_ref, m_ref, wqt_ref, keys_ref, h_ref, a_ref, b_ref, g_ref):
    tm = x_ref.shape[0]
    hb = (x_ref[...] * (1.0 + m_ref[4]) + m_ref[3]).astype(BF16)
    h_ref[...] = hb
    qt = _dot_nt(wqt_ref[...], hb).astype(BF16)
    k = PEER_TOPK
    a_rows, b_rows, g_rows = [], [], []
    for head in range(PEER_HEADS):
        tv, ti = [], []
        for half in range(2):
            g = 2 * head + half
            sc = _dot(keys_ref[g], qt[g * PEER_HALF:(g + 1) * PEER_HALF])
            v, i, _ = _top_rows(sc, k)
            tv.append(v)
            ti.append(i)
        cs = [tv[0][0:1] + tv[1]]
        ca = [jnp.broadcast_to(ti[0][0:1], (k, tm))]
        cb = [ti[1]]
        for k1 in range(1, 4):
            cs.append(tv[0][k1:k1 + 1] + tv[1][0:8])
            ca.append(jnp.broadcast_to(ti[0][k1:k1 + 1], (8, tm)))
            cb.append(ti[1][0:8])
        low = _iota((8, tm), 0) < 4
        v2_dup = jnp.where(low, tv[1][0:8], pltpu.roll(tv[1][0:8], 4, 0))
        i2_dup = jnp.where(low, ti[1][0:8], pltpu.roll(ti[1][0:8], 4, 0))
        for k1 in (4, 6):
            cs.append(jnp.where(low, tv[0][k1:k1 + 1], tv[0][k1 + 1:k1 + 2]) + v2_dup)
            ca.append(jnp.where(low, ti[0][k1:k1 + 1], ti[0][k1 + 1:k1 + 2]))
            cb.append(i2_dup)
        cs.append(tv[0][8:16] + tv[1][0:1])
        ca.append(ti[0][8:16])
        cb.append(jnp.broadcast_to(ti[1][0:1], (8, tm)))
        best, _, (sel_a, sel_b) = _top_rows(jnp.concatenate(cs, axis=0), k,
                                            extra=(jnp.concatenate(ca, axis=0), jnp.concatenate(cb, axis=0)))
        e = jnp.exp(best - best[0:1])
        g_rows.append(e / jnp.sum(e, axis=0, keepdims=True))
        a_rows.append(sel_a)
        b_rows.append(sel_b)
    a_ref[...] = jnp.concatenate(a_rows, axis=0).T.astype(I32)
    b_ref[...] = jnp.concatenate(b_rows, axis=0).T.astype(I32)
    g_ref[...] = jnp.concatenate(g_rows, axis=0).T


def _router(x, mods_l, wq_t, keys, t_ctx, s_lat):
    t, d = x.shape
    tm = 128
    nslot = PEER_HEADS * PEER_TOPK
    return pl.pallas_call(
        _router_kernel,
        grid=(t // tm,),
        in_specs=[pl.BlockSpec((tm, d), lambda i: (i, 0)),
                  pl.BlockSpec((None, 6, 1, d), _mod_row_map(tm, t_ctx, s_lat)),
                  pl.BlockSpec(wq_t.shape, lambda i: (0, 0)),
                  pl.BlockSpec(keys.shape, lambda i: (0, 0, 0))],
        out_specs=[pl.BlockSpec((tm, d), lambda i: (i, 0)),
                   pl.BlockSpec((tm, nslot), lambda i: (i, 0)),
                   pl.BlockSpec((tm, nslot), lambda i: (i, 0)),
                   pl.BlockSpec((tm, nslot), lambda i: (i, 0))],
        out_shape=[jax.ShapeDtypeStruct((t, d), BF16),
                   jax.ShapeDtypeStruct((t, nslot), I32),
                   jax.ShapeDtypeStruct((t, nslot), I32),
                   jax.ShapeDtypeStruct((t, nslot), F32)],
        compiler_params=_cparams("arbitrary"),
        name="peer_router",
    )(x, mods_l, wq_t, keys)


def _gates_kernel(a_ref, b_ref, g_ref, u_ref, v_ref, o_ref, ub_ref, vb_ref):
    tm = a_ref.shape[0]
    n = PEER_KEYS
    sub = 16
    ub_ref[...] = u_ref[...].astype(BF16)
    vb_ref[...] = v_ref[...].astype(BF16)
    key = _iota((sub, n, a_ref.shape[2]), 1).astype(F32).astype(BF16)
    zero = jnp.zeros((), BF16)
    for t0 in range(0, tm, sub):
        a = a_ref[t0:t0 + sub].astype(F32).astype(BF16)
        b = b_ref[t0:t0 + sub].astype(F32).astype(BF16)
        g = g_ref[t0:t0 + sub].astype(BF16)
        onehot_a = jnp.where(key == a, jnp.ones((), BF16), zero)
        gated_b = jnp.where(key == b, g, zero)
        w = lax.dot_general(onehot_a, gated_b, (((2,), (2,)), ((0,), (0,))),
                            preferred_element_type=F32)
        w_t = jnp.swapaxes(w.astype(BF16), 0, 1)
        for r in range(n):
            o_ref[t0:t0 + sub, r * n:(r + 1) * n] = w_t[r]


def _gates(a_idx, b_idx, gate, peer_u, peer_v, layer):
    t, nslot = a_idx.shape
    d = peer_u.shape[-1]
    tm = 96
    steps = t // tm
    te = N_EXPERTS // steps
    spec = pl.BlockSpec((tm, 1, nslot), lambda i: (i, 0, 0))
    tab_in = pl.BlockSpec((None, te, d), lambda i: (layer, i, 0))
    tab_out = pl.BlockSpec((te, d), lambda i: (i, 0))
    return pl.pallas_call(
        _gates_kernel,
        grid=(steps,),
        in_specs=[spec, spec, spec, tab_in, tab_in],
        out_specs=[pl.BlockSpec((tm, N_EXPERTS), lambda i: (i, 0)), tab_out, tab_out],
        out_shape=[jax.ShapeDtypeStruct((t, N_EXPERTS), BF16),
                   jax.ShapeDtypeStruct((N_EXPERTS, d), BF16),
                   jax.ShapeDtypeStruct((N_EXPERTS, d), BF16)],
        compiler_params=_cparams("arbitrary"),
        name="peer_gates",
    )(a_idx.reshape(t, 1, nslot), b_idx.reshape(t, 1, nslot), gate.reshape(t, 1, nslot), peer_u, peer_v)


def _experts_kernel(h_ref, u_ref, v_ref, w_ref, x_ref, m_ref, g_ref, b_ref, o_ref, acc_ref):
    j = pl.program_id(1)

    @pl.when(j == 0)
    def _():
        acc_ref[...] = jnp.zeros_like(acc_ref)

    act = _gelu_tanh(_dot_nt(h_ref[...], u_ref[...]))
    acc_ref[...] += _dot((act * w_ref[...].astype(F32)).astype(BF16), v_ref[...])

    @pl.when(j == pl.num_programs(1) - 1)
    def _():
        o_ref[...] = _layernorm(ALPHA * x_ref[...] + m_ref[5] * acc_ref[...], g_ref[...], b_ref[...])


def _experts(h_bf, u_bf, v_bf, w_gate, x, mods_l, ln_g, ln_b, t_ctx, s_lat):
    t, d = x.shape
    tm, te = 1024, 1024
    return pl.pallas_call(
        _experts_kernel,
        grid=(t // tm, N_EXPERTS // te),
        in_specs=[pl.BlockSpec((tm, d), lambda i, j: (i, 0)),
                  pl.BlockSpec((te, d), lambda i, j: (j, 0)),
                  pl.BlockSpec((te, d), lambda i, j: (j, 0)),
                  pl.BlockSpec((tm, te), lambda i, j: (i, j)),
                  pl.BlockSpec((tm, d), lambda i, j: (i, 0)),
                  pl.BlockSpec((None, 6, 1, d), _mod_row_map(tm, t_ctx, s_lat)),
                  pl.BlockSpec((1, d), lambda i, j: (0, 0)),
                  pl.BlockSpec((1, d), lambda i, j: (0, 0))],
        out_specs=pl.BlockSpec((tm, d), lambda i, j: (i, 0)),
        out_shape=jax.ShapeDtypeStruct((t, d), F32),
        scratch_shapes=[pltpu.VMEM((tm, d), F32)],
        compiler_params=_cparams("arbitrary", "arbitrary"),
        name="peer_experts",
    )(h_bf, u_bf, v_bf, w_gate, x, mods_l, ln_g, ln_b)


def _pad_cols(w, n):
    return jnp.concatenate([w, jnp.zeros((w.shape[0], n), w.dtype)], axis=1) if n else w


def _layout_w_in(w):
    a = w[:, 0:1280]
    mcq, mckv, mkpe = w[:, 1280:1472], w[:, 1472:1600], w[:, 1600:1632]
    b = jnp.concatenate([mckv, _pad_cols(mcq, 64), mkpe, mkpe, mkpe, mkpe], axis=1)
    c = w[:, 1632:2400]
    d = _pad_cols(w[:, 2400:3176], PROJ_D - 776)
    return jnp.concatenate([a, b, c, d], axis=1).astype(BF16)


def _layout_w_qb(w):
    w4 = w.reshape(MLA_Q_RANK, N_HEADS, MLA_NOPE + MLA_ROPE)
    return jnp.concatenate([w4[:, :, :MLA_NOPE].reshape(MLA_Q_RANK, -1),
                            w4[:, :, MLA_NOPE:].reshape(MLA_Q_RANK, -1)], axis=1).astype(BF16)


def _layout_w_kvb(w):
    w4 = w.reshape(MLA_KV_RANK, N_HEADS, MLA_NOPE + HEAD_DIM)
    return jnp.concatenate([w4[:, :, :MLA_NOPE].reshape(MLA_KV_RANK, -1),
                            w4[:, :, MLA_NOPE:].reshape(MLA_KV_RANK, -1)], axis=1).astype(BF16)


def _rope_tables(seq):
    rows = seq // GRID_W
    row = jnp.repeat(jnp.arange(rows, dtype=F32), GRID_W)
    col = jnp.tile(jnp.arange(GRID_W, dtype=F32), rows)
    freqs = ROPE_BASE ** (-jnp.arange(ROPE_PAIRS, dtype=F32) / ROPE_PAIRS)
    cos_l, sin_l = [], []
    for pos in (row, col):
        ang = pos[:, None] * freqs
        cos_l += [jnp.cos(ang), jnp.cos(ang)]
        sin_l += [-jnp.sin(ang), jnp.sin(ang)]
    return jnp.concatenate(cos_l, axis=1), jnp.concatenate(sin_l, axis=1)


def _hgrn_state_pack(st):
    b = st.shape[0]
    st_t = jnp.swapaxes(st, -1, -2)
    zero = jnp.zeros_like(st_t[:, :, 0])
    rows = [jnp.concatenate([st_t[:, :, h] if g == h else zero for g in range(N_HEADS)], axis=-1)
            for h in range(N_HEADS)]
    return jnp.concatenate(rows, axis=-2).reshape(b, 2, GROUP, GROUP)


def _hgrn_state_unpack(sb):
    blocks = [sb[:, :, h * HEAD_DIM:(h + 1) * HEAD_DIM, h * HEAD_DIM:(h + 1) * HEAD_DIM]
              for h in range(N_HEADS)]
    return jnp.swapaxes(jnp.stack(blocks, axis=2), -1, -2)


def _ssd_state_pack(st):
    st_t = jnp.swapaxes(st, -1, -2)
    zero = jnp.zeros_like(st_t[:, :, 0])
    rows = [jnp.concatenate([st_t[:, :, h] if h // 2 == g else zero for h in range(N_HEADS)], axis=-1)
            for g in range(2)]
    return jnp.concatenate(rows, axis=-2)


def _ssd_state_unpack(sb):
    blocks = [sb[:, :, (h // 2) * SSD_STATE:(h // 2 + 1) * SSD_STATE, h * HEAD_DIM:(h + 1) * HEAD_DIM]
              for h in range(N_HEADS)]
    return jnp.swapaxes(jnp.stack(blocks, axis=2), -1, -2)


def _tile_lanes(v, n):
    return jnp.tile(v.reshape(1, -1), (1, n))


def kernel(x_prompt, x_sample, cache_mla_ckv, cache_mla_kpe, cache_diff_k, cache_diff_v, state_hgrn, state_ssd, c, c_ctx, w_mod, b_mod, w_in, hgrn_lb, hgrn_norm, mla_q_norm, mla_w_qb, mla_kv_norm, mla_w_kvb, diff_lambda, diff_norm, ssd_conv_w, ssd_conv_b, ssd_dt_bias, ssd_a_log, ssd_d, ssd_norm, w_out, ln1_g, ln1_b, peer_wq, peer_keys, peer_u, peer_v, ln2_g, ln2_b):
    nb, seq, d = x_prompt.shape
    nlat, lseq, _ = x_sample.shape
    depth = w_in.shape[0]
    t_ctx = nb * seq
    x = jnp.concatenate([x_prompt.reshape(t_ctx, d), x_sample.reshape(nlat * lseq, d)], axis=0)

    cond8 = jnp.concatenate([c_ctx.reshape(1, d), c, jnp.zeros((8 - 1 - nlat, d), F32)], axis=0)
    mods = _mods(cond8, w_mod, b_mod)
    mods = mods[:, :1 + nlat].reshape(depth, 1 + nlat, 6, 1, d)

    cos32, sin32 = _rope_tables(lseq)
    cos128, sin128 = jnp.tile(cos32, (1, 4)), jnp.tile(sin32, (1, 4))
    cos256, sin256 = jnp.tile(cos32, (1, 8)), jnp.tile(sin32, (1, 8))

    produced = []
    for l in range(depth):
        mods_l = mods[l]
        pa, pb, pc, pd = _inproj(x, mods_l, _layout_w_in(w_in[l]), t_ctx, lseq)

        norm_hg = _tile_lanes(hgrn_norm[l], N_HEADS)
        mla_w = (mla_q_norm[l].reshape(1, -1), _layout_w_qb(mla_w_qb[l]),
                 mla_kv_norm[l].reshape(1, -1), _layout_w_kvb(mla_w_kvb[l]))
        lam_init = 0.8 - 0.6 * math.exp(-0.3 * l)
        norm_df = _tile_lanes(diff_norm[l], N_HEADS)
        past = cache_diff_k.shape[2]
        ssd_w = (ssd_conv_w[l], ssd_conv_b[l].reshape(1, -1),
                 _pad_cols(ssd_dt_bias[l].reshape(1, -1), 120), _pad_cols(ssd_a_log[l].reshape(1, -1), 120),
                 jnp.repeat(ssd_d[l], HEAD_DIM).reshape(1, -1), ssd_norm[l].reshape(1, -1))

        (mla_ctx, new_ckv, new_kpe), (df_ctx, new_dk, new_dv), (ssd_ctx, ssd_fin), (hg_ctx, hg_fin) = _mixers(
            nb, "mixers_ctx",
            _mla_parts(pb, 0, nb, seq, *mla_w, None),
            _diff_parts(pc, 0, nb, seq, diff_lambda[l], norm_df, lam_init, None),
            _ssd_parts(pd, 0, nb, seq, *ssd_w, None),
            _hgrn_parts(pa, 0, nb, seq, hgrn_lb, norm_hg, None, l))
        (mla_lat,), (df_lat,) = _mixers(
            nlat, "attention_lat",
            _mla_parts(pb, t_ctx, nlat, lseq, *mla_w,
                       (cache_mla_ckv[:, l], jnp.tile(cache_mla_kpe[:, l], (1, 1, 4)), cos128, sin128)),
            _diff_parts(pc, t_ctx, nlat, lseq, diff_lambda[l], norm_df, lam_init,
                        (cache_diff_k[:, l].reshape(nlat, past, GROUP),
                         cache_diff_v[:, l].reshape(nlat, past, GROUP), cos256, sin256)))
        ((ssd_lat, _),) = _mixers(nlat, "ssd_lat",
                                  _ssd_parts(pd, t_ctx, nlat, lseq, *ssd_w, _ssd_state_pack(state_ssd[:, l])))
        ((hg_lat, _),) = _mixers(nlat, "hgrn_lat",
                                 _hgrn_parts(pa, t_ctx, nlat, lseq, hgrn_lb, norm_hg,
                                             _hgrn_state_pack(state_hgrn[:, l]), l))

        parts_ctx = [a.reshape(t_ctx, GROUP) for a in (hg_ctx, mla_ctx, df_ctx, ssd_ctx)]
        parts_lat = [a.reshape(nlat * lseq, GROUP) for a in (hg_lat, mla_lat, df_lat, ssd_lat)]
        x = _outproj(parts_ctx, parts_lat, w_out[l].astype(BF16), x, mods_l,
                     ln1_g[l].reshape(1, d), ln1_b[l].reshape(1, d), t_ctx, lseq)

        keys = peer_keys[l].reshape(2 * PEER_HEADS, PEER_KEYS, PEER_HALF).astype(BF16)
        h_bf, a_idx, b_idx, gate = _router(x, mods_l, peer_wq[l].T.astype(BF16), keys, t_ctx, lseq)
        w_gate, u_bf, v_bf = _gates(a_idx, b_idx, gate, peer_u, peer_v, l)
        x = _experts(h_bf, u_bf, v_bf, w_gate, x, mods_l,
                     ln2_g[l].reshape(1, d), ln2_b[l].reshape(1, d), t_ctx, lseq)

        produced.append((new_ckv, new_kpe,
                         new_dk.reshape(nb, seq, N_HEADS, 2, DIFF_DIM),
                         new_dv.reshape(nb, seq, N_HEADS, 2 * DIFF_DIM),
                         _hgrn_state_unpack(hg_fin), _ssd_state_unpack(ssd_fin)))

    y_prompt = x[:t_ctx].reshape(nb, seq, d)
    y_sample = x[t_ctx:].reshape(nlat, lseq, d)
    stacked = tuple(jnp.stack([p[i] for p in produced], axis=1) for i in range(6))
    return (y_prompt, y_sample) + stacked
```

```python
import functools
import math

import jax
import jax.numpy as jnp
from jax import lax
from jax.experimental import pallas as pl
from jax.experimental.pallas import tpu as pltpu

F32 = jnp.float32
BF16 = jnp.bfloat16
I32 = jnp.int32
HIGHEST = lax.Precision.HIGHEST

D_MODEL = 1024
GROUP = 256
N_HEADS = 4
HEAD_DIM = 64
HG_BLOCK = 16
HG_SLAB = 256
HG_UNROLL = 8
SSD_CHUNK = 128
SSD_STATE = 64
MLA_Q_RANK = 192
MLA_KV_RANK = 128
MLA_NOPE = 64
MLA_ROPE = 32
DIFF_DIM = 32
GRID_W = 64
ROPE_PAIRS = 8
ROPE_BASE = 10000.0
PEER_HEADS = 8
PEER_KEYS = 128
PEER_TOPK = 16
PEER_HALF = 64
N_EXPERTS = PEER_KEYS * PEER_KEYS
NORM_EPS = 1e-6
LN_EPS = 1e-5
DEPTH = 2
ALPHA = (2.0 * DEPTH) ** 0.25
LOG2_E = 1.4426950408889634

PROJ_A = 5 * GROUP
PROJ_B = 512
PROJ_C = 3 * GROUP
PROJ_D = 896
VMEM_LIMIT = 56 * 1024 * 1024


def _cparams(*sem):
    return pltpu.CompilerParams(dimension_semantics=sem, vmem_limit_bytes=VMEM_LIMIT)


def _sigmoid(x):
    return 1.0 / (1.0 + jnp.exp(-x))


def _silu(x):
    return x * _sigmoid(x)


def _softplus(x):
    return jnp.maximum(x, 0.0) + jnp.log(1.0 + jnp.exp(-jnp.abs(x)))


def _gelu_tanh(x):
    return 0.5 * x * (1.0 + jnp.tanh(math.sqrt(2.0 / math.pi) * (x + 0.044715 * (x * x * x))))


def _dot(a, b, precision=None):
    return jnp.dot(a, b, preferred_element_type=F32, precision=precision)


def _dot_nt(a, b, precision=None):
    return lax.dot_general(a, b, (((1,), (1,)), ((), ())), preferred_element_type=F32,
                           precision=precision)


def _dot_tn(a, b, precision=None):
    return lax.dot_general(a, b, (((0,), (0,)), ((), ())), preferred_element_type=F32,
                           precision=precision)


def _iota(shape, dim):
    return lax.broadcasted_iota(I32, shape, dim)


def _block_mask(rows, cols, rblk, cblk):
    return (_iota((rows, cols), 0) // rblk) == (_iota((rows, cols), 1) // cblk)


def _lane_group_mask(width, start, size):
    lane = _iota((1, width), 1)
    return (lane >= start) & (lane < start + size)


def _layernorm(v, g, b):
    mu = jnp.mean(v, axis=-1, keepdims=True)
    d = v - mu
    var = jnp.mean(d * d, axis=-1, keepdims=True)
    return d * lax.rsqrt(var + LN_EPS) * g + b


def _swap_halves16(x):
    width = x.shape[-1]
    lane = _iota(x.shape, x.ndim - 1)
    up = pltpu.roll(x, width - 8, x.ndim - 1)
    down = pltpu.roll(x, 8, x.ndim - 1)
    return jnp.where((lane % 16) < 8, up, down)


def _rope(x, cos, sin_signed):
    return x * cos + _swap_halves16(x) * sin_signed


def _mods_kernel(c_ref, w_ref, b_ref, o_ref):
    s = _silu(c_ref[...]).astype(BF16)
    o_ref[...] = _dot(s, w_ref[...].astype(BF16)) + b_ref[...]


def _mods(cond8, w_mod, b_mod):
    depth, d, n = w_mod.shape
    tn = 1536
    return pl.pallas_call(
        _mods_kernel,
        grid=(depth, n // tn),
        in_specs=[pl.BlockSpec((8, d), lambda l, j: (0, 0)),
                  pl.BlockSpec((None, d, tn), lambda l, j: (l, 0, j)),
                  pl.BlockSpec((None, 1, tn), lambda l, j: (l, 0, j))],
        out_specs=pl.BlockSpec((None, 8, tn), lambda l, j: (l, 0, j)),
        out_shape=jax.ShapeDtypeStruct((depth, 8, n), F32),
        compiler_params=_cparams("arbitrary", "arbitrary"),
        name="mods",
    )(cond8, w_mod, b_mod.reshape(depth, 1, n))


def _mod_row_map(tm, t_ctx, s_lat):
    def index_map(i, *_):
        start = i * tm
        return (jnp.where(start < t_ctx, 0, 1 + (start - t_ctx) // s_lat), 0, 0, 0)
    return index_map


def _inproj_kernel(x_ref, m_ref, w_ref, oa_ref, ob_ref, oc_ref, od_ref):
    h = (x_ref[...] * (1.0 + m_ref[1]) + m_ref[0]).astype(BF16)
    start = 0
    for o_ref in (oa_ref, ob_ref, oc_ref, od_ref):
        width = o_ref.shape[-1]
        o_ref[...] = _dot(h, w_ref[:, start:start + width])
        start += width


def _inproj(x, mods_l, w_in_p, t_ctx, s_lat):
    t, d = x.shape
    tm = 512
    widths = (PROJ_A, PROJ_B, PROJ_C, PROJ_D)
    return pl.pallas_call(
        _inproj_kernel,
        grid=(t // tm,),
        in_specs=[pl.BlockSpec((tm, d), lambda i: (i, 0)),
                  pl.BlockSpec((None, 6, 1, d), _mod_row_map(tm, t_ctx, s_lat)),
                  pl.BlockSpec(w_in_p.shape, lambda i: (0, 0))],
        out_specs=[pl.BlockSpec((tm, w), lambda i: (i, 0)) for w in widths],
        out_shape=[jax.ShapeDtypeStruct((t, w), F32) for w in widths],
        compiler_params=_cparams("arbitrary"),
        name="inproj",
    )(x, mods_l, w_in_p)


def _hgrn_kernel(layer, has_state, *refs):
    if has_state:
        (a_ref, lb_ref, norm_ref, s0_ref, o_ref, sfin_ref,
         q_scr, k_scr, bc_scr, dec_scr, qt_scr, kt_scr, st_scr, o_scr) = refs
    else:
        (a_ref, lb_ref, norm_ref, o_ref, sfin_ref,
         q_scr, k_scr, bc_scr, dec_scr, qt_scr, kt_scr, st_scr, o_scr) = refs
        s0_ref = None
    seq = a_ref.shape[0]
    c = HG_BLOCK
    nblk = seq // c
    slab = HG_SLAB
    nb = slab // c

    lbp = lb_ref[...]
    e = jnp.exp(lbp - jnp.max(lbp, axis=0, keepdims=True))
    p = e / jnp.sum(e, axis=0, keepdims=True)
    lower = jnp.sum(p[1:layer + 1], axis=0) if layer > 0 else jnp.zeros_like(p[0])

    q = _silu(a_ref[:, 0:GROUP])
    q_scr[...] = q
    srow = _iota((slab, slab), 0)
    scol = _iota((slab, slab), 1)
    same = (srow // c) == (scol // c)
    cum_op = (jnp.where(same & (scol <= srow), 1.0, 0.0), jnp.where(same & (scol >= srow), 1.0, 0.0))
    for d in range(2):
        lb = lower[d:d + 1]
        f = lb + (1.0 - lb) * _sigmoid(a_ref[:, (1 + d) * GROUP:(2 + d) * GROUP])
        k = 1.0 - f
        lf = jnp.log(f)
        k_scr[d] = k
        for s0 in range(0, seq, slab):
            bc = _dot(cum_op[d], lf[s0:s0 + slab], precision=HIGHEST)
            bc3 = bc.reshape(nb, c, GROUP)
            edge = bc3[:, c - 1:c, :] if d == 0 else bc3[:, 0:1, :]
            tot = jnp.broadcast_to(edge, (nb, c, GROUP)).reshape(slab, GROUP)
            bc_scr[d, s0:s0 + slab, :] = bc * LOG2_E
            dec_scr[d, s0:s0 + slab, :] = jnp.exp(tot)
            qt_scr[d, s0:s0 + slab, :] = (q[s0:s0 + slab] * jnp.exp(bc)).astype(BF16)
            kt_scr[d, s0:s0 + slab, :] = (k[s0:s0 + slab] * jnp.exp(tot - bc)).astype(BF16)
    if has_state:
        st_scr[...] = s0_ref[...]
    else:
        st_scr[...] = jnp.zeros_like(st_scr)

    bd_ones = _block_mask(GROUP, GROUP, HEAD_DIM, HEAD_DIM).astype(BF16)
    rib = _iota((1, c, GROUP), 1)

    def slab_step(i, carry):
        r0 = pl.multiple_of(i * slab, slab)
        q3 = q_scr[pl.ds(r0, slab), :].reshape(nb, c, GROUP)
        v3 = a_ref[pl.ds(r0, slab), 3 * GROUP:4 * GROUP].reshape(nb, c, GROUP)
        o3 = jnp.zeros((nb, c, GROUP), F32)
        for d in range(2):
            bc3 = bc_scr[d, pl.ds(r0, slab), :].reshape(nb, c, GROUP)
            k3 = k_scr[d, pl.ds(r0, slab), :].reshape(nb, c, GROUP)
            for j in range(c):
                keep = (rib >= j) if d == 0 else (rib <= j)
                dec = jnp.exp2(jnp.where(keep, bc3 - bc3[:, j:j + 1, :], -jnp.inf))
                pj = (dec * q3 * k3[:, j:j + 1, :]).astype(BF16).reshape(slab, GROUP)
                srep = _dot(pj, bd_ones).reshape(nb, c, GROUP)
                o3 = o3 + srep * v3[:, j:j + 1, :]
        o_scr[0, pl.ds(r0, slab), :] = o3.reshape(slab, GROUP)
        return carry

    lax.fori_loop(0, seq // slab, slab_step, 0)

    bd_mask = _block_mask(GROUP, GROUP, HEAD_DIM, HEAD_DIM)

    def body(n, carry):
        rows = [[pl.multiple_of(((n * HG_UNROLL + u) if d == 0 else nblk - 1 - (n * HG_UNROLL + u)) * c, c)
                 for u in range(HG_UNROLL)] for d in range(2)]
        upd = [[_dot_tn(a_ref[pl.ds(r0, c), 3 * GROUP:4 * GROUP].astype(BF16), kt_scr[d, pl.ds(r0, c), :])
                for r0 in rows[d]] for d in range(2)]
        for d in range(2):
            st = st_scr[d]
            for u, r0 in enumerate(rows[d]):
                o_scr[1 + d, pl.ds(r0, c), :] = _dot_nt(qt_scr[d, pl.ds(r0, c), :], st.astype(BF16))
                st = st * dec_scr[d, pl.ds(r0, 1), :] + jnp.where(bd_mask, upd[d][u], 0.0)
            st_scr[d] = st
        return carry

    lax.fori_loop(0, nblk // HG_UNROLL, body, 0)

    o = o_scr[0] + o_scr[1] + o_scr[2]
    mean_op = jnp.where(_block_mask(GROUP, GROUP, HEAD_DIM, HEAD_DIM), 1.0 / HEAD_DIM, 0.0)
    ms = _dot(o * o, mean_op, precision=HIGHEST)
    y = o * lax.rsqrt(ms + NORM_EPS) * norm_ref[...]
    o_ref[...] = y * _silu(a_ref[:, 4 * GROUP:5 * GROUP])
    sfin_ref[...] = st_scr[...]


def _hgrn_parts(proj_a, row0, nseq, seq, hgrn_lb, norm_t, s0, layer):
    has_state = s0 is not None
    blk0 = row0 // seq
    in_specs = [pl.BlockSpec((seq, PROJ_A), lambda b: (blk0 + b, 0)),
                pl.BlockSpec(hgrn_lb.shape, lambda b: (0, 0, 0)),
                pl.BlockSpec((1, GROUP), lambda b: (0, 0))]
    args = [proj_a, hgrn_lb, norm_t]
    if has_state:
        in_specs.append(pl.BlockSpec((None, 2, GROUP, GROUP), lambda b: (b, 0, 0, 0)))
        args.append(s0)
    out_specs = [pl.BlockSpec((None, seq, GROUP), lambda b: (b, 0, 0)),
                 pl.BlockSpec((None, 2, GROUP, GROUP), lambda b: (b, 0, 0, 0))]
    out_shape = [jax.ShapeDtypeStruct((nseq, seq, GROUP), F32),
                 jax.ShapeDtypeStruct((nseq, 2, GROUP, GROUP), F32)]
    scratch = [pltpu.VMEM((seq, GROUP), F32),
               pltpu.VMEM((2, seq, GROUP), F32),
               pltpu.VMEM((2, seq, GROUP), F32),
               pltpu.VMEM((2, seq, GROUP), F32),
               pltpu.VMEM((2, seq, GROUP), BF16),
               pltpu.VMEM((2, seq, GROUP), BF16),
               pltpu.VMEM((2, GROUP, GROUP), F32),
               pltpu.VMEM((3, seq, GROUP), F32)]
    return functools.partial(_hgrn_kernel, layer, has_state), in_specs, args, out_specs, out_shape, scratch


def _mla_kernel(latent, *refs):
    if latent:
        (b_ref, qn_ref, wq_ref, kvn_ref, wkv_ref, cckv_ref, ckpe_ref, cos_ref, sin_ref,
         o_ref) = refs
    else:
        (b_ref, qn_ref, wq_ref, kvn_ref, wkv_ref, o_ref, ckv_ref, kpe_ref) = refs
    seq = b_ref.shape[0]
    mckv = b_ref[:, 0:MLA_KV_RANK]
    mcq = b_ref[:, MLA_KV_RANK:MLA_KV_RANK + MLA_Q_RANK]
    kpe_t = b_ref[:, 384:512]

    cq = mcq * lax.rsqrt(jnp.mean(mcq * mcq, axis=-1, keepdims=True) + NORM_EPS) * qn_ref[...]
    qf = _dot(cq.astype(BF16), wq_ref[...])
    ckv = mckv * lax.rsqrt(jnp.mean(mckv * mckv, axis=-1, keepdims=True) + NORM_EPS) * kvn_ref[...]
    q_nope = qf[:, 0:N_HEADS * MLA_NOPE]
    q_rope = qf[:, N_HEADS * MLA_NOPE:]
    if latent:
        cos = cos_ref[...]
        sin = sin_ref[...]
        q_rope = _rope(q_rope, cos, sin)
        ckv_all = jnp.concatenate([cckv_ref[...], ckv], axis=0)
        kpe_all = jnp.concatenate([ckpe_ref[...], _rope(kpe_t, cos, sin)], axis=0)
    else:
        ckv_ref[...] = ckv
        kpe_ref[...] = kpe_t[:, 0:MLA_ROPE]
        ckv_all = ckv
        kpe_all = kpe_t
    kv = _dot(ckv_all.astype(BF16), wkv_ref[...])
    kcat = jnp.concatenate([kv[:, 0:GROUP], kpe_all], axis=1).astype(BF16)
    v = kv[:, GROUP:].astype(BF16)
    qcat = jnp.concatenate([q_nope, q_rope], axis=1)
    scale = (MLA_NOPE + MLA_ROPE) ** -0.5
    qb = min(seq, 256)
    width = qcat.shape[1]
    for r0 in range(0, seq, qb):
        qblk = qcat[r0:r0 + qb]
        acc = jnp.zeros((qb, GROUP), F32)
        for h in range(N_HEADS):
            hm = (_lane_group_mask(width, h * MLA_NOPE, MLA_NOPE)
                  | _lane_group_mask(width, N_HEADS * MLA_NOPE + h * MLA_ROPE, MLA_ROPE))
            s = _dot_nt(jnp.where(hm, qblk, 0.0).astype(BF16), kcat) * scale
            e = jnp.exp(s - jnp.max(s, axis=-1, keepdims=True))
            z = jnp.sum(e, axis=-1, keepdims=True)
            oh = _dot(e.astype(BF16), v) / z
            acc = acc + jnp.where(_lane_group_mask(GROUP, h * HEAD_DIM, HEAD_DIM), oh, 0.0)
        o_ref[r0:r0 + qb, :] = acc


def _mla_parts(proj_b, row0, nseq, seq, q_norm, w_qb_p, kv_norm, w_kvb_p, latent_args):
    latent = latent_args is not None
    blk0 = row0 // seq
    in_specs = [pl.BlockSpec((seq, PROJ_B), lambda b: (blk0 + b, 0)),
                pl.BlockSpec(q_norm.shape, lambda b: (0, 0)),
                pl.BlockSpec(w_qb_p.shape, lambda b: (0, 0)),
                pl.BlockSpec(kv_norm.shape, lambda b: (0, 0)),
                pl.BlockSpec(w_kvb_p.shape, lambda b: (0, 0))]
    args = [proj_b, q_norm, w_qb_p, kv_norm, w_kvb_p]
    out_specs = [pl.BlockSpec((None, seq, GROUP), lambda b: (b, 0, 0))]
    out_shape = [jax.ShapeDtypeStruct((nseq, seq, GROUP), F32)]
    if latent:
        cckv, ckpe_t, cos, sin = latent_args
        past = cckv.shape[1]
        in_specs += [pl.BlockSpec((None, past, MLA_KV_RANK), lambda b: (b, 0, 0)),
                     pl.BlockSpec((None, past, 128), lambda b: (b, 0, 0)),
                     pl.BlockSpec(cos.shape, lambda b: (0, 0)),
                     pl.BlockSpec(sin.shape, lambda b: (0, 0))]
        args += [cckv, ckpe_t, cos, sin]
    else:
        out_specs += [pl.BlockSpec((None, seq, MLA_KV_RANK), lambda b: (b, 0, 0)),
                      pl.BlockSpec((None, seq, MLA_ROPE), lambda b: (b, 0, 0))]
        out_shape += [jax.ShapeDtypeStruct((nseq, seq, MLA_KV_RANK), F32),
                      jax.ShapeDtypeStruct((nseq, seq, MLA_ROPE), F32)]
    return functools.partial(_mla_kernel, latent), in_specs, args, out_specs, out_shape, []


def _diff_kernel(latent, lam_init, *refs):
    if latent:
        (c_ref, lam_ref, norm_ref, ck_ref, cv_ref, cos_ref, sin_ref, o_ref) = refs
    else:
        (c_ref, lam_ref, norm_ref, o_ref, k_ref, v_ref) = refs
    seq = c_ref.shape[0]
    dq = c_ref[:, 0:GROUP]
    dk = c_ref[:, GROUP:2 * GROUP]
    dv = c_ref[:, 2 * GROUP:3 * GROUP]
    if latent:
        cos = cos_ref[...]
        sin = sin_ref[...]
        dq = _rope(dq, cos, sin)
        k_all = jnp.concatenate([ck_ref[...], _rope(dk, cos, sin)], axis=0)
        v_all = jnp.concatenate([cv_ref[...], dv], axis=0)
    else:
        k_ref[...] = dk
        v_ref[...] = dv
        k_all = dk
        v_all = dv
    lv = lam_ref[...]
    lam = (jnp.exp(jnp.sum(lv[0:1] * lv[1:2], axis=-1, keepdims=True))
           - jnp.exp(jnp.sum(lv[2:3] * lv[3:4], axis=-1, keepdims=True)) + lam_init)
    k_bf = k_all.astype(BF16)
    v_bf = v_all.astype(BF16)
    scale = DIFF_DIM ** -0.5
    mean_op = jnp.where(_block_mask(GROUP, GROUP, HEAD_DIM, HEAD_DIM), 1.0 / HEAD_DIM, 0.0)
    qb = min(seq, 256)
    for r0 in range(0, seq, qb):
        qblk = dq[r0:r0 + qb]
        acc = jnp.zeros((qb, GROUP), F32)
        for h in range(N_HEADS):
            probs = []
            for comp in range(2):
                cm = _lane_group_mask(GROUP, h * HEAD_DIM + comp * DIFF_DIM, DIFF_DIM)
                s = _dot_nt(jnp.where(cm, qblk, 0.0).astype(BF16), k_bf) * scale
                e = jnp.exp(s - jnp.max(s, axis=-1, keepdims=True))
                probs.append(e / jnp.sum(e, axis=-1, keepdims=True))
            w = (probs[0] - lam * probs[1]).astype(BF16)
            acc = acc + jnp.where(_lane_group_mask(GROUP, h * HEAD_DIM, HEAD_DIM), _dot(w, v_bf), 0.0)
        ms = _dot(acc * acc, mean_op, precision=HIGHEST)
        o_ref[r0:r0 + qb, :] = acc * lax.rsqrt(ms + NORM_EPS) * norm_ref[...] * (1.0 - lam_init)


def _diff_parts(proj_c, row0, nseq, seq, lam_p, norm_t, lam_init, latent_args):
    latent = latent_args is not None
    blk0 = row0 // seq
    in_specs = [pl.BlockSpec((seq, PROJ_C), lambda b: (blk0 + b, 0)),
                pl.BlockSpec(lam_p.shape, lambda b: (0, 0)),
                pl.BlockSpec(norm_t.shape, lambda b: (0, 0))]
    args = [proj_c, lam_p, norm_t]
    out_specs = [pl.BlockSpec((None, seq, GROUP), lambda b: (b, 0, 0))]
    out_shape = [jax.ShapeDtypeStruct((nseq, seq, GROUP), F32)]
    if latent:
        ck, cv, cos, sin = latent_args
        past = ck.shape[1]
        in_specs += [pl.BlockSpec((None, past, GROUP), lambda b: (b, 0, 0)),
                     pl.BlockSpec((None, past, GROUP), lambda b: (b, 0, 0)),
                     pl.BlockSpec(cos.shape, lambda b: (0, 0)),
                     pl.BlockSpec(sin.shape, lambda b: (0, 0))]
        args += [ck, cv, cos, sin]
    else:
        out_specs += [pl.BlockSpec((None, seq, GROUP), lambda b: (b, 0, 0))] * 2
        out_shape += [jax.ShapeDtypeStruct((nseq, seq, GROUP), F32)] * 2
    return functools.partial(_diff_kernel, latent, lam_init), in_specs, args, out_specs, out_shape, []


def _mixers(nseq, name, *parts):
    n_in = [len(p[1]) for p in parts]
    n_out = [len(p[3]) for p in parts]
    n_scr = [len(p[5]) for p in parts]

    def body(*refs):
        ins = refs[:sum(n_in)]
        outs = refs[sum(n_in):sum(n_in) + sum(n_out)]
        scr = refs[sum(n_in) + sum(n_out):]
        i0 = o0 = s0 = 0
        for part, ni, no, ns in zip(parts, n_in, n_out, n_scr):
            part[0](*ins[i0:i0 + ni], *outs[o0:o0 + no], *scr[s0:s0 + ns])
            i0, o0, s0 = i0 + ni, o0 + no, s0 + ns

    res = pl.pallas_call(
        body,
        grid=(nseq,),
        in_specs=[x for p in parts for x in p[1]],
        out_specs=[x for p in parts for x in p[3]],
        out_shape=[x for p in parts for x in p[4]],
        scratch_shapes=[x for p in parts for x in p[5]],
        compiler_params=_cparams("arbitrary"),
        name=name,
    )(*[x for p in parts for x in p[2]])
    out, o0 = [], 0
    for no in n_out:
        out.append(tuple(res[o0:o0 + no]))
        o0 += no
    return out


def _ssd_kernel(has_state, *refs):
    if has_state:
        (d_ref, cw_ref, cb_ref, dtb_ref, alog_ref, dskip_ref, norm_ref, s0_ref,
         o_ref, sfin_ref, xs_scr, bm_scr, cm_scr, xdt_scr, a_scr, st_scr, yf_scr, yb_scr) = refs
    else:
        (d_ref, cw_ref, cb_ref, dtb_ref, alog_ref, dskip_ref, norm_ref,
         o_ref, sfin_ref, xs_scr, bm_scr, cm_scr, xdt_scr, a_scr, st_scr, yf_scr, yb_scr) = refs
    seq = d_ref.shape[0]
    c = SSD_CHUNK
    nchunk = seq // c
    ngrp = 2 * SSD_STATE

    xin = d_ref[:, GROUP:GROUP + 512]
    rows = _iota(xin.shape, 0)
    prev = jnp.where(rows == 0, 0.0, pltpu.roll(xin, 1, 0))
    nxt = jnp.where(rows == seq - 1, 0.0, pltpu.roll(xin, seq - 1, 0))
    cw = cw_ref[...]
    xbc = _silu(cw[0:1] * prev + cw[1:2] * xin + cw[2:3] * nxt + cb_ref[...])
    xs = xbc[:, 0:GROUP]
    xs_scr[...] = xs
    bm_scr[...] = xbc[:, GROUP:GROUP + ngrp]
    cm_scr[...] = xbc[:, GROUP + ngrp:GROUP + 2 * ngrp]
    dt = _softplus(d_ref[:, GROUP + 512:GROUP + 640] + dtb_ref[...])
    a_scr[...] = dt * (-jnp.exp(alog_ref[...]))
    erow = _iota((128, GROUP), 0)
    ehead = _iota((128, GROUP), 1) // HEAD_DIM
    expand = tuple((erow == 4 * d + ehead).astype(F32) for d in range(2))
    for d in range(2):
        xdt_scr[d] = xs * _dot(dt, expand[d], precision=HIGHEST)
    if has_state:
        st_scr[...] = s0_ref[...]
    else:
        st_scr[...] = jnp.zeros_like(st_scr)

    row = _iota((c, c), 0)
    col = _iota((c, c), 1)
    tri = ((col <= row).astype(F32), (col >= row).astype(F32))
    keep = (col <= row, col >= row)
    grp_lane = _iota((1, ngrp), 1) // SSD_STATE
    valid = (_iota((ngrp, GROUP), 0) // SSD_STATE) == (_iota((ngrp, GROUP), 1) // (2 * HEAD_DIM))

    def chunk_step(d, r0, out_scr):
        a_c = a_scr[pl.ds(r0, c), :]
        bm_c = bm_scr[pl.ds(r0, c), :]
        cm_c = cm_scr[pl.ds(r0, c), :].astype(BF16)
        xdt_c = xdt_scr[d, pl.ds(r0, c), :]
        acum = _dot(tri[d], a_c, precision=HIGHEST)
        acum_t = acum.T
        acum_rep = _dot(acum, expand[d], precision=HIGHEST)
        bm2 = jnp.concatenate([jnp.where(grp_lane == g, bm_c, 0.0) for g in range(2)], axis=0)
        cb = _dot_nt(cm_c, bm2.astype(BF16))
        scores = []
        xparts = []
        for h in range(N_HEADS):
            lane = 4 * d + h
            seg = jnp.exp(jnp.where(keep[d], acum[:, lane:lane + 1] - acum_t[lane:lane + 1, :], -jnp.inf))
            g = h // 2
            scores.append((cb[:, g * c:(g + 1) * c] * seg).astype(BF16))
            xparts.append(jnp.where(_lane_group_mask(GROUP, h * HEAD_DIM, HEAD_DIM), xdt_c, 0.0))
        y = _dot(jnp.concatenate(scores, axis=1), jnp.concatenate(xparts, axis=0).astype(BF16))
        st = st_scr[d]
        y = y + _dot(cm_c, st.astype(BF16)) * jnp.exp(acum_rep)
        out_scr[pl.ds(r0, c), :] = y
        edge = acum_rep[c - 1:c] if d == 0 else acum_rep[0:1]
        xt = (xdt_c * jnp.exp(edge - acum_rep)).astype(BF16)
        upd = _dot_tn(bm_c.astype(BF16), xt)
        st_scr[d] = st * jnp.exp(edge) + jnp.where(valid, upd, 0.0)

    def body(n, carry):
        chunk_step(0, pl.multiple_of(n * c, c), yf_scr)
        chunk_step(1, pl.multiple_of((nchunk - 1 - n) * c, c), yb_scr)
        return carry

    lax.fori_loop(0, nchunk, body, 0)

    y = yf_scr[...] + yb_scr[...] + dskip_ref[...] * xs_scr[...]
    y = y * _silu(d_ref[:, 0:GROUP])
    o_ref[...] = y * lax.rsqrt(jnp.mean(y * y, axis=-1, keepdims=True) + NORM_EPS) * norm_ref[...]
    sfin_ref[...] = st_scr[...]


def _ssd_parts(proj_d, row0, nseq, seq, conv_w, conv_b, dt_bias_p, a_log_p, d_rep, norm, s0):
    has_state = s0 is not None
    blk0 = row0 // seq
    ngrp = 2 * SSD_STATE
    small = [conv_w, conv_b, dt_bias_p, a_log_p, d_rep, norm]
    in_specs = ([pl.BlockSpec((seq, PROJ_D), lambda b: (blk0 + b, 0))]
                + [pl.BlockSpec(s.shape, lambda b: (0, 0)) for s in small])
    args = [proj_d] + small
    if has_state:
        in_specs.append(pl.BlockSpec((None, 2, ngrp, GROUP), lambda b: (b, 0, 0, 0)))
        args.append(s0)
    out_specs = [pl.BlockSpec((None, seq, GROUP), lambda b: (b, 0, 0)),
                 pl.BlockSpec((None, 2, ngrp, GROUP), lambda b: (b, 0, 0, 0))]
    out_shape = [jax.ShapeDtypeStruct((nseq, seq, GROUP), F32),
                 jax.ShapeDtypeStruct((nseq, 2, ngrp, GROUP), F32)]
    scratch = [pltpu.VMEM((seq, GROUP), F32),
               pltpu.VMEM((seq, ngrp), F32),
               pltpu.VMEM((seq, ngrp), F32),
               pltpu.VMEM((2, seq, GROUP), F32),
               pltpu.VMEM((seq, 128), F32),
               pltpu.VMEM((2, ngrp, GROUP), F32),
               pltpu.VMEM((seq, GROUP), F32),
               pltpu.VMEM((seq, GROUP), F32)]
    return functools.partial(_ssd_kernel, has_state), in_specs, args, out_specs, out_shape, scratch


def _outproj_kernel(n_ctx_tiles, *refs):
    ctx_refs, lat_refs = refs[0:4], refs[4:8]
    w_ref, x_ref, m_ref, g_ref, b_ref, o_ref = refs[8:]
    is_ctx = pl.program_id(0) < n_ctx_tiles
    mixed = None
    for i, (c_ref, l_ref) in enumerate(zip(ctx_refs, lat_refs)):
        part = jnp.where(is_ctx, c_ref[...], l_ref[...]).astype(BF16)
        term = _dot(part, w_ref[i * GROUP:(i + 1) * GROUP, :])
        mixed = term if mixed is None else mixed + term
    o_ref[...] = _layernorm(ALPHA * x_ref[...] + m_ref[2] * mixed, g_ref[...], b_ref[...])


def _outproj(parts_ctx, parts_lat, w_out_bf, x, mods_l, ln_g, ln_b, t_ctx, s_lat):
    t, d = x.shape
    tm = 512
    n_ctx = t_ctx // tm
    n_lat = (t - t_ctx) // tm
    ctx_spec = pl.BlockSpec((tm, GROUP), lambda i: (jnp.minimum(i, n_ctx - 1), 0))
    lat_spec = pl.BlockSpec((tm, GROUP), lambda i: (jnp.clip(i - n_ctx, 0, n_lat - 1), 0))
    return pl.pallas_call(
        functools.partial(_outproj_kernel, n_ctx),
        grid=(t // tm,),
        in_specs=[ctx_spec] * 4 + [lat_spec] * 4
        + [pl.BlockSpec(w_out_bf.shape, lambda i: (0, 0)),
           pl.BlockSpec((tm, d), lambda i: (i, 0)),
           pl.BlockSpec((None, 6, 1, d), _mod_row_map(tm, t_ctx, s_lat)),
           pl.BlockSpec((1, d), lambda i: (0, 0)),
           pl.BlockSpec((1, d), lambda i: (0, 0))],
        out_specs=pl.BlockSpec((tm, d), lambda i: (i, 0)),
        out_shape=jax.ShapeDtypeStruct((t, d), F32),
        compiler_params=_cparams("arbitrary"),
        name="outproj_ln",
    )(*parts_ctx, *parts_lat, w_out_bf, x, mods_l, ln_g, ln_b)


def _top_rows(s, k, extra=()):
    r = s.shape[0]
    rid = _iota(s.shape, 0).astype(F32)
    vals, ids = [], []
    picked = [[] for _ in extra]
    for _ in range(k):
        m = jnp.max(s, axis=0, keepdims=True)
        cand = jnp.where(s == m, rid, float(r))
        i = jnp.min(cand, axis=0, keepdims=True)
        hit = cand == i
        vals.append(m)
        ids.append(i)
        for lst, arr in zip(picked, extra):
            lst.append(jnp.max(jnp.where(hit, arr, -1.0), axis=0, keepdims=True))
        s = jnp.where(hit, -jnp.inf, s)
    cat = lambda xs: jnp.concatenate(xs, axis=0)
    return cat(vals), cat(ids), [cat(p) for p in picked]


def _router_kernel(x_ref, m_ref, wqt_ref, keys_ref, h_ref, a_ref, b_ref, g_ref):
    tm = x_ref.shape[0]
    hb = (x_ref[...] * (1.0 + m_ref[4]) + m_ref[3]).astype(BF16)
    h_ref[...] = hb
    qt = _dot_nt(wqt_ref[...], hb).astype(BF16)
    k = PEER_TOPK
    a_rows, b_rows, g_rows = [], [], []
    for head in range(PEER_HEADS):
        tv, ti = [], []
        for half in range(2):
            g = 2 * head + half
            sc = _dot(keys_ref[g], qt[g * PEER_HALF:(g + 1) * PEER_HALF])
            v, i, _ = _top_rows(sc, k)
            tv.append(v)
            ti.append(i)
        cs = [tv[0][0:1] + tv[1]]
        ca = [jnp.broadcast_to(ti[0][0:1], (k, tm))]
        cb = [ti[1]]
        for k1 in range(1, 4):
            cs.append(tv[0][k1:k1 + 1] + tv[1][0:8])
            ca.append(jnp.broadcast_to(ti[0][k1:k1 + 1], (8, tm)))
            cb.append(ti[1][0:8])
        low = _iota((8, tm), 0) < 4
        v2_dup = jnp.where(low, tv[1][0:8], pltpu.roll(tv[1][0:8], 4, 0))
        i2_dup = jnp.where(low, ti[1][0:8], pltpu.roll(ti[1][0:8], 4, 0))
        for k1 in (4, 6):
            cs.append(jnp.where(low, tv[0][k1:k1 + 1], tv[0][k1 + 1:k1 + 2]) + v2_dup)
            ca.append(jnp.where(low, ti[0][k1:k1 + 1], ti[0][k1 + 1:k1 + 2]))
            cb.append(i2_dup)
        cs.append(tv[0][8:16] + tv[1][0:1])
        ca.append(ti[0][8:16])
        cb.append(jnp.broadcast_to(ti[1][0:1], (8, tm)))
        best, _, (sel_a, sel_b) = _top_rows(jnp.concatenate(cs, axis=0), k,
                                            extra=(jnp.concatenate(ca, axis=0), jnp.concatenate(cb, axis=0)))
        e = jnp.exp(best - best[0:1])
        g_rows.append(e / jnp.sum(e, axis=0, keepdims=True))
        a_rows.append(sel_a)
        b_rows.append(sel_b)
    a_ref[...] = jnp.concatenate(a_rows, axis=0).T.astype(I32)
    b_ref[...] = jnp.concatenate(b_rows, axis=0).T.astype(I32)
    g_ref[...] = jnp.concatenate(g_rows, axis=0).T


def _router(x, mods_l, wq_t, keys, t_ctx, s_lat):
    t, d = x.shape
    tm = 128
    nslot = PEER_HEADS * PEER_TOPK
    return pl.pallas_call(
        _router_kernel,
        grid=(t // tm,),
        in_specs=[pl.BlockSpec((tm, d), lambda i: (i, 0)),
                  pl.BlockSpec((None, 6, 1, d), _mod_row_map(tm, t_ctx, s_lat)),
                  pl.BlockSpec(wq_t.shape, lambda i: (0, 0)),
                  pl.BlockSpec(keys.shape, lambda i: (0, 0, 0))],
        out_specs=[pl.BlockSpec((tm, d), lambda i: (i, 0)),
                   pl.BlockSpec((tm, nslot), lambda i: (i, 0)),
                   pl.BlockSpec((tm, nslot), lambda i: (i, 0)),
                   pl.BlockSpec((tm, nslot), lambda i: (i, 0))],
        out_shape=[jax.ShapeDtypeStruct((t, d), BF16),
                   jax.ShapeDtypeStruct((t, nslot), I32),
                   jax.ShapeDtypeStruct((t, nslot), I32),
                   jax.ShapeDtypeStruct((t, nslot), F32)],
        compiler_params=_cparams("arbitrary"),
        name="peer_router",
    )(x, mods_l, wq_t, keys)


def _gates_kernel(a_ref, b_ref, g_ref, u_ref, v_ref, o_ref, ub_ref, vb_ref):
    tm = a_ref.shape[0]
    n = PEER_KEYS
    sub = 16
    ub_ref[...] = u_ref[...].astype(BF16)
    vb_ref[...] = v_ref[...].astype(BF16)
    key = _iota((sub, n, a_ref.shape[2]), 1).astype(F32).astype(BF16)
    zero = jnp.zeros((), BF16)
    for t0 in range(0, tm, sub):
        a = a_ref[t0:t0 + sub].astype(F32).astype(BF16)
        b = b_ref[t0:t0 + sub].astype(F32).astype(BF16)
        g = g_ref[t0:t0 + sub].astype(BF16)
        onehot_a = jnp.where(key == a, jnp.ones((), BF16), zero)
        gated_b = jnp.where(key == b, g, zero)
        w = lax.dot_general(onehot_a, gated_b, (((2,), (2,)), ((0,), (0,))),
                            preferred_element_type=F32)
        w_t = jnp.swapaxes(w.astype(BF16), 0, 1)
        for r in range(n):
            o_ref[t0:t0 + sub, r * n:(r + 1) * n] = w_t[r]


def _gates(a_idx, b_idx, gate, peer_u, peer_v, layer):
    t, nslot = a_idx.shape
    d = peer_u.shape[-1]
    tm = 96
    steps = t // tm
    te = N_EXPERTS // steps
    spec = pl.BlockSpec((tm, 1, nslot), lambda i: (i, 0, 0))
    tab_in = pl.BlockSpec((None, te, d), lambda i: (layer, i, 0))
    tab_out = pl.BlockSpec((te, d), lambda i: (i, 0))
    return pl.pallas_call(
        _gates_kernel,
        grid=(steps,),
        in_specs=[spec, spec, spec, tab_in, tab_in],
        out_specs=[pl.BlockSpec((tm, N_EXPERTS), lambda i: (i, 0)), tab_out, tab_out],
        out_shape=[jax.ShapeDtypeStruct((t, N_EXPERTS), BF16),
                   jax.ShapeDtypeStruct((N_EXPERTS, d), BF16),
                   jax.ShapeDtypeStruct((N_EXPERTS, d), BF16)],
        compiler_params=_cparams("arbitrary"),
        name="peer_gates",
    )(a_idx.reshape(t, 1, nslot), b_idx.reshape(t, 1, nslot), gate.reshape(t, 1, nslot), peer_u, peer_v)


def _experts_kernel(h_ref, u_ref, v_ref, w_ref, x_ref, m_ref, g_ref, b_ref, o_ref, acc_ref):
    j = pl.program_id(1)

    @pl.when(j == 0)
    def _():
        acc_ref[...] = jnp.zeros_like(acc_ref)

    act = _gelu_tanh(_dot_nt(h_ref[...], u_ref[...]))
    acc_ref[...] += _dot((act * w_ref[...].astype(F32)).astype(BF16), v_ref[...])

    @pl.when(j == pl.num_programs(1) - 1)
    def _():
        o_ref[...] = _layernorm(ALPHA * x_ref[...] + m_ref[5] * acc_ref[...], g_ref[...], b_ref[...])


def _experts(h_bf, u_bf, v_bf, w_gate, x, mods_l, ln_g, ln_b, t_ctx, s_lat):
    t, d = x.shape
    tm, te = 1024, 1024
    return pl.pallas_call(
        _experts_kernel,
        grid=(t // tm, N_EXPERTS // te),
        in_specs=[pl.BlockSpec((tm, d), lambda i, j: (i, 0)),
                  pl.BlockSpec((te, d), lambda i, j: (j, 0)),
                  pl.BlockSpec((te, d), lambda i, j: (j, 0)),
                  pl.BlockSpec((tm, te), lambda i, j: (i, j)),
                  pl.BlockSpec((tm, d), lambda i, j: (i, 0)),
                  pl.BlockSpec((None, 6, 1, d), _mod_row_map(tm, t_ctx, s_lat)),
                  pl.BlockSpec((1, d), lambda i, j: (0, 0)),
                  pl.BlockSpec((1, d), lambda i, j: (0, 0))],
        out_specs=pl.BlockSpec((tm, d), lambda i, j: (i, 0)),
        out_shape=jax.ShapeDtypeStruct((t, d), F32),
        scratch_shapes=[pltpu.VMEM((tm, d), F32)],
        compiler_params=_cparams("arbitrary", "arbitrary"),
        name="peer_experts",
    )(h_bf, u_bf, v_bf, w_gate, x, mods_l, ln_g, ln_b)


def _pad_cols(w, n):
    return jnp.concatenate([w, jnp.zeros((w.shape[0], n), w.dtype)], axis=1) if n else w


def _layout_w_in(w):
    a = w[:, 0:1280]
    mcq, mckv, mkpe = w[:, 1280:1472], w[:, 1472:1600], w[:, 1600:1632]
    b = jnp.concatenate([mckv, _pad_cols(mcq, 64), mkpe, mkpe, mkpe, mkpe], axis=1)
    c = w[:, 1632:2400]
    d = _pad_cols(w[:, 2400:3176], PROJ_D - 776)
    return jnp.concatenate([a, b, c, d], axis=1).astype(BF16)


def _layout_w_qb(w):
    w4 = w.reshape(MLA_Q_RANK, N_HEADS, MLA_NOPE + MLA_ROPE)
    return jnp.concatenate([w4[:, :, :MLA_NOPE].reshape(MLA_Q_RANK, -1),
                            w4[:, :, MLA_NOPE:].reshape(MLA_Q_RANK, -1)], axis=1).astype(BF16)


def _layout_w_kvb(w):
    w4 = w.reshape(MLA_KV_RANK, N_HEADS, MLA_NOPE + HEAD_DIM)
    return jnp.concatenate([w4[:, :, :MLA_NOPE].reshape(MLA_KV_RANK, -1),
                            w4[:, :, MLA_NOPE:].reshape(MLA_KV_RANK, -1)], axis=1).astype(BF16)


def _rope_tables(seq):
    rows = seq // GRID_W
    row = jnp.repeat(jnp.arange(rows, dtype=F32), GRID_W)
    col = jnp.tile(jnp.arange(GRID_W, dtype=F32), rows)
    freqs = ROPE_BASE ** (-jnp.arange(ROPE_PAIRS, dtype=F32) / ROPE_PAIRS)
    cos_l, sin_l = [], []
    for pos in (row, col):
        ang = pos[:, None] * freqs
        cos_l += [jnp.cos(ang), jnp.cos(ang)]
        sin_l += [-jnp.sin(ang), jnp.sin(ang)]
    return jnp.concatenate(cos_l, axis=1), jnp.concatenate(sin_l, axis=1)


def _hgrn_state_pack(st):
    b = st.shape[0]
    st_t = jnp.swapaxes(st, -1, -2)
    zero = jnp.zeros_like(st_t[:, :, 0])
    rows = [jnp.concatenate([st_t[:, :, h] if g == h else zero for g in range(N_HEADS)], axis=-1)
            for h in range(N_HEADS)]
    return jnp.concatenate(rows, axis=-2).reshape(b, 2, GROUP, GROUP)


def _hgrn_state_unpack(sb):
    blocks = [sb[:, :, h * HEAD_DIM:(h + 1) * HEAD_DIM, h * HEAD_DIM:(h + 1) * HEAD_DIM]
              for h in range(N_HEADS)]
    return jnp.swapaxes(jnp.stack(blocks, axis=2), -1, -2)


def _ssd_state_pack(st):
    st_t = jnp.swapaxes(st, -1, -2)
    zero = jnp.zeros_like(st_t[:, :, 0])
    rows = [jnp.concatenate([st_t[:, :, h] if h // 2 == g else zero for h in range(N_HEADS)], axis=-1)
            for g in range(2)]
    return jnp.concatenate(rows, axis=-2)


def _ssd_state_unpack(sb):
    blocks = [sb[:, :, (h // 2) * SSD_STATE:(h // 2 + 1) * SSD_STATE, h * HEAD_DIM:(h + 1) * HEAD_DIM]
              for h in range(N_HEADS)]
    return jnp.swapaxes(jnp.stack(blocks, axis=2), -1, -2)


def _tile_lanes(v, n):
    return jnp.tile(v.reshape(1, -1), (1, n))


def kernel(x_prompt, x_sample, cache_mla_ckv, cache_mla_kpe, cache_diff_k, cache_diff_v, state_hgrn, state_ssd, c, c_ctx, w_mod, b_mod, w_in, hgrn_lb, hgrn_norm, mla_q_norm, mla_w_qb, mla_kv_norm, mla_w_kvb, diff_lambda, diff_norm, ssd_conv_w, ssd_conv_b, ssd_dt_bias, ssd_a_log, ssd_d, ssd_norm, w_out, ln1_g, ln1_b, peer_wq, peer_keys, peer_u, peer_v, ln2_g, ln2_b):
    nb, seq, d = x_prompt.shape
    nlat, lseq, _ = x_sample.shape
    depth = w_in.shape[0]
    t_ctx = nb * seq
    x = jnp.concatenate([x_prompt.reshape(t_ctx, d), x_sample.reshape(nlat * lseq, d)], axis=0)

    cond8 = jnp.concatenate([c_ctx.reshape(1, d), c, jnp.zeros((8 - 1 - nlat, d), F32)], axis=0)
    mods = _mods(cond8, w_mod, b_mod)
    mods = mods[:, :1 + nlat].reshape(depth, 1 + nlat, 6, 1, d)

    cos32, sin32 = _rope_tables(lseq)
    cos128, sin128 = jnp.tile(cos32, (1, 4)), jnp.tile(sin32, (1, 4))
    cos256, sin256 = jnp.tile(cos32, (1, 8)), jnp.tile(sin32, (1, 8))

    produced = []
    for l in range(depth):
        mods_l = mods[l]
        pa, pb, pc, pd = _inproj(x, mods_l, _layout_w_in(w_in[l]), t_ctx, lseq)

        norm_hg = _tile_lanes(hgrn_norm[l], N_HEADS)
        mla_w = (mla_q_norm[l].reshape(1, -1), _layout_w_qb(mla_w_qb[l]),
                 mla_kv_norm[l].reshape(1, -1), _layout_w_kvb(mla_w_kvb[l]))
        lam_init = 0.8 - 0.6 * math.exp(-0.3 * l)
        norm_df = _tile_lanes(diff_norm[l], N_HEADS)
        past = cache_diff_k.shape[2]
        ssd_w = (ssd_conv_w[l], ssd_conv_b[l].reshape(1, -1),
                 _pad_cols(ssd_dt_bias[l].reshape(1, -1), 120), _pad_cols(ssd_a_log[l].reshape(1, -1), 120),
                 jnp.repeat(ssd_d[l], HEAD_DIM).reshape(1, -1), ssd_norm[l].reshape(1, -1))

        (mla_ctx, new_ckv, new_kpe), (df_ctx, new_dk, new_dv), (ssd_ctx, ssd_fin), (hg_ctx, hg_fin) = _mixers(
            nb, "mixers_ctx",
            _mla_parts(pb, 0, nb, seq, *mla_w, None),
            _diff_parts(pc, 0, nb, seq, diff_lambda[l], norm_df, lam_init, None),
            _ssd_parts(pd, 0, nb, seq, *ssd_w, None),
            _hgrn_parts(pa, 0, nb, seq, hgrn_lb, norm_hg, None, l))
        ((mla_lat,),) = _mixers(
            nlat, "mla_lat",
            _mla_parts(pb, t_ctx, nlat, lseq, *mla_w,
                       (cache_mla_ckv[:, l], jnp.tile(cache_mla_kpe[:, l], (1, 1, 4)), cos128, sin128)))
        ((df_lat,),) = _mixers(
            nlat, "diffattn_lat",
            _diff_parts(pc, t_ctx, nlat, lseq, diff_lambda[l], norm_df, lam_init,
                        (cache_diff_k[:, l].reshape(nlat, past, GROUP),
                         cache_diff_v[:, l].reshape(nlat, past, GROUP), cos256, sin256)))
        ((ssd_lat, _),) = _mixers(nlat, "ssd_lat",
                                  _ssd_parts(pd, t_ctx, nlat, lseq, *ssd_w, _ssd_state_pack(state_ssd[:, l])))
        ((hg_lat, _),) = _mixers(nlat, "hgrn_lat",
                                 _hgrn_parts(pa, t_ctx, nlat, lseq, hgrn_lb, norm_hg,
                                             _hgrn_state_pack(state_hgrn[:, l]), l))

        parts_ctx = [a.reshape(t_ctx, GROUP) for a in (hg_ctx, mla_ctx, df_ctx, ssd_ctx)]
        parts_lat = [a.reshape(nlat * lseq, GROUP) for a in (hg_lat, mla_lat, df_lat, ssd_lat)]
        x = _outproj(parts_ctx, parts_lat, w_out[l].astype(BF16), x, mods_l,
                     ln1_g[l].reshape(1, d), ln1_b[l].reshape(1, d), t_ctx, lseq)

        keys = peer_keys[l].reshape(2 * PEER_HEADS, PEER_KEYS, PEER_HALF).astype(BF16)
        h_bf, a_idx, b_idx, gate = _router(x, mods_l, peer_wq[l].T.astype(BF16), keys, t_ctx, lseq)
        w_gate, u_bf, v_bf = _gates(a_idx, b_idx, gate, peer_u, peer_v, l)
        x = _experts(h_bf, u_bf, v_bf, w_gate, x, mods_l,
                     ln2_g[l].reshape(1, d), ln2_b[l].reshape(1, d), t_ctx, lseq)

        produced.append((new_ckv, new_kpe,
                         new_dk.reshape(nb, seq, N_HEADS, 2, DIFF_DIM),
                         new_dv.reshape(nb, seq, N_HEADS, 2 * DIFF_DIM),
                         _hgrn_state_unpack(hg_fin), _ssd_state_unpack(ssd_fin)))

    y_prompt = x[:t_ctx].reshape(nb, seq, d)
    y_sample = x[t_ctx:].reshape(nlat, lseq, d)
    stacked = tuple(jnp.stack([p[i] for p in produced], axis=1) for i in range(6))
    return (y_prompt, y_sample) + stacked
```

```python
import functools
import math

import jax
import jax.numpy as jnp
from jax import lax
from jax.experimental import pallas as pl
from jax.experimental.pallas import tpu as pltpu

F32 = jnp.float32
BF16 = jnp.bfloat16
I32 = jnp.int32
HIGHEST = lax.Precision.HIGHEST

D_MODEL = 1024
GROUP = 256
N_HEADS = 4
HEAD_DIM = 64
HG_BLOCK = 16
HG_SLAB = 256
HG_UNROLL = 8
SSD_CHUNK = 128
SSD_STATE = 64
MLA_Q_RANK = 192
MLA_KV_RANK = 128
MLA_NOPE = 64
MLA_ROPE = 32
DIFF_DIM = 32
GRID_W = 64
ROPE_PAIRS = 8
ROPE_BASE = 10000.0
PEER_HEADS = 8
PEER_KEYS = 128
PEER_TOPK = 16
PEER_HALF = 64
N_EXPERTS = PEER_KEYS * PEER_KEYS
NORM_EPS = 1e-6
LN_EPS = 1e-5
DEPTH = 2
ALPHA = (2.0 * DEPTH) ** 0.25
LOG2_E = 1.4426950408889634

PROJ_A = 5 * GROUP
PROJ_B = 512
PROJ_C = 3 * GROUP
PROJ_D = 896
VMEM_LIMIT = 56 * 1024 * 1024


def _cparams(*sem):
    return pltpu.CompilerParams(dimension_semantics=sem, vmem_limit_bytes=VMEM_LIMIT)


def _sigmoid(x):
    return 1.0 / (1.0 + jnp.exp(-x))


def _silu(x):
    return x * _sigmoid(x)


def _softplus(x):
    return jnp.maximum(x, 0.0) + jnp.log(1.0 + jnp.exp(-jnp.abs(x)))


def _gelu_tanh(x):
    return 0.5 * x * (1.0 + jnp.tanh(math.sqrt(2.0 / math.pi) * (x + 0.044715 * (x * x * x))))


def _dot(a, b, precision=None):
    return jnp.dot(a, b, preferred_element_type=F32, precision=precision)


def _dot_nt(a, b, precision=None):
    return lax.dot_general(a, b, (((1,), (1,)), ((), ())), preferred_element_type=F32,
                           precision=precision)


def _dot_tn(a, b, precision=None):
    return lax.dot_general(a, b, (((0,), (0,)), ((), ())), preferred_element_type=F32,
                           precision=precision)


def _iota(shape, dim):
    return lax.broadcasted_iota(I32, shape, dim)


def _block_mask(rows, cols, rblk, cblk):
    return (_iota((rows, cols), 0) // rblk) == (_iota((rows, cols), 1) // cblk)


def _lane_group_mask(width, start, size):
    lane = _iota((1, width), 1)
    return (lane >= start) & (lane < start + size)


def _layernorm(v, g, b):
    mu = jnp.mean(v, axis=-1, keepdims=True)
    d = v - mu
    var = jnp.mean(d * d, axis=-1, keepdims=True)
    return d * lax.rsqrt(var + LN_EPS) * g + b


def _swap_halves16(x):
    width = x.shape[-1]
    lane = _iota(x.shape, x.ndim - 1)
    up = pltpu.roll(x, width - 8, x.ndim - 1)
    down = pltpu.roll(x, 8, x.ndim - 1)
    return jnp.where((lane % 16) < 8, up, down)


def _rope(x, cos, sin_signed):
    return x * cos + _swap_halves16(x) * sin_signed


def _mods_kernel(c_ref, w_ref, b_ref, o_ref):
    s = _silu(c_ref[...]).astype(BF16)
    o_ref[...] = _dot(s, w_ref[...].astype(BF16)) + b_ref[...]


def _mods(cond8, w_mod, b_mod):
    depth, d, n = w_mod.shape
    tn = 1536
    return pl.pallas_call(
        _mods_kernel,
        grid=(depth, n // tn),
        in_specs=[pl.BlockSpec((8, d), lambda l, j: (0, 0)),
                  pl.BlockSpec((None, d, tn), lambda l, j: (l, 0, j)),
                  pl.BlockSpec((None, 1, tn), lambda l, j: (l, 0, j))],
        out_specs=pl.BlockSpec((None, 8, tn), lambda l, j: (l, 0, j)),
        out_shape=jax.ShapeDtypeStruct((depth, 8, n), F32),
        compiler_params=_cparams("arbitrary", "arbitrary"),
        name="mods",
    )(cond8, w_mod, b_mod.reshape(depth, 1, n))


def _mod_row_map(tm, t_ctx, s_lat):
    def index_map(i, *_):
        start = i * tm
        return (jnp.where(start < t_ctx, 0, 1 + (start - t_ctx) // s_lat), 0, 0, 0)
    return index_map


def _inproj_kernel(x_ref, m_ref, w_ref, oa_ref, ob_ref, oc_ref, od_ref):
    h = (x_ref[...] * (1.0 + m_ref[1]) + m_ref[0]).astype(BF16)
    start = 0
    for o_ref in (oa_ref, ob_ref, oc_ref, od_ref):
        width = o_ref.shape[-1]
        o_ref[...] = _dot(h, w_ref[:, start:start + width])
        start += width


def _inproj(x, mods_l, w_in_p, t_ctx, s_lat):
    t, d = x.shape
    tm = 512
    widths = (PROJ_A, PROJ_B, PROJ_C, PROJ_D)
    return pl.pallas_call(
        _inproj_kernel,
        grid=(t // tm,),
        in_specs=[pl.BlockSpec((tm, d), lambda i: (i, 0)),
                  pl.BlockSpec((None, 6, 1, d), _mod_row_map(tm, t_ctx, s_lat)),
                  pl.BlockSpec(w_in_p.shape, lambda i: (0, 0))],
        out_specs=[pl.BlockSpec((tm, w), lambda i: (i, 0)) for w in widths],
        out_shape=[jax.ShapeDtypeStruct((t, w), F32) for w in widths],
        compiler_params=_cparams("arbitrary"),
        name="inproj",
    )(x, mods_l, w_in_p)


def _hgrn_kernel(layer, has_state, *refs):
    if has_state:
        (a_ref, lb_ref, norm_ref, s0_ref, o_ref, sfin_ref,
         q_scr, k_scr, bc_scr, dec_scr, qt_scr, kt_scr, st_scr, o_scr) = refs
    else:
        (a_ref, lb_ref, norm_ref, o_ref, sfin_ref,
         q_scr, k_scr, bc_scr, dec_scr, qt_scr, kt_scr, st_scr, o_scr) = refs
        s0_ref = None
    seq = a_ref.shape[0]
    c = HG_BLOCK
    nblk = seq // c
    slab = HG_SLAB
    nb = slab // c

    lbp = lb_ref[...]
    e = jnp.exp(lbp - jnp.max(lbp, axis=0, keepdims=True))
    p = e / jnp.sum(e, axis=0, keepdims=True)
    lower = jnp.sum(p[1:layer + 1], axis=0) if layer > 0 else jnp.zeros_like(p[0])

    q = _silu(a_ref[:, 0:GROUP])
    q_scr[...] = q
    srow = _iota((slab, slab), 0)
    scol = _iota((slab, slab), 1)
    same = (srow // c) == (scol // c)
    cum_op = (jnp.where(same & (scol <= srow), 1.0, 0.0), jnp.where(same & (scol >= srow), 1.0, 0.0))
    for d in range(2):
        lb = lower[d:d + 1]
        f = lb + (1.0 - lb) * _sigmoid(a_ref[:, (1 + d) * GROUP:(2 + d) * GROUP])
        k = 1.0 - f
        lf = jnp.log(f)
        k_scr[d] = k
        for s0 in range(0, seq, slab):
            bc = _dot(cum_op[d], lf[s0:s0 + slab], precision=HIGHEST)
            bc3 = bc.reshape(nb, c, GROUP)
            edge = bc3[:, c - 1:c, :] if d == 0 else bc3[:, 0:1, :]
            tot = jnp.broadcast_to(edge, (nb, c, GROUP)).reshape(slab, GROUP)
            bc_scr[d, s0:s0 + slab, :] = bc * LOG2_E
            dec_scr[d, s0:s0 + slab, :] = jnp.exp(tot)
            qt_scr[d, s0:s0 + slab, :] = (q[s0:s0 + slab] * jnp.exp(bc)).astype(BF16)
            kt_scr[d, s0:s0 + slab, :] = (k[s0:s0 + slab] * jnp.exp(tot - bc)).astype(BF16)
    if has_state:
        st_scr[...] = s0_ref[...]
    else:
        st_scr[...] = jnp.zeros_like(st_scr)

    bd_ones = _block_mask(GROUP, GROUP, HEAD_DIM, HEAD_DIM).astype(BF16)
    rib = _iota((1, c, GROUP), 1)

    def slab_step(i, carry):
        r0 = pl.multiple_of(i * slab, slab)
        q3 = q_scr[pl.ds(r0, slab), :].reshape(nb, c, GROUP)
        v3 = a_ref[pl.ds(r0, slab), 3 * GROUP:4 * GROUP].reshape(nb, c, GROUP)
        o3 = jnp.zeros((nb, c, GROUP), F32)
        for d in range(2):
            bc3 = bc_scr[d, pl.ds(r0, slab), :].reshape(nb, c, GROUP)
            k3 = k_scr[d, pl.ds(r0, slab), :].reshape(nb, c, GROUP)
            for j in range(c):
                keep = (rib >= j) if d == 0 else (rib <= j)
                dec = jnp.exp2(jnp.where(keep, bc3 - bc3[:, j:j + 1, :], -jnp.inf))
                pj = (dec * q3 * k3[:, j:j + 1, :]).astype(BF16).reshape(slab, GROUP)
                srep = _dot(pj, bd_ones).reshape(nb, c, GROUP)
                o3 = o3 + srep * v3[:, j:j + 1, :]
        o_scr[0, pl.ds(r0, slab), :] = o3.reshape(slab, GROUP)
        return carry

    lax.fori_loop(0, seq // slab, slab_step, 0)

    bd_mask = _block_mask(GROUP, GROUP, HEAD_DIM, HEAD_DIM)

    def body(n, carry):
        rows = [[pl.multiple_of(((n * HG_UNROLL + u) if d == 0 else nblk - 1 - (n * HG_UNROLL + u)) * c, c)
                 for u in range(HG_UNROLL)] for d in range(2)]
        upd = [[_dot_tn(a_ref[pl.ds(r0, c), 3 * GROUP:4 * GROUP].astype(BF16), kt_scr[d, pl.ds(r0, c), :])
                for r0 in rows[d]] for d in range(2)]
        for d in range(2):
            st = st_scr[d]
            for u, r0 in enumerate(rows[d]):
                o_scr[1 + d, pl.ds(r0, c), :] = _dot_nt(qt_scr[d, pl.ds(r0, c), :], st.astype(BF16))
                st = st * dec_scr[d, pl.ds(r0, 1), :] + jnp.where(bd_mask, upd[d][u], 0.0)
            st_scr[d] = st
        return carry

    lax.fori_loop(0, nblk // HG_UNROLL, body, 0)

    o = o_scr[0] + o_scr[1] + o_scr[2]
    mean_op = jnp.where(_block_mask(GROUP, GROUP, HEAD_DIM, HEAD_DIM), 1.0 / HEAD_DIM, 0.0)
    ms = _dot(o * o, mean_op, precision=HIGHEST)
    y = o * lax.rsqrt(ms + NORM_EPS) * norm_ref[...]
    o_ref[...] = y * _silu(a_ref[:, 4 * GROUP:5 * GROUP])
    sfin_ref[...] = st_scr[...]


def _hgrn_parts(proj_a, row0, nseq, seq, hgrn_lb, norm_t, s0, layer):
    has_state = s0 is not None
    blk0 = row0 // seq
    in_specs = [pl.BlockSpec((seq, PROJ_A), lambda b: (blk0 + b, 0)),
                pl.BlockSpec(hgrn_lb.shape, lambda b: (0, 0, 0)),
                pl.BlockSpec((1, GROUP), lambda b: (0, 0))]
    args = [proj_a, hgrn_lb, norm_t]
    if has_state:
        in_specs.append(pl.BlockSpec((None, 2, GROUP, GROUP), lambda b: (b, 0, 0, 0)))
        args.append(s0)
    out_specs = [pl.BlockSpec((None, seq, GROUP), lambda b: (b, 0, 0)),
                 pl.BlockSpec((None, 2, GROUP, GROUP), lambda b: (b, 0, 0, 0))]
    out_shape = [jax.ShapeDtypeStruct((nseq, seq, GROUP), F32),
                 jax.ShapeDtypeStruct((nseq, 2, GROUP, GROUP), F32)]
    scratch = [pltpu.VMEM((seq, GROUP), F32),
               pltpu.VMEM((2, seq, GROUP), F32),
               pltpu.VMEM((2, seq, GROUP), F32),
               pltpu.VMEM((2, seq, GROUP), F32),
               pltpu.VMEM((2, seq, GROUP), BF16),
               pltpu.VMEM((2, seq, GROUP), BF16),
               pltpu.VMEM((2, GROUP, GROUP), F32),
               pltpu.VMEM((3, seq, GROUP), F32)]
    return functools.partial(_hgrn_kernel, layer, has_state), in_specs, args, out_specs, out_shape, scratch


def _mla_kernel(latent, *refs):
    if latent:
        (b_ref, qn_ref, wq_ref, kvn_ref, wkv_ref, cckv_ref, ckpe_ref, cos_ref, sin_ref,
         o_ref) = refs
    else:
        (b_ref, qn_ref, wq_ref, kvn_ref, wkv_ref, o_ref, ckv_ref, kpe_ref) = refs
    seq = b_ref.shape[0]
    mckv = b_ref[:, 0:MLA_KV_RANK]
    mcq = b_ref[:, MLA_KV_RANK:MLA_KV_RANK + MLA_Q_RANK]
    kpe_t = b_ref[:, 384:512]

    cq = mcq * lax.rsqrt(jnp.mean(mcq * mcq, axis=-1, keepdims=True) + NORM_EPS) * qn_ref[...]
    qf = _dot(cq.astype(BF16), wq_ref[...])
    ckv = mckv * lax.rsqrt(jnp.mean(mckv * mckv, axis=-1, keepdims=True) + NORM_EPS) * kvn_ref[...]
    q_nope = qf[:, 0:N_HEADS * MLA_NOPE]
    q_rope = qf[:, N_HEADS * MLA_NOPE:]
    if latent:
        cos = cos_ref[...]
        sin = sin_ref[...]
        q_rope = _rope(q_rope, cos, sin)
        ckv_all = jnp.concatenate([cckv_ref[...], ckv], axis=0)
        kpe_all = jnp.concatenate([ckpe_ref[...], _rope(kpe_t, cos, sin)], axis=0)
    else:
        ckv_ref[...] = ckv
        kpe_ref[...] = kpe_t[:, 0:MLA_ROPE]
        ckv_all = ckv
        kpe_all = kpe_t
    kv = _dot(ckv_all.astype(BF16), wkv_ref[...])
    kcat = jnp.concatenate([kv[:, 0:GROUP], kpe_all], axis=1).astype(BF16)
    v = kv[:, GROUP:].astype(BF16)
    qcat = jnp.concatenate([q_nope, q_rope], axis=1)
    scale = (MLA_NOPE + MLA_ROPE) ** -0.5
    qb = min(seq, 256)
    width = qcat.shape[1]
    for r0 in range(0, seq, qb):
        qblk = qcat[r0:r0 + qb]
        acc = jnp.zeros((qb, GROUP), F32)
        for h in range(N_HEADS):
            hm = (_lane_group_mask(width, h * MLA_NOPE, MLA_NOPE)
                  | _lane_group_mask(width, N_HEADS * MLA_NOPE + h * MLA_ROPE, MLA_ROPE))
            s = _dot_nt(jnp.where(hm, qblk, 0.0).astype(BF16), kcat)
            e = jnp.exp2((s - jnp.max(s, axis=-1, keepdims=True)) * (scale * LOG2_E))
            z = jnp.sum(e, axis=-1, keepdims=True)
            oh = _dot(e.astype(BF16), v) / z
            acc = acc + jnp.where(_lane_group_mask(GROUP, h * HEAD_DIM, HEAD_DIM), oh, 0.0)
        o_ref[r0:r0 + qb, :] = acc


def _mla_parts(proj_b, row0, nseq, seq, q_norm, w_qb_p, kv_norm, w_kvb_p, latent_args):
    latent = latent_args is not None
    blk0 = row0 // seq
    in_specs = [pl.BlockSpec((seq, PROJ_B), lambda b: (blk0 + b, 0)),
                pl.BlockSpec(q_norm.shape, lambda b: (0, 0)),
                pl.BlockSpec(w_qb_p.shape, lambda b: (0, 0)),
                pl.BlockSpec(kv_norm.shape, lambda b: (0, 0)),
                pl.BlockSpec(w_kvb_p.shape, lambda b: (0, 0))]
    args = [proj_b, q_norm, w_qb_p, kv_norm, w_kvb_p]
    out_specs = [pl.BlockSpec((None, seq, GROUP), lambda b: (b, 0, 0))]
    out_shape = [jax.ShapeDtypeStruct((nseq, seq, GROUP), F32)]
    if latent:
        cckv, ckpe_t, cos, sin = latent_args
        past = cckv.shape[1]
        in_specs += [pl.BlockSpec((None, past, MLA_KV_RANK), lambda b: (b, 0, 0)),
                     pl.BlockSpec((None, past, 128), lambda b: (b, 0, 0)),
                     pl.BlockSpec(cos.shape, lambda b: (0, 0)),
                     pl.BlockSpec(sin.shape, lambda b: (0, 0))]
        args += [cckv, ckpe_t, cos, sin]
    else:
        out_specs += [pl.BlockSpec((None, seq, MLA_KV_RANK), lambda b: (b, 0, 0)),
                      pl.BlockSpec((None, seq, MLA_ROPE), lambda b: (b, 0, 0))]
        out_shape += [jax.ShapeDtypeStruct((nseq, seq, MLA_KV_RANK), F32),
                      jax.ShapeDtypeStruct((nseq, seq, MLA_ROPE), F32)]
    return functools.partial(_mla_kernel, latent), in_specs, args, out_specs, out_shape, []


def _diff_kernel(latent, lam_init, *refs):
    if latent:
        (c_ref, lam_ref, norm_ref, ck_ref, cv_ref, cos_ref, sin_ref, o_ref) = refs
    else:
        (c_ref, lam_ref, norm_ref, o_ref, k_ref, v_ref) = refs
    seq = c_ref.shape[0]
    dq = c_ref[:, 0:GROUP]
    dk = c_ref[:, GROUP:2 * GROUP]
    dv = c_ref[:, 2 * GROUP:3 * GROUP]
    if latent:
        cos = cos_ref[...]
        sin = sin_ref[...]
        dq = _rope(dq, cos, sin)
        k_all = jnp.concatenate([ck_ref[...], _rope(dk, cos, sin)], axis=0)
        v_all = jnp.concatenate([cv_ref[...], dv], axis=0)
    else:
        k_ref[...] = dk
        v_ref[...] = dv
        k_all = dk
        v_all = dv
    lv = lam_ref[...]
    lam = (jnp.exp(jnp.sum(lv[0:1] * lv[1:2], axis=-1, keepdims=True))
           - jnp.exp(jnp.sum(lv[2:3] * lv[3:4], axis=-1, keepdims=True)) + lam_init)
    k_bf = k_all.astype(BF16)
    v_bf = v_all.astype(BF16)
    scale = DIFF_DIM ** -0.5
    mean_op = jnp.where(_block_mask(GROUP, GROUP, HEAD_DIM, HEAD_DIM), 1.0 / HEAD_DIM, 0.0)
    qb = min(seq, 256)
    for r0 in range(0, seq, qb):
        qblk = dq[r0:r0 + qb]
        acc = jnp.zeros((qb, GROUP), F32)
        for h in range(N_HEADS):
            outs = []
            for comp in range(2):
                cm = _lane_group_mask(GROUP, h * HEAD_DIM + comp * DIFF_DIM, DIFF_DIM)
                s = _dot_nt(jnp.where(cm, qblk, 0.0).astype(BF16), k_bf)
                e = jnp.exp2((s - jnp.max(s, axis=-1, keepdims=True)) * (scale * LOG2_E))
                outs.append(_dot(e.astype(BF16), v_bf) / jnp.sum(e, axis=-1, keepdims=True))
            oh = outs[0] - lam * outs[1]
            acc = acc + jnp.where(_lane_group_mask(GROUP, h * HEAD_DIM, HEAD_DIM), oh, 0.0)
        ms = _dot(acc * acc, mean_op, precision=HIGHEST)
        o_ref[r0:r0 + qb, :] = acc * lax.rsqrt(ms + NORM_EPS) * norm_ref[...] * (1.0 - lam_init)


def _diff_parts(proj_c, row0, nseq, seq, lam_p, norm_t, lam_init, latent_args):
    latent = latent_args is not None
    blk0 = row0 // seq
    in_specs = [pl.BlockSpec((seq, PROJ_C), lambda b: (blk0 + b, 0)),
                pl.BlockSpec(lam_p.shape, lambda b: (0, 0)),
                pl.BlockSpec(norm_t.shape, lambda b: (0, 0))]
    args = [proj_c, lam_p, norm_t]
    out_specs = [pl.BlockSpec((None, seq, GROUP), lambda b: (b, 0, 0))]
    out_shape = [jax.ShapeDtypeStruct((nseq, seq, GROUP), F32)]
    if latent:
        ck, cv, cos, sin = latent_args
        past = ck.shape[1]
        in_specs += [pl.BlockSpec((None, past, GROUP), lambda b: (b, 0, 0)),
                     pl.BlockSpec((None, past, GROUP), lambda b: (b, 0, 0)),
                     pl.BlockSpec(cos.shape, lambda b: (0, 0)),
                     pl.BlockSpec(sin.shape, lambda b: (0, 0))]
        args += [ck, cv, cos, sin]
    else:
        out_specs += [pl.BlockSpec((None, seq, GROUP), lambda b: (b, 0, 0))] * 2
        out_shape += [jax.ShapeDtypeStruct((nseq, seq, GROUP), F32)] * 2
    return functools.partial(_diff_kernel, latent, lam_init), in_specs, args, out_specs, out_shape, []


def _mixers(nseq, name, *parts):
    n_in = [len(p[1]) for p in parts]
    n_out = [len(p[3]) for p in parts]
    n_scr = [len(p[5]) for p in parts]

    def body(*refs):
        ins = refs[:sum(n_in)]
        outs = refs[sum(n_in):sum(n_in) + sum(n_out)]
        scr = refs[sum(n_in) + sum(n_out):]
        i0 = o0 = s0 = 0
        for part, ni, no, ns in zip(parts, n_in, n_out, n_scr):
            part[0](*ins[i0:i0 + ni], *outs[o0:o0 + no], *scr[s0:s0 + ns])
            i0, o0, s0 = i0 + ni, o0 + no, s0 + ns

    res = pl.pallas_call(
        body,
        grid=(nseq,),
        in_specs=[x for p in parts for x in p[1]],
        out_specs=[x for p in parts for x in p[3]],
        out_shape=[x for p in parts for x in p[4]],
        scratch_shapes=[x for p in parts for x in p[5]],
        compiler_params=_cparams("arbitrary"),
        name=name,
    )(*[x for p in parts for x in p[2]])
    out, o0 = [], 0
    for no in n_out:
        out.append(tuple(res[o0:o0 + no]))
        o0 += no
    return out


def _ssd_kernel(has_state, *refs):
    if has_state:
        (d_ref, cw_ref, cb_ref, dtb_ref, alog_ref, dskip_ref, norm_ref, s0_ref,
         o_ref, sfin_ref, xs_scr, bm_scr, cm_scr, xdt_scr, a_scr, st_scr, yf_scr, yb_scr) = refs
    else:
        (d_ref, cw_ref, cb_ref, dtb_ref, alog_ref, dskip_ref, norm_ref,
         o_ref, sfin_ref, xs_scr, bm_scr, cm_scr, xdt_scr, a_scr, st_scr, yf_scr, yb_scr) = refs
    seq = d_ref.shape[0]
    c = SSD_CHUNK
    nchunk = seq // c
    ngrp = 2 * SSD_STATE

    xin = d_ref[:, GROUP:GROUP + 512]
    rows = _iota(xin.shape, 0)
    prev = jnp.where(rows == 0, 0.0, pltpu.roll(xin, 1, 0))
    nxt = jnp.where(rows == seq - 1, 0.0, pltpu.roll(xin, seq - 1, 0))
    cw = cw_ref[...]
    xbc = _silu(cw[0:1] * prev + cw[1:2] * xin + cw[2:3] * nxt + cb_ref[...])
    xs = xbc[:, 0:GROUP]
    xs_scr[...] = xs
    bm_scr[...] = xbc[:, GROUP:GROUP + ngrp]
    cm_scr[...] = xbc[:, GROUP + ngrp:GROUP + 2 * ngrp]
    dt = _softplus(d_ref[:, GROUP + 512:GROUP + 640] + dtb_ref[...])
    a_scr[...] = dt * (-jnp.exp(alog_ref[...]))
    erow = _iota((128, GROUP), 0)
    ehead = _iota((128, GROUP), 1) // HEAD_DIM
    expand = tuple((erow == 4 * d + ehead).astype(F32) for d in range(2))
    for d in range(2):
        xdt_scr[d] = xs * _dot(dt, expand[d], precision=HIGHEST)
    if has_state:
        st_scr[...] = s0_ref[...]
    else:
        st_scr[...] = jnp.zeros_like(st_scr)

    row = _iota((c, c), 0)
    col = _iota((c, c), 1)
    tri = ((col <= row).astype(F32), (col >= row).astype(F32))
    keep = (col <= row, col >= row)
    grp_lane = _iota((1, ngrp), 1) // SSD_STATE
    valid = (_iota((ngrp, GROUP), 0) // SSD_STATE) == (_iota((ngrp, GROUP), 1) // (2 * HEAD_DIM))

    def chunk_step(d, r0, out_scr):
        a_c = a_scr[pl.ds(r0, c), :]
        bm_c = bm_scr[pl.ds(r0, c), :]
        cm_c = cm_scr[pl.ds(r0, c), :].astype(BF16)
        xdt_c = xdt_scr[d, pl.ds(r0, c), :]
        acum = _dot(tri[d], a_c, precision=HIGHEST)
        acum_t = acum.T
        acum_rep = _dot(acum, expand[d], precision=HIGHEST)
        bm2 = jnp.concatenate([jnp.where(grp_lane == g, bm_c, 0.0) for g in range(2)], axis=0)
        cb = _dot_nt(cm_c, bm2.astype(BF16))
        scores = []
        xparts = []
        for h in range(N_HEADS):
            lane = 4 * d + h
            seg = jnp.exp(jnp.where(keep[d], acum[:, lane:lane + 1] - acum_t[lane:lane + 1, :], -jnp.inf))
            g = h // 2
            scores.append((cb[:, g * c:(g + 1) * c] * seg).astype(BF16))
            xparts.append(jnp.where(_lane_group_mask(GROUP, h * HEAD_DIM, HEAD_DIM), xdt_c, 0.0))
        y = _dot(jnp.concatenate(scores, axis=1), jnp.concatenate(xparts, axis=0).astype(BF16))
        st = st_scr[d]
        y = y + _dot(cm_c, st.astype(BF16)) * jnp.exp(acum_rep)
        out_scr[pl.ds(r0, c), :] = y
        edge = acum_rep[c - 1:c] if d == 0 else acum_rep[0:1]
        xt = (xdt_c * jnp.exp(edge - acum_rep)).astype(BF16)
        upd = _dot_tn(bm_c.astype(BF16), xt)
        st_scr[d] = st * jnp.exp(edge) + jnp.where(valid, upd, 0.0)

    def body(n, carry):
        chunk_step(0, pl.multiple_of(n * c, c), yf_scr)
        chunk_step(1, pl.multiple_of((nchunk - 1 - n) * c, c), yb_scr)
        return carry

    lax.fori_loop(0, nchunk, body, 0)

    y = yf_scr[...] + yb_scr[...] + dskip_ref[...] * xs_scr[...]
    y = y * _silu(d_ref[:, 0:GROUP])
    o_ref[...] = y * lax.rsqrt(jnp.mean(y * y, axis=-1, keepdims=True) + NORM_EPS) * norm_ref[...]
    sfin_ref[...] = st_scr[...]


def _ssd_parts(proj_d, row0, nseq, seq, conv_w, conv_b, dt_bias_p, a_log_p, d_rep, norm, s0):
    has_state = s0 is not None
    blk0 = row0 // seq
    ngrp = 2 * SSD_STATE
    small = [conv_w, conv_b, dt_bias_p, a_log_p, d_rep, norm]
    in_specs = ([pl.BlockSpec((seq, PROJ_D), lambda b: (blk0 + b, 0))]
                + [pl.BlockSpec(s.shape, lambda b: (0, 0)) for s in small])
    args = [proj_d] + small
    if has_state:
        in_specs.append(pl.BlockSpec((None, 2, ngrp, GROUP), lambda b: (b, 0, 0, 0)))
        args.append(s0)
    out_specs = [pl.BlockSpec((None, seq, GROUP), lambda b: (b, 0, 0)),
                 pl.BlockSpec((None, 2, ngrp, GROUP), lambda b: (b, 0, 0, 0))]
    out_shape = [jax.ShapeDtypeStruct((nseq, seq, GROUP), F32),
                 jax.ShapeDtypeStruct((nseq, 2, ngrp, GROUP), F32)]
    scratch = [pltpu.VMEM((seq, GROUP), F32),
               pltpu.VMEM((seq, ngrp), F32),
               pltpu.VMEM((seq, ngrp), F32),
               pltpu.VMEM((2, seq, GROUP), F32),
               pltpu.VMEM((seq, 128), F32),
               pltpu.VMEM((2, ngrp, GROUP), F32),
               pltpu.VMEM((seq, GROUP), F32),
               pltpu.VMEM((seq, GROUP), F32)]
    return functools.partial(_ssd_kernel, has_state), in_specs, args, out_specs, out_shape, scratch


def _outproj_kernel(n_ctx_tiles, *refs):
    ctx_refs, lat_refs = refs[0:4], refs[4:8]
    w_ref, x_ref, m_ref, g_ref, b_ref, o_ref = refs[8:]
    is_ctx = pl.program_id(0) < n_ctx_tiles
    mixed = None
    for i, (c_ref, l_ref) in enumerate(zip(ctx_refs, lat_refs)):
        part = jnp.where(is_ctx, c_ref[...], l_ref[...]).astype(BF16)
        term = _dot(part, w_ref[i * GROUP:(i + 1) * GROUP, :])
        mixed = term if mixed is None else mixed + term
    o_ref[...] = _layernorm(ALPHA * x_ref[...] + m_ref[2] * mixed, g_ref[...], b_ref[...])


def _outproj(parts_ctx, parts_lat, w_out_bf, x, mods_l, ln_g, ln_b, t_ctx, s_lat):
    t, d = x.shape
    tm = 512
    n_ctx = t_ctx // tm
    n_lat = (t - t_ctx) // tm
    ctx_spec = pl.BlockSpec((tm, GROUP), lambda i: (jnp.minimum(i, n_ctx - 1), 0))
    lat_spec = pl.BlockSpec((tm, GROUP), lambda i: (jnp.clip(i - n_ctx, 0, n_lat - 1), 0))
    return pl.pallas_call(
        functools.partial(_outproj_kernel, n_ctx),
        grid=(t // tm,),
        in_specs=[ctx_spec] * 4 + [lat_spec] * 4
        + [pl.BlockSpec(w_out_bf.shape, lambda i: (0, 0)),
           pl.BlockSpec((tm, d), lambda i: (i, 0)),
           pl.BlockSpec((None, 6, 1, d), _mod_row_map(tm, t_ctx, s_lat)),
           pl.BlockSpec((1, d), lambda i: (0, 0)),
           pl.BlockSpec((1, d), lambda i: (0, 0))],
        out_specs=pl.BlockSpec((tm, d), lambda i: (i, 0)),
        out_shape=jax.ShapeDtypeStruct((t, d), F32),
        compiler_params=_cparams("arbitrary"),
        name="outproj_ln",
    )(*parts_ctx, *parts_lat, w_out_bf, x, mods_l, ln_g, ln_b)


def _top_rows(s, k, extra=()):
    r = s.shape[0]
    rid = _iota(s.shape, 0).astype(F32)
    vals, ids = [], []
    picked = [[] for _ in extra]
    for _ in range(k):
        m = jnp.max(s, axis=0, keepdims=True)
        cand = jnp.where(s == m, rid, float(r))
        i = jnp.min(cand, axis=0, keepdims=True)
        hit = cand == i
        vals.append(m)
        ids.append(i)
        for lst, arr in zip(picked, extra):
            lst.append(jnp.max(jnp.where(hit, arr, -1.0), axis=0, keepdims=True))
        s = jnp.where(hit, -jnp.inf, s)
    cat = lambda xs: jnp.concatenate(xs, axis=0)
    return cat(vals), cat(ids), [cat(p) for p in picked]


def _router_kernel(x_ref, m_ref, wqt_ref, keys_ref, h_ref, a_ref, b_ref, g_ref):
    tm = x_ref.shape[0]
    hb = (x_ref[...] * (1.0 + m_ref[4]) + m_ref[3]).astype(BF16)
    h_ref[...] = hb
    qt = _dot_nt(wqt_ref[...], hb).astype(BF16)
    k = PEER_TOPK
    a_rows, b_rows, g_rows = [], [], []
    for head in range(PEER_HEADS):
        tv, ti = [], []
        for half in range(2):
            g = 2 * head + half
            sc = _dot(keys_ref[g], qt[g * PEER_HALF:(g + 1) * PEER_HALF])
            v, i, _ = _top_rows(sc, k)
            tv.append(v)
            ti.append(i)
        cs = [tv[0][0:1] + tv[1]]
        ca = [jnp.broadcast_to(ti[0][0:1], (k, tm))]
        cb = [ti[1]]
        for k1 in range(1, 4):
            cs.append(tv[0][k1:k1 + 1] + tv[1][0:8])
            ca.append(jnp.broadcast_to(ti[0][k1:k1 + 1], (8, tm)))
            cb.append(ti[1][0:8])
        low = _iota((8, tm), 0) < 4
        v2_dup = jnp.where(low, tv[1][0:8], pltpu.roll(tv[1][0:8], 4, 0))
        i2_dup = jnp.where(low, ti[1][0:8], pltpu.roll(ti[1][0:8], 4, 0))
        for k1 in (4, 6):
            cs.append(jnp.where(low, tv[0][k1:k1 + 1], tv[0][k1 + 1:k1 + 2]) + v2_dup)
            ca.append(jnp.where(low, ti[0][k1:k1 + 1], ti[0][k1 + 1:k1 + 2]))
            cb.append(i2_dup)
        cs.append(tv[0][8:16] + tv[1][0:1])
        ca.append(ti[0][8:16])
        cb.append(jnp.broadcast_to(ti[1][0:1], (8, tm)))
        best, _, (sel_a, sel_b) = _top_rows(jnp.concatenate(cs, axis=0), k,
                                            extra=(jnp.concatenate(ca, axis=0), jnp.concatenate(cb, axis=0)))
        e = jnp.exp(best - best[0:1])
        g_rows.append(e / jnp.sum(e, axis=0, keepdims=True))
        a_rows.append(sel_a)
        b_rows.append(sel_b)
    a_ref[...] = jnp.concatenate(a_rows, axis=0).T.astype(I32)
    b_ref[...] = jnp.concatenate(b_rows, axis=0).T.astype(I32)
    g_ref[...] = jnp.concatenate(g_rows, axis=0).T


def _router(x, mods_l, wq_t, keys, t_ctx, s_lat):
    t, d = x.shape
    tm = 128
    nslot = PEER_HEADS * PEER_TOPK
    return pl.pallas_call(
        _router_kernel,
        grid=(t // tm,),
        in_specs=[pl.BlockSpec((tm, d), lambda i: (i, 0)),
                  pl.BlockSpec((None, 6, 1, d), _mod_row_map(tm, t_ctx, s_lat)),
                  pl.BlockSpec(wq_t.shape, lambda i: (0, 0)),
                  pl.BlockSpec(keys.shape, lambda i: (0, 0, 0))],
        out_specs=[pl.BlockSpec((tm, d), lambda i: (i, 0)),
                   pl.BlockSpec((tm, nslot), lambda i: (i, 0)),
                   pl.BlockSpec((tm, nslot), lambda i: (i, 0)),
                   pl.BlockSpec((tm, nslot), lambda i: (i, 0))],
        out_shape=[jax.ShapeDtypeStruct((t, d), BF16),
                   jax.ShapeDtypeStruct((t, nslot), I32),
                   jax.ShapeDtypeStruct((t, nslot), I32),
                   jax.ShapeDtypeStruct((t, nslot), F32)],
        compiler_params=_cparams("arbitrary"),
        name="peer_router",
    )(x, mods_l, wq_t, keys)


def _gates_kernel(a_ref, b_ref, g_ref, u_ref, v_ref, o_ref, ub_ref, vb_ref):
    tm = a_ref.shape[0]
    n = PEER_KEYS
    sub = 16
    ub_ref[...] = u_ref[...].astype(BF16)
    vb_ref[...] = v_ref[...].astype(BF16)
    key = _iota((sub, n, a_ref.shape[2]), 1).astype(F32).astype(BF16)
    zero = jnp.zeros((), BF16)
    for t0 in range(0, tm, sub):
        a = a_ref[t0:t0 + sub].astype(F32).astype(BF16)
        b = b_ref[t0:t0 + sub].astype(F32).astype(BF16)
        g = g_ref[t0:t0 + sub].astype(BF16)
        onehot_a = jnp.where(key == a, jnp.ones((), BF16), zero)
        gated_b = jnp.where(key == b, g, zero)
        w = lax.dot_general(onehot_a, gated_b, (((2,), (2,)), ((0,), (0,))),
                            preferred_element_type=F32)
        w_t = jnp.swapaxes(w.astype(BF16), 0, 1)
        for r in range(n):
            o_ref[t0:t0 + sub, r * n:(r + 1) * n] = w_t[r]


def _gates(a_idx, b_idx, gate, peer_u, peer_v, layer):
    t, nslot = a_idx.shape
    d = peer_u.shape[-1]
    tm = 96
    steps = t // tm
    te = N_EXPERTS // steps
    spec = pl.BlockSpec((tm, 1, nslot), lambda i: (i, 0, 0))
    tab_in = pl.BlockSpec((None, te, d), lambda i: (layer, i, 0))
    tab_out = pl.BlockSpec((te, d), lambda i: (i, 0))
    return pl.pallas_call(
        _gates_kernel,
        grid=(steps,),
        in_specs=[spec, spec, spec, tab_in, tab_in],
        out_specs=[pl.BlockSpec((tm, N_EXPERTS), lambda i: (i, 0)), tab_out, tab_out],
        out_shape=[jax.ShapeDtypeStruct((t, N_EXPERTS), BF16),
                   jax.ShapeDtypeStruct((N_EXPERTS, d), BF16),
                   jax.ShapeDtypeStruct((N_EXPERTS, d), BF16)],
        compiler_params=_cparams("arbitrary"),
        name="peer_gates",
    )(a_idx.reshape(t, 1, nslot), b_idx.reshape(t, 1, nslot), gate.reshape(t, 1, nslot), peer_u, peer_v)


def _experts_kernel(h_ref, u_ref, v_ref, w_ref, x_ref, m_ref, g_ref, b_ref, o_ref, acc_ref):
    j = pl.program_id(1)

    @pl.when(j == 0)
    def _():
        acc_ref[...] = jnp.zeros_like(acc_ref)

    act = _gelu_tanh(_dot_nt(h_ref[...], u_ref[...]))
    acc_ref[...] += _dot((act * w_ref[...].astype(F32)).astype(BF16), v_ref[...])

    @pl.when(j == pl.num_programs(1) - 1)
    def _():
        o_ref[...] = _layernorm(ALPHA * x_ref[...] + m_ref[5] * acc_ref[...], g_ref[...], b_ref[...])


def _experts(h_bf, u_bf, v_bf, w_gate, x, mods_l, ln_g, ln_b, t_ctx, s_lat):
    t, d = x.shape
    tm, te = 1024, 1024
    return pl.pallas_call(
        _experts_kernel,
        grid=(t // tm, N_EXPERTS // te),
        in_specs=[pl.BlockSpec((tm, d), lambda i, j: (i, 0)),
                  pl.BlockSpec((te, d), lambda i, j: (j, 0)),
                  pl.BlockSpec((te, d), lambda i, j: (j, 0)),
                  pl.BlockSpec((tm, te), lambda i, j: (i, j)),
                  pl.BlockSpec((tm, d), lambda i, j: (i, 0)),
                  pl.BlockSpec((None, 6, 1, d), _mod_row_map(tm, t_ctx, s_lat)),
                  pl.BlockSpec((1, d), lambda i, j: (0, 0)),
                  pl.BlockSpec((1, d), lambda i, j: (0, 0))],
        out_specs=pl.BlockSpec((tm, d), lambda i, j: (i, 0)),
        out_shape=jax.ShapeDtypeStruct((t, d), F32),
        scratch_shapes=[pltpu.VMEM((tm, d), F32)],
        compiler_params=_cparams("arbitrary", "arbitrary"),
        name="peer_experts",
    )(h_bf, u_bf, v_bf, w_gate, x, mods_l, ln_g, ln_b)


def _pad_cols(w, n):
    return jnp.concatenate([w, jnp.zeros((w.shape[0], n), w.dtype)], axis=1) if n else w


def _layout_w_in(w):
    a = w[:, 0:1280]
    mcq, mckv, mkpe = w[:, 1280:1472], w[:, 1472:1600], w[:, 1600:1632]
    b = jnp.concatenate([mckv, _pad_cols(mcq, 64), mkpe, mkpe, mkpe, mkpe], axis=1)
    c = w[:, 1632:2400]
    d = _pad_cols(w[:, 2400:3176], PROJ_D - 776)
    return jnp.concatenate([a, b, c, d], axis=1).astype(BF16)


def _layout_w_qb(w):
    w4 = w.reshape(MLA_Q_RANK, N_HEADS, MLA_NOPE + MLA_ROPE)
    return jnp.concatenate([w4[:, :, :MLA_NOPE].reshape(MLA_Q_RANK, -1),
                            w4[:, :, MLA_NOPE:].reshape(MLA_Q_RANK, -1)], axis=1).astype(BF16)


def _layout_w_kvb(w):
    w4 = w.reshape(MLA_KV_RANK, N_HEADS, MLA_NOPE + HEAD_DIM)
    return jnp.concatenate([w4[:, :, :MLA_NOPE].reshape(MLA_KV_RANK, -1),
                            w4[:, :, MLA_NOPE:].reshape(MLA_KV_RANK, -1)], axis=1).astype(BF16)


def _rope_tables(seq):
    rows = seq // GRID_W
    row = jnp.repeat(jnp.arange(rows, dtype=F32), GRID_W)
    col = jnp.tile(jnp.arange(GRID_W, dtype=F32), rows)
    freqs = ROPE_BASE ** (-jnp.arange(ROPE_PAIRS, dtype=F32) / ROPE_PAIRS)
    cos_l, sin_l = [], []
    for pos in (row, col):
        ang = pos[:, None] * freqs
        cos_l += [jnp.cos(ang), jnp.cos(ang)]
        sin_l += [-jnp.sin(ang), jnp.sin(ang)]
    return jnp.concatenate(cos_l, axis=1), jnp.concatenate(sin_l, axis=1)


def _hgrn_state_pack(st):
    b = st.shape[0]
    st_t = jnp.swapaxes(st, -1, -2)
    zero = jnp.zeros_like(st_t[:, :, 0])
    rows = [jnp.concatenate([st_t[:, :, h] if g == h else zero for g in range(N_HEADS)], axis=-1)
            for h in range(N_HEADS)]
    return jnp.concatenate(rows, axis=-2).reshape(b, 2, GROUP, GROUP)


def _hgrn_state_unpack(sb):
    blocks = [sb[:, :, h * HEAD_DIM:(h + 1) * HEAD_DIM, h * HEAD_DIM:(h + 1) * HEAD_DIM]
              for h in range(N_HEADS)]
    return jnp.swapaxes(jnp.stack(blocks, axis=2), -1, -2)


def _ssd_state_pack(st):
    st_t = jnp.swapaxes(st, -1, -2)
    zero = jnp.zeros_like(st_t[:, :, 0])
    rows = [jnp.concatenate([st_t[:, :, h] if h // 2 == g else zero for h in range(N_HEADS)], axis=-1)
            for g in range(2)]
    return jnp.concatenate(rows, axis=-2)


def _ssd_state_unpack(sb):
    blocks = [sb[:, :, (h // 2) * SSD_STATE:(h // 2 + 1) * SSD_STATE, h * HEAD_DIM:(h + 1) * HEAD_DIM]
              for h in range(N_HEADS)]
    return jnp.swapaxes(jnp.stack(blocks, axis=2), -1, -2)


def _tile_lanes(v, n):
    return jnp.tile(v.reshape(1, -1), (1, n))


def kernel(x_prompt, x_sample, cache_mla_ckv, cache_mla_kpe, cache_diff_k, cache_diff_v, state_hgrn, state_ssd, c, c_ctx, w_mod, b_mod, w_in, hgrn_lb, hgrn_norm, mla_q_norm, mla_w_qb, mla_kv_norm, mla_w_kvb, diff_lambda, diff_norm, ssd_conv_w, ssd_conv_b, ssd_dt_bias, ssd_a_log, ssd_d, ssd_norm, w_out, ln1_g, ln1_b, peer_wq, peer_keys, peer_u, peer_v, ln2_g, ln2_b):
    nb, seq, d = x_prompt.shape
    nlat, lseq, _ = x_sample.shape
    depth = w_in.shape[0]
    t_ctx = nb * seq
    x = jnp.concatenate([x_prompt.reshape(t_ctx, d), x_sample.reshape(nlat * lseq, d)], axis=0)

    cond8 = jnp.concatenate([c_ctx.reshape(1, d), c, jnp.zeros((8 - 1 - nlat, d), F32)], axis=0)
    mods = _mods(cond8, w_mod, b_mod)
    mods = mods[:, :1 + nlat].reshape(depth, 1 + nlat, 6, 1, d)

    cos32, sin32 = _rope_tables(lseq)
    cos128, sin128 = jnp.tile(cos32, (1, 4)), jnp.tile(sin32, (1, 4))
    cos256, sin256 = jnp.tile(cos32, (1, 8)), jnp.tile(sin32, (1, 8))

    produced = []
    for l in range(depth):
        mods_l = mods[l]
        pa, pb, pc, pd = _inproj(x, mods_l, _layout_w_in(w_in[l]), t_ctx, lseq)

        norm_hg = _tile_lanes(hgrn_norm[l], N_HEADS)
        mla_w = (mla_q_norm[l].reshape(1, -1), _layout_w_qb(mla_w_qb[l]),
                 mla_kv_norm[l].reshape(1, -1), _layout_w_kvb(mla_w_kvb[l]))
        lam_init = 0.8 - 0.6 * math.exp(-0.3 * l)
        norm_df = _tile_lanes(diff_norm[l], N_HEADS)
        past = cache_diff_k.shape[2]
        ssd_w = (ssd_conv_w[l], ssd_conv_b[l].reshape(1, -1),
                 _pad_cols(ssd_dt_bias[l].reshape(1, -1), 120), _pad_cols(ssd_a_log[l].reshape(1, -1), 120),
                 jnp.repeat(ssd_d[l], HEAD_DIM).reshape(1, -1), ssd_norm[l].reshape(1, -1))

        (mla_ctx, new_ckv, new_kpe), (df_ctx, new_dk, new_dv), (ssd_ctx, ssd_fin), (hg_ctx, hg_fin) = _mixers(
            nb, "mixers_ctx",
            _mla_parts(pb, 0, nb, seq, *mla_w, None),
            _diff_parts(pc, 0, nb, seq, diff_lambda[l], norm_df, lam_init, None),
            _ssd_parts(pd, 0, nb, seq, *ssd_w, None),
            _hgrn_parts(pa, 0, nb, seq, hgrn_lb, norm_hg, None, l))
        ((mla_lat,),) = _mixers(
            nlat, "mla_lat",
            _mla_parts(pb, t_ctx, nlat, lseq, *mla_w,
                       (cache_mla_ckv[:, l], jnp.tile(cache_mla_kpe[:, l], (1, 1, 4)), cos128, sin128)))
        ((df_lat,),) = _mixers(
            nlat, "diffattn_lat",
            _diff_parts(pc, t_ctx, nlat, lseq, diff_lambda[l], norm_df, lam_init,
                        (cache_diff_k[:, l].reshape(nlat, past, GROUP),
                         cache_diff_v[:, l].reshape(nlat, past, GROUP), cos256, sin256)))
        ((ssd_lat, _),) = _mixers(nlat, "ssd_lat",
                                  _ssd_parts(pd, t_ctx, nlat, lseq, *ssd_w, _ssd_state_pack(state_ssd[:, l])))
        ((hg_lat, _),) = _mixers(nlat, "hgrn_lat",
                                 _hgrn_parts(pa, t_ctx, nlat, lseq, hgrn_lb, norm_hg,
                                             _hgrn_state_pack(state_hgrn[:, l]), l))

        parts_ctx = [a.reshape(t_ctx, GROUP) for a in (hg_ctx, mla_ctx, df_ctx, ssd_ctx)]
        parts_lat = [a.reshape(nlat * lseq, GROUP) for a in (hg_lat, mla_lat, df_lat, ssd_lat)]
        x = _outproj(parts_ctx, parts_lat, w_out[l].astype(BF16), x, mods_l,
                     ln1_g[l].reshape(1, d), ln1_b[l].reshape(1, d), t_ctx, lseq)

        keys = peer_keys[l].reshape(2 * PEER_HEADS, PEER_KEYS, PEER_HALF).astype(BF16)
        h_bf, a_idx, b_idx, gate = _router(x, mods_l, peer_wq[l].T.astype(BF16), keys, t_ctx, lseq)
        w_gate, u_bf, v_bf = _gates(a_idx, b_idx, gate, peer_u, peer_v, l)
        x = _experts(h_bf, u_bf, v_bf, w_gate, x, mods_l,
                     ln2_g[l].reshape(1, d), ln2_b[l].reshape(1, d), t_ctx, lseq)

        produced.append((new_ckv, new_kpe,
                         new_dk.reshape(nb, seq, N_HEADS, 2, DIFF_DIM),
                         new_dv.reshape(nb, seq, N_HEADS, 2 * DIFF_DIM),
                         _hgrn_state_unpack(hg_fin), _ssd_state_unpack(ssd_fin)))

    y_prompt = x[:t_ctx].reshape(nb, seq, d)
    y_sample = x[t_ctx:].reshape(nlat, lseq, d)
    stacked = tuple(jnp.stack([p[i] for p in produced], axis=1) for i in range(6))
    return (y_prompt, y_sample) + stacked
```

```python
import functools
import math

import jax
import jax.numpy as jnp
from jax import lax
from jax.experimental import pallas as pl
from jax.experimental.pallas import tpu as pltpu

F32 = jnp.float32
BF16 = jnp.bfloat16
I32 = jnp.int32
HIGHEST = lax.Precision.HIGHEST

D_MODEL = 1024
GROUP = 256
N_HEADS = 4
HEAD_DIM = 64
HG_BLOCK = 16
HG_SLAB = 256
HG_UNROLL = 8
SSD_CHUNK = 128
SSD_STATE = 64
MLA_Q_RANK = 192
MLA_KV_RANK = 128
MLA_NOPE = 64
MLA_ROPE = 32
DIFF_DIM = 32
GRID_W = 64
ROPE_PAIRS = 8
ROPE_BASE = 10000.0
PEER_HEADS = 8
PEER_KEYS = 128
PEER_TOPK = 16
PEER_HALF = 64
N_EXPERTS = PEER_KEYS * PEER_KEYS
NORM_EPS = 1e-6
LN_EPS = 1e-5
DEPTH = 2
ALPHA = (2.0 * DEPTH) ** 0.25
LOG2_E = 1.4426950408889634

PROJ_A = 5 * GROUP
PROJ_B = 512
PROJ_C = 3 * GROUP
PROJ_D = 896
VMEM_LIMIT = 56 * 1024 * 1024


def _cparams(*sem):
    return pltpu.CompilerParams(dimension_semantics=sem, vmem_limit_bytes=VMEM_LIMIT)


def _sigmoid(x):
    return 1.0 / (1.0 + jnp.exp(-x))


def _silu(x):
    return x * _sigmoid(x)


def _softplus(x):
    return jnp.maximum(x, 0.0) + jnp.log(1.0 + jnp.exp(-jnp.abs(x)))


def _gelu_tanh(x):
    return 0.5 * x * (1.0 + jnp.tanh(math.sqrt(2.0 / math.pi) * (x + 0.044715 * (x * x * x))))


def _dot(a, b, precision=None):
    return jnp.dot(a, b, preferred_element_type=F32, precision=precision)


def _dot_nt(a, b, precision=None):
    return lax.dot_general(a, b, (((1,), (1,)), ((), ())), preferred_element_type=F32,
                           precision=precision)


def _dot_tn(a, b, precision=None):
    return lax.dot_general(a, b, (((0,), (0,)), ((), ())), preferred_element_type=F32,
                           precision=precision)


def _iota(shape, dim):
    return lax.broadcasted_iota(I32, shape, dim)


def _block_mask(rows, cols, rblk, cblk):
    return (_iota((rows, cols), 0) // rblk) == (_iota((rows, cols), 1) // cblk)


def _lane_group_mask(width, start, size):
    lane = _iota((1, width), 1)
    return (lane >= start) & (lane < start + size)


def _layernorm(v, g, b):
    mu = jnp.mean(v, axis=-1, keepdims=True)
    d = v - mu
    var = jnp.mean(d * d, axis=-1, keepdims=True)
    return d * lax.rsqrt(var + LN_EPS) * g + b


def _swap_halves16(x):
    width = x.shape[-1]
    lane = _iota(x.shape, x.ndim - 1)
    up = pltpu.roll(x, width - 8, x.ndim - 1)
    down = pltpu.roll(x, 8, x.ndim - 1)
    return jnp.where((lane % 16) < 8, up, down)


def _rope(x, cos, sin_signed):
    return x * cos + _swap_halves16(x) * sin_signed


def _mods_kernel(c_ref, w_ref, b_ref, o_ref):
    s = _silu(c_ref[...]).astype(BF16)
    o_ref[...] = _dot(s, w_ref[...].astype(BF16)) + b_ref[...]


def _mods(cond8, w_mod, b_mod):
    depth, d, n = w_mod.shape
    tn = 1536
    return pl.pallas_call(
        _mods_kernel,
        grid=(depth, n // tn),
        in_specs=[pl.BlockSpec((8, d), lambda l, j: (0, 0)),
                  pl.BlockSpec((None, d, tn), lambda l, j: (l, 0, j)),
                  pl.BlockSpec((None, 1, tn), lambda l, j: (l, 0, j))],
        out_specs=pl.BlockSpec((None, 8, tn), lambda l, j: (l, 0, j)),
        out_shape=jax.ShapeDtypeStruct((depth, 8, n), F32),
        compiler_params=_cparams("arbitrary", "arbitrary"),
        name="mods",
    )(cond8, w_mod, b_mod.reshape(depth, 1, n))


def _mod_row_map(tm, t_ctx, s_lat):
    def index_map(i, *_):
        start = i * tm
        return (jnp.where(start < t_ctx, 0, 1 + (start - t_ctx) // s_lat), 0, 0, 0)
    return index_map


def _inproj_kernel(x_ref, m_ref, w_ref, oa_ref, ob_ref, oc_ref, od_ref):
    h = (x_ref[...] * (1.0 + m_ref[1]) + m_ref[0]).astype(BF16)
    start = 0
    for o_ref in (oa_ref, ob_ref, oc_ref, od_ref):
        width = o_ref.shape[-1]
        o_ref[...] = _dot(h, w_ref[:, start:start + width])
        start += width


def _inproj(x, mods_l, w_in_p, t_ctx, s_lat):
    t, d = x.shape
    tm = 512
    widths = (PROJ_A, PROJ_B, PROJ_C, PROJ_D)
    return pl.pallas_call(
        _inproj_kernel,
        grid=(t // tm,),
        in_specs=[pl.BlockSpec((tm, d), lambda i: (i, 0)),
                  pl.BlockSpec((None, 6, 1, d), _mod_row_map(tm, t_ctx, s_lat)),
                  pl.BlockSpec(w_in_p.shape, lambda i: (0, 0))],
        out_specs=[pl.BlockSpec((tm, w), lambda i: (i, 0)) for w in widths],
        out_shape=[jax.ShapeDtypeStruct((t, w), F32) for w in widths],
        compiler_params=_cparams("arbitrary"),
        name="inproj",
    )(x, mods_l, w_in_p)


def _hgrn_kernel(layer, has_state, *refs):
    if has_state:
        (a_ref, lb_ref, norm_ref, s0_ref, o_ref, sfin_ref,
         q_scr, k_scr, bc_scr, dec_scr, qt_scr, kt_scr, st_scr, o_scr) = refs
    else:
        (a_ref, lb_ref, norm_ref, o_ref, sfin_ref,
         q_scr, k_scr, bc_scr, dec_scr, qt_scr, kt_scr, st_scr, o_scr) = refs
        s0_ref = None
    seq = a_ref.shape[0]
    c = HG_BLOCK
    nblk = seq // c
    slab = HG_SLAB
    nb = slab // c

    lbp = lb_ref[...]
    e = jnp.exp(lbp - jnp.max(lbp, axis=0, keepdims=True))
    p = e / jnp.sum(e, axis=0, keepdims=True)
    lower = jnp.sum(p[1:layer + 1], axis=0) if layer > 0 else jnp.zeros_like(p[0])

    q = _silu(a_ref[:, 0:GROUP])
    q_scr[...] = q
    srow = _iota((slab, slab), 0)
    scol = _iota((slab, slab), 1)
    same = (srow // c) == (scol // c)
    cum_op = (jnp.where(same & (scol <= srow), 1.0, 0.0), jnp.where(same & (scol >= srow), 1.0, 0.0))
    for d in range(2):
        lb = lower[d:d + 1]
        f = lb + (1.0 - lb) * _sigmoid(a_ref[:, (1 + d) * GROUP:(2 + d) * GROUP])
        k = 1.0 - f
        lf = jnp.log(f)
        k_scr[d] = k
        for s0 in range(0, seq, slab):
            bc = _dot(cum_op[d], lf[s0:s0 + slab], precision=HIGHEST)
            bc3 = bc.reshape(nb, c, GROUP)
            edge = bc3[:, c - 1:c, :] if d == 0 else bc3[:, 0:1, :]
            tot = jnp.broadcast_to(edge, (nb, c, GROUP)).reshape(slab, GROUP)
            bc_scr[d, s0:s0 + slab, :] = bc * LOG2_E
            dec_scr[d, s0:s0 + slab, :] = jnp.exp(tot)
            qt_scr[d, s0:s0 + slab, :] = (q[s0:s0 + slab] * jnp.exp(bc)).astype(BF16)
            kt_scr[d, s0:s0 + slab, :] = (k[s0:s0 + slab] * jnp.exp(tot - bc)).astype(BF16)
    if has_state:
        st_scr[...] = s0_ref[...]
    else:
        st_scr[...] = jnp.zeros_like(st_scr)

    bd_ones = _block_mask(GROUP, GROUP, HEAD_DIM, HEAD_DIM).astype(BF16)
    rib = _iota((1, c, GROUP), 1)

    def slab_step(i, carry):
        r0 = pl.multiple_of(i * slab, slab)
        q3 = q_scr[pl.ds(r0, slab), :].reshape(nb, c, GROUP)
        v3 = a_ref[pl.ds(r0, slab), 3 * GROUP:4 * GROUP].reshape(nb, c, GROUP)
        o3 = jnp.zeros((nb, c, GROUP), F32)
        for d in range(2):
            bc3 = bc_scr[d, pl.ds(r0, slab), :].reshape(nb, c, GROUP)
            k3 = k_scr[d, pl.ds(r0, slab), :].reshape(nb, c, GROUP)
            for j in range(c):
                keep = (rib >= j) if d == 0 else (rib <= j)
                dec = jnp.exp2(jnp.where(keep, bc3 - bc3[:, j:j + 1, :], -jnp.inf))
                pj = (dec * q3 * k3[:, j:j + 1, :]).astype(BF16).reshape(slab, GROUP)
                srep = _dot(pj, bd_ones).reshape(nb, c, GROUP)
                o3 = o3 + srep * v3[:, j:j + 1, :]
        o_scr[0, pl.ds(r0, slab), :] = o3.reshape(slab, GROUP)
        return carry

    lax.fori_loop(0, seq // slab, slab_step, 0)

    bd_mask = _block_mask(GROUP, GROUP, HEAD_DIM, HEAD_DIM)

    def body(n, carry):
        rows = [[pl.multiple_of(((n * HG_UNROLL + u) if d == 0 else nblk - 1 - (n * HG_UNROLL + u)) * c, c)
                 for u in range(HG_UNROLL)] for d in range(2)]
        upd = [[_dot_tn(a_ref[pl.ds(r0, c), 3 * GROUP:4 * GROUP].astype(BF16), kt_scr[d, pl.ds(r0, c), :])
                for r0 in rows[d]] for d in range(2)]
        for d in range(2):
            st = st_scr[d]
            for u, r0 in enumerate(rows[d]):
                o_scr[1 + d, pl.ds(r0, c), :] = _dot_nt(qt_scr[d, pl.ds(r0, c), :], st.astype(BF16))
                st = st * dec_scr[d, pl.ds(r0, 1), :] + jnp.where(bd_mask, upd[d][u], 0.0)
            st_scr[d] = st
        return carry

    lax.fori_loop(0, nblk // HG_UNROLL, body, 0)

    o = o_scr[0] + o_scr[1] + o_scr[2]
    mean_op = jnp.where(_block_mask(GROUP, GROUP, HEAD_DIM, HEAD_DIM), 1.0 / HEAD_DIM, 0.0)
    ms = _dot(o * o, mean_op, precision=HIGHEST)
    y = o * lax.rsqrt(ms + NORM_EPS) * norm_ref[...]
    o_ref[...] = y * _silu(a_ref[:, 4 * GROUP:5 * GROUP])
    sfin_ref[...] = st_scr[...]


def _layer_slot(nseq, stack, tail):
    zeros = (0,) * len(tail)
    if stack is None:
        return (pl.BlockSpec((None,) + tail, lambda b: (b,) + zeros),
                jax.ShapeDtypeStruct((nseq,) + tail, F32))
    layer, depth = stack
    return (pl.BlockSpec((None, None) + tail, lambda b: (b, layer) + zeros),
            jax.ShapeDtypeStruct((nseq, depth) + tail, F32))


def _hgrn_parts(proj_a, row0, nseq, seq, hgrn_lb, norm_t, s0, layer, stack=None):
    has_state = s0 is not None
    blk0 = row0 // seq
    in_specs = [pl.BlockSpec((seq, PROJ_A), lambda b: (blk0 + b, 0)),
                pl.BlockSpec(hgrn_lb.shape, lambda b: (0, 0, 0)),
                pl.BlockSpec((1, GROUP), lambda b: (0, 0))]
    args = [proj_a, hgrn_lb, norm_t]
    if has_state:
        in_specs.append(pl.BlockSpec((None, 2, GROUP, GROUP), lambda b: (b, 0, 0, 0)))
        args.append(s0)
    out_specs, out_shape = zip(_layer_slot(nseq, None, (seq, GROUP)),
                               _layer_slot(nseq, stack, (2, GROUP, GROUP)))
    out_specs, out_shape = list(out_specs), list(out_shape)
    scratch = [pltpu.VMEM((seq, GROUP), F32),
               pltpu.VMEM((2, seq, GROUP), F32),
               pltpu.VMEM((2, seq, GROUP), F32),
               pltpu.VMEM((2, seq, GROUP), F32),
               pltpu.VMEM((2, seq, GROUP), BF16),
               pltpu.VMEM((2, seq, GROUP), BF16),
               pltpu.VMEM((2, GROUP, GROUP), F32),
               pltpu.VMEM((3, seq, GROUP), F32)]
    return functools.partial(_hgrn_kernel, layer, has_state), in_specs, args, out_specs, out_shape, scratch


def _mla_kernel(latent, *refs):
    if latent:
        (b_ref, qn_ref, wq_ref, kvn_ref, wkv_ref, cckv_ref, ckpe_ref, cos_ref, sin_ref,
         o_ref) = refs
    else:
        (b_ref, qn_ref, wq_ref, kvn_ref, wkv_ref, o_ref, ckv_ref, kpe_ref) = refs
    seq = b_ref.shape[0]
    mckv = b_ref[:, 0:MLA_KV_RANK]
    mcq = b_ref[:, MLA_KV_RANK:MLA_KV_RANK + MLA_Q_RANK]
    kpe_t = b_ref[:, 384:512]

    cq = mcq * lax.rsqrt(jnp.mean(mcq * mcq, axis=-1, keepdims=True) + NORM_EPS) * qn_ref[...]
    qf = _dot(cq.astype(BF16), wq_ref[...])
    ckv = mckv * lax.rsqrt(jnp.mean(mckv * mckv, axis=-1, keepdims=True) + NORM_EPS) * kvn_ref[...]
    q_nope = qf[:, 0:N_HEADS * MLA_NOPE]
    q_rope = qf[:, N_HEADS * MLA_NOPE:]
    if latent:
        cos = cos_ref[...]
        sin = sin_ref[...]
        q_rope = _rope(q_rope, cos, sin)
        ckv_all = jnp.concatenate([cckv_ref[...], ckv], axis=0)
        kpe_all = jnp.concatenate([ckpe_ref[...], _rope(kpe_t, cos, sin)], axis=0)
    else:
        ckv_ref[...] = ckv
        kpe_ref[...] = kpe_t[:, 0:MLA_ROPE]
        ckv_all = ckv
        kpe_all = kpe_t
    kv = _dot(ckv_all.astype(BF16), wkv_ref[...])
    kcat = jnp.concatenate([kv[:, 0:GROUP], kpe_all], axis=1).astype(BF16)
    v = kv[:, GROUP:].astype(BF16)
    qcat = jnp.concatenate([q_nope, q_rope], axis=1)
    scale = (MLA_NOPE + MLA_ROPE) ** -0.5
    qb = min(seq, 256)
    width = qcat.shape[1]
    for r0 in range(0, seq, qb):
        qblk = qcat[r0:r0 + qb]
        acc = jnp.zeros((qb, GROUP), F32)
        for h in range(N_HEADS):
            hm = (_lane_group_mask(width, h * MLA_NOPE, MLA_NOPE)
                  | _lane_group_mask(width, N_HEADS * MLA_NOPE + h * MLA_ROPE, MLA_ROPE))
            s = _dot_nt(jnp.where(hm, qblk, 0.0).astype(BF16), kcat)
            e = jnp.exp2((s - jnp.max(s, axis=-1, keepdims=True)) * (scale * LOG2_E))
            z = jnp.sum(e, axis=-1, keepdims=True)
            oh = _dot(e.astype(BF16), v) / z
            acc = acc + jnp.where(_lane_group_mask(GROUP, h * HEAD_DIM, HEAD_DIM), oh, 0.0)
        o_ref[r0:r0 + qb, :] = acc


def _mla_parts(proj_b, row0, nseq, seq, q_norm, w_qb_p, kv_norm, w_kvb_p, latent_args, stack=None):
    latent = latent_args is not None
    blk0 = row0 // seq
    in_specs = [pl.BlockSpec((seq, PROJ_B), lambda b: (blk0 + b, 0)),
                pl.BlockSpec(q_norm.shape, lambda b: (0, 0)),
                pl.BlockSpec(w_qb_p.shape, lambda b: (0, 0)),
                pl.BlockSpec(kv_norm.shape, lambda b: (0, 0)),
                pl.BlockSpec(w_kvb_p.shape, lambda b: (0, 0))]
    args = [proj_b, q_norm, w_qb_p, kv_norm, w_kvb_p]
    out_specs = [pl.BlockSpec((None, seq, GROUP), lambda b: (b, 0, 0))]
    out_shape = [jax.ShapeDtypeStruct((nseq, seq, GROUP), F32)]
    if latent:
        cckv, ckpe_t, cos, sin = latent_args
        past = cckv.shape[1]
        in_specs += [pl.BlockSpec((None, past, MLA_KV_RANK), lambda b: (b, 0, 0)),
                     pl.BlockSpec((None, past, 128), lambda b: (b, 0, 0)),
                     pl.BlockSpec(cos.shape, lambda b: (0, 0)),
                     pl.BlockSpec(sin.shape, lambda b: (0, 0))]
        args += [cckv, ckpe_t, cos, sin]
    else:
        for width in (MLA_KV_RANK, MLA_ROPE):
            spec, shape = _layer_slot(nseq, stack, (seq, width))
            out_specs.append(spec)
            out_shape.append(shape)
    return functools.partial(_mla_kernel, latent), in_specs, args, out_specs, out_shape, []


def _diff_kernel(latent, lam_init, *refs):
    if latent:
        (c_ref, lam_ref, norm_ref, ck_ref, cv_ref, cos_ref, sin_ref, o_ref) = refs
    else:
        (c_ref, lam_ref, norm_ref, o_ref, k_ref, v_ref) = refs
    seq = c_ref.shape[0]
    dq = c_ref[:, 0:GROUP]
    dk = c_ref[:, GROUP:2 * GROUP]
    dv = c_ref[:, 2 * GROUP:3 * GROUP]
    if latent:
        cos = cos_ref[...]
        sin = sin_ref[...]
        dq = _rope(dq, cos, sin)
        k_all = jnp.concatenate([ck_ref[...], _rope(dk, cos, sin)], axis=0)
        v_all = jnp.concatenate([cv_ref[...], dv], axis=0)
    else:
        k_ref[...] = dk
        v_ref[...] = dv
        k_all = dk
        v_all = dv
    lv = lam_ref[...]
    lam = (jnp.exp(jnp.sum(lv[0:1] * lv[1:2], axis=-1, keepdims=True))
           - jnp.exp(jnp.sum(lv[2:3] * lv[3:4], axis=-1, keepdims=True)) + lam_init)
    k_bf = k_all.astype(BF16)
    v_bf = v_all.astype(BF16)
    scale = DIFF_DIM ** -0.5
    mean_op = jnp.where(_block_mask(GROUP, GROUP, HEAD_DIM, HEAD_DIM), 1.0 / HEAD_DIM, 0.0)
    qb = min(seq, 256)
    for r0 in range(0, seq, qb):
        qblk = dq[r0:r0 + qb]
        acc = jnp.zeros((qb, GROUP), F32)
        for h in range(N_HEADS):
            outs = []
            for comp in range(2):
                cm = _lane_group_mask(GROUP, h * HEAD_DIM + comp * DIFF_DIM, DIFF_DIM)
                s = _dot_nt(jnp.where(cm, qblk, 0.0).astype(BF16), k_bf)
                e = jnp.exp2((s - jnp.max(s, axis=-1, keepdims=True)) * (scale * LOG2_E))
                outs.append(_dot(e.astype(BF16), v_bf) / jnp.sum(e, axis=-1, keepdims=True))
            oh = outs[0] - lam * outs[1]
            acc = acc + jnp.where(_lane_group_mask(GROUP, h * HEAD_DIM, HEAD_DIM), oh, 0.0)
        ms = _dot(acc * acc, mean_op, precision=HIGHEST)
        o_ref[r0:r0 + qb, :] = acc * lax.rsqrt(ms + NORM_EPS) * norm_ref[...] * (1.0 - lam_init)


def _diff_parts(proj_c, row0, nseq, seq, lam_p, norm_t, lam_init, latent_args, stack=None):
    latent = latent_args is not None
    blk0 = row0 // seq
    in_specs = [pl.BlockSpec((seq, PROJ_C), lambda b: (blk0 + b, 0)),
                pl.BlockSpec(lam_p.shape, lambda b: (0, 0)),
                pl.BlockSpec(norm_t.shape, lambda b: (0, 0))]
    args = [proj_c, lam_p, norm_t]
    out_specs = [pl.BlockSpec((None, seq, GROUP), lambda b: (b, 0, 0))]
    out_shape = [jax.ShapeDtypeStruct((nseq, seq, GROUP), F32)]
    if latent:
        ck, cv, cos, sin = latent_args
        past = ck.shape[1]
        in_specs += [pl.BlockSpec((None, past, GROUP), lambda b: (b, 0, 0)),
                     pl.BlockSpec((None, past, GROUP), lambda b: (b, 0, 0)),
                     pl.BlockSpec(cos.shape, lambda b: (0, 0)),
                     pl.BlockSpec(sin.shape, lambda b: (0, 0))]
        args += [ck, cv, cos, sin]
    else:
        for _ in range(2):
            spec, shape = _layer_slot(nseq, stack, (seq, GROUP))
            out_specs.append(spec)
            out_shape.append(shape)
    return functools.partial(_diff_kernel, latent, lam_init), in_specs, args, out_specs, out_shape, []


def _mixers(nseq, name, *parts, carried=None):
    n_in = [len(p[1]) for p in parts]
    n_out = [len(p[3]) for p in parts]
    n_scr = [len(p[5]) for p in parts]
    carried = [None] * sum(n_out) if carried is None else list(carried)
    kept = [(o, arr) for o, arr in enumerate(carried) if arr is not None]

    def body(*refs):
        ins = refs[:sum(n_in)]
        outs = refs[sum(n_in) + len(kept):sum(n_in) + len(kept) + sum(n_out)]
        scr = refs[sum(n_in) + len(kept) + sum(n_out):]
        i0 = o0 = s0 = 0
        for part, ni, no, ns in zip(parts, n_in, n_out, n_scr):
            part[0](*ins[i0:i0 + ni], *outs[o0:o0 + no], *scr[s0:s0 + ns])
            i0, o0, s0 = i0 + ni, o0 + no, s0 + ns

    res = pl.pallas_call(
        body,
        grid=(nseq,),
        in_specs=[x for p in parts for x in p[1]] + [pl.BlockSpec(memory_space=pl.ANY)] * len(kept),
        out_specs=[x for p in parts for x in p[3]],
        out_shape=[x for p in parts for x in p[4]],
        scratch_shapes=[x for p in parts for x in p[5]],
        input_output_aliases={sum(n_in) + k: o for k, (o, _) in enumerate(kept)},
        compiler_params=_cparams("arbitrary"),
        name=name,
    )(*[x for p in parts for x in p[2]], *[arr for _, arr in kept])
    out, o0 = [], 0
    for no in n_out:
        out.append(tuple(res[o0:o0 + no]))
        o0 += no
    return out


def _ssd_kernel(has_state, *refs):
    if has_state:
        (d_ref, cw_ref, cb_ref, dtb_ref, alog_ref, dskip_ref, norm_ref, s0_ref,
         o_ref, sfin_ref, xs_scr, bm_scr, cm_scr, xdt_scr, a_scr, st_scr, yf_scr, yb_scr) = refs
    else:
        (d_ref, cw_ref, cb_ref, dtb_ref, alog_ref, dskip_ref, norm_ref,
         o_ref, sfin_ref, xs_scr, bm_scr, cm_scr, xdt_scr, a_scr, st_scr, yf_scr, yb_scr) = refs
    seq = d_ref.shape[0]
    c = SSD_CHUNK
    nchunk = seq // c
    ngrp = 2 * SSD_STATE

    xin = d_ref[:, GROUP:GROUP + 512]
    rows = _iota(xin.shape, 0)
    prev = jnp.where(rows == 0, 0.0, pltpu.roll(xin, 1, 0))
    nxt = jnp.where(rows == seq - 1, 0.0, pltpu.roll(xin, seq - 1, 0))
    cw = cw_ref[...]
    xbc = _silu(cw[0:1] * prev + cw[1:2] * xin + cw[2:3] * nxt + cb_ref[...])
    xs = xbc[:, 0:GROUP]
    xs_scr[...] = xs
    bm_scr[...] = xbc[:, GROUP:GROUP + ngrp]
    cm_scr[...] = xbc[:, GROUP + ngrp:GROUP + 2 * ngrp]
    dt = _softplus(d_ref[:, GROUP + 512:GROUP + 640] + dtb_ref[...])
    a_scr[...] = dt * (-jnp.exp(alog_ref[...]))
    erow = _iota((128, GROUP), 0)
    ehead = _iota((128, GROUP), 1) // HEAD_DIM
    expand = tuple((erow == 4 * d + ehead).astype(F32) for d in range(2))
    for d in range(2):
        xdt_scr[d] = xs * _dot(dt, expand[d], precision=HIGHEST)
    if has_state:
        st_scr[...] = s0_ref[...]
    else:
        st_scr[...] = jnp.zeros_like(st_scr)

    row = _iota((c, c), 0)
    col = _iota((c, c), 1)
    tri = ((col <= row).astype(F32), (col >= row).astype(F32))
    keep = (col <= row, col >= row)
    grp_lane = _iota((1, ngrp), 1) // SSD_STATE
    valid = (_iota((ngrp, GROUP), 0) // SSD_STATE) == (_iota((ngrp, GROUP), 1) // (2 * HEAD_DIM))

    def chunk_step(d, r0, out_scr):
        a_c = a_scr[pl.ds(r0, c), :]
        bm_c = bm_scr[pl.ds(r0, c), :]
        cm_c = cm_scr[pl.ds(r0, c), :].astype(BF16)
        xdt_c = xdt_scr[d, pl.ds(r0, c), :]
        acum = _dot(tri[d], a_c, precision=HIGHEST)
        acum_t = acum.T
        acum_rep = _dot(acum, expand[d], precision=HIGHEST)
        bm2 = jnp.concatenate([jnp.where(grp_lane == g, bm_c, 0.0) for g in range(2)], axis=0)
        cb = _dot_nt(cm_c, bm2.astype(BF16))
        scores = []
        xparts = []
        for h in range(N_HEADS):
            lane = 4 * d + h
            seg = jnp.exp(jnp.where(keep[d], acum[:, lane:lane + 1] - acum_t[lane:lane + 1, :], -jnp.inf))
            g = h // 2
            scores.append((cb[:, g * c:(g + 1) * c] * seg).astype(BF16))
            xparts.append(jnp.where(_lane_group_mask(GROUP, h * HEAD_DIM, HEAD_DIM), xdt_c, 0.0))
        y = _dot(jnp.concatenate(scores, axis=1), jnp.concatenate(xparts, axis=0).astype(BF16))
        st = st_scr[d]
        y = y + _dot(cm_c, st.astype(BF16)) * jnp.exp(acum_rep)
        out_scr[pl.ds(r0, c), :] = y
        edge = acum_rep[c - 1:c] if d == 0 else acum_rep[0:1]
        xt = (xdt_c * jnp.exp(edge - acum_rep)).astype(BF16)
        upd = _dot_tn(bm_c.astype(BF16), xt)
        st_scr[d] = st * jnp.exp(edge) + jnp.where(valid, upd, 0.0)

    def body(n, carry):
        chunk_step(0, pl.multiple_of(n * c, c), yf_scr)
        chunk_step(1, pl.multiple_of((nchunk - 1 - n) * c, c), yb_scr)
        return carry

    lax.fori_loop(0, nchunk, body, 0)

    y = yf_scr[...] + yb_scr[...] + dskip_ref[...] * xs_scr[...]
    y = y * _silu(d_ref[:, 0:GROUP])
    o_ref[...] = y * lax.rsqrt(jnp.mean(y * y, axis=-1, keepdims=True) + NORM_EPS) * norm_ref[...]
    sfin_ref[...] = st_scr[...]


def _ssd_parts(proj_d, row0, nseq, seq, conv_w, conv_b, dt_bias_p, a_log_p, d_rep, norm, s0, stack=None):
    has_state = s0 is not None
    blk0 = row0 // seq
    ngrp = 2 * SSD_STATE
    small = [conv_w, conv_b, dt_bias_p, a_log_p, d_rep, norm]
    in_specs = ([pl.BlockSpec((seq, PROJ_D), lambda b: (blk0 + b, 0))]
                + [pl.BlockSpec(s.shape, lambda b: (0, 0)) for s in small])
    args = [proj_d] + small
    if has_state:
        in_specs.append(pl.BlockSpec((None, 2, ngrp, GROUP), lambda b: (b, 0, 0, 0)))
        args.append(s0)
    out_specs, out_shape = zip(_layer_slot(nseq, None, (seq, GROUP)),
                               _layer_slot(nseq, stack, (2, ngrp, GROUP)))
    out_specs, out_shape = list(out_specs), list(out_shape)
    scratch = [pltpu.VMEM((seq, GROUP), F32),
               pltpu.VMEM((seq, ngrp), F32),
               pltpu.VMEM((seq, ngrp), F32),
               pltpu.VMEM((2, seq, GROUP), F32),
               pltpu.VMEM((seq, 128), F32),
               pltpu.VMEM((2, ngrp, GROUP), F32),
               pltpu.VMEM((seq, GROUP), F32),
               pltpu.VMEM((seq, GROUP), F32)]
    return functools.partial(_ssd_kernel, has_state), in_specs, args, out_specs, out_shape, scratch


def _outproj_kernel(n_ctx_tiles, *refs):
    ctx_refs, lat_refs = refs[0:4], refs[4:8]
    w_ref, x_ref, m_ref, g_ref, b_ref, o_ref = refs[8:]
    is_ctx = pl.program_id(0) < n_ctx_tiles
    mixed = None
    for i, (c_ref, l_ref) in enumerate(zip(ctx_refs, lat_refs)):
        part = jnp.where(is_ctx, c_ref[...], l_ref[...]).astype(BF16)
        term = _dot(part, w_ref[i * GROUP:(i + 1) * GROUP, :])
        mixed = term if mixed is None else mixed + term
    o_ref[...] = _layernorm(ALPHA * x_ref[...] + m_ref[2] * mixed, g_ref[...], b_ref[...])


def _outproj(parts_ctx, parts_lat, w_out_bf, x, mods_l, ln_g, ln_b, t_ctx, s_lat):
    t, d = x.shape
    tm = 512
    n_ctx = t_ctx // tm
    n_lat = (t - t_ctx) // tm
    ctx_spec = pl.BlockSpec((tm, GROUP), lambda i: (jnp.minimum(i, n_ctx - 1), 0))
    lat_spec = pl.BlockSpec((tm, GROUP), lambda i: (jnp.clip(i - n_ctx, 0, n_lat - 1), 0))
    return pl.pallas_call(
        functools.partial(_outproj_kernel, n_ctx),
        grid=(t // tm,),
        in_specs=[ctx_spec] * 4 + [lat_spec] * 4
        + [pl.BlockSpec(w_out_bf.shape, lambda i: (0, 0)),
           pl.BlockSpec((tm, d), lambda i: (i, 0)),
           pl.BlockSpec((None, 6, 1, d), _mod_row_map(tm, t_ctx, s_lat)),
           pl.BlockSpec((1, d), lambda i: (0, 0)),
           pl.BlockSpec((1, d), lambda i: (0, 0))],
        out_specs=pl.BlockSpec((tm, d), lambda i: (i, 0)),
        out_shape=jax.ShapeDtypeStruct((t, d), F32),
        compiler_params=_cparams("arbitrary"),
        name="outproj_ln",
    )(*parts_ctx, *parts_lat, w_out_bf, x, mods_l, ln_g, ln_b)


def _top_rows(s, k, extra=()):
    r = s.shape[0]
    rid = _iota(s.shape, 0).astype(F32)
    vals, ids = [], []
    picked = [[] for _ in extra]
    for _ in range(k):
        m = jnp.max(s, axis=0, keepdims=True)
        cand = jnp.where(s == m, rid, float(r))
        i = jnp.min(cand, axis=0, keepdims=True)
        hit = cand == i
        vals.append(m)
        ids.append(i)
        for lst, arr in zip(picked, extra):
            lst.append(jnp.max(jnp.where(hit, arr, -1.0), axis=0, keepdims=True))
        s = jnp.where(hit, -jnp.inf, s)
    cat = lambda xs: jnp.concatenate(xs, axis=0)
    return cat(vals), cat(ids), [cat(p) for p in picked]


def _router_kernel(x_ref, m_ref, wqt_ref, keys_ref, h_ref, a_ref, b_ref, g_ref):
    tm = x_ref.shape[0]
    hb = (x_ref[...] * (1.0 + m_ref[4]) + m_ref[3]).astype(BF16)
    h_ref[...] = hb
    qt = _dot_nt(wqt_ref[...], hb).astype(BF16)
    k = PEER_TOPK
    a_rows, b_rows, g_rows = [], [], []
    for head in range(PEER_HEADS):
        tv, ti = [], []
        for half in range(2):
            g = 2 * head + half
            sc = _dot(keys_ref[g], qt[g * PEER_HALF:(g + 1) * PEER_HALF])
            v, i, _ = _top_rows(sc, k)
            tv.append(v)
            ti.append(i)
        cs = [tv[0][0:1] + tv[1]]
        ca = [jnp.broadcast_to(ti[0][0:1], (k, tm))]
        cb = [ti[1]]
        for k1 in range(1, 4):
            cs.append(tv[0][k1:k1 + 1] + tv[1][0:8])
            ca.append(jnp.broadcast_to(ti[0][k1:k1 + 1], (8, tm)))
            cb.append(ti[1][0:8])
        low = _iota((8, tm), 0) < 4
        v2_dup = jnp.where(low, tv[1][0:8], pltpu.roll(tv[1][0:8], 4, 0))
        i2_dup = jnp.where(low, ti[1][0:8], pltpu.roll(ti[1][0:8], 4, 0))
        for k1 in (4, 6):
            cs.append(jnp.where(low, tv[0][k1:k1 + 1], tv[0][k1 + 1:k1 + 2]) + v2_dup)
            ca.append(jnp.where(low, ti[0][k1:k1 + 1], ti[0][k1 + 1:k1 + 2]))
            cb.append(i2_dup)
        cs.append(tv[0][8:16] + tv[1][0:1])
        ca.append(ti[0][8:16])
        cb.append(jnp.broadcast_to(ti[1][0:1], (8, tm)))
        best, _, (sel_a, sel_b) = _top_rows(jnp.concatenate(cs, axis=0), k,
                                            extra=(jnp.concatenate(ca, axis=0), jnp.concatenate(cb, axis=0)))
        e = jnp.exp(best - best[0:1])
        g_rows.append(e / jnp.sum(e, axis=0, keepdims=True))
        a_rows.append(sel_a)
        b_rows.append(sel_b)
    a_ref[...] = jnp.concatenate(a_rows, axis=0).T.astype(I32)
    b_ref[...] = jnp.concatenate(b_rows, axis=0).T.astype(I32)
    g_ref[...] = jnp.concatenate(g_rows, axis=0).T


def _router(x, mods_l, wq_t, keys, t_ctx, s_lat):
    t, d = x.shape
    tm = 128
    nslot = PEER_HEADS * PEER_TOPK
    return pl.pallas_call(
        _router_kernel,
        grid=(t // tm,),
        in_specs=[pl.BlockSpec((tm, d), lambda i: (i, 0)),
                  pl.BlockSpec((None, 6, 1, d), _mod_row_map(tm, t_ctx, s_lat)),
                  pl.BlockSpec(wq_t.shape, lambda i: (0, 0)),
                  pl.BlockSpec(keys.shape, lambda i: (0, 0, 0))],
        out_specs=[pl.BlockSpec((tm, d), lambda i: (i, 0)),
                   pl.BlockSpec((tm, nslot), lambda i: (i, 0)),
                   pl.BlockSpec((tm, nslot), lambda i: (i, 0)),
                   pl.BlockSpec((tm, nslot), lambda i: (i, 0))],
        out_shape=[jax.ShapeDtypeStruct((t, d), BF16),
                   jax.ShapeDtypeStruct((t, nslot), I32),
                   jax.ShapeDtypeStruct((t, nslot), I32),
                   jax.ShapeDtypeStruct((t, nslot), F32)],
        compiler_params=_cparams("arbitrary"),
        name="peer_router",
    )(x, mods_l, wq_t, keys)


def _gates_kernel(a_ref, b_ref, g_ref, u_ref, v_ref, o_ref, ub_ref, vb_ref):
    tm = a_ref.shape[0]
    n = PEER_KEYS
    sub = 16
    ub_ref[...] = u_ref[...].astype(BF16)
    vb_ref[...] = v_ref[...].astype(BF16)
    key = _iota((sub, n, a_ref.shape[2]), 1).astype(F32).astype(BF16)
    zero = jnp.zeros((), BF16)
    for t0 in range(0, tm, sub):
        a = a_ref[t0:t0 + sub].astype(F32).astype(BF16)
        b = b_ref[t0:t0 + sub].astype(F32).astype(BF16)
        g = g_ref[t0:t0 + sub].astype(BF16)
        onehot_a = jnp.where(key == a, jnp.ones((), BF16), zero)
        gated_b = jnp.where(key == b, g, zero)
        w = lax.dot_general(onehot_a, gated_b, (((2,), (2,)), ((0,), (0,))),
                            preferred_element_type=F32)
        w_t = jnp.swapaxes(w.astype(BF16), 0, 1)
        for r in range(n):
            o_ref[t0:t0 + sub, r * n:(r + 1) * n] = w_t[r]


def _gates(a_idx, b_idx, gate, peer_u, peer_v, layer):
    t, nslot = a_idx.shape
    d = peer_u.shape[-1]
    tm = 96
    steps = t // tm
    te = N_EXPERTS // steps
    spec = pl.BlockSpec((tm, 1, nslot), lambda i: (i, 0, 0))
    tab_in = pl.BlockSpec((None, te, d), lambda i: (layer, i, 0))
    tab_out = pl.BlockSpec((te, d), lambda i: (i, 0))
    return pl.pallas_call(
        _gates_kernel,
        grid=(steps,),
        in_specs=[spec, spec, spec, tab_in, tab_in],
        out_specs=[pl.BlockSpec((tm, N_EXPERTS), lambda i: (i, 0)), tab_out, tab_out],
        out_shape=[jax.ShapeDtypeStruct((t, N_EXPERTS), BF16),
                   jax.ShapeDtypeStruct((N_EXPERTS, d), BF16),
                   jax.ShapeDtypeStruct((N_EXPERTS, d), BF16)],
        compiler_params=_cparams("arbitrary"),
        name="peer_gates",
    )(a_idx.reshape(t, 1, nslot), b_idx.reshape(t, 1, nslot), gate.reshape(t, 1, nslot), peer_u, peer_v)


def _experts_kernel(h_ref, u_ref, v_ref, w_ref, x_ref, m_ref, g_ref, b_ref, o_ref, acc_ref):
    j = pl.program_id(1)

    @pl.when(j == 0)
    def _():
        acc_ref[...] = jnp.zeros_like(acc_ref)

    act = _gelu_tanh(_dot_nt(h_ref[...], u_ref[...]))
    acc_ref[...] += _dot((act * w_ref[...].astype(F32)).astype(BF16), v_ref[...])

    @pl.when(j == pl.num_programs(1) - 1)
    def _():
        o_ref[...] = _layernorm(ALPHA * x_ref[...] + m_ref[5] * acc_ref[...], g_ref[...], b_ref[...])


def _experts(h_bf, u_bf, v_bf, w_gate, x, mods_l, ln_g, ln_b, t_ctx, s_lat):
    t, d = x.shape
    tm, te = 1024, 1024
    return pl.pallas_call(
        _experts_kernel,
        grid=(t // tm, N_EXPERTS // te),
        in_specs=[pl.BlockSpec((tm, d), lambda i, j: (i, 0)),
                  pl.BlockSpec((te, d), lambda i, j: (j, 0)),
                  pl.BlockSpec((te, d), lambda i, j: (j, 0)),
                  pl.BlockSpec((tm, te), lambda i, j: (i, j)),
                  pl.BlockSpec((tm, d), lambda i, j: (i, 0)),
                  pl.BlockSpec((None, 6, 1, d), _mod_row_map(tm, t_ctx, s_lat)),
                  pl.BlockSpec((1, d), lambda i, j: (0, 0)),
                  pl.BlockSpec((1, d), lambda i, j: (0, 0))],
        out_specs=pl.BlockSpec((tm, d), lambda i, j: (i, 0)),
        out_shape=jax.ShapeDtypeStruct((t, d), F32),
        scratch_shapes=[pltpu.VMEM((tm, d), F32)],
        compiler_params=_cparams("arbitrary", "arbitrary"),
        name="peer_experts",
    )(h_bf, u_bf, v_bf, w_gate, x, mods_l, ln_g, ln_b)


def _pad_cols(w, n):
    return jnp.concatenate([w, jnp.zeros((w.shape[0], n), w.dtype)], axis=1) if n else w


def _layout_w_in(w):
    a = w[:, 0:1280]
    mcq, mckv, mkpe = w[:, 1280:1472], w[:, 1472:1600], w[:, 1600:1632]
    b = jnp.concatenate([mckv, _pad_cols(mcq, 64), mkpe, mkpe, mkpe, mkpe], axis=1)
    c = w[:, 1632:2400]
    d = _pad_cols(w[:, 2400:3176], PROJ_D - 776)
    return jnp.concatenate([a, b, c, d], axis=1).astype(BF16)


def _layout_w_qb(w):
    w4 = w.reshape(MLA_Q_RANK, N_HEADS, MLA_NOPE + MLA_ROPE)
    return jnp.concatenate([w4[:, :, :MLA_NOPE].reshape(MLA_Q_RANK, -1),
                            w4[:, :, MLA_NOPE:].reshape(MLA_Q_RANK, -1)], axis=1).astype(BF16)


def _layout_w_kvb(w):
    w4 = w.reshape(MLA_KV_RANK, N_HEADS, MLA_NOPE + HEAD_DIM)
    return jnp.concatenate([w4[:, :, :MLA_NOPE].reshape(MLA_KV_RANK, -1),
                            w4[:, :, MLA_NOPE:].reshape(MLA_KV_RANK, -1)], axis=1).astype(BF16)


def _rope_tables(seq):
    rows = seq // GRID_W
    row = jnp.repeat(jnp.arange(rows, dtype=F32), GRID_W)
    col = jnp.tile(jnp.arange(GRID_W, dtype=F32), rows)
    freqs = ROPE_BASE ** (-jnp.arange(ROPE_PAIRS, dtype=F32) / ROPE_PAIRS)
    cos_l, sin_l = [], []
    for pos in (row, col):
        ang = pos[:, None] * freqs
        cos_l += [jnp.cos(ang), jnp.cos(ang)]
        sin_l += [-jnp.sin(ang), jnp.sin(ang)]
    return jnp.concatenate(cos_l, axis=1), jnp.concatenate(sin_l, axis=1)


def _hgrn_state_pack(st):
    b = st.shape[0]
    st_t = jnp.swapaxes(st, -1, -2)
    zero = jnp.zeros_like(st_t[:, :, 0])
    rows = [jnp.concatenate([st_t[:, :, h] if g == h else zero for g in range(N_HEADS)], axis=-1)
            for h in range(N_HEADS)]
    return jnp.concatenate(rows, axis=-2).reshape(b, 2, GROUP, GROUP)


def _hgrn_state_unpack(sb):
    blocks = [sb[..., h * HEAD_DIM:(h + 1) * HEAD_DIM, h * HEAD_DIM:(h + 1) * HEAD_DIM]
              for h in range(N_HEADS)]
    return jnp.swapaxes(jnp.stack(blocks, axis=-3), -1, -2)


def _ssd_state_pack(st):
    st_t = jnp.swapaxes(st, -1, -2)
    zero = jnp.zeros_like(st_t[:, :, 0])
    rows = [jnp.concatenate([st_t[:, :, h] if h // 2 == g else zero for h in range(N_HEADS)], axis=-1)
            for g in range(2)]
    return jnp.concatenate(rows, axis=-2)


def _ssd_state_unpack(sb):
    blocks = [sb[..., (h // 2) * SSD_STATE:(h // 2 + 1) * SSD_STATE, h * HEAD_DIM:(h + 1) * HEAD_DIM]
              for h in range(N_HEADS)]
    return jnp.swapaxes(jnp.stack(blocks, axis=-3), -1, -2)


def _tile_lanes(v, n):
    return jnp.tile(v.reshape(1, -1), (1, n))


def kernel(x_prompt, x_sample, cache_mla_ckv, cache_mla_kpe, cache_diff_k, cache_diff_v, state_hgrn, state_ssd, c, c_ctx, w_mod, b_mod, w_in, hgrn_lb, hgrn_norm, mla_q_norm, mla_w_qb, mla_kv_norm, mla_w_kvb, diff_lambda, diff_norm, ssd_conv_w, ssd_conv_b, ssd_dt_bias, ssd_a_log, ssd_d, ssd_norm, w_out, ln1_g, ln1_b, peer_wq, peer_keys, peer_u, peer_v, ln2_g, ln2_b):
    nb, seq, d = x_prompt.shape
    nlat, lseq, _ = x_sample.shape
    depth = w_in.shape[0]
    t_ctx = nb * seq
    x = jnp.concatenate([x_prompt.reshape(t_ctx, d), x_sample.reshape(nlat * lseq, d)], axis=0)

    cond8 = jnp.concatenate([c_ctx.reshape(1, d), c, jnp.zeros((8 - 1 - nlat, d), F32)], axis=0)
    mods = _mods(cond8, w_mod, b_mod)
    mods = mods[:, :1 + nlat].reshape(depth, 1 + nlat, 6, 1, d)

    cos32, sin32 = _rope_tables(lseq)
    cos128, sin128 = jnp.tile(cos32, (1, 4)), jnp.tile(sin32, (1, 4))
    cos256, sin256 = jnp.tile(cos32, (1, 8)), jnp.tile(sin32, (1, 8))

    carried = None
    for l in range(depth):
        mods_l = mods[l]
        pa, pb, pc, pd = _inproj(x, mods_l, _layout_w_in(w_in[l]), t_ctx, lseq)

        norm_hg = _tile_lanes(hgrn_norm[l], N_HEADS)
        mla_w = (mla_q_norm[l].reshape(1, -1), _layout_w_qb(mla_w_qb[l]),
                 mla_kv_norm[l].reshape(1, -1), _layout_w_kvb(mla_w_kvb[l]))
        lam_init = 0.8 - 0.6 * math.exp(-0.3 * l)
        norm_df = _tile_lanes(diff_norm[l], N_HEADS)
        past = cache_diff_k.shape[2]
        ssd_w = (ssd_conv_w[l], ssd_conv_b[l].reshape(1, -1),
                 _pad_cols(ssd_dt_bias[l].reshape(1, -1), 120), _pad_cols(ssd_a_log[l].reshape(1, -1), 120),
                 jnp.repeat(ssd_d[l], HEAD_DIM).reshape(1, -1), ssd_norm[l].reshape(1, -1))

        stack = (l, depth)
        ctx_out = _mixers(
            nb, "mixers_ctx",
            _mla_parts(pb, 0, nb, seq, *mla_w, None, stack),
            _diff_parts(pc, 0, nb, seq, diff_lambda[l], norm_df, lam_init, None, stack),
            _ssd_parts(pd, 0, nb, seq, *ssd_w, None, stack),
            _hgrn_parts(pa, 0, nb, seq, hgrn_lb, norm_hg, None, l, stack),
            carried=carried)
        (mla_ctx, new_ckv, new_kpe), (df_ctx, new_dk, new_dv), (ssd_ctx, ssd_fin), (hg_ctx, hg_fin) = ctx_out
        carried = [None, new_ckv, new_kpe, None, new_dk, new_dv, None, ssd_fin, None, hg_fin]
        ((mla_lat,),) = _mixers(
            nlat, "mla_lat",
            _mla_parts(pb, t_ctx, nlat, lseq, *mla_w,
                       (cache_mla_ckv[:, l], jnp.tile(cache_mla_kpe[:, l], (1, 1, 4)), cos128, sin128)))
        ((df_lat,),) = _mixers(
            nlat, "diffattn_lat",
            _diff_parts(pc, t_ctx, nlat, lseq, diff_lambda[l], norm_df, lam_init,
                        (cache_diff_k[:, l].reshape(nlat, past, GROUP),
                         cache_diff_v[:, l].reshape(nlat, past, GROUP), cos256, sin256)))
        ((ssd_lat, _),) = _mixers(nlat, "ssd_lat",
                                  _ssd_parts(pd, t_ctx, nlat, lseq, *ssd_w, _ssd_state_pack(state_ssd[:, l])))
        ((hg_lat, _),) = _mixers(nlat, "hgrn_lat",
                                 _hgrn_parts(pa, t_ctx, nlat, lseq, hgrn_lb, norm_hg,
                                             _hgrn_state_pack(state_hgrn[:, l]), l))

        parts_ctx = [a.reshape(t_ctx, GROUP) for a in (hg_ctx, mla_ctx, df_ctx, ssd_ctx)]
        parts_lat = [a.reshape(nlat * lseq, GROUP) for a in (hg_lat, mla_lat, df_lat, ssd_lat)]
        x = _outproj(parts_ctx, parts_lat, w_out[l].astype(BF16), x, mods_l,
                     ln1_g[l].reshape(1, d), ln1_b[l].reshape(1, d), t_ctx, lseq)

        keys = peer_keys[l].reshape(2 * PEER_HEADS, PEER_KEYS, PEER_HALF).astype(BF16)
        h_bf, a_idx, b_idx, gate = _router(x, mods_l, peer_wq[l].T.astype(BF16), keys, t_ctx, lseq)
        w_gate, u_bf, v_bf = _gates(a_idx, b_idx, gate, peer_u, peer_v, l)
        x = _experts(h_bf, u_bf, v_bf, w_gate, x, mods_l,
                     ln2_g[l].reshape(1, d), ln2_b[l].reshape(1, d), t_ctx, lseq)

    y_prompt = x[:t_ctx].reshape(nb, seq, d)
    y_sample = x[t_ctx:].reshape(nlat, lseq, d)
    return (y_prompt, y_sample, new_ckv, new_kpe,
            new_dk.reshape(nb, depth, seq, N_HEADS, 2, DIFF_DIM),
            new_dv.reshape(nb, depth, seq, N_HEADS, 2 * DIFF_DIM),
            _hgrn_state_unpack(hg_fin), _ssd_state_unpack(ssd_fin))
```

```python
import functools
import math

import jax
import jax.numpy as jnp
from jax import lax
from jax.experimental import pallas as pl
from jax.experimental.pallas import tpu as pltpu

F32 = jnp.float32
BF16 = jnp.bfloat16
I32 = jnp.int32
HIGHEST = lax.Precision.HIGHEST

D_MODEL = 1024
GROUP = 256
N_HEADS = 4
HEAD_DIM = 64
HG_BLOCK = 16
HG_SLAB = 256
HG_UNROLL = 8
SSD_CHUNK = 128
SSD_STATE = 64
MLA_Q_RANK = 192
MLA_KV_RANK = 128
MLA_NOPE = 64
MLA_ROPE = 32
DIFF_DIM = 32
GRID_W = 64
ROPE_PAIRS = 8
ROPE_BASE = 10000.0
PEER_HEADS = 8
PEER_KEYS = 128
PEER_TOPK = 16
PEER_HALF = 64
N_EXPERTS = PEER_KEYS * PEER_KEYS
NORM_EPS = 1e-6
LN_EPS = 1e-5
DEPTH = 2
ALPHA = (2.0 * DEPTH) ** 0.25
LOG2_E = 1.4426950408889634

PROJ_A = 5 * GROUP
PROJ_B = 512
PROJ_C = 3 * GROUP
PROJ_D = 896
VMEM_LIMIT = 56 * 1024 * 1024


def _cparams(*sem):
    return pltpu.CompilerParams(dimension_semantics=sem, vmem_limit_bytes=VMEM_LIMIT)


def _sigmoid(x):
    return 1.0 / (1.0 + jnp.exp(-x))


def _silu(x):
    return x * _sigmoid(x)


def _softplus(x):
    return jnp.maximum(x, 0.0) + jnp.log(1.0 + jnp.exp(-jnp.abs(x)))


def _gelu_tanh(x):
    return 0.5 * x * (1.0 + jnp.tanh(math.sqrt(2.0 / math.pi) * (x + 0.044715 * (x * x * x))))


def _dot(a, b, precision=None):
    return jnp.dot(a, b, preferred_element_type=F32, precision=precision)


def _dot_nt(a, b, precision=None):
    return lax.dot_general(a, b, (((1,), (1,)), ((), ())), preferred_element_type=F32,
                           precision=precision)


def _dot_tn(a, b, precision=None):
    return lax.dot_general(a, b, (((0,), (0,)), ((), ())), preferred_element_type=F32,
                           precision=precision)


def _iota(shape, dim):
    return lax.broadcasted_iota(I32, shape, dim)


def _block_mask(rows, cols, rblk, cblk):
    return (_iota((rows, cols), 0) // rblk) == (_iota((rows, cols), 1) // cblk)


def _lane_group_mask(width, start, size):
    lane = _iota((1, width), 1)
    return (lane >= start) & (lane < start + size)


def _layernorm(v, g, b):
    mu = jnp.mean(v, axis=-1, keepdims=True)
    d = v - mu
    var = jnp.mean(d * d, axis=-1, keepdims=True)
    return d * lax.rsqrt(var + LN_EPS) * g + b


def _swap_halves16(x):
    width = x.shape[-1]
    lane = _iota(x.shape, x.ndim - 1)
    up = pltpu.roll(x, width - 8, x.ndim - 1)
    down = pltpu.roll(x, 8, x.ndim - 1)
    return jnp.where((lane % 16) < 8, up, down)


def _rope(x, cos, sin_signed):
    return x * cos + _swap_halves16(x) * sin_signed


def _mods_kernel(c_ref, w_ref, b_ref, o_ref):
    s = _silu(c_ref[...]).astype(BF16)
    o_ref[...] = _dot(s, w_ref[...].astype(BF16)) + b_ref[...]


def _mods(cond8, w_mod, b_mod):
    depth, d, n = w_mod.shape
    tn = 1536
    return pl.pallas_call(
        _mods_kernel,
        grid=(depth, n // tn),
        in_specs=[pl.BlockSpec((8, d), lambda l, j: (0, 0)),
                  pl.BlockSpec((None, d, tn), lambda l, j: (l, 0, j)),
                  pl.BlockSpec((None, 1, tn), lambda l, j: (l, 0, j))],
        out_specs=pl.BlockSpec((None, 8, tn), lambda l, j: (l, 0, j)),
        out_shape=jax.ShapeDtypeStruct((depth, 8, n), F32),
        compiler_params=_cparams("arbitrary", "arbitrary"),
        name="mods",
    )(cond8, w_mod, b_mod.reshape(depth, 1, n))


def _mod_row_map(tm, t_ctx, s_lat):
    def index_map(i, *_):
        start = i * tm
        return (jnp.where(start < t_ctx, 0, 1 + (start - t_ctx) // s_lat), 0, 0, 0)
    return index_map


def _inproj_kernel(x_ref, m_ref, w_ref, oa_ref, ob_ref, oc_ref, od_ref):
    h = (x_ref[...] * (1.0 + m_ref[1]) + m_ref[0]).astype(BF16)
    start = 0
    for o_ref in (oa_ref, ob_ref, oc_ref, od_ref):
        width = o_ref.shape[-1]
        o_ref[...] = _dot(h, w_ref[:, start:start + width])
        start += width


def _inproj(x, mods_l, w_in_p, t_ctx, s_lat):
    t, d = x.shape
    tm = 512
    widths = (PROJ_A, PROJ_B, PROJ_C, PROJ_D)
    return pl.pallas_call(
        _inproj_kernel,
        grid=(t // tm,),
        in_specs=[pl.BlockSpec((tm, d), lambda i: (i, 0)),
                  pl.BlockSpec((None, 6, 1, d), _mod_row_map(tm, t_ctx, s_lat)),
                  pl.BlockSpec(w_in_p.shape, lambda i: (0, 0))],
        out_specs=[pl.BlockSpec((tm, w), lambda i: (i, 0)) for w in widths],
        out_shape=[jax.ShapeDtypeStruct((t, w), F32) for w in widths],
        compiler_params=_cparams("arbitrary"),
        name="inproj",
    )(x, mods_l, w_in_p)


def _hgrn_kernel(layer, has_state, *refs):
    if has_state:
        (a_ref, lb_ref, norm_ref, s0_ref, o_ref, sfin_ref,
         q_scr, k_scr, bc_scr, dec_scr, qt_scr, kt_scr, st_scr, o_scr) = refs
    else:
        (a_ref, lb_ref, norm_ref, o_ref, sfin_ref,
         q_scr, k_scr, bc_scr, dec_scr, qt_scr, kt_scr, st_scr, o_scr) = refs
        s0_ref = None
    seq = a_ref.shape[0]
    c = HG_BLOCK
    nblk = seq // c
    slab = HG_SLAB
    nb = slab // c

    lbp = lb_ref[...]
    e = jnp.exp(lbp - jnp.max(lbp, axis=0, keepdims=True))
    p = e / jnp.sum(e, axis=0, keepdims=True)
    lower = jnp.sum(p[1:layer + 1], axis=0) if layer > 0 else jnp.zeros_like(p[0])

    q = _silu(a_ref[:, 0:GROUP])
    q_scr[...] = q
    srow = _iota((slab, slab), 0)
    scol = _iota((slab, slab), 1)
    same = (srow // c) == (scol // c)
    cum_op = (jnp.where(same & (scol <= srow), 1.0, 0.0), jnp.where(same & (scol >= srow), 1.0, 0.0))
    for d in range(2):
        lb = lower[d:d + 1]
        f = lb + (1.0 - lb) * _sigmoid(a_ref[:, (1 + d) * GROUP:(2 + d) * GROUP])
        k = 1.0 - f
        lf = jnp.log(f)
        k_scr[d] = k
        for s0 in range(0, seq, slab):
            bc = _dot(cum_op[d], lf[s0:s0 + slab], precision=HIGHEST)
            bc3 = bc.reshape(nb, c, GROUP)
            edge = bc3[:, c - 1:c, :] if d == 0 else bc3[:, 0:1, :]
            tot = jnp.broadcast_to(edge, (nb, c, GROUP)).reshape(slab, GROUP)
            bc_scr[d, s0:s0 + slab, :] = bc * LOG2_E
            dec_scr[d, s0:s0 + slab, :] = jnp.exp(tot)
            qt_scr[d, s0:s0 + slab, :] = (q[s0:s0 + slab] * jnp.exp(bc)).astype(BF16)
            kt_scr[d, s0:s0 + slab, :] = (k[s0:s0 + slab] * jnp.exp(tot - bc)).astype(BF16)
    if has_state:
        st_scr[...] = s0_ref[...]
    else:
        st_scr[...] = jnp.zeros_like(st_scr)

    bd_ones = _block_mask(GROUP, GROUP, HEAD_DIM, HEAD_DIM).astype(BF16)
    rib = _iota((1, c, GROUP), 1)

    def slab_step(i, carry):
        r0 = pl.multiple_of(i * slab, slab)
        q3 = q_scr[pl.ds(r0, slab), :].reshape(nb, c, GROUP)
        v3 = a_ref[pl.ds(r0, slab), 3 * GROUP:4 * GROUP].reshape(nb, c, GROUP)
        o3 = jnp.zeros((nb, c, GROUP), F32)
        for d in range(2):
            bc3 = bc_scr[d, pl.ds(r0, slab), :].reshape(nb, c, GROUP)
            k3 = k_scr[d, pl.ds(r0, slab), :].reshape(nb, c, GROUP)
            for j in range(c):
                keep = (rib >= j) if d == 0 else (rib <= j)
                dec = jnp.exp2(jnp.where(keep, bc3 - bc3[:, j:j + 1, :], -jnp.inf))
                pj = (dec * q3 * k3[:, j:j + 1, :]).astype(BF16).reshape(slab, GROUP)
                srep = _dot(pj, bd_ones).reshape(nb, c, GROUP)
                o3 = o3 + srep * v3[:, j:j + 1, :]
        o_scr[0, pl.ds(r0, slab), :] = o3.reshape(slab, GROUP)
        return carry

    lax.fori_loop(0, seq // slab, slab_step, 0)

    bd_mask = _block_mask(GROUP, GROUP, HEAD_DIM, HEAD_DIM)

    def body(n, carry):
        rows = [[pl.multiple_of(((n * HG_UNROLL + u) if d == 0 else nblk - 1 - (n * HG_UNROLL + u)) * c, c)
                 for u in range(HG_UNROLL)] for d in range(2)]
        upd = [[_dot_tn(a_ref[pl.ds(r0, c), 3 * GROUP:4 * GROUP].astype(BF16), kt_scr[d, pl.ds(r0, c), :])
                for r0 in rows[d]] for d in range(2)]
        for d in range(2):
            st = st_scr[d]
            for u, r0 in enumerate(rows[d]):
                o_scr[1 + d, pl.ds(r0, c), :] = _dot_nt(qt_scr[d, pl.ds(r0, c), :], st.astype(BF16))
                st = st * dec_scr[d, pl.ds(r0, 1), :] + jnp.where(bd_mask, upd[d][u], 0.0)
            st_scr[d] = st
        return carry

    lax.fori_loop(0, nblk // HG_UNROLL, body, 0)

    o = o_scr[0] + o_scr[1] + o_scr[2]
    mean_op = jnp.where(_block_mask(GROUP, GROUP, HEAD_DIM, HEAD_DIM), 1.0 / HEAD_DIM, 0.0)
    ms = _dot(o * o, mean_op, precision=HIGHEST)
    y = o * lax.rsqrt(ms + NORM_EPS) * norm_ref[...]
    o_ref[...] = y * _silu(a_ref[:, 4 * GROUP:5 * GROUP])
    sfin_ref[...] = st_scr[...]


def _layer_slot(nseq, stack, tail):
    zeros = (0,) * len(tail)
    if stack is None:
        return (pl.BlockSpec((None,) + tail, lambda b: (b,) + zeros),
                jax.ShapeDtypeStruct((nseq,) + tail, F32))
    layer, depth = stack
    return (pl.BlockSpec((None, None) + tail, lambda b: (b, layer) + zeros),
            jax.ShapeDtypeStruct((nseq, depth) + tail, F32))


def _hgrn_parts(proj_a, row0, nseq, seq, hgrn_lb, norm_t, s0, layer, stack=None):
    has_state = s0 is not None
    blk0 = row0 // seq
    in_specs = [pl.BlockSpec((seq, PROJ_A), lambda b: (blk0 + b, 0)),
                pl.BlockSpec(hgrn_lb.shape, lambda b: (0, 0, 0)),
                pl.BlockSpec((1, GROUP), lambda b: (0, 0))]
    args = [proj_a, hgrn_lb, norm_t]
    if has_state:
        in_specs.append(pl.BlockSpec((None, 2, GROUP, GROUP), lambda b: (b, 0, 0, 0)))
        args.append(s0)
    out_specs, out_shape = zip(_layer_slot(nseq, None, (seq, GROUP)),
                               _layer_slot(nseq, stack, (2, GROUP, GROUP)))
    out_specs, out_shape = list(out_specs), list(out_shape)
    scratch = [pltpu.VMEM((seq, GROUP), F32),
               pltpu.VMEM((2, seq, GROUP), F32),
               pltpu.VMEM((2, seq, GROUP), F32),
               pltpu.VMEM((2, seq, GROUP), F32),
               pltpu.VMEM((2, seq, GROUP), BF16),
               pltpu.VMEM((2, seq, GROUP), BF16),
               pltpu.VMEM((2, GROUP, GROUP), F32),
               pltpu.VMEM((3, seq, GROUP), F32)]
    return functools.partial(_hgrn_kernel, layer, has_state), in_specs, args, out_specs, out_shape, scratch


def _mla_kernel(latent, *refs):
    if latent:
        (b_ref, qn_ref, wq_ref, kvn_ref, wkv_ref, cckv_ref, ckpe_ref, cos_ref, sin_ref,
         o_ref) = refs
    else:
        (b_ref, qn_ref, wq_ref, kvn_ref, wkv_ref, o_ref, ckv_ref, kpe_ref) = refs
    seq = b_ref.shape[0]
    mckv = b_ref[:, 0:MLA_KV_RANK]
    mcq = b_ref[:, MLA_KV_RANK:MLA_KV_RANK + MLA_Q_RANK]
    kpe_t = b_ref[:, 384:512]

    cq = mcq * lax.rsqrt(jnp.mean(mcq * mcq, axis=-1, keepdims=True) + NORM_EPS) * qn_ref[...]
    qf = _dot(cq.astype(BF16), wq_ref[...])
    ckv = mckv * lax.rsqrt(jnp.mean(mckv * mckv, axis=-1, keepdims=True) + NORM_EPS) * kvn_ref[...]
    q_nope = qf[:, 0:N_HEADS * MLA_NOPE]
    q_rope = qf[:, N_HEADS * MLA_NOPE:]
    if latent:
        cos = cos_ref[...]
        sin = sin_ref[...]
        q_rope = _rope(q_rope, cos, sin)
        ckv_all = jnp.concatenate([cckv_ref[...], ckv], axis=0)
        kpe_all = jnp.concatenate([ckpe_ref[...], _rope(kpe_t, cos, sin)], axis=0)
    else:
        ckv_ref[...] = ckv
        kpe_ref[...] = kpe_t[:, 0:MLA_ROPE]
        ckv_all = ckv
        kpe_all = kpe_t
    kv = _dot(ckv_all.astype(BF16), wkv_ref[...])
    kcat = jnp.concatenate([kv[:, 0:GROUP], kpe_all], axis=1).astype(BF16)
    v = kv[:, GROUP:].astype(BF16)
    qcat = jnp.concatenate([q_nope, q_rope], axis=1)
    scale = (MLA_NOPE + MLA_ROPE) ** -0.5
    qb = min(seq, 256)
    width = qcat.shape[1]
    for r0 in range(0, seq, qb):
        qblk = qcat[r0:r0 + qb]
        acc = jnp.zeros((qb, GROUP), F32)
        for h in range(N_HEADS):
            hm = (_lane_group_mask(width, h * MLA_NOPE, MLA_NOPE)
                  | _lane_group_mask(width, N_HEADS * MLA_NOPE + h * MLA_ROPE, MLA_ROPE))
            s = _dot_nt(jnp.where(hm, qblk, 0.0).astype(BF16), kcat)
            e = jnp.exp2((s - jnp.max(s, axis=-1, keepdims=True)) * (scale * LOG2_E))
            z = jnp.sum(e, axis=-1, keepdims=True)
            oh = _dot(e.astype(BF16), v) / z
            acc = acc + jnp.where(_lane_group_mask(GROUP, h * HEAD_DIM, HEAD_DIM), oh, 0.0)
        o_ref[r0:r0 + qb, :] = acc


def _mla_parts(proj_b, row0, nseq, seq, q_norm, w_qb_p, kv_norm, w_kvb_p, latent_args, stack=None):
    latent = latent_args is not None
    blk0 = row0 // seq
    in_specs = [pl.BlockSpec((seq, PROJ_B), lambda b: (blk0 + b, 0)),
                pl.BlockSpec(q_norm.shape, lambda b: (0, 0)),
                pl.BlockSpec(w_qb_p.shape, lambda b: (0, 0)),
                pl.BlockSpec(kv_norm.shape, lambda b: (0, 0)),
                pl.BlockSpec(w_kvb_p.shape, lambda b: (0, 0))]
    args = [proj_b, q_norm, w_qb_p, kv_norm, w_kvb_p]
    out_specs = [pl.BlockSpec((None, seq, GROUP), lambda b: (b, 0, 0))]
    out_shape = [jax.ShapeDtypeStruct((nseq, seq, GROUP), F32)]
    if latent:
        cckv, ckpe_t, cos, sin = latent_args
        past = cckv.shape[1]
        in_specs += [pl.BlockSpec((None, past, MLA_KV_RANK), lambda b: (b, 0, 0)),
                     pl.BlockSpec((None, past, 128), lambda b: (b, 0, 0)),
                     pl.BlockSpec(cos.shape, lambda b: (0, 0)),
                     pl.BlockSpec(sin.shape, lambda b: (0, 0))]
        args += [cckv, ckpe_t, cos, sin]
    else:
        for width in (MLA_KV_RANK, MLA_ROPE):
            spec, shape = _layer_slot(nseq, stack, (seq, width))
            out_specs.append(spec)
            out_shape.append(shape)
    return functools.partial(_mla_kernel, latent), in_specs, args, out_specs, out_shape, []


def _diff_kernel(latent, lam_init, *refs):
    if latent:
        (c_ref, lam_ref, norm_ref, ck_ref, cv_ref, cos_ref, sin_ref, o_ref) = refs
    else:
        (c_ref, lam_ref, norm_ref, o_ref, k_ref, v_ref) = refs
    seq = c_ref.shape[0]
    dq = c_ref[:, 0:GROUP]
    dk = c_ref[:, GROUP:2 * GROUP]
    dv = c_ref[:, 2 * GROUP:3 * GROUP]
    if latent:
        cos = cos_ref[...]
        sin = sin_ref[...]
        dq = _rope(dq, cos, sin)
        k_all = jnp.concatenate([ck_ref[...], _rope(dk, cos, sin)], axis=0)
        v_all = jnp.concatenate([cv_ref[...], dv], axis=0)
    else:
        k_ref[...] = dk
        v_ref[...] = dv
        k_all = dk
        v_all = dv
    lv = lam_ref[...]
    lam = (jnp.exp(jnp.sum(lv[0:1] * lv[1:2], axis=-1, keepdims=True))
           - jnp.exp(jnp.sum(lv[2:3] * lv[3:4], axis=-1, keepdims=True)) + lam_init)
    k_bf = k_all.astype(BF16)
    v_bf = v_all.astype(BF16)
    scale = DIFF_DIM ** -0.5
    mean_op = jnp.where(_block_mask(GROUP, GROUP, HEAD_DIM, HEAD_DIM), 1.0 / HEAD_DIM, 0.0)
    qb = min(seq, 256)
    for r0 in range(0, seq, qb):
        qblk = dq[r0:r0 + qb]
        acc = jnp.zeros((qb, GROUP), F32)
        for h in range(N_HEADS):
            outs = []
            for comp in range(2):
                cm = _lane_group_mask(GROUP, h * HEAD_DIM + comp * DIFF_DIM, DIFF_DIM)
                s = _dot_nt(jnp.where(cm, qblk, 0.0).astype(BF16), k_bf)
                e = jnp.exp2((s - jnp.max(s, axis=-1, keepdims=True)) * (scale * LOG2_E))
                outs.append(_dot(e.astype(BF16), v_bf) / jnp.sum(e, axis=-1, keepdims=True))
            oh = outs[0] - lam * outs[1]
            acc = acc + jnp.where(_lane_group_mask(GROUP, h * HEAD_DIM, HEAD_DIM), oh, 0.0)
        ms = _dot(acc * acc, mean_op, precision=HIGHEST)
        o_ref[r0:r0 + qb, :] = acc * lax.rsqrt(ms + NORM_EPS) * norm_ref[...] * (1.0 - lam_init)


def _diff_parts(proj_c, row0, nseq, seq, lam_p, norm_t, lam_init, latent_args, stack=None):
    latent = latent_args is not None
    blk0 = row0 // seq
    in_specs = [pl.BlockSpec((seq, PROJ_C), lambda b: (blk0 + b, 0)),
                pl.BlockSpec(lam_p.shape, lambda b: (0, 0)),
                pl.BlockSpec(norm_t.shape, lambda b: (0, 0))]
    args = [proj_c, lam_p, norm_t]
    out_specs = [pl.BlockSpec((None, seq, GROUP), lambda b: (b, 0, 0))]
    out_shape = [jax.ShapeDtypeStruct((nseq, seq, GROUP), F32)]
    if latent:
        ck, cv, cos, sin = latent_args
        past = ck.shape[1]
        in_specs += [pl.BlockSpec((None, past, GROUP), lambda b: (b, 0, 0)),
                     pl.BlockSpec((None, past, GROUP), lambda b: (b, 0, 0)),
                     pl.BlockSpec(cos.shape, lambda b: (0, 0)),
                     pl.BlockSpec(sin.shape, lambda b: (0, 0))]
        args += [ck, cv, cos, sin]
    else:
        for _ in range(2):
            spec, shape = _layer_slot(nseq, stack, (seq, GROUP))
            out_specs.append(spec)
            out_shape.append(shape)
    return functools.partial(_diff_kernel, latent, lam_init), in_specs, args, out_specs, out_shape, []


def _mixers(nseq, name, *parts, carried=None):
    n_in = [len(p[1]) for p in parts]
    n_out = [len(p[3]) for p in parts]
    n_scr = [len(p[5]) for p in parts]
    carried = [None] * sum(n_out) if carried is None else list(carried)
    kept = [(o, arr) for o, arr in enumerate(carried) if arr is not None]

    def body(*refs):
        ins = refs[:sum(n_in)]
        outs = refs[sum(n_in) + len(kept):sum(n_in) + len(kept) + sum(n_out)]
        scr = refs[sum(n_in) + len(kept) + sum(n_out):]
        i0 = o0 = s0 = 0
        for part, ni, no, ns in zip(parts, n_in, n_out, n_scr):
            part[0](*ins[i0:i0 + ni], *outs[o0:o0 + no], *scr[s0:s0 + ns])
            i0, o0, s0 = i0 + ni, o0 + no, s0 + ns

    res = pl.pallas_call(
        body,
        grid=(nseq,),
        in_specs=[x for p in parts for x in p[1]] + [pl.BlockSpec(memory_space=pl.ANY)] * len(kept),
        out_specs=[x for p in parts for x in p[3]],
        out_shape=[x for p in parts for x in p[4]],
        scratch_shapes=[x for p in parts for x in p[5]],
        input_output_aliases={sum(n_in) + k: o for k, (o, _) in enumerate(kept)},
        compiler_params=_cparams("arbitrary"),
        name=name,
    )(*[x for p in parts for x in p[2]], *[arr for _, arr in kept])
    out, o0 = [], 0
    for no in n_out:
        out.append(tuple(res[o0:o0 + no]))
        o0 += no
    return out


def _ssd_kernel(has_state, *refs):
    if has_state:
        (d_ref, cw_ref, cb_ref, dtb_ref, alog_ref, dskip_ref, norm_ref, s0_ref,
         o_ref, sfin_ref, xs_scr, bm_scr, cm_scr, xdt_scr, a_scr, st_scr, yf_scr, yb_scr) = refs
    else:
        (d_ref, cw_ref, cb_ref, dtb_ref, alog_ref, dskip_ref, norm_ref,
         o_ref, sfin_ref, xs_scr, bm_scr, cm_scr, xdt_scr, a_scr, st_scr, yf_scr, yb_scr) = refs
    seq = d_ref.shape[0]
    c = SSD_CHUNK
    nchunk = seq // c
    ngrp = 2 * SSD_STATE

    xin = d_ref[:, GROUP:GROUP + 512]
    rows = _iota(xin.shape, 0)
    prev = jnp.where(rows == 0, 0.0, pltpu.roll(xin, 1, 0))
    nxt = jnp.where(rows == seq - 1, 0.0, pltpu.roll(xin, seq - 1, 0))
    cw = cw_ref[...]
    xbc = _silu(cw[0:1] * prev + cw[1:2] * xin + cw[2:3] * nxt + cb_ref[...])
    xs = xbc[:, 0:GROUP]
    xs_scr[...] = xs
    bm_scr[...] = xbc[:, GROUP:GROUP + ngrp]
    cm_scr[...] = xbc[:, GROUP + ngrp:GROUP + 2 * ngrp]
    dt = _softplus(d_ref[:, GROUP + 512:GROUP + 640] + dtb_ref[...])
    a_scr[...] = dt * (-jnp.exp(alog_ref[...]))
    erow = _iota((128, GROUP), 0)
    ehead = _iota((128, GROUP), 1) // HEAD_DIM
    expand = tuple((erow == 4 * d + ehead).astype(F32) for d in range(2))
    for d in range(2):
        xdt_scr[d] = xs * _dot(dt, expand[d], precision=HIGHEST)
    if has_state:
        st_scr[...] = s0_ref[...]
    else:
        st_scr[...] = jnp.zeros_like(st_scr)

    row = _iota((c, c), 0)
    col = _iota((c, c), 1)
    tri = ((col <= row).astype(F32), (col >= row).astype(F32))
    keep = (col <= row, col >= row)
    grp_lane = _iota((1, ngrp), 1) // SSD_STATE
    valid = (_iota((ngrp, GROUP), 0) // SSD_STATE) == (_iota((ngrp, GROUP), 1) // (2 * HEAD_DIM))

    def chunk_step(d, r0, out_scr):
        a_c = a_scr[pl.ds(r0, c), :]
        bm_c = bm_scr[pl.ds(r0, c), :]
        cm_c = cm_scr[pl.ds(r0, c), :].astype(BF16)
        xdt_c = xdt_scr[d, pl.ds(r0, c), :]
        acum = _dot(tri[d], a_c, precision=HIGHEST)
        acum_t = acum.T
        acum_rep = _dot(acum, expand[d], precision=HIGHEST)
        bm2 = jnp.concatenate([jnp.where(grp_lane == g, bm_c, 0.0) for g in range(2)], axis=0)
        cb = _dot_nt(cm_c, bm2.astype(BF16))
        scores = []
        xparts = []
        for h in range(N_HEADS):
            lane = 4 * d + h
            seg = jnp.exp(jnp.where(keep[d], acum[:, lane:lane + 1] - acum_t[lane:lane + 1, :], -jnp.inf))
            g = h // 2
            scores.append((cb[:, g * c:(g + 1) * c] * seg).astype(BF16))
            xparts.append(jnp.where(_lane_group_mask(GROUP, h * HEAD_DIM, HEAD_DIM), xdt_c, 0.0))
        y = _dot(jnp.concatenate(scores, axis=1), jnp.concatenate(xparts, axis=0).astype(BF16))
        st = st_scr[d]
        y = y + _dot(cm_c, st.astype(BF16)) * jnp.exp(acum_rep)
        out_scr[pl.ds(r0, c), :] = y
        edge = acum_rep[c - 1:c] if d == 0 else acum_rep[0:1]
        xt = (xdt_c * jnp.exp(edge - acum_rep)).astype(BF16)
        upd = _dot_tn(bm_c.astype(BF16), xt)
        st_scr[d] = st * jnp.exp(edge) + jnp.where(valid, upd, 0.0)

    def body(n, carry):
        chunk_step(0, pl.multiple_of(n * c, c), yf_scr)
        chunk_step(1, pl.multiple_of((nchunk - 1 - n) * c, c), yb_scr)
        return carry

    lax.fori_loop(0, nchunk, body, 0)

    y = yf_scr[...] + yb_scr[...] + dskip_ref[...] * xs_scr[...]
    y = y * _silu(d_ref[:, 0:GROUP])
    o_ref[...] = y * lax.rsqrt(jnp.mean(y * y, axis=-1, keepdims=True) + NORM_EPS) * norm_ref[...]
    sfin_ref[...] = st_scr[...]


def _ssd_parts(proj_d, row0, nseq, seq, conv_w, conv_b, dt_bias_p, a_log_p, d_rep, norm, s0, stack=None):
    has_state = s0 is not None
    blk0 = row0 // seq
    ngrp = 2 * SSD_STATE
    small = [conv_w, conv_b, dt_bias_p, a_log_p, d_rep, norm]
    in_specs = ([pl.BlockSpec((seq, PROJ_D), lambda b: (blk0 + b, 0))]
                + [pl.BlockSpec(s.shape, lambda b: (0, 0)) for s in small])
    args = [proj_d] + small
    if has_state:
        in_specs.append(pl.BlockSpec((None, 2, ngrp, GROUP), lambda b: (b, 0, 0, 0)))
        args.append(s0)
    out_specs, out_shape = zip(_layer_slot(nseq, None, (seq, GROUP)),
                               _layer_slot(nseq, stack, (2, ngrp, GROUP)))
    out_specs, out_shape = list(out_specs), list(out_shape)
    scratch = [pltpu.VMEM((seq, GROUP), F32),
               pltpu.VMEM((seq, ngrp), F32),
               pltpu.VMEM((seq, ngrp), F32),
               pltpu.VMEM((2, seq, GROUP), F32),
               pltpu.VMEM((seq, 128), F32),
               pltpu.VMEM((2, ngrp, GROUP), F32),
               pltpu.VMEM((seq, GROUP), F32),
               pltpu.VMEM((seq, GROUP), F32)]
    return functools.partial(_ssd_kernel, has_state), in_specs, args, out_specs, out_shape, scratch


def _outproj_kernel(n_ctx_tiles, *refs):
    ctx_refs, lat_refs = refs[0:4], refs[4:8]
    w_ref, x_ref, m_ref, g_ref, b_ref, o_ref = refs[8:]
    is_ctx = pl.program_id(0) < n_ctx_tiles
    mixed = None
    for i, (c_ref, l_ref) in enumerate(zip(ctx_refs, lat_refs)):
        part = jnp.where(is_ctx, c_ref[...], l_ref[...]).astype(BF16)
        term = _dot(part, w_ref[i * GROUP:(i + 1) * GROUP, :])
        mixed = term if mixed is None else mixed + term
    o_ref[...] = _layernorm(ALPHA * x_ref[...] + m_ref[2] * mixed, g_ref[...], b_ref[...])


def _outproj(parts_ctx, parts_lat, w_out_bf, x, mods_l, ln_g, ln_b, t_ctx, s_lat):
    t, d = x.shape
    tm = 512
    n_ctx = t_ctx // tm
    n_lat = (t - t_ctx) // tm
    ctx_spec = pl.BlockSpec((tm, GROUP), lambda i: (jnp.minimum(i, n_ctx - 1), 0))
    lat_spec = pl.BlockSpec((tm, GROUP), lambda i: (jnp.clip(i - n_ctx, 0, n_lat - 1), 0))
    return pl.pallas_call(
        functools.partial(_outproj_kernel, n_ctx),
        grid=(t // tm,),
        in_specs=[ctx_spec] * 4 + [lat_spec] * 4
        + [pl.BlockSpec(w_out_bf.shape, lambda i: (0, 0)),
           pl.BlockSpec((tm, d), lambda i: (i, 0)),
           pl.BlockSpec((None, 6, 1, d), _mod_row_map(tm, t_ctx, s_lat)),
           pl.BlockSpec((1, d), lambda i: (0, 0)),
           pl.BlockSpec((1, d), lambda i: (0, 0))],
        out_specs=pl.BlockSpec((tm, d), lambda i: (i, 0)),
        out_shape=jax.ShapeDtypeStruct((t, d), F32),
        compiler_params=_cparams("arbitrary"),
        name="outproj_ln",
    )(*parts_ctx, *parts_lat, w_out_bf, x, mods_l, ln_g, ln_b)


def _top_rows(s, k, extra=()):
    r = s.shape[0]
    rid = _iota(s.shape, 0).astype(F32)
    vals, ids = [], []
    picked = [[] for _ in extra]
    for _ in range(k):
        m = jnp.max(s, axis=0, keepdims=True)
        cand = jnp.where(s == m, rid, float(r))
        i = jnp.min(cand, axis=0, keepdims=True)
        hit = cand == i
        vals.append(m)
        ids.append(i)
        for lst, arr in zip(picked, extra):
            lst.append(jnp.max(jnp.where(hit, arr, -1.0), axis=0, keepdims=True))
        s = jnp.where(hit, -jnp.inf, s)
    cat = lambda xs: jnp.concatenate(xs, axis=0)
    return cat(vals), cat(ids), [cat(p) for p in picked]


def _router_kernel(x_ref, m_ref, wq_ref, keys_ref, h_ref, a_ref, b_ref, g_ref):
    tm = x_ref.shape[0]
    hb = (x_ref[...] * (1.0 + m_ref[4]) + m_ref[3]).astype(BF16)
    h_ref[...] = hb
    qt = _dot(hb, wq_ref[...]).T.astype(BF16)
    k = PEER_TOPK
    code_rows, g_rows = [], []
    for head in range(PEER_HEADS):
        tv, ti = [], []
        for half in range(2):
            g = 2 * head + half
            sc = _dot(keys_ref[g], qt[g * PEER_HALF:(g + 1) * PEER_HALF])
            v, i, _ = _top_rows(sc, k)
            tv.append(v)
            ti.append(i)
        cs = [tv[0][0:1] + tv[1]]
        ca = [jnp.broadcast_to(ti[0][0:1], (k, tm))]
        cb = [ti[1]]
        for k1 in range(1, 4):
            cs.append(tv[0][k1:k1 + 1] + tv[1][0:8])
            ca.append(jnp.broadcast_to(ti[0][k1:k1 + 1], (8, tm)))
            cb.append(ti[1][0:8])
        low = _iota((8, tm), 0) < 4
        v2_dup = jnp.where(low, tv[1][0:8], pltpu.roll(tv[1][0:8], 4, 0))
        i2_dup = jnp.where(low, ti[1][0:8], pltpu.roll(ti[1][0:8], 4, 0))
        for k1 in (4, 6):
            cs.append(jnp.where(low, tv[0][k1:k1 + 1], tv[0][k1 + 1:k1 + 2]) + v2_dup)
            ca.append(jnp.where(low, ti[0][k1:k1 + 1], ti[0][k1 + 1:k1 + 2]))
            cb.append(i2_dup)
        cs.append(tv[0][8:16] + tv[1][0:1])
        ca.append(ti[0][8:16])
        cb.append(jnp.broadcast_to(ti[1][0:1], (8, tm)))
        code = jnp.concatenate(ca, axis=0) * float(PEER_KEYS) + jnp.concatenate(cb, axis=0)
        best, _, (sel_code,) = _top_rows(jnp.concatenate(cs, axis=0), k, extra=(code,))
        e = jnp.exp(best - best[0:1])
        g_rows.append(e / jnp.sum(e, axis=0, keepdims=True))
        code_rows.append(sel_code)
    codes = jnp.concatenate(code_rows, axis=0)
    key1 = jnp.floor(codes * (1.0 / PEER_KEYS))
    a_ref[...] = key1.T.astype(I32)
    b_ref[...] = (codes - key1 * float(PEER_KEYS)).T.astype(I32)
    g_ref[...] = jnp.concatenate(g_rows, axis=0).T


def _router(x, mods_l, wq_bf, keys, t_ctx, s_lat):
    t, d = x.shape
    tm = 128
    nslot = PEER_HEADS * PEER_TOPK
    return pl.pallas_call(
        _router_kernel,
        grid=(t // tm,),
        in_specs=[pl.BlockSpec((tm, d), lambda i: (i, 0)),
                  pl.BlockSpec((None, 6, 1, d), _mod_row_map(tm, t_ctx, s_lat)),
                  pl.BlockSpec(wq_bf.shape, lambda i: (0, 0)),
                  pl.BlockSpec(keys.shape, lambda i: (0, 0, 0))],
        out_specs=[pl.BlockSpec((tm, d), lambda i: (i, 0)),
                   pl.BlockSpec((tm, nslot), lambda i: (i, 0)),
                   pl.BlockSpec((tm, nslot), lambda i: (i, 0)),
                   pl.BlockSpec((tm, nslot), lambda i: (i, 0))],
        out_shape=[jax.ShapeDtypeStruct((t, d), BF16),
                   jax.ShapeDtypeStruct((t, nslot), I32),
                   jax.ShapeDtypeStruct((t, nslot), I32),
                   jax.ShapeDtypeStruct((t, nslot), F32)],
        compiler_params=_cparams("arbitrary"),
        name="peer_router",
    )(x, mods_l, wq_bf, keys)


def _gates_kernel(a_ref, b_ref, g_ref, u_ref, v_ref, o_ref, ub_ref, vb_ref):
    tm = a_ref.shape[0]
    n = PEER_KEYS
    sub = 16
    ub_ref[...] = u_ref[...].astype(BF16)
    vb_ref[...] = v_ref[...].astype(BF16)
    key = _iota((sub, n, a_ref.shape[2]), 1).astype(F32).astype(BF16)
    zero = jnp.zeros((), BF16)
    for t0 in range(0, tm, sub):
        a = a_ref[t0:t0 + sub].astype(F32).astype(BF16)
        b = b_ref[t0:t0 + sub].astype(F32).astype(BF16)
        g = g_ref[t0:t0 + sub].astype(BF16)
        onehot_a = jnp.where(key == a, jnp.ones((), BF16), zero)
        gated_b = jnp.where(key == b, g, zero)
        w = lax.dot_general(onehot_a, gated_b, (((2,), (2,)), ((0,), (0,))),
                            preferred_element_type=F32)
        w_t = jnp.swapaxes(w.astype(BF16), 0, 1)
        for r in range(n):
            o_ref[t0:t0 + sub, r * n:(r + 1) * n] = w_t[r]


def _gates(a_idx, b_idx, gate, peer_u, peer_v, layer):
    t, nslot = a_idx.shape
    d = peer_u.shape[-1]
    tm = 96
    steps = t // tm
    te = N_EXPERTS // steps
    spec = pl.BlockSpec((tm, 1, nslot), lambda i: (i, 0, 0))
    tab_in = pl.BlockSpec((None, te, d), lambda i: (layer, i, 0))
    tab_out = pl.BlockSpec((te, d), lambda i: (i, 0))
    return pl.pallas_call(
        _gates_kernel,
        grid=(steps,),
        in_specs=[spec, spec, spec, tab_in, tab_in],
        out_specs=[pl.BlockSpec((tm, N_EXPERTS), lambda i: (i, 0)), tab_out, tab_out],
        out_shape=[jax.ShapeDtypeStruct((t, N_EXPERTS), BF16),
                   jax.ShapeDtypeStruct((N_EXPERTS, d), BF16),
                   jax.ShapeDtypeStruct((N_EXPERTS, d), BF16)],
        compiler_params=_cparams("arbitrary"),
        name="peer_gates",
    )(a_idx.reshape(t, 1, nslot), b_idx.reshape(t, 1, nslot), gate.reshape(t, 1, nslot), peer_u, peer_v)


def _experts_kernel(h_ref, u_ref, v_ref, w_ref, x_ref, m_ref, g_ref, b_ref, o_ref, acc_ref):
    j = pl.program_id(1)

    @pl.when(j == 0)
    def _():
        acc_ref[...] = jnp.zeros_like(acc_ref)

    act = _gelu_tanh(_dot_nt(h_ref[...], u_ref[...]))
    acc_ref[...] += _dot((act * w_ref[...].astype(F32)).astype(BF16), v_ref[...])

    @pl.when(j == pl.num_programs(1) - 1)
    def _():
        o_ref[...] = _layernorm(ALPHA * x_ref[...] + m_ref[5] * acc_ref[...], g_ref[...], b_ref[...])


def _experts(h_bf, u_bf, v_bf, w_gate, x, mods_l, ln_g, ln_b, t_ctx, s_lat):
    t, d = x.shape
    tm, te = 1024, 1024
    return pl.pallas_call(
        _experts_kernel,
        grid=(t // tm, N_EXPERTS // te),
        in_specs=[pl.BlockSpec((tm, d), lambda i, j: (i, 0)),
                  pl.BlockSpec((te, d), lambda i, j: (j, 0)),
                  pl.BlockSpec((te, d), lambda i, j: (j, 0)),
                  pl.BlockSpec((tm, te), lambda i, j: (i, j)),
                  pl.BlockSpec((tm, d), lambda i, j: (i, 0)),
                  pl.BlockSpec((None, 6, 1, d), _mod_row_map(tm, t_ctx, s_lat)),
                  pl.BlockSpec((1, d), lambda i, j: (0, 0)),
                  pl.BlockSpec((1, d), lambda i, j: (0, 0))],
        out_specs=pl.BlockSpec((tm, d), lambda i, j: (i, 0)),
        out_shape=jax.ShapeDtypeStruct((t, d), F32),
        scratch_shapes=[pltpu.VMEM((tm, d), F32)],
        compiler_params=_cparams("arbitrary", "arbitrary"),
        name="peer_experts",
    )(h_bf, u_bf, v_bf, w_gate, x, mods_l, ln_g, ln_b)


def _pad_cols(w, n):
    return jnp.concatenate([w, jnp.zeros((w.shape[0], n), w.dtype)], axis=1) if n else w


def _layout_w_in(w):
    a = w[:, 0:1280]
    mcq, mckv, mkpe = w[:, 1280:1472], w[:, 1472:1600], w[:, 1600:1632]
    b = jnp.concatenate([mckv, _pad_cols(mcq, 64), mkpe, mkpe, mkpe, mkpe], axis=1)
    c = w[:, 1632:2400]
    d = _pad_cols(w[:, 2400:3176], PROJ_D - 776)
    return jnp.concatenate([a, b, c, d], axis=1).astype(BF16)


def _layout_w_qb(w):
    w4 = w.reshape(MLA_Q_RANK, N_HEADS, MLA_NOPE + MLA_ROPE)
    return jnp.concatenate([w4[:, :, :MLA_NOPE].reshape(MLA_Q_RANK, -1),
                            w4[:, :, MLA_NOPE:].reshape(MLA_Q_RANK, -1)], axis=1).astype(BF16)


def _layout_w_kvb(w):
    w4 = w.reshape(MLA_KV_RANK, N_HEADS, MLA_NOPE + HEAD_DIM)
    return jnp.concatenate([w4[:, :, :MLA_NOPE].reshape(MLA_KV_RANK, -1),
                            w4[:, :, MLA_NOPE:].reshape(MLA_KV_RANK, -1)], axis=1).astype(BF16)


def _rope_tables(seq):
    rows = seq // GRID_W
    row = jnp.repeat(jnp.arange(rows, dtype=F32), GRID_W)
    col = jnp.tile(jnp.arange(GRID_W, dtype=F32), rows)
    freqs = ROPE_BASE ** (-jnp.arange(ROPE_PAIRS, dtype=F32) / ROPE_PAIRS)
    cos_l, sin_l = [], []
    for pos in (row, col):
        ang = pos[:, None] * freqs
        cos_l += [jnp.cos(ang), jnp.cos(ang)]
        sin_l += [-jnp.sin(ang), jnp.sin(ang)]
    return jnp.concatenate(cos_l, axis=1), jnp.concatenate(sin_l, axis=1)


def _hgrn_state_pack(st):
    b = st.shape[0]
    st_t = jnp.swapaxes(st, -1, -2)
    zero = jnp.zeros_like(st_t[:, :, 0])
    rows = [jnp.concatenate([st_t[:, :, h] if g == h else zero for g in range(N_HEADS)], axis=-1)
            for h in range(N_HEADS)]
    return jnp.concatenate(rows, axis=-2).reshape(b, 2, GROUP, GROUP)


def _hgrn_state_unpack(sb):
    blocks = [sb[..., h * HEAD_DIM:(h + 1) * HEAD_DIM, h * HEAD_DIM:(h + 1) * HEAD_DIM]
              for h in range(N_HEADS)]
    return jnp.swapaxes(jnp.stack(blocks, axis=-3), -1, -2)


def _ssd_state_pack(st):
    st_t = jnp.swapaxes(st, -1, -2)
    zero = jnp.zeros_like(st_t[:, :, 0])
    rows = [jnp.concatenate([st_t[:, :, h] if h // 2 == g else zero for h in range(N_HEADS)], axis=-1)
            for g in range(2)]
    return jnp.concatenate(rows, axis=-2)


def _ssd_state_unpack(sb):
    blocks = [sb[..., (h // 2) * SSD_STATE:(h // 2 + 1) * SSD_STATE, h * HEAD_DIM:(h + 1) * HEAD_DIM]
              for h in range(N_HEADS)]
    return jnp.swapaxes(jnp.stack(blocks, axis=-3), -1, -2)


def _tile_lanes(v, n):
    return jnp.tile(v.reshape(1, -1), (1, n))


def kernel(x_prompt, x_sample, cache_mla_ckv, cache_mla_kpe, cache_diff_k, cache_diff_v, state_hgrn, state_ssd, c, c_ctx, w_mod, b_mod, w_in, hgrn_lb, hgrn_norm, mla_q_norm, mla_w_qb, mla_kv_norm, mla_w_kvb, diff_lambda, diff_norm, ssd_conv_w, ssd_conv_b, ssd_dt_bias, ssd_a_log, ssd_d, ssd_norm, w_out, ln1_g, ln1_b, peer_wq, peer_keys, peer_u, peer_v, ln2_g, ln2_b):
    nb, seq, d = x_prompt.shape
    nlat, lseq, _ = x_sample.shape
    depth = w_in.shape[0]
    t_ctx = nb * seq
    x = jnp.concatenate([x_prompt.reshape(t_ctx, d), x_sample.reshape(nlat * lseq, d)], axis=0)

    cond8 = jnp.concatenate([c_ctx.reshape(1, d), c, jnp.zeros((8 - 1 - nlat, d), F32)], axis=0)
    mods = _mods(cond8, w_mod, b_mod)
    mods = mods[:, :1 + nlat].reshape(depth, 1 + nlat, 6, 1, d)

    cos32, sin32 = _rope_tables(lseq)
    cos128, sin128 = jnp.tile(cos32, (1, 4)), jnp.tile(sin32, (1, 4))
    cos256, sin256 = jnp.tile(cos32, (1, 8)), jnp.tile(sin32, (1, 8))

    carried = None
    for l in range(depth):
        mods_l = mods[l]
        pa, pb, pc, pd = _inproj(x, mods_l, _layout_w_in(w_in[l]), t_ctx, lseq)

        norm_hg = _tile_lanes(hgrn_norm[l], N_HEADS)
        mla_w = (mla_q_norm[l].reshape(1, -1), _layout_w_qb(mla_w_qb[l]),
                 mla_kv_norm[l].reshape(1, -1), _layout_w_kvb(mla_w_kvb[l]))
        lam_init = 0.8 - 0.6 * math.exp(-0.3 * l)
        norm_df = _tile_lanes(diff_norm[l], N_HEADS)
        past = cache_diff_k.shape[2]
        ssd_w = (ssd_conv_w[l], ssd_conv_b[l].reshape(1, -1),
                 _pad_cols(ssd_dt_bias[l].reshape(1, -1), 120), _pad_cols(ssd_a_log[l].reshape(1, -1), 120),
                 jnp.repeat(ssd_d[l], HEAD_DIM).reshape(1, -1), ssd_norm[l].reshape(1, -1))

        stack = (l, depth)
        ctx_out = _mixers(
            nb, "mixers_ctx",
            _mla_parts(pb, 0, nb, seq, *mla_w, None, stack),
            _diff_parts(pc, 0, nb, seq, diff_lambda[l], norm_df, lam_init, None, stack),
            _ssd_parts(pd, 0, nb, seq, *ssd_w, None, stack),
            _hgrn_parts(pa, 0, nb, seq, hgrn_lb, norm_hg, None, l, stack),
            carried=carried)
        (mla_ctx, new_ckv, new_kpe), (df_ctx, new_dk, new_dv), (ssd_ctx, ssd_fin), (hg_ctx, hg_fin) = ctx_out
        carried = [None, new_ckv, new_kpe, None, new_dk, new_dv, None, ssd_fin, None, hg_fin]
        ((mla_lat,),) = _mixers(
            nlat, "mla_lat",
            _mla_parts(pb, t_ctx, nlat, lseq, *mla_w,
                       (cache_mla_ckv[:, l], jnp.tile(cache_mla_kpe[:, l], (1, 1, 4)), cos128, sin128)))
        ((df_lat,),) = _mixers(
            nlat, "diffattn_lat",
            _diff_parts(pc, t_ctx, nlat, lseq, diff_lambda[l], norm_df, lam_init,
                        (cache_diff_k[:, l].reshape(nlat, past, GROUP),
                         cache_diff_v[:, l].reshape(nlat, past, GROUP), cos256, sin256)))
        ((ssd_lat, _),) = _mixers(nlat, "ssd_lat",
                                  _ssd_parts(pd, t_ctx, nlat, lseq, *ssd_w, _ssd_state_pack(state_ssd[:, l])))
        ((hg_lat, _),) = _mixers(nlat, "hgrn_lat",
                                 _hgrn_parts(pa, t_ctx, nlat, lseq, hgrn_lb, norm_hg,
                                             _hgrn_state_pack(state_hgrn[:, l]), l))

        parts_ctx = [a.reshape(t_ctx, GROUP) for a in (hg_ctx, mla_ctx, df_ctx, ssd_ctx)]
        parts_lat = [a.reshape(nlat * lseq, GROUP) for a in (hg_lat, mla_lat, df_lat, ssd_lat)]
        x = _outproj(parts_ctx, parts_lat, w_out[l].astype(BF16), x, mods_l,
                     ln1_g[l].reshape(1, d), ln1_b[l].reshape(1, d), t_ctx, lseq)

        keys = peer_keys[l].reshape(2 * PEER_HEADS, PEER_KEYS, PEER_HALF).astype(BF16)
        h_bf, a_idx, b_idx, gate = _router(x, mods_l, peer_wq[l].astype(BF16), keys, t_ctx, lseq)
        w_gate, u_bf, v_bf = _gates(a_idx, b_idx, gate, peer_u, peer_v, l)
        x = _experts(h_bf, u_bf, v_bf, w_gate, x, mods_l,
                     ln2_g[l].reshape(1, d), ln2_b[l].reshape(1, d), t_ctx, lseq)

    y_prompt = x[:t_ctx].reshape(nb, seq, d)
    y_sample = x[t_ctx:].reshape(nlat, lseq, d)
    return (y_prompt, y_sample, new_ckv, new_kpe,
            new_dk.reshape(nb, depth, seq, N_HEADS, 2, DIFF_DIM),
            new_dv.reshape(nb, depth, seq, N_HEADS, 2 * DIFF_DIM),
            _hgrn_state_unpack(hg_fin), _ssd_state_unpack(ssd_fin))
```

```python
import functools
import math

import jax
import jax.numpy as jnp
from jax import lax
from jax.experimental import pallas as pl
from jax.experimental.pallas import tpu as pltpu

F32 = jnp.float32
BF16 = jnp.bfloat16
I32 = jnp.int32
HIGHEST = lax.Precision.HIGHEST

D_MODEL = 1024
GROUP = 256
N_HEADS = 4
HEAD_DIM = 64
HG_BLOCK = 16
HG_SLAB = 256
HG_UNROLL = 8
SSD_CHUNK = 128
SSD_STATE = 64
MLA_Q_RANK = 192
MLA_KV_RANK = 128
MLA_NOPE = 64
MLA_ROPE = 32
DIFF_DIM = 32
GRID_W = 64
ROPE_PAIRS = 8
ROPE_BASE = 10000.0
PEER_HEADS = 8
PEER_KEYS = 128
PEER_TOPK = 16
PEER_HALF = 64
N_EXPERTS = PEER_KEYS * PEER_KEYS
NORM_EPS = 1e-6
LN_EPS = 1e-5
DEPTH = 2
ALPHA = (2.0 * DEPTH) ** 0.25
LOG2_E = 1.4426950408889634

PROJ_A = 5 * GROUP
PROJ_B = 512
PROJ_C = 3 * GROUP
PROJ_D = 896
VMEM_LIMIT = 56 * 1024 * 1024


def _cparams(*sem):
    return pltpu.CompilerParams(dimension_semantics=sem, vmem_limit_bytes=VMEM_LIMIT)


def _sigmoid(x):
    return 1.0 / (1.0 + jnp.exp(-x))


def _silu(x):
    return x * _sigmoid(x)


def _softplus(x):
    return jnp.maximum(x, 0.0) + jnp.log(1.0 + jnp.exp(-jnp.abs(x)))


def _gelu_tanh(x):
    return 0.5 * x * (1.0 + jnp.tanh(math.sqrt(2.0 / math.pi) * (x + 0.044715 * (x * x * x))))


def _dot(a, b, precision=None):
    return jnp.dot(a, b, preferred_element_type=F32, precision=precision)


def _dot_nt(a, b, precision=None):
    return lax.dot_general(a, b, (((1,), (1,)), ((), ())), preferred_element_type=F32,
                           precision=precision)


def _dot_tn(a, b, precision=None):
    return lax.dot_general(a, b, (((0,), (0,)), ((), ())), preferred_element_type=F32,
                           precision=precision)


def _iota(shape, dim):
    return lax.broadcasted_iota(I32, shape, dim)


def _block_mask(rows, cols, rblk, cblk):
    return (_iota((rows, cols), 0) // rblk) == (_iota((rows, cols), 1) // cblk)


def _lane_group_mask(width, start, size):
    lane = _iota((1, width), 1)
    return (lane >= start) & (lane < start + size)


def _layernorm(v, g, b):
    mu = jnp.mean(v, axis=-1, keepdims=True)
    d = v - mu
    var = jnp.mean(d * d, axis=-1, keepdims=True)
    return d * lax.rsqrt(var + LN_EPS) * g + b


def _swap_halves16(x):
    width = x.shape[-1]
    lane = _iota(x.shape, x.ndim - 1)
    up = pltpu.roll(x, width - 8, x.ndim - 1)
    down = pltpu.roll(x, 8, x.ndim - 1)
    return jnp.where((lane % 16) < 8, up, down)


def _rope(x, cos, sin_signed):
    return x * cos + _swap_halves16(x) * sin_signed


def _mods_kernel(c_ref, w_ref, b_ref, o_ref):
    s = _silu(c_ref[...]).astype(BF16)
    o_ref[...] = _dot(s, w_ref[...].astype(BF16)) + b_ref[...]


def _mods(cond8, w_mod, b_mod):
    depth, d, n = w_mod.shape
    tn = 1536
    return pl.pallas_call(
        _mods_kernel,
        grid=(depth, n // tn),
        in_specs=[pl.BlockSpec((8, d), lambda l, j: (0, 0)),
                  pl.BlockSpec((None, d, tn), lambda l, j: (l, 0, j)),
                  pl.BlockSpec((None, 1, tn), lambda l, j: (l, 0, j))],
        out_specs=pl.BlockSpec((None, 8, tn), lambda l, j: (l, 0, j)),
        out_shape=jax.ShapeDtypeStruct((depth, 8, n), F32),
        compiler_params=_cparams("arbitrary", "arbitrary"),
        name="mods",
    )(cond8, w_mod, b_mod.reshape(depth, 1, n))


def _mod_row_map(tm, t_ctx, s_lat):
    def index_map(i, *_):
        start = i * tm
        return (jnp.where(start < t_ctx, 0, 1 + (start - t_ctx) // s_lat), 0, 0, 0)
    return index_map


def _inproj_kernel(x_ref, m_ref, w_ref, oa_ref, ob_ref, oc_ref, od_ref):
    h = (x_ref[...] * (1.0 + m_ref[1]) + m_ref[0]).astype(BF16)
    start = 0
    for o_ref in (oa_ref, ob_ref, oc_ref, od_ref):
        width = o_ref.shape[-1]
        o_ref[...] = _dot(h, w_ref[:, start:start + width])
        start += width


def _inproj(x, mods_l, w_in_p, t_ctx, s_lat):
    t, d = x.shape
    tm = 512
    widths = (PROJ_A, PROJ_B, PROJ_C, PROJ_D)
    return pl.pallas_call(
        _inproj_kernel,
        grid=(t // tm,),
        in_specs=[pl.BlockSpec((tm, d), lambda i: (i, 0)),
                  pl.BlockSpec((None, 6, 1, d), _mod_row_map(tm, t_ctx, s_lat)),
                  pl.BlockSpec(w_in_p.shape, lambda i: (0, 0))],
        out_specs=[pl.BlockSpec((tm, w), lambda i: (i, 0)) for w in widths],
        out_shape=[jax.ShapeDtypeStruct((t, w), F32) for w in widths],
        compiler_params=_cparams("arbitrary"),
        name="inproj",
    )(x, mods_l, w_in_p)


def _hgrn_kernel(layer, has_state, *refs):
    if has_state:
        (a_ref, lb_ref, norm_ref, s0_ref, o_ref, sfin_ref,
         q_scr, k_scr, bc_scr, dec_scr, qt_scr, kt_scr, st_scr, o_scr) = refs
    else:
        (a_ref, lb_ref, norm_ref, o_ref, sfin_ref,
         q_scr, k_scr, bc_scr, dec_scr, qt_scr, kt_scr, st_scr, o_scr) = refs
        s0_ref = None
    seq = a_ref.shape[0]
    c = HG_BLOCK
    nblk = seq // c
    slab = HG_SLAB
    nb = slab // c

    lbp = lb_ref[...]
    e = jnp.exp(lbp - jnp.max(lbp, axis=0, keepdims=True))
    p = e / jnp.sum(e, axis=0, keepdims=True)
    lower = jnp.sum(p[1:layer + 1], axis=0) if layer > 0 else jnp.zeros_like(p[0])

    q = _silu(a_ref[:, 0:GROUP])
    q_scr[...] = q
    srow = _iota((slab, slab), 0)
    scol = _iota((slab, slab), 1)
    same = (srow // c) == (scol // c)
    cum_op = (jnp.where(same & (scol <= srow), 1.0, 0.0), jnp.where(same & (scol >= srow), 1.0, 0.0))
    for d in range(2):
        lb = lower[d:d + 1]
        f = lb + (1.0 - lb) * _sigmoid(a_ref[:, (1 + d) * GROUP:(2 + d) * GROUP])
        k = 1.0 - f
        lf = jnp.log(f)
        k_scr[d] = k
        for s0 in range(0, seq, slab):
            bc = _dot(cum_op[d], lf[s0:s0 + slab], precision=HIGHEST)
            bc3 = bc.reshape(nb, c, GROUP)
            edge = bc3[:, c - 1:c, :] if d == 0 else bc3[:, 0:1, :]
            tot = jnp.broadcast_to(edge, (nb, c, GROUP)).reshape(slab, GROUP)
            bc_scr[d, s0:s0 + slab, :] = bc * LOG2_E
            dec_scr[d, s0:s0 + slab, :] = jnp.exp(tot)
            qt_scr[d, s0:s0 + slab, :] = (q[s0:s0 + slab] * jnp.exp(bc)).astype(BF16)
            kt_scr[d, s0:s0 + slab, :] = (k[s0:s0 + slab] * jnp.exp(tot - bc)).astype(BF16)
    if has_state:
        st_scr[...] = s0_ref[...]
    else:
        st_scr[...] = jnp.zeros_like(st_scr)

    bd_ones = _block_mask(GROUP, GROUP, HEAD_DIM, HEAD_DIM).astype(BF16)
    rib = _iota((1, c, GROUP), 1)

    def slab_step(i, carry):
        r0 = pl.multiple_of(i * slab, slab)
        q3 = q_scr[pl.ds(r0, slab), :].reshape(nb, c, GROUP)
        v3 = a_ref[pl.ds(r0, slab), 3 * GROUP:4 * GROUP].reshape(nb, c, GROUP)
        o3 = jnp.zeros((nb, c, GROUP), F32)
        for d in range(2):
            bc3 = bc_scr[d, pl.ds(r0, slab), :].reshape(nb, c, GROUP)
            k3 = k_scr[d, pl.ds(r0, slab), :].reshape(nb, c, GROUP)
            for j in range(c):
                keep = (rib >= j) if d == 0 else (rib <= j)
                dec = jnp.exp2(jnp.where(keep, bc3 - bc3[:, j:j + 1, :], -jnp.inf))
                pj = (dec * q3 * k3[:, j:j + 1, :]).astype(BF16).reshape(slab, GROUP)
                srep = _dot(pj, bd_ones).reshape(nb, c, GROUP)
                o3 = o3 + srep * v3[:, j:j + 1, :]
        o_scr[0, pl.ds(r0, slab), :] = o3.reshape(slab, GROUP)
        return carry

    lax.fori_loop(0, seq // slab, slab_step, 0)

    bd_mask = _block_mask(GROUP, GROUP, HEAD_DIM, HEAD_DIM)

    def body(n, carry):
        rows = [[pl.multiple_of(((n * HG_UNROLL + u) if d == 0 else nblk - 1 - (n * HG_UNROLL + u)) * c, c)
                 for u in range(HG_UNROLL)] for d in range(2)]
        upd = [[_dot_tn(a_ref[pl.ds(r0, c), 3 * GROUP:4 * GROUP].astype(BF16), kt_scr[d, pl.ds(r0, c), :])
                for r0 in rows[d]] for d in range(2)]
        for d in range(2):
            st = st_scr[d]
            for u, r0 in enumerate(rows[d]):
                o_scr[1 + d, pl.ds(r0, c), :] = _dot_nt(qt_scr[d, pl.ds(r0, c), :], st.astype(BF16))
                st = st * dec_scr[d, pl.ds(r0, 1), :] + jnp.where(bd_mask, upd[d][u], 0.0)
            st_scr[d] = st
        return carry

    lax.fori_loop(0, nblk // HG_UNROLL, body, 0)

    o = o_scr[0] + o_scr[1] + o_scr[2]
    mean_op = jnp.where(_block_mask(GROUP, GROUP, HEAD_DIM, HEAD_DIM), 1.0 / HEAD_DIM, 0.0)
    ms = _dot(o * o, mean_op, precision=HIGHEST)
    y = o * lax.rsqrt(ms + NORM_EPS) * norm_ref[...]
    o_ref[...] = y * _silu(a_ref[:, 4 * GROUP:5 * GROUP])
    sfin_ref[...] = st_scr[...]


def _layer_slot(nseq, stack, tail):
    zeros = (0,) * len(tail)
    if stack is None:
        return (pl.BlockSpec((None,) + tail, lambda b: (b,) + zeros),
                jax.ShapeDtypeStruct((nseq,) + tail, F32))
    layer, depth = stack
    return (pl.BlockSpec((None, None) + tail, lambda b: (b, layer) + zeros),
            jax.ShapeDtypeStruct((nseq, depth) + tail, F32))


def _hgrn_parts(proj_a, row0, nseq, seq, hgrn_lb, norm_t, s0, layer, stack=None):
    has_state = s0 is not None
    blk0 = row0 // seq
    in_specs = [pl.BlockSpec((seq, PROJ_A), lambda b: (blk0 + b, 0)),
                pl.BlockSpec(hgrn_lb.shape, lambda b: (0, 0, 0)),
                pl.BlockSpec((1, GROUP), lambda b: (0, 0))]
    args = [proj_a, hgrn_lb, norm_t]
    if has_state:
        in_specs.append(pl.BlockSpec((None, 2, GROUP, GROUP), lambda b: (b, 0, 0, 0)))
        args.append(s0)
    out_specs, out_shape = zip(_layer_slot(nseq, None, (seq, GROUP)),
                               _layer_slot(nseq, stack, (2, GROUP, GROUP)))
    out_specs, out_shape = list(out_specs), list(out_shape)
    scratch = [pltpu.VMEM((seq, GROUP), F32),
               pltpu.VMEM((2, seq, GROUP), F32),
               pltpu.VMEM((2, seq, GROUP), F32),
               pltpu.VMEM((2, seq, GROUP), F32),
               pltpu.VMEM((2, seq, GROUP), BF16),
               pltpu.VMEM((2, seq, GROUP), BF16),
               pltpu.VMEM((2, GROUP, GROUP), F32),
               pltpu.VMEM((3, seq, GROUP), F32)]
    return functools.partial(_hgrn_kernel, layer, has_state), in_specs, args, out_specs, out_shape, scratch


def _mla_kernel(latent, *refs):
    if latent:
        (b_ref, qn_ref, wq_ref, kvn_ref, wkv_ref, cckv_ref, ckpe_ref, cos_ref, sin_ref,
         o_ref) = refs
    else:
        (b_ref, qn_ref, wq_ref, kvn_ref, wkv_ref, o_ref, ckv_ref, kpe_ref) = refs
    seq = b_ref.shape[0]
    mckv = b_ref[:, 0:MLA_KV_RANK]
    mcq = b_ref[:, MLA_KV_RANK:MLA_KV_RANK + MLA_Q_RANK]
    kpe_t = b_ref[:, 384:512]

    cq = mcq * lax.rsqrt(jnp.mean(mcq * mcq, axis=-1, keepdims=True) + NORM_EPS) * qn_ref[...]
    qf = _dot(cq.astype(BF16), wq_ref[...])
    ckv = mckv * lax.rsqrt(jnp.mean(mckv * mckv, axis=-1, keepdims=True) + NORM_EPS) * kvn_ref[...]
    q_nope = qf[:, 0:N_HEADS * MLA_NOPE]
    q_rope = qf[:, N_HEADS * MLA_NOPE:]
    if latent:
        cos = cos_ref[...]
        sin = sin_ref[...]
        q_rope = _rope(q_rope, cos, sin)
        ckv_all = jnp.concatenate([cckv_ref[...], ckv], axis=0)
        kpe_all = jnp.concatenate([ckpe_ref[...], _rope(kpe_t, cos, sin)], axis=0)
    else:
        ckv_ref[...] = ckv
        kpe_ref[...] = kpe_t[:, 0:MLA_ROPE]
        ckv_all = ckv
        kpe_all = kpe_t
    kv = _dot(ckv_all.astype(BF16), wkv_ref[...])
    kcat = jnp.concatenate([kv[:, 0:GROUP], kpe_all], axis=1).astype(BF16)
    v = kv[:, GROUP:].astype(BF16)
    qcat = jnp.concatenate([q_nope, q_rope], axis=1)
    scale = (MLA_NOPE + MLA_ROPE) ** -0.5
    qb = min(seq, 256)
    width = qcat.shape[1]
    for r0 in range(0, seq, qb):
        qblk = qcat[r0:r0 + qb]
        acc = jnp.zeros((qb, GROUP), F32)
        for h in range(N_HEADS):
            hm = (_lane_group_mask(width, h * MLA_NOPE, MLA_NOPE)
                  | _lane_group_mask(width, N_HEADS * MLA_NOPE + h * MLA_ROPE, MLA_ROPE))
            s = _dot_nt(jnp.where(hm, qblk, 0.0).astype(BF16), kcat)
            e = jnp.exp2((s - jnp.max(s, axis=-1, keepdims=True)) * (scale * LOG2_E))
            z = jnp.sum(e, axis=-1, keepdims=True)
            oh = _dot(e.astype(BF16), v) / z
            acc = acc + jnp.where(_lane_group_mask(GROUP, h * HEAD_DIM, HEAD_DIM), oh, 0.0)
        o_ref[r0:r0 + qb, :] = acc


def _mla_parts(proj_b, row0, nseq, seq, q_norm, w_qb_p, kv_norm, w_kvb_p, latent_args, stack=None):
    latent = latent_args is not None
    blk0 = row0 // seq
    in_specs = [pl.BlockSpec((seq, PROJ_B), lambda b: (blk0 + b, 0)),
                pl.BlockSpec(q_norm.shape, lambda b: (0, 0)),
                pl.BlockSpec(w_qb_p.shape, lambda b: (0, 0)),
                pl.BlockSpec(kv_norm.shape, lambda b: (0, 0)),
                pl.BlockSpec(w_kvb_p.shape, lambda b: (0, 0))]
    args = [proj_b, q_norm, w_qb_p, kv_norm, w_kvb_p]
    out_specs = [pl.BlockSpec((None, seq, GROUP), lambda b: (b, 0, 0))]
    out_shape = [jax.ShapeDtypeStruct((nseq, seq, GROUP), F32)]
    if latent:
        cckv, ckpe_t, cos, sin = latent_args
        past = cckv.shape[1]
        in_specs += [pl.BlockSpec((None, past, MLA_KV_RANK), lambda b: (b, 0, 0)),
                     pl.BlockSpec((None, past, 128), lambda b: (b, 0, 0)),
                     pl.BlockSpec(cos.shape, lambda b: (0, 0)),
                     pl.BlockSpec(sin.shape, lambda b: (0, 0))]
        args += [cckv, ckpe_t, cos, sin]
    else:
        for width in (MLA_KV_RANK, MLA_ROPE):
            spec, shape = _layer_slot(nseq, stack, (seq, width))
            out_specs.append(spec)
            out_shape.append(shape)
    return functools.partial(_mla_kernel, latent), in_specs, args, out_specs, out_shape, []


def _diff_kernel(latent, lam_init, *refs):
    if latent:
        (c_ref, lam_ref, norm_ref, ck_ref, cv_ref, cos_ref, sin_ref, o_ref) = refs
    else:
        (c_ref, lam_ref, norm_ref, o_ref, k_ref, v_ref) = refs
    seq = c_ref.shape[0]
    dq = c_ref[:, 0:GROUP]
    dk = c_ref[:, GROUP:2 * GROUP]
    dv = c_ref[:, 2 * GROUP:3 * GROUP]
    if latent:
        cos = cos_ref[...]
        sin = sin_ref[...]
        dq = _rope(dq, cos, sin)
        k_all = jnp.concatenate([ck_ref[...], _rope(dk, cos, sin)], axis=0)
        v_all = jnp.concatenate([cv_ref[...], dv], axis=0)
    else:
        k_ref[...] = dk
        v_ref[...] = dv
        k_all = dk
        v_all = dv
    lv = lam_ref[...]
    lam = (jnp.exp(jnp.sum(lv[0:1] * lv[1:2], axis=-1, keepdims=True))
           - jnp.exp(jnp.sum(lv[2:3] * lv[3:4], axis=-1, keepdims=True)) + lam_init)
    k_bf = k_all.astype(BF16)
    v_bf = v_all.astype(BF16)
    scale = DIFF_DIM ** -0.5
    mean_op = jnp.where(_block_mask(GROUP, GROUP, HEAD_DIM, HEAD_DIM), 1.0 / HEAD_DIM, 0.0)
    qb = min(seq, 256)
    for r0 in range(0, seq, qb):
        qblk = dq[r0:r0 + qb]
        acc = jnp.zeros((qb, GROUP), F32)
        for h in range(N_HEADS):
            outs = []
            for comp in range(2):
                cm = _lane_group_mask(GROUP, h * HEAD_DIM + comp * DIFF_DIM, DIFF_DIM)
                s = _dot_nt(jnp.where(cm, qblk, 0.0).astype(BF16), k_bf)
                e = jnp.exp2((s - jnp.max(s, axis=-1, keepdims=True)) * (scale * LOG2_E))
                outs.append(_dot(e.astype(BF16), v_bf) / jnp.sum(e, axis=-1, keepdims=True))
            oh = outs[0] - lam * outs[1]
            acc = acc + jnp.where(_lane_group_mask(GROUP, h * HEAD_DIM, HEAD_DIM), oh, 0.0)
        ms = _dot(acc * acc, mean_op, precision=HIGHEST)
        o_ref[r0:r0 + qb, :] = acc * lax.rsqrt(ms + NORM_EPS) * norm_ref[...] * (1.0 - lam_init)


def _diff_parts(proj_c, row0, nseq, seq, lam_p, norm_t, lam_init, latent_args, stack=None):
    latent = latent_args is not None
    blk0 = row0 // seq
    in_specs = [pl.BlockSpec((seq, PROJ_C), lambda b: (blk0 + b, 0)),
                pl.BlockSpec(lam_p.shape, lambda b: (0, 0)),
                pl.BlockSpec(norm_t.shape, lambda b: (0, 0))]
    args = [proj_c, lam_p, norm_t]
    out_specs = [pl.BlockSpec((None, seq, GROUP), lambda b: (b, 0, 0))]
    out_shape = [jax.ShapeDtypeStruct((nseq, seq, GROUP), F32)]
    if latent:
        ck, cv, cos, sin = latent_args
        past = ck.shape[1]
        in_specs += [pl.BlockSpec((None, past, GROUP), lambda b: (b, 0, 0)),
                     pl.BlockSpec((None, past, GROUP), lambda b: (b, 0, 0)),
                     pl.BlockSpec(cos.shape, lambda b: (0, 0)),
                     pl.BlockSpec(sin.shape, lambda b: (0, 0))]
        args += [ck, cv, cos, sin]
    else:
        for _ in range(2):
            spec, shape = _layer_slot(nseq, stack, (seq, GROUP))
            out_specs.append(spec)
            out_shape.append(shape)
    return functools.partial(_diff_kernel, latent, lam_init), in_specs, args, out_specs, out_shape, []


def _mixers(nseq, name, *parts, carried=None):
    n_in = [len(p[1]) for p in parts]
    n_out = [len(p[3]) for p in parts]
    n_scr = [len(p[5]) for p in parts]
    carried = [None] * sum(n_out) if carried is None else list(carried)
    kept = [(o, arr) for o, arr in enumerate(carried) if arr is not None]

    def body(*refs):
        ins = refs[:sum(n_in)]
        outs = refs[sum(n_in) + len(kept):sum(n_in) + len(kept) + sum(n_out)]
        scr = refs[sum(n_in) + len(kept) + sum(n_out):]
        i0 = o0 = s0 = 0
        for part, ni, no, ns in zip(parts, n_in, n_out, n_scr):
            part[0](*ins[i0:i0 + ni], *outs[o0:o0 + no], *scr[s0:s0 + ns])
            i0, o0, s0 = i0 + ni, o0 + no, s0 + ns

    res = pl.pallas_call(
        body,
        grid=(nseq,),
        in_specs=[x for p in parts for x in p[1]] + [pl.BlockSpec(memory_space=pl.ANY)] * len(kept),
        out_specs=[x for p in parts for x in p[3]],
        out_shape=[x for p in parts for x in p[4]],
        scratch_shapes=[x for p in parts for x in p[5]],
        input_output_aliases={sum(n_in) + k: o for k, (o, _) in enumerate(kept)},
        compiler_params=_cparams("arbitrary"),
        name=name,
    )(*[x for p in parts for x in p[2]], *[arr for _, arr in kept])
    out, o0 = [], 0
    for no in n_out:
        out.append(tuple(res[o0:o0 + no]))
        o0 += no
    return out


def _ssd_kernel(has_state, *refs):
    if has_state:
        (d_ref, cw_ref, cb_ref, dtb_ref, alog_ref, dskip_ref, norm_ref, s0_ref,
         o_ref, sfin_ref, xs_scr, bm_scr, cm_scr, xdt_scr, a_scr, st_scr, yf_scr, yb_scr) = refs
    else:
        (d_ref, cw_ref, cb_ref, dtb_ref, alog_ref, dskip_ref, norm_ref,
         o_ref, sfin_ref, xs_scr, bm_scr, cm_scr, xdt_scr, a_scr, st_scr, yf_scr, yb_scr) = refs
    seq = d_ref.shape[0]
    c = SSD_CHUNK
    nchunk = seq // c
    ngrp = 2 * SSD_STATE

    xin = d_ref[:, GROUP:GROUP + 512]
    rows = _iota(xin.shape, 0)
    prev = jnp.where(rows == 0, 0.0, pltpu.roll(xin, 1, 0))
    nxt = jnp.where(rows == seq - 1, 0.0, pltpu.roll(xin, seq - 1, 0))
    cw = cw_ref[...]
    xbc = _silu(cw[0:1] * prev + cw[1:2] * xin + cw[2:3] * nxt + cb_ref[...])
    xs = xbc[:, 0:GROUP]
    xs_scr[...] = xs
    bm_scr[...] = xbc[:, GROUP:GROUP + ngrp]
    cm_scr[...] = xbc[:, GROUP + ngrp:GROUP + 2 * ngrp]
    dt = _softplus(d_ref[:, GROUP + 512:GROUP + 640] + dtb_ref[...])
    a_scr[...] = dt * (-jnp.exp(alog_ref[...]))
    erow = _iota((128, GROUP), 0)
    ehead = _iota((128, GROUP), 1) // HEAD_DIM
    expand = tuple((erow == 4 * d + ehead).astype(F32) for d in range(2))
    for d in range(2):
        xdt_scr[d] = xs * _dot(dt, expand[d], precision=HIGHEST)
    if has_state:
        st_scr[...] = s0_ref[...]
    else:
        st_scr[...] = jnp.zeros_like(st_scr)

    row = _iota((c, c), 0)
    col = _iota((c, c), 1)
    tri = ((col <= row).astype(F32), (col >= row).astype(F32))
    keep = (col <= row, col >= row)
    grp_lane = _iota((1, ngrp), 1) // SSD_STATE
    valid = (_iota((ngrp, GROUP), 0) // SSD_STATE) == (_iota((ngrp, GROUP), 1) // (2 * HEAD_DIM))

    def chunk_step(d, r0, out_scr):
        a_c = a_scr[pl.ds(r0, c), :]
        bm_c = bm_scr[pl.ds(r0, c), :]
        cm_c = cm_scr[pl.ds(r0, c), :].astype(BF16)
        xdt_c = xdt_scr[d, pl.ds(r0, c), :]
        acum = _dot(tri[d], a_c, precision=HIGHEST)
        acum_t = acum.T
        acum_rep = _dot(acum, expand[d], precision=HIGHEST)
        bm2 = jnp.concatenate([jnp.where(grp_lane == g, bm_c, 0.0) for g in range(2)], axis=0)
        cb = _dot_nt(cm_c, bm2.astype(BF16))
        scores = []
        xparts = []
        for h in range(N_HEADS):
            lane = 4 * d + h
            seg = jnp.exp(jnp.where(keep[d], acum[:, lane:lane + 1] - acum_t[lane:lane + 1, :], -jnp.inf))
            g = h // 2
            scores.append((cb[:, g * c:(g + 1) * c] * seg).astype(BF16))
            xparts.append(jnp.where(_lane_group_mask(GROUP, h * HEAD_DIM, HEAD_DIM), xdt_c, 0.0))
        y = _dot(jnp.concatenate(scores, axis=1), jnp.concatenate(xparts, axis=0).astype(BF16))
        st = st_scr[d]
        y = y + _dot(cm_c, st.astype(BF16)) * jnp.exp(acum_rep)
        out_scr[pl.ds(r0, c), :] = y
        edge = acum_rep[c - 1:c] if d == 0 else acum_rep[0:1]
        xt = (xdt_c * jnp.exp(edge - acum_rep)).astype(BF16)
        upd = _dot_tn(bm_c.astype(BF16), xt)
        st_scr[d] = st * jnp.exp(edge) + jnp.where(valid, upd, 0.0)

    def body(n, carry):
        chunk_step(0, pl.multiple_of(n * c, c), yf_scr)
        chunk_step(1, pl.multiple_of((nchunk - 1 - n) * c, c), yb_scr)
        return carry

    lax.fori_loop(0, nchunk, body, 0)

    y = yf_scr[...] + yb_scr[...] + dskip_ref[...] * xs_scr[...]
    y = y * _silu(d_ref[:, 0:GROUP])
    o_ref[...] = y * lax.rsqrt(jnp.mean(y * y, axis=-1, keepdims=True) + NORM_EPS) * norm_ref[...]
    sfin_ref[...] = st_scr[...]


def _ssd_parts(proj_d, row0, nseq, seq, conv_w, conv_b, dt_bias_p, a_log_p, d_rep, norm, s0, stack=None):
    has_state = s0 is not None
    blk0 = row0 // seq
    ngrp = 2 * SSD_STATE
    small = [conv_w, conv_b, dt_bias_p, a_log_p, d_rep, norm]
    in_specs = ([pl.BlockSpec((seq, PROJ_D), lambda b: (blk0 + b, 0))]
                + [pl.BlockSpec(s.shape, lambda b: (0, 0)) for s in small])
    args = [proj_d] + small
    if has_state:
        in_specs.append(pl.BlockSpec((None, 2, ngrp, GROUP), lambda b: (b, 0, 0, 0)))
        args.append(s0)
    out_specs, out_shape = zip(_layer_slot(nseq, None, (seq, GROUP)),
                               _layer_slot(nseq, stack, (2, ngrp, GROUP)))
    out_specs, out_shape = list(out_specs), list(out_shape)
    scratch = [pltpu.VMEM((seq, GROUP), F32),
               pltpu.VMEM((seq, ngrp), F32),
               pltpu.VMEM((seq, ngrp), F32),
               pltpu.VMEM((2, seq, GROUP), F32),
               pltpu.VMEM((seq, 128), F32),
               pltpu.VMEM((2, ngrp, GROUP), F32),
               pltpu.VMEM((seq, GROUP), F32),
               pltpu.VMEM((seq, GROUP), F32)]
    return functools.partial(_ssd_kernel, has_state), in_specs, args, out_specs, out_shape, scratch


def _outproj_kernel(n_ctx_tiles, *refs):
    ctx_refs, lat_refs = refs[0:4], refs[4:8]
    w_ref, x_ref, m_ref, g_ref, b_ref, o_ref = refs[8:]
    is_ctx = pl.program_id(0) < n_ctx_tiles
    mixed = None
    for i, (c_ref, l_ref) in enumerate(zip(ctx_refs, lat_refs)):
        part = jnp.where(is_ctx, c_ref[...], l_ref[...]).astype(BF16)
        term = _dot(part, w_ref[i * GROUP:(i + 1) * GROUP, :])
        mixed = term if mixed is None else mixed + term
    o_ref[...] = _layernorm(ALPHA * x_ref[...] + m_ref[2] * mixed, g_ref[...], b_ref[...])


def _outproj(parts_ctx, parts_lat, w_out_bf, x, mods_l, ln_g, ln_b, t_ctx, s_lat):
    t, d = x.shape
    tm = 512
    n_ctx = t_ctx // tm
    n_lat = (t - t_ctx) // tm
    ctx_spec = pl.BlockSpec((tm, GROUP), lambda i: (jnp.minimum(i, n_ctx - 1), 0))
    lat_spec = pl.BlockSpec((tm, GROUP), lambda i: (jnp.clip(i - n_ctx, 0, n_lat - 1), 0))
    return pl.pallas_call(
        functools.partial(_outproj_kernel, n_ctx),
        grid=(t // tm,),
        in_specs=[ctx_spec] * 4 + [lat_spec] * 4
        + [pl.BlockSpec(w_out_bf.shape, lambda i: (0, 0)),
           pl.BlockSpec((tm, d), lambda i: (i, 0)),
           pl.BlockSpec((None, 6, 1, d), _mod_row_map(tm, t_ctx, s_lat)),
           pl.BlockSpec((1, d), lambda i: (0, 0)),
           pl.BlockSpec((1, d), lambda i: (0, 0))],
        out_specs=pl.BlockSpec((tm, d), lambda i: (i, 0)),
        out_shape=jax.ShapeDtypeStruct((t, d), F32),
        compiler_params=_cparams("arbitrary"),
        name="outproj_ln",
    )(*parts_ctx, *parts_lat, w_out_bf, x, mods_l, ln_g, ln_b)


def _top_rows(s, k, extra=()):
    r = s.shape[0]
    rid = _iota(s.shape, 0).astype(F32)
    vals, ids = [], []
    picked = [[] for _ in extra]
    for _ in range(k):
        m = jnp.max(s, axis=0, keepdims=True)
        cand = jnp.where(s == m, rid, float(r))
        i = jnp.min(cand, axis=0, keepdims=True)
        hit = cand == i
        vals.append(m)
        ids.append(i)
        for lst, arr in zip(picked, extra):
            lst.append(jnp.max(jnp.where(hit, arr, -1.0), axis=0, keepdims=True))
        s = jnp.where(hit, -jnp.inf, s)
    cat = lambda xs: jnp.concatenate(xs, axis=0)
    return cat(vals), cat(ids), [cat(p) for p in picked]


def _router_kernel(x_ref, m_ref, wq_ref, keys_ref, h_ref, a_ref, b_ref, g_ref):
    tm = x_ref.shape[0]
    hb = (x_ref[...] * (1.0 + m_ref[4]) + m_ref[3]).astype(BF16)
    h_ref[...] = hb
    qt = _dot(hb, wq_ref[...]).T.astype(BF16)
    k = PEER_TOPK
    code_rows, g_rows = [], []
    for head in range(PEER_HEADS):
        tv, ti = [], []
        for half in range(2):
            g = 2 * head + half
            sc = _dot(keys_ref[g], qt[g * PEER_HALF:(g + 1) * PEER_HALF])
            v, i, _ = _top_rows(sc, k)
            tv.append(v)
            ti.append(i)
        cs = [tv[0][0:1] + tv[1]]
        ca = [jnp.broadcast_to(ti[0][0:1], (k, tm))]
        cb = [ti[1]]
        for k1 in range(1, 4):
            cs.append(tv[0][k1:k1 + 1] + tv[1][0:8])
            ca.append(jnp.broadcast_to(ti[0][k1:k1 + 1], (8, tm)))
            cb.append(ti[1][0:8])
        low = _iota((8, tm), 0) < 4
        v2_dup = jnp.where(low, tv[1][0:8], pltpu.roll(tv[1][0:8], 4, 0))
        i2_dup = jnp.where(low, ti[1][0:8], pltpu.roll(ti[1][0:8], 4, 0))
        for k1 in (4, 6):
            cs.append(jnp.where(low, tv[0][k1:k1 + 1], tv[0][k1 + 1:k1 + 2]) + v2_dup)
            ca.append(jnp.where(low, ti[0][k1:k1 + 1], ti[0][k1 + 1:k1 + 2]))
            cb.append(i2_dup)
        cs.append(tv[0][8:16] + tv[1][0:1])
        ca.append(ti[0][8:16])
        cb.append(jnp.broadcast_to(ti[1][0:1], (8, tm)))
        code = jnp.concatenate(ca, axis=0) * float(PEER_KEYS) + jnp.concatenate(cb, axis=0)
        best, _, (sel_code,) = _top_rows(jnp.concatenate(cs, axis=0), k, extra=(code,))
        e = jnp.exp(best - best[0:1])
        g_rows.append(e / jnp.sum(e, axis=0, keepdims=True))
        code_rows.append(sel_code)
    codes = jnp.concatenate(code_rows, axis=0)
    key1 = jnp.floor(codes * (1.0 / PEER_KEYS))
    a_ref[...] = key1.T.astype(I32)
    b_ref[...] = (codes - key1 * float(PEER_KEYS)).T.astype(I32)
    g_ref[...] = jnp.concatenate(g_rows, axis=0).T


def _router(x, mods_l, wq_bf, keys, t_ctx, s_lat):
    t, d = x.shape
    tm = 256
    nslot = PEER_HEADS * PEER_TOPK
    return pl.pallas_call(
        _router_kernel,
        grid=(t // tm,),
        in_specs=[pl.BlockSpec((tm, d), lambda i: (i, 0)),
                  pl.BlockSpec((None, 6, 1, d), _mod_row_map(tm, t_ctx, s_lat)),
                  pl.BlockSpec(wq_bf.shape, lambda i: (0, 0)),
                  pl.BlockSpec(keys.shape, lambda i: (0, 0, 0))],
        out_specs=[pl.BlockSpec((tm, d), lambda i: (i, 0)),
                   pl.BlockSpec((tm, nslot), lambda i: (i, 0)),
                   pl.BlockSpec((tm, nslot), lambda i: (i, 0)),
                   pl.BlockSpec((tm, nslot), lambda i: (i, 0))],
        out_shape=[jax.ShapeDtypeStruct((t, d), BF16),
                   jax.ShapeDtypeStruct((t, nslot), I32),
                   jax.ShapeDtypeStruct((t, nslot), I32),
                   jax.ShapeDtypeStruct((t, nslot), F32)],
        compiler_params=_cparams("arbitrary"),
        name="peer_router",
    )(x, mods_l, wq_bf, keys)


def _gates_kernel(a_ref, b_ref, g_ref, u_ref, v_ref, o_ref, ub_ref, vb_ref):
    tm = a_ref.shape[0]
    n = PEER_KEYS
    sub = 16
    ub_ref[...] = u_ref[...].astype(BF16)
    vb_ref[...] = v_ref[...].astype(BF16)
    key = _iota((sub, n, a_ref.shape[2]), 1).astype(F32).astype(BF16)
    zero = jnp.zeros((), BF16)
    for t0 in range(0, tm, sub):
        a = a_ref[t0:t0 + sub].astype(F32).astype(BF16)
        b = b_ref[t0:t0 + sub].astype(F32).astype(BF16)
        g = g_ref[t0:t0 + sub].astype(BF16)
        onehot_a = jnp.where(key == a, jnp.ones((), BF16), zero)
        gated_b = jnp.where(key == b, g, zero)
        w = lax.dot_general(onehot_a, gated_b, (((2,), (2,)), ((0,), (0,))),
                            preferred_element_type=F32)
        w_t = jnp.swapaxes(w.astype(BF16), 0, 1)
        for r in range(n):
            o_ref[t0:t0 + sub, r * n:(r + 1) * n] = w_t[r]


def _gates(a_idx, b_idx, gate, peer_u, peer_v, layer):
    t, nslot = a_idx.shape
    d = peer_u.shape[-1]
    tm = 192
    steps = t // tm
    te = N_EXPERTS // steps
    spec = pl.BlockSpec((tm, 1, nslot), lambda i: (i, 0, 0))
    tab_in = pl.BlockSpec((None, te, d), lambda i: (layer, i, 0))
    tab_out = pl.BlockSpec((te, d), lambda i: (i, 0))
    return pl.pallas_call(
        _gates_kernel,
        grid=(steps,),
        in_specs=[spec, spec, spec, tab_in, tab_in],
        out_specs=[pl.BlockSpec((tm, N_EXPERTS), lambda i: (i, 0)), tab_out, tab_out],
        out_shape=[jax.ShapeDtypeStruct((t, N_EXPERTS), BF16),
                   jax.ShapeDtypeStruct((N_EXPERTS, d), BF16),
                   jax.ShapeDtypeStruct((N_EXPERTS, d), BF16)],
        compiler_params=_cparams("arbitrary"),
        name="peer_gates",
    )(a_idx.reshape(t, 1, nslot), b_idx.reshape(t, 1, nslot), gate.reshape(t, 1, nslot), peer_u, peer_v)


def _experts_kernel(h_ref, u_ref, v_ref, w_ref, x_ref, m_ref, g_ref, b_ref, o_ref, acc_ref):
    j = pl.program_id(1)

    @pl.when(j == 0)
    def _():
        acc_ref[...] = jnp.zeros_like(acc_ref)

    act = _gelu_tanh(_dot_nt(h_ref[...], u_ref[...]))
    acc_ref[...] += _dot((act * w_ref[...].astype(F32)).astype(BF16), v_ref[...])

    @pl.when(j == pl.num_programs(1) - 1)
    def _():
        o_ref[...] = _layernorm(ALPHA * x_ref[...] + m_ref[5] * acc_ref[...], g_ref[...], b_ref[...])


def _experts(h_bf, u_bf, v_bf, w_gate, x, mods_l, ln_g, ln_b, t_ctx, s_lat):
    t, d = x.shape
    tm, te = 1024, 1024
    return pl.pallas_call(
        _experts_kernel,
        grid=(t // tm, N_EXPERTS // te),
        in_specs=[pl.BlockSpec((tm, d), lambda i, j: (i, 0)),
                  pl.BlockSpec((te, d), lambda i, j: (j, 0)),
                  pl.BlockSpec((te, d), lambda i, j: (j, 0)),
                  pl.BlockSpec((tm, te), lambda i, j: (i, j)),
                  pl.BlockSpec((tm, d), lambda i, j: (i, 0)),
                  pl.BlockSpec((None, 6, 1, d), _mod_row_map(tm, t_ctx, s_lat)),
                  pl.BlockSpec((1, d), lambda i, j: (0, 0)),
                  pl.BlockSpec((1, d), lambda i, j: (0, 0))],
        out_specs=pl.BlockSpec((tm, d), lambda i, j: (i, 0)),
        out_shape=jax.ShapeDtypeStruct((t, d), F32),
        scratch_shapes=[pltpu.VMEM((tm, d), F32)],
        compiler_params=_cparams("arbitrary", "arbitrary"),
        name="peer_experts",
    )(h_bf, u_bf, v_bf, w_gate, x, mods_l, ln_g, ln_b)


def _pad_cols(w, n):
    return jnp.concatenate([w, jnp.zeros((w.shape[0], n), w.dtype)], axis=1) if n else w


def _layout_w_in(w):
    a = w[:, 0:1280]
    mcq, mckv, mkpe = w[:, 1280:1472], w[:, 1472:1600], w[:, 1600:1632]
    b = jnp.concatenate([mckv, _pad_cols(mcq, 64), mkpe, mkpe, mkpe, mkpe], axis=1)
    c = w[:, 1632:2400]
    d = _pad_cols(w[:, 2400:3176], PROJ_D - 776)
    return jnp.concatenate([a, b, c, d], axis=1).astype(BF16)


def _layout_w_qb(w):
    w4 = w.reshape(MLA_Q_RANK, N_HEADS, MLA_NOPE + MLA_ROPE)
    return jnp.concatenate([w4[:, :, :MLA_NOPE].reshape(MLA_Q_RANK, -1),
                            w4[:, :, MLA_NOPE:].reshape(MLA_Q_RANK, -1)], axis=1).astype(BF16)


def _layout_w_kvb(w):
    w4 = w.reshape(MLA_KV_RANK, N_HEADS, MLA_NOPE + HEAD_DIM)
    return jnp.concatenate([w4[:, :, :MLA_NOPE].reshape(MLA_KV_RANK, -1),
                            w4[:, :, MLA_NOPE:].reshape(MLA_KV_RANK, -1)], axis=1).astype(BF16)


def _rope_tables(seq):
    rows = seq // GRID_W
    row = jnp.repeat(jnp.arange(rows, dtype=F32), GRID_W)
    col = jnp.tile(jnp.arange(GRID_W, dtype=F32), rows)
    freqs = ROPE_BASE ** (-jnp.arange(ROPE_PAIRS, dtype=F32) / ROPE_PAIRS)
    cos_l, sin_l = [], []
    for pos in (row, col):
        ang = pos[:, None] * freqs
        cos_l += [jnp.cos(ang), jnp.cos(ang)]
        sin_l += [-jnp.sin(ang), jnp.sin(ang)]
    return jnp.concatenate(cos_l, axis=1), jnp.concatenate(sin_l, axis=1)


def _hgrn_state_pack(st):
    b = st.shape[0]
    st_t = jnp.swapaxes(st, -1, -2)
    zero = jnp.zeros_like(st_t[:, :, 0])
    rows = [jnp.concatenate([st_t[:, :, h] if g == h else zero for g in range(N_HEADS)], axis=-1)
            for h in range(N_HEADS)]
    return jnp.concatenate(rows, axis=-2).reshape(b, 2, GROUP, GROUP)


def _hgrn_state_unpack(sb):
    blocks = [sb[..., h * HEAD_DIM:(h + 1) * HEAD_DIM, h * HEAD_DIM:(h + 1) * HEAD_DIM]
              for h in range(N_HEADS)]
    return jnp.swapaxes(jnp.stack(blocks, axis=-3), -1, -2)


def _ssd_state_pack(st):
    st_t = jnp.swapaxes(st, -1, -2)
    zero = jnp.zeros_like(st_t[:, :, 0])
    rows = [jnp.concatenate([st_t[:, :, h] if h // 2 == g else zero for h in range(N_HEADS)], axis=-1)
            for g in range(2)]
    return jnp.concatenate(rows, axis=-2)


def _ssd_state_unpack(sb):
    blocks = [sb[..., (h // 2) * SSD_STATE:(h // 2 + 1) * SSD_STATE, h * HEAD_DIM:(h + 1) * HEAD_DIM]
              for h in range(N_HEADS)]
    return jnp.swapaxes(jnp.stack(blocks, axis=-3), -1, -2)


def _tile_lanes(v, n):
    return jnp.tile(v.reshape(1, -1), (1, n))


def kernel(x_prompt, x_sample, cache_mla_ckv, cache_mla_kpe, cache_diff_k, cache_diff_v, state_hgrn, state_ssd, c, c_ctx, w_mod, b_mod, w_in, hgrn_lb, hgrn_norm, mla_q_norm, mla_w_qb, mla_kv_norm, mla_w_kvb, diff_lambda, diff_norm, ssd_conv_w, ssd_conv_b, ssd_dt_bias, ssd_a_log, ssd_d, ssd_norm, w_out, ln1_g, ln1_b, peer_wq, peer_keys, peer_u, peer_v, ln2_g, ln2_b):
    nb, seq, d = x_prompt.shape
    nlat, lseq, _ = x_sample.shape
    depth = w_in.shape[0]
    t_ctx = nb * seq
    x = jnp.concatenate([x_prompt.reshape(t_ctx, d), x_sample.reshape(nlat * lseq, d)], axis=0)

    cond8 = jnp.concatenate([c_ctx.reshape(1, d), c, jnp.zeros((8 - 1 - nlat, d), F32)], axis=0)
    mods = _mods(cond8, w_mod, b_mod)
    mods = mods[:, :1 + nlat].reshape(depth, 1 + nlat, 6, 1, d)

    cos32, sin32 = _rope_tables(lseq)
    cos128, sin128 = jnp.tile(cos32, (1, 4)), jnp.tile(sin32, (1, 4))
    cos256, sin256 = jnp.tile(cos32, (1, 8)), jnp.tile(sin32, (1, 8))

    carried = None
    for l in range(depth):
        mods_l = mods[l]
        pa, pb, pc, pd = _inproj(x, mods_l, _layout_w_in(w_in[l]), t_ctx, lseq)

        norm_hg = _tile_lanes(hgrn_norm[l], N_HEADS)
        mla_w = (mla_q_norm[l].reshape(1, -1), _layout_w_qb(mla_w_qb[l]),
                 mla_kv_norm[l].reshape(1, -1), _layout_w_kvb(mla_w_kvb[l]))
        lam_init = 0.8 - 0.6 * math.exp(-0.3 * l)
        norm_df = _tile_lanes(diff_norm[l], N_HEADS)
        past = cache_diff_k.shape[2]
        ssd_w = (ssd_conv_w[l], ssd_conv_b[l].reshape(1, -1),
                 _pad_cols(ssd_dt_bias[l].reshape(1, -1), 120), _pad_cols(ssd_a_log[l].reshape(1, -1), 120),
                 jnp.repeat(ssd_d[l], HEAD_DIM).reshape(1, -1), ssd_norm[l].reshape(1, -1))

        stack = (l, depth)
        ctx_out = _mixers(
            nb, "mixers_ctx",
            _mla_parts(pb, 0, nb, seq, *mla_w, None, stack),
            _diff_parts(pc, 0, nb, seq, diff_lambda[l], norm_df, lam_init, None, stack),
            _ssd_parts(pd, 0, nb, seq, *ssd_w, None, stack),
            _hgrn_parts(pa, 0, nb, seq, hgrn_lb, norm_hg, None, l, stack),
            carried=carried)
        (mla_ctx, new_ckv, new_kpe), (df_ctx, new_dk, new_dv), (ssd_ctx, ssd_fin), (hg_ctx, hg_fin) = ctx_out
        carried = [None, new_ckv, new_kpe, None, new_dk, new_dv, None, ssd_fin, None, hg_fin]
        ((mla_lat,),) = _mixers(
            nlat, "mla_lat",
            _mla_parts(pb, t_ctx, nlat, lseq, *mla_w,
                       (cache_mla_ckv[:, l], jnp.tile(cache_mla_kpe[:, l], (1, 1, 4)), cos128, sin128)))
        ((df_lat,),) = _mixers(
            nlat, "diffattn_lat",
            _diff_parts(pc, t_ctx, nlat, lseq, diff_lambda[l], norm_df, lam_init,
                        (cache_diff_k[:, l].reshape(nlat, past, GROUP),
                         cache_diff_v[:, l].reshape(nlat, past, GROUP), cos256, sin256)))
        ((ssd_lat, _),) = _mixers(nlat, "ssd_lat",
                                  _ssd_parts(pd, t_ctx, nlat, lseq, *ssd_w, _ssd_state_pack(state_ssd[:, l])))
        ((hg_lat, _),) = _mixers(nlat, "hgrn_lat",
                                 _hgrn_parts(pa, t_ctx, nlat, lseq, hgrn_lb, norm_hg,
                                             _hgrn_state_pack(state_hgrn[:, l]), l))

        parts_ctx = [a.reshape(t_ctx, GROUP) for a in (hg_ctx, mla_ctx, df_ctx, ssd_ctx)]
        parts_lat = [a.reshape(nlat * lseq, GROUP) for a in (hg_lat, mla_lat, df_lat, ssd_lat)]
        x = _outproj(parts_ctx, parts_lat, w_out[l].astype(BF16), x, mods_l,
                     ln1_g[l].reshape(1, d), ln1_b[l].reshape(1, d), t_ctx, lseq)

        keys = peer_keys[l].reshape(2 * PEER_HEADS, PEER_KEYS, PEER_HALF).astype(BF16)
        h_bf, a_idx, b_idx, gate = _router(x, mods_l, peer_wq[l].astype(BF16), keys, t_ctx, lseq)
        w_gate, u_bf, v_bf = _gates(a_idx, b_idx, gate, peer_u, peer_v, l)
        x = _experts(h_bf, u_bf, v_bf, w_gate, x, mods_l,
                     ln2_g[l].reshape(1, d), ln2_b[l].reshape(1, d), t_ctx, lseq)

    y_prompt = x[:t_ctx].reshape(nb, seq, d)
    y_sample = x[t_ctx:].reshape(nlat, lseq, d)
    return (y_prompt, y_sample, new_ckv, new_kpe,
            new_dk.reshape(nb, depth, seq, N_HEADS, 2, DIFF_DIM),
            new_dv.reshape(nb, depth, seq, N_HEADS, 2 * DIFF_DIM),
            _hgrn_state_unpack(hg_fin), _ssd_state_unpack(ssd_fin))
```

```python
import functools
import math

import jax
import jax.numpy as jnp
from jax import lax
from jax.experimental import pallas as pl
from jax.experimental.pallas import tpu as pltpu

F32 = jnp.float32
BF16 = jnp.bfloat16
I32 = jnp.int32

D_MODEL = 1024
GROUP = 256
N_HEADS = 4
HEAD_DIM = 64
HG_BLOCK = 16
HG_SLAB = 256
HG_UNROLL = 8
SSD_CHUNK = 128
SSD_STATE = 64
MLA_Q_RANK = 192
MLA_KV_RANK = 128
MLA_NOPE = 64
MLA_ROPE = 32
DIFF_DIM = 32
GRID_W = 64
ROPE_PAIRS = 8
ROPE_BASE = 10000.0
PEER_HEADS = 8
PEER_KEYS = 128
PEER_TOPK = 16
PEER_HALF = 64
N_EXPERTS = PEER_KEYS * PEER_KEYS
NORM_EPS = 1e-6
LN_EPS = 1e-5
DEPTH = 2
ALPHA = (2.0 * DEPTH) ** 0.25
LOG2_E = 1.4426950408889634

PROJ_A = 5 * GROUP
PROJ_B = 512
PROJ_C = 3 * GROUP
PROJ_D = 896
VMEM_LIMIT = 56 * 1024 * 1024


def _cparams(*sem):
    return pltpu.CompilerParams(dimension_semantics=sem, vmem_limit_bytes=VMEM_LIMIT)


def _sigmoid(x):
    return 1.0 / (1.0 + jnp.exp(-x))


def _silu(x):
    return x * _sigmoid(x)


def _softplus(x):
    return jnp.maximum(x, 0.0) + jnp.log(1.0 + jnp.exp(-jnp.abs(x)))


def _gelu_tanh(x):
    return 0.5 * x * (1.0 + jnp.tanh(math.sqrt(2.0 / math.pi) * (x + 0.044715 * (x * x * x))))


def _dot(a, b, precision=None):
    return jnp.dot(a, b, preferred_element_type=F32, precision=precision)


def _dot_nt(a, b, precision=None):
    return lax.dot_general(a, b, (((1,), (1,)), ((), ())), preferred_element_type=F32,
                           precision=precision)


def _dot_tn(a, b, precision=None):
    return lax.dot_general(a, b, (((0,), (0,)), ((), ())), preferred_element_type=F32,
                           precision=precision)


def _split3(x):
    hi = x.astype(BF16)
    rest = x - hi.astype(F32)
    mid = rest.astype(BF16)
    lo = (rest - mid.astype(F32)).astype(BF16)
    return hi, mid, lo


def _dot_sel(sel, x):
    sel = sel.astype(BF16)
    hi, mid, lo = _split3(x)
    return _dot(sel, hi) + _dot(sel, mid) + _dot(sel, lo)


def _dot_sel_r(x, sel):
    sel = sel.astype(BF16)
    hi, mid, lo = _split3(x)
    return _dot(hi, sel) + _dot(mid, sel) + _dot(lo, sel)


def _iota(shape, dim):
    return lax.broadcasted_iota(I32, shape, dim)


def _block_mask(rows, cols, rblk, cblk):
    return (_iota((rows, cols), 0) // rblk) == (_iota((rows, cols), 1) // cblk)


def _lane_group_mask(width, start, size):
    lane = _iota((1, width), 1)
    return (lane >= start) & (lane < start + size)


def _layernorm(v, g, b):
    mu = jnp.mean(v, axis=-1, keepdims=True)
    d = v - mu
    var = jnp.mean(d * d, axis=-1, keepdims=True)
    return d * lax.rsqrt(var + LN_EPS) * g + b


def _swap_halves16(x):
    width = x.shape[-1]
    lane = _iota(x.shape, x.ndim - 1)
    up = pltpu.roll(x, width - 8, x.ndim - 1)
    down = pltpu.roll(x, 8, x.ndim - 1)
    return jnp.where((lane % 16) < 8, up, down)


def _rope(x, cos, sin_signed):
    return x * cos + _swap_halves16(x) * sin_signed


def _mods_kernel(c_ref, w_ref, b_ref, o_ref):
    s = _silu(c_ref[...]).astype(BF16)
    o_ref[...] = _dot(s, w_ref[...].astype(BF16)) + b_ref[...]


def _mods(cond8, w_mod, b_mod):
    depth, d, n = w_mod.shape
    tn = 1536
    return pl.pallas_call(
        _mods_kernel,
        grid=(depth, n // tn),
        in_specs=[pl.BlockSpec((8, d), lambda l, j: (0, 0)),
                  pl.BlockSpec((None, d, tn), lambda l, j: (l, 0, j)),
                  pl.BlockSpec((None, 1, tn), lambda l, j: (l, 0, j))],
        out_specs=pl.BlockSpec((None, 8, tn), lambda l, j: (l, 0, j)),
        out_shape=jax.ShapeDtypeStruct((depth, 8, n), F32),
        compiler_params=_cparams("arbitrary", "arbitrary"),
        name="mods",
    )(cond8, w_mod, b_mod.reshape(depth, 1, n))


def _mod_row_map(tm, t_ctx, s_lat):
    def index_map(i, *_):
        start = i * tm
        return (jnp.where(start < t_ctx, 0, 1 + (start - t_ctx) // s_lat), 0, 0, 0)
    return index_map


def _inproj_kernel(x_ref, m_ref, w_ref, oa_ref, ob_ref, oc_ref, od_ref):
    h = (x_ref[...] * (1.0 + m_ref[1]) + m_ref[0]).astype(BF16)
    start = 0
    for o_ref in (oa_ref, ob_ref, oc_ref, od_ref):
        width = o_ref.shape[-1]
        o_ref[...] = _dot(h, w_ref[:, start:start + width])
        start += width


def _inproj(x, mods_l, w_in_p, t_ctx, s_lat):
    t, d = x.shape
    tm = 512
    widths = (PROJ_A, PROJ_B, PROJ_C, PROJ_D)
    return pl.pallas_call(
        _inproj_kernel,
        grid=(t // tm,),
        in_specs=[pl.BlockSpec((tm, d), lambda i: (i, 0)),
                  pl.BlockSpec((None, 6, 1, d), _mod_row_map(tm, t_ctx, s_lat)),
                  pl.BlockSpec(w_in_p.shape, lambda i: (0, 0))],
        out_specs=[pl.BlockSpec((tm, w), lambda i: (i, 0)) for w in widths],
        out_shape=[jax.ShapeDtypeStruct((t, w), F32) for w in widths],
        compiler_params=_cparams("arbitrary"),
        name="inproj",
    )(x, mods_l, w_in_p)


def _hgrn_kernel(layer, has_state, *refs):
    if has_state:
        (a_ref, lb_ref, norm_ref, s0_ref, o_ref, sfin_ref,
         q_scr, k_scr, bc_scr, dec_scr, qt_scr, kt_scr, st_scr, o_scr) = refs
    else:
        (a_ref, lb_ref, norm_ref, o_ref, sfin_ref,
         q_scr, k_scr, bc_scr, dec_scr, qt_scr, kt_scr, st_scr, o_scr) = refs
        s0_ref = None
    seq = a_ref.shape[0]
    c = HG_BLOCK
    nblk = seq // c
    slab = HG_SLAB
    nb = slab // c

    lbp = lb_ref[...]
    e = jnp.exp(lbp - jnp.max(lbp, axis=0, keepdims=True))
    p = e / jnp.sum(e, axis=0, keepdims=True)
    lower = jnp.sum(p[1:layer + 1], axis=0) if layer > 0 else jnp.zeros_like(p[0])

    q = _silu(a_ref[:, 0:GROUP])
    q_scr[...] = q
    srow = _iota((slab, slab), 0)
    scol = _iota((slab, slab), 1)
    same = (srow // c) == (scol // c)
    cum_op = (jnp.where(same & (scol <= srow), 1.0, 0.0), jnp.where(same & (scol >= srow), 1.0, 0.0))
    for d in range(2):
        lb = lower[d:d + 1]
        f = lb + (1.0 - lb) * _sigmoid(a_ref[:, (1 + d) * GROUP:(2 + d) * GROUP])
        k = 1.0 - f
        lf = jnp.log(f)
        k_scr[d] = k
        for s0 in range(0, seq, slab):
            bc = _dot_sel(cum_op[d], lf[s0:s0 + slab])
            bc3 = bc.reshape(nb, c, GROUP)
            edge = bc3[:, c - 1:c, :] if d == 0 else bc3[:, 0:1, :]
            tot = jnp.broadcast_to(edge, (nb, c, GROUP)).reshape(slab, GROUP)
            bc_scr[d, s0:s0 + slab, :] = bc * LOG2_E
            dec_scr[d, s0:s0 + slab, :] = jnp.exp(tot)
            qt_scr[d, s0:s0 + slab, :] = (q[s0:s0 + slab] * jnp.exp(bc)).astype(BF16)
            kt_scr[d, s0:s0 + slab, :] = (k[s0:s0 + slab] * jnp.exp(tot - bc)).astype(BF16)
    if has_state:
        st_scr[...] = s0_ref[...]
    else:
        st_scr[...] = jnp.zeros_like(st_scr)

    bd_ones = _block_mask(GROUP, GROUP, HEAD_DIM, HEAD_DIM).astype(BF16)
    rib = _iota((1, c, GROUP), 1)

    def slab_step(i, carry):
        r0 = pl.multiple_of(i * slab, slab)
        q3 = q_scr[pl.ds(r0, slab), :].reshape(nb, c, GROUP)
        v3 = a_ref[pl.ds(r0, slab), 3 * GROUP:4 * GROUP].reshape(nb, c, GROUP)
        o3 = jnp.zeros((nb, c, GROUP), F32)
        for d in range(2):
            bc3 = bc_scr[d, pl.ds(r0, slab), :].reshape(nb, c, GROUP)
            k3 = k_scr[d, pl.ds(r0, slab), :].reshape(nb, c, GROUP)
            for j in range(c):
                keep = (rib >= j) if d == 0 else (rib <= j)
                dec = jnp.exp2(jnp.where(keep, bc3 - bc3[:, j:j + 1, :], -jnp.inf))
                pj = (dec * q3 * k3[:, j:j + 1, :]).astype(BF16).reshape(slab, GROUP)
                srep = _dot(pj, bd_ones).reshape(nb, c, GROUP)
                o3 = o3 + srep * v3[:, j:j + 1, :]
        o_scr[0, pl.ds(r0, slab), :] = o3.reshape(slab, GROUP)
        return carry

    lax.fori_loop(0, seq // slab, slab_step, 0)

    bd_mask = _block_mask(GROUP, GROUP, HEAD_DIM, HEAD_DIM)

    def body(n, carry):
        rows = [[pl.multiple_of(((n * HG_UNROLL + u) if d == 0 else nblk - 1 - (n * HG_UNROLL + u)) * c, c)
                 for u in range(HG_UNROLL)] for d in range(2)]
        upd = [[_dot_tn(a_ref[pl.ds(r0, c), 3 * GROUP:4 * GROUP].astype(BF16), kt_scr[d, pl.ds(r0, c), :])
                for r0 in rows[d]] for d in range(2)]
        for d in range(2):
            st = st_scr[d]
            for u, r0 in enumerate(rows[d]):
                o_scr[1 + d, pl.ds(r0, c), :] = _dot_nt(qt_scr[d, pl.ds(r0, c), :], st.astype(BF16))
                st = st * dec_scr[d, pl.ds(r0, 1), :] + jnp.where(bd_mask, upd[d][u], 0.0)
            st_scr[d] = st
        return carry

    lax.fori_loop(0, nblk // HG_UNROLL, body, 0)

    o = o_scr[0] + o_scr[1] + o_scr[2]
    mean_op = jnp.where(_block_mask(GROUP, GROUP, HEAD_DIM, HEAD_DIM), 1.0 / HEAD_DIM, 0.0)
    ms = _dot_sel_r(o * o, mean_op)
    y = o * lax.rsqrt(ms + NORM_EPS) * norm_ref[...]
    o_ref[...] = y * _silu(a_ref[:, 4 * GROUP:5 * GROUP])
    sfin_ref[...] = st_scr[...]


def _layer_slot(nseq, stack, tail):
    zeros = (0,) * len(tail)
    if stack is None:
        return (pl.BlockSpec((None,) + tail, lambda b: (b,) + zeros),
                jax.ShapeDtypeStruct((nseq,) + tail, F32))
    layer, depth = stack
    return (pl.BlockSpec((None, None) + tail, lambda b: (b, layer) + zeros),
            jax.ShapeDtypeStruct((nseq, depth) + tail, F32))


def _hgrn_parts(proj_a, row0, nseq, seq, hgrn_lb, norm_t, s0, layer, stack=None):
    has_state = s0 is not None
    blk0 = row0 // seq
    in_specs = [pl.BlockSpec((seq, PROJ_A), lambda b: (blk0 + b, 0)),
                pl.BlockSpec(hgrn_lb.shape, lambda b: (0, 0, 0)),
                pl.BlockSpec((1, GROUP), lambda b: (0, 0))]
    args = [proj_a, hgrn_lb, norm_t]
    if has_state:
        in_specs.append(pl.BlockSpec((None, 2, GROUP, GROUP), lambda b: (b, 0, 0, 0)))
        args.append(s0)
    out_specs, out_shape = zip(_layer_slot(nseq, None, (seq, GROUP)),
                               _layer_slot(nseq, stack, (2, GROUP, GROUP)))
    out_specs, out_shape = list(out_specs), list(out_shape)
    scratch = [pltpu.VMEM((seq, GROUP), F32),
               pltpu.VMEM((2, seq, GROUP), F32),
               pltpu.VMEM((2, seq, GROUP), F32),
               pltpu.VMEM((2, seq, GROUP), F32),
               pltpu.VMEM((2, seq, GROUP), BF16),
               pltpu.VMEM((2, seq, GROUP), BF16),
               pltpu.VMEM((2, GROUP, GROUP), F32),
               pltpu.VMEM((3, seq, GROUP), F32)]
    return functools.partial(_hgrn_kernel, layer, has_state), in_specs, args, out_specs, out_shape, scratch


def _mla_kernel(latent, *refs):
    if latent:
        (b_ref, qn_ref, wq_ref, kvn_ref, wkv_ref, cckv_ref, ckpe_ref, cos_ref, sin_ref,
         o_ref) = refs
    else:
        (b_ref, qn_ref, wq_ref, kvn_ref, wkv_ref, o_ref, ckv_ref, kpe_ref) = refs
    seq = b_ref.shape[0]
    mckv = b_ref[:, 0:MLA_KV_RANK]
    mcq = b_ref[:, MLA_KV_RANK:MLA_KV_RANK + MLA_Q_RANK]
    kpe_t = b_ref[:, 384:512]

    cq = mcq * lax.rsqrt(jnp.mean(mcq * mcq, axis=-1, keepdims=True) + NORM_EPS) * qn_ref[...]
    qf = _dot(cq.astype(BF16), wq_ref[...])
    ckv = mckv * lax.rsqrt(jnp.mean(mckv * mckv, axis=-1, keepdims=True) + NORM_EPS) * kvn_ref[...]
    q_nope = qf[:, 0:N_HEADS * MLA_NOPE]
    q_rope = qf[:, N_HEADS * MLA_NOPE:]
    if latent:
        cos = cos_ref[...]
        sin = sin_ref[...]
        q_rope = _rope(q_rope, cos, sin)
        ckv_all = jnp.concatenate([cckv_ref[...], ckv], axis=0)
        kpe_all = jnp.concatenate([ckpe_ref[...], _rope(kpe_t, cos, sin)], axis=0)
    else:
        ckv_ref[...] = ckv
        kpe_ref[...] = kpe_t[:, 0:MLA_ROPE]
        ckv_all = ckv
        kpe_all = kpe_t
    kv = _dot(ckv_all.astype(BF16), wkv_ref[...])
    kcat = jnp.concatenate([kv[:, 0:GROUP], kpe_all], axis=1).astype(BF16)
    v = kv[:, GROUP:].astype(BF16)
    qcat = jnp.concatenate([q_nope, q_rope], axis=1)
    scale = (MLA_NOPE + MLA_ROPE) ** -0.5
    qb = min(seq, 256)
    width = qcat.shape[1]
    for r0 in range(0, seq, qb):
        qblk = qcat[r0:r0 + qb]
        acc = jnp.zeros((qb, GROUP), F32)
        for h in range(N_HEADS):
            hm = (_lane_group_mask(width, h * MLA_NOPE, MLA_NOPE)
                  | _lane_group_mask(width, N_HEADS * MLA_NOPE + h * MLA_ROPE, MLA_ROPE))
            s = _dot_nt(jnp.where(hm, qblk, 0.0).astype(BF16), kcat)
            e = jnp.exp2((s - jnp.max(s, axis=-1, keepdims=True)) * (scale * LOG2_E))
            z = jnp.sum(e, axis=-1, keepdims=True)
            oh = _dot(e.astype(BF16), v) / z
            acc = acc + jnp.where(_lane_group_mask(GROUP, h * HEAD_DIM, HEAD_DIM), oh, 0.0)
        o_ref[r0:r0 + qb, :] = acc


def _mla_parts(proj_b, row0, nseq, seq, q_norm, w_qb_p, kv_norm, w_kvb_p, latent_args, stack=None):
    latent = latent_args is not None
    blk0 = row0 // seq
    in_specs = [pl.BlockSpec((seq, PROJ_B), lambda b: (blk0 + b, 0)),
                pl.BlockSpec(q_norm.shape, lambda b: (0, 0)),
                pl.BlockSpec(w_qb_p.shape, lambda b: (0, 0)),
                pl.BlockSpec(kv_norm.shape, lambda b: (0, 0)),
                pl.BlockSpec(w_kvb_p.shape, lambda b: (0, 0))]
    args = [proj_b, q_norm, w_qb_p, kv_norm, w_kvb_p]
    out_specs = [pl.BlockSpec((None, seq, GROUP), lambda b: (b, 0, 0))]
    out_shape = [jax.ShapeDtypeStruct((nseq, seq, GROUP), F32)]
    if latent:
        cckv, ckpe_t, cos, sin = latent_args
        past = cckv.shape[1]
        in_specs += [pl.BlockSpec((None, past, MLA_KV_RANK), lambda b: (b, 0, 0)),
                     pl.BlockSpec((None, past, 128), lambda b: (b, 0, 0)),
                     pl.BlockSpec(cos.shape, lambda b: (0, 0)),
                     pl.BlockSpec(sin.shape, lambda b: (0, 0))]
        args += [cckv, ckpe_t, cos, sin]
    else:
        for width in (MLA_KV_RANK, MLA_ROPE):
            spec, shape = _layer_slot(nseq, stack, (seq, width))
            out_specs.append(spec)
            out_shape.append(shape)
    return functools.partial(_mla_kernel, latent), in_specs, args, out_specs, out_shape, []


def _diff_kernel(latent, lam_init, *refs):
    if latent:
        (c_ref, lam_ref, norm_ref, ck_ref, cv_ref, cos_ref, sin_ref, o_ref) = refs
    else:
        (c_ref, lam_ref, norm_ref, o_ref, k_ref, v_ref) = refs
    seq = c_ref.shape[0]
    dq = c_ref[:, 0:GROUP]
    dk = c_ref[:, GROUP:2 * GROUP]
    dv = c_ref[:, 2 * GROUP:3 * GROUP]
    if latent:
        cos = cos_ref[...]
        sin = sin_ref[...]
        dq = _rope(dq, cos, sin)
        k_all = jnp.concatenate([ck_ref[...], _rope(dk, cos, sin)], axis=0)
        v_all = jnp.concatenate([cv_ref[...], dv], axis=0)
    else:
        k_ref[...] = dk
        v_ref[...] = dv
        k_all = dk
        v_all = dv
    lv = lam_ref[...]
    lam = (jnp.exp(jnp.sum(lv[0:1] * lv[1:2], axis=-1, keepdims=True))
           - jnp.exp(jnp.sum(lv[2:3] * lv[3:4], axis=-1, keepdims=True)) + lam_init)
    k_bf = k_all.astype(BF16)
    v_bf = v_all.astype(BF16)
    scale = DIFF_DIM ** -0.5
    mean_op = jnp.where(_block_mask(GROUP, GROUP, HEAD_DIM, HEAD_DIM), 1.0 / HEAD_DIM, 0.0)
    qb = min(seq, 256)
    for r0 in range(0, seq, qb):
        qblk = dq[r0:r0 + qb]
        acc = jnp.zeros((qb, GROUP), F32)
        for h in range(N_HEADS):
            outs = []
            for comp in range(2):
                cm = _lane_group_mask(GROUP, h * HEAD_DIM + comp * DIFF_DIM, DIFF_DIM)
                s = _dot_nt(jnp.where(cm, qblk, 0.0).astype(BF16), k_bf)
                e = jnp.exp2((s - jnp.max(s, axis=-1, keepdims=True)) * (scale * LOG2_E))
                outs.append(_dot(e.astype(BF16), v_bf) / jnp.sum(e, axis=-1, keepdims=True))
            oh = outs[0] - lam * outs[1]
            acc = acc + jnp.where(_lane_group_mask(GROUP, h * HEAD_DIM, HEAD_DIM), oh, 0.0)
        ms = _dot_sel_r(acc * acc, mean_op)
        o_ref[r0:r0 + qb, :] = acc * lax.rsqrt(ms + NORM_EPS) * norm_ref[...] * (1.0 - lam_init)


def _diff_parts(proj_c, row0, nseq, seq, lam_p, norm_t, lam_init, latent_args, stack=None):
    latent = latent_args is not None
    blk0 = row0 // seq
    in_specs = [pl.BlockSpec((seq, PROJ_C), lambda b: (blk0 + b, 0)),
                pl.BlockSpec(lam_p.shape, lambda b: (0, 0)),
                pl.BlockSpec(norm_t.shape, lambda b: (0, 0))]
    args = [proj_c, lam_p, norm_t]
    out_specs = [pl.BlockSpec((None, seq, GROUP), lambda b: (b, 0, 0))]
    out_shape = [jax.ShapeDtypeStruct((nseq, seq, GROUP), F32)]
    if latent:
        ck, cv, cos, sin = latent_args
        past = ck.shape[1]
        in_specs += [pl.BlockSpec((None, past, GROUP), lambda b: (b, 0, 0)),
                     pl.BlockSpec((None, past, GROUP), lambda b: (b, 0, 0)),
                     pl.BlockSpec(cos.shape, lambda b: (0, 0)),
                     pl.BlockSpec(sin.shape, lambda b: (0, 0))]
        args += [ck, cv, cos, sin]
    else:
        for _ in range(2):
            spec, shape = _layer_slot(nseq, stack, (seq, GROUP))
            out_specs.append(spec)
            out_shape.append(shape)
    return functools.partial(_diff_kernel, latent, lam_init), in_specs, args, out_specs, out_shape, []


def _mixers(nseq, name, *parts, carried=None):
    n_in = [len(p[1]) for p in parts]
    n_out = [len(p[3]) for p in parts]
    n_scr = [len(p[5]) for p in parts]
    carried = [None] * sum(n_out) if carried is None else list(carried)
    kept = [(o, arr) for o, arr in enumerate(carried) if arr is not None]

    def body(*refs):
        ins = refs[:sum(n_in)]
        outs = refs[sum(n_in) + len(kept):sum(n_in) + len(kept) + sum(n_out)]
        scr = refs[sum(n_in) + len(kept) + sum(n_out):]
        i0 = o0 = s0 = 0
        for part, ni, no, ns in zip(parts, n_in, n_out, n_scr):
            part[0](*ins[i0:i0 + ni], *outs[o0:o0 + no], *scr[s0:s0 + ns])
            i0, o0, s0 = i0 + ni, o0 + no, s0 + ns

    res = pl.pallas_call(
        body,
        grid=(nseq,),
        in_specs=[x for p in parts for x in p[1]] + [pl.BlockSpec(memory_space=pl.ANY)] * len(kept),
        out_specs=[x for p in parts for x in p[3]],
        out_shape=[x for p in parts for x in p[4]],
        scratch_shapes=[x for p in parts for x in p[5]],
        input_output_aliases={sum(n_in) + k: o for k, (o, _) in enumerate(kept)},
        compiler_params=_cparams("arbitrary"),
        name=name,
    )(*[x for p in parts for x in p[2]], *[arr for _, arr in kept])
    out, o0 = [], 0
    for no in n_out:
        out.append(tuple(res[o0:o0 + no]))
        o0 += no
    return out


def _ssd_kernel(has_state, *refs):
    if has_state:
        (d_ref, cw_ref, cb_ref, dtb_ref, alog_ref, dskip_ref, norm_ref, s0_ref,
         o_ref, sfin_ref, xs_scr, bm_scr, cm_scr, xdt_scr, a_scr, st_scr, yf_scr, yb_scr) = refs
    else:
        (d_ref, cw_ref, cb_ref, dtb_ref, alog_ref, dskip_ref, norm_ref,
         o_ref, sfin_ref, xs_scr, bm_scr, cm_scr, xdt_scr, a_scr, st_scr, yf_scr, yb_scr) = refs
    seq = d_ref.shape[0]
    c = SSD_CHUNK
    nchunk = seq // c
    ngrp = 2 * SSD_STATE

    xin = d_ref[:, GROUP:GROUP + 512]
    rows = _iota(xin.shape, 0)
    prev = jnp.where(rows == 0, 0.0, pltpu.roll(xin, 1, 0))
    nxt = jnp.where(rows == seq - 1, 0.0, pltpu.roll(xin, seq - 1, 0))
    cw = cw_ref[...]
    xbc = _silu(cw[0:1] * prev + cw[1:2] * xin + cw[2:3] * nxt + cb_ref[...])
    xs = xbc[:, 0:GROUP]
    xs_scr[...] = xs
    bm_scr[...] = xbc[:, GROUP:GROUP + ngrp]
    cm_scr[...] = xbc[:, GROUP + ngrp:GROUP + 2 * ngrp]
    dt = _softplus(d_ref[:, GROUP + 512:GROUP + 640] + dtb_ref[...])
    a_scr[...] = dt * (-jnp.exp(alog_ref[...]))
    erow = _iota((128, GROUP), 0)
    ehead = _iota((128, GROUP), 1) // HEAD_DIM
    expand = tuple((erow == 4 * d + ehead).astype(F32) for d in range(2))
    for d in range(2):
        xdt_scr[d] = xs * _dot_sel_r(dt, expand[d])
    if has_state:
        st_scr[...] = s0_ref[...]
    else:
        st_scr[...] = jnp.zeros_like(st_scr)

    row = _iota((c, c), 0)
    col = _iota((c, c), 1)
    tri = ((col <= row).astype(F32), (col >= row).astype(F32))
    keep = (col <= row, col >= row)
    grp_lane = _iota((1, ngrp), 1) // SSD_STATE
    valid = (_iota((ngrp, GROUP), 0) // SSD_STATE) == (_iota((ngrp, GROUP), 1) // (2 * HEAD_DIM))

    def chunk_step(d, r0, out_scr):
        a_c = a_scr[pl.ds(r0, c), :]
        bm_c = bm_scr[pl.ds(r0, c), :]
        cm_c = cm_scr[pl.ds(r0, c), :].astype(BF16)
        xdt_c = xdt_scr[d, pl.ds(r0, c), :]
        acum = _dot_sel(tri[d], a_c)
        acum_t = acum.T
        acum_rep = _dot_sel_r(acum, expand[d])
        bm2 = jnp.concatenate([jnp.where(grp_lane == g, bm_c, 0.0) for g in range(2)], axis=0)
        cb = _dot_nt(cm_c, bm2.astype(BF16))
        scores = []
        xparts = []
        for h in range(N_HEADS):
            lane = 4 * d + h
            seg = jnp.exp(jnp.where(keep[d], acum[:, lane:lane + 1] - acum_t[lane:lane + 1, :], -jnp.inf))
            g = h // 2
            scores.append((cb[:, g * c:(g + 1) * c] * seg).astype(BF16))
            xparts.append(jnp.where(_lane_group_mask(GROUP, h * HEAD_DIM, HEAD_DIM), xdt_c, 0.0))
        y = _dot(jnp.concatenate(scores, axis=1), jnp.concatenate(xparts, axis=0).astype(BF16))
        st = st_scr[d]
        y = y + _dot(cm_c, st.astype(BF16)) * jnp.exp(acum_rep)
        out_scr[pl.ds(r0, c), :] = y
        edge = acum_rep[c - 1:c] if d == 0 else acum_rep[0:1]
        xt = (xdt_c * jnp.exp(edge - acum_rep)).astype(BF16)
        upd = _dot_tn(bm_c.astype(BF16), xt)
        st_scr[d] = st * jnp.exp(edge) + jnp.where(valid, upd, 0.0)

    def body(n, carry):
        chunk_step(0, pl.multiple_of(n * c, c), yf_scr)
        chunk_step(1, pl.multiple_of((nchunk - 1 - n) * c, c), yb_scr)
        return carry

    lax.fori_loop(0, nchunk, body, 0)

    y = yf_scr[...] + yb_scr[...] + dskip_ref[...] * xs_scr[...]
    y = y * _silu(d_ref[:, 0:GROUP])
    o_ref[...] = y * lax.rsqrt(jnp.mean(y * y, axis=-1, keepdims=True) + NORM_EPS) * norm_ref[...]
    sfin_ref[...] = st_scr[...]


def _ssd_parts(proj_d, row0, nseq, seq, conv_w, conv_b, dt_bias_p, a_log_p, d_rep, norm, s0, stack=None):
    has_state = s0 is not None
    blk0 = row0 // seq
    ngrp = 2 * SSD_STATE
    small = [conv_w, conv_b, dt_bias_p, a_log_p, d_rep, norm]
    in_specs = ([pl.BlockSpec((seq, PROJ_D), lambda b: (blk0 + b, 0))]
                + [pl.BlockSpec(s.shape, lambda b: (0, 0)) for s in small])
    args = [proj_d] + small
    if has_state:
        in_specs.append(pl.BlockSpec((None, 2, ngrp, GROUP), lambda b: (b, 0, 0, 0)))
        args.append(s0)
    out_specs, out_shape = zip(_layer_slot(nseq, None, (seq, GROUP)),
                               _layer_slot(nseq, stack, (2, ngrp, GROUP)))
    out_specs, out_shape = list(out_specs), list(out_shape)
    scratch = [pltpu.VMEM((seq, GROUP), F32),
               pltpu.VMEM((seq, ngrp), F32),
               pltpu.VMEM((seq, ngrp), F32),
               pltpu.VMEM((2, seq, GROUP), F32),
               pltpu.VMEM((seq, 128), F32),
               pltpu.VMEM((2, ngrp, GROUP), F32),
               pltpu.VMEM((seq, GROUP), F32),
               pltpu.VMEM((seq, GROUP), F32)]
    return functools.partial(_ssd_kernel, has_state), in_specs, args, out_specs, out_shape, scratch


def _outproj_kernel(n_ctx_tiles, *refs):
    ctx_refs, lat_refs = refs[0:4], refs[4:8]
    w_ref, x_ref, m_ref, g_ref, b_ref, o_ref = refs[8:]
    is_ctx = pl.program_id(0) < n_ctx_tiles
    mixed = None
    for i, (c_ref, l_ref) in enumerate(zip(ctx_refs, lat_refs)):
        part = jnp.where(is_ctx, c_ref[...], l_ref[...]).astype(BF16)
        term = _dot(part, w_ref[i * GROUP:(i + 1) * GROUP, :])
        mixed = term if mixed is None else mixed + term
    o_ref[...] = _layernorm(ALPHA * x_ref[...] + m_ref[2] * mixed, g_ref[...], b_ref[...])


def _outproj(parts_ctx, parts_lat, w_out_bf, x, mods_l, ln_g, ln_b, t_ctx, s_lat):
    t, d = x.shape
    tm = 512
    n_ctx = t_ctx // tm
    n_lat = (t - t_ctx) // tm
    ctx_spec = pl.BlockSpec((tm, GROUP), lambda i: (jnp.minimum(i, n_ctx - 1), 0))
    lat_spec = pl.BlockSpec((tm, GROUP), lambda i: (jnp.clip(i - n_ctx, 0, n_lat - 1), 0))
    return pl.pallas_call(
        functools.partial(_outproj_kernel, n_ctx),
        grid=(t // tm,),
        in_specs=[ctx_spec] * 4 + [lat_spec] * 4
        + [pl.BlockSpec(w_out_bf.shape, lambda i: (0, 0)),
           pl.BlockSpec((tm, d), lambda i: (i, 0)),
           pl.BlockSpec((None, 6, 1, d), _mod_row_map(tm, t_ctx, s_lat)),
           pl.BlockSpec((1, d), lambda i: (0, 0)),
           pl.BlockSpec((1, d), lambda i: (0, 0))],
        out_specs=pl.BlockSpec((tm, d), lambda i: (i, 0)),
        out_shape=jax.ShapeDtypeStruct((t, d), F32),
        compiler_params=_cparams("arbitrary"),
        name="outproj_ln",
    )(*parts_ctx, *parts_lat, w_out_bf, x, mods_l, ln_g, ln_b)


def _top_rows(s, k, extra=()):
    r = s.shape[0]
    rid = _iota(s.shape, 0).astype(F32)
    vals, ids = [], []
    picked = [[] for _ in extra]
    for _ in range(k):
        m = jnp.max(s, axis=0, keepdims=True)
        cand = jnp.where(s == m, rid, float(r))
        i = jnp.min(cand, axis=0, keepdims=True)
        hit = cand == i
        vals.append(m)
        ids.append(i)
        for lst, arr in zip(picked, extra):
            lst.append(jnp.max(jnp.where(hit, arr, -1.0), axis=0, keepdims=True))
        s = jnp.where(hit, -jnp.inf, s)
    cat = lambda xs: jnp.concatenate(xs, axis=0)
    return cat(vals), cat(ids), [cat(p) for p in picked]


def _router_kernel(x_ref, m_ref, wq_ref, keys_ref, h_ref, a_ref, b_ref, g_ref):
    tm = x_ref.shape[0]
    hb = (x_ref[...] * (1.0 + m_ref[4]) + m_ref[3]).astype(BF16)
    h_ref[...] = hb
    qt = _dot(hb, wq_ref[...]).T.astype(BF16)
    k = PEER_TOPK
    code_rows, g_rows = [], []
    for head in range(PEER_HEADS):
        tv, ti = [], []
        for half in range(2):
            g = 2 * head + half
            sc = _dot(keys_ref[g], qt[g * PEER_HALF:(g + 1) * PEER_HALF])
            v, i, _ = _top_rows(sc, k)
            tv.append(v)
            ti.append(i)
        cs = [tv[0][0:1] + tv[1]]
        ca = [jnp.broadcast_to(ti[0][0:1], (k, tm))]
        cb = [ti[1]]
        for k1 in range(1, 4):
            cs.append(tv[0][k1:k1 + 1] + tv[1][0:8])
            ca.append(jnp.broadcast_to(ti[0][k1:k1 + 1], (8, tm)))
            cb.append(ti[1][0:8])
        low = _iota((8, tm), 0) < 4
        v2_dup = jnp.where(low, tv[1][0:8], pltpu.roll(tv[1][0:8], 4, 0))
        i2_dup = jnp.where(low, ti[1][0:8], pltpu.roll(ti[1][0:8], 4, 0))
        for k1 in (4, 6):
            cs.append(jnp.where(low, tv[0][k1:k1 + 1], tv[0][k1 + 1:k1 + 2]) + v2_dup)
            ca.append(jnp.where(low, ti[0][k1:k1 + 1], ti[0][k1 + 1:k1 + 2]))
            cb.append(i2_dup)
        cs.append(tv[0][8:16] + tv[1][0:1])
        ca.append(ti[0][8:16])
        cb.append(jnp.broadcast_to(ti[1][0:1], (8, tm)))
        code = jnp.concatenate(ca, axis=0) * float(PEER_KEYS) + jnp.concatenate(cb, axis=0)
        best, _, (sel_code,) = _top_rows(jnp.concatenate(cs, axis=0), k, extra=(code,))
        e = jnp.exp(best - best[0:1])
        g_rows.append(e / jnp.sum(e, axis=0, keepdims=True))
        code_rows.append(sel_code)
    codes = jnp.concatenate(code_rows, axis=0)
    key1 = jnp.floor(codes * (1.0 / PEER_KEYS))
    a_ref[...] = key1.T.astype(I32)
    b_ref[...] = (codes - key1 * float(PEER_KEYS)).T.astype(I32)
    g_ref[...] = jnp.concatenate(g_rows, axis=0).T


def _router(x, mods_l, wq_bf, keys, t_ctx, s_lat):
    t, d = x.shape
    tm = 256
    nslot = PEER_HEADS * PEER_TOPK
    return pl.pallas_call(
        _router_kernel,
        grid=(t // tm,),
        in_specs=[pl.BlockSpec((tm, d), lambda i: (i, 0)),
                  pl.BlockSpec((None, 6, 1, d), _mod_row_map(tm, t_ctx, s_lat)),
                  pl.BlockSpec(wq_bf.shape, lambda i: (0, 0)),
                  pl.BlockSpec(keys.shape, lambda i: (0, 0, 0))],
        out_specs=[pl.BlockSpec((tm, d), lambda i: (i, 0)),
                   pl.BlockSpec((tm, nslot), lambda i: (i, 0)),
                   pl.BlockSpec((tm, nslot), lambda i: (i, 0)),
                   pl.BlockSpec((tm, nslot), lambda i: (i, 0))],
        out_shape=[jax.ShapeDtypeStruct((t, d), BF16),
                   jax.ShapeDtypeStruct((t, nslot), I32),
                   jax.ShapeDtypeStruct((t, nslot), I32),
                   jax.ShapeDtypeStruct((t, nslot), F32)],
        compiler_params=_cparams("arbitrary"),
        name="peer_router",
    )(x, mods_l, wq_bf, keys)


def _gates_kernel(a_ref, b_ref, g_ref, u_ref, v_ref, o_ref, ub_ref, vb_ref):
    tm = a_ref.shape[0]
    n = PEER_KEYS
    sub = 16
    ub_ref[...] = u_ref[...].astype(BF16)
    vb_ref[...] = v_ref[...].astype(BF16)
    key = _iota((sub, n, a_ref.shape[2]), 1).astype(F32).astype(BF16)
    zero = jnp.zeros((), BF16)
    for t0 in range(0, tm, sub):
        a = a_ref[t0:t0 + sub].astype(F32).astype(BF16)
        b = b_ref[t0:t0 + sub].astype(F32).astype(BF16)
        g = g_ref[t0:t0 + sub].astype(BF16)
        onehot_a = jnp.where(key == a, jnp.ones((), BF16), zero)
        gated_b = jnp.where(key == b, g, zero)
        w = lax.dot_general(onehot_a, gated_b, (((2,), (2,)), ((0,), (0,))),
                            preferred_element_type=F32)
        w_t = jnp.swapaxes(w.astype(BF16), 0, 1)
        for r in range(n):
            o_ref[t0:t0 + sub, r * n:(r + 1) * n] = w_t[r]


def _gates(a_idx, b_idx, gate, peer_u, peer_v, layer):
    t, nslot = a_idx.shape
    d = peer_u.shape[-1]
    tm = 192
    steps = t // tm
    te = N_EXPERTS // steps
    spec = pl.BlockSpec((tm, 1, nslot), lambda i: (i, 0, 0))
    tab_in = pl.BlockSpec((None, te, d), lambda i: (layer, i, 0))
    tab_out = pl.BlockSpec((te, d), lambda i: (i, 0))
    return pl.pallas_call(
        _gates_kernel,
        grid=(steps,),
        in_specs=[spec, spec, spec, tab_in, tab_in],
        out_specs=[pl.BlockSpec((tm, N_EXPERTS), lambda i: (i, 0)), tab_out, tab_out],
        out_shape=[jax.ShapeDtypeStruct((t, N_EXPERTS), BF16),
                   jax.ShapeDtypeStruct((N_EXPERTS, d), BF16),
                   jax.ShapeDtypeStruct((N_EXPERTS, d), BF16)],
        compiler_params=_cparams("arbitrary"),
        name="peer_gates",
    )(a_idx.reshape(t, 1, nslot), b_idx.reshape(t, 1, nslot), gate.reshape(t, 1, nslot), peer_u, peer_v)


def _experts_kernel(h_ref, u_ref, v_ref, w_ref, x_ref, m_ref, g_ref, b_ref, o_ref, acc_ref):
    j = pl.program_id(1)

    @pl.when(j == 0)
    def _():
        acc_ref[...] = jnp.zeros_like(acc_ref)

    act = _gelu_tanh(_dot_nt(h_ref[...], u_ref[...]))
    acc_ref[...] += _dot((act * w_ref[...].astype(F32)).astype(BF16), v_ref[...])

    @pl.when(j == pl.num_programs(1) - 1)
    def _():
        o_ref[...] = _layernorm(ALPHA * x_ref[...] + m_ref[5] * acc_ref[...], g_ref[...], b_ref[...])


def _experts(h_bf, u_bf, v_bf, w_gate, x, mods_l, ln_g, ln_b, t_ctx, s_lat):
    t, d = x.shape
    tm, te = 1024, 1024
    return pl.pallas_call(
        _experts_kernel,
        grid=(t // tm, N_EXPERTS // te),
        in_specs=[pl.BlockSpec((tm, d), lambda i, j: (i, 0)),
                  pl.BlockSpec((te, d), lambda i, j: (j, 0)),
                  pl.BlockSpec((te, d), lambda i, j: (j, 0)),
                  pl.BlockSpec((tm, te), lambda i, j: (i, j)),
                  pl.BlockSpec((tm, d), lambda i, j: (i, 0)),
                  pl.BlockSpec((None, 6, 1, d), _mod_row_map(tm, t_ctx, s_lat)),
                  pl.BlockSpec((1, d), lambda i, j: (0, 0)),
                  pl.BlockSpec((1, d), lambda i, j: (0, 0))],
        out_specs=pl.BlockSpec((tm, d), lambda i, j: (i, 0)),
        out_shape=jax.ShapeDtypeStruct((t, d), F32),
        scratch_shapes=[pltpu.VMEM((tm, d), F32)],
        compiler_params=_cparams("arbitrary", "arbitrary"),
        name="peer_experts",
    )(h_bf, u_bf, v_bf, w_gate, x, mods_l, ln_g, ln_b)


def _pad_cols(w, n):
    return jnp.concatenate([w, jnp.zeros((w.shape[0], n), w.dtype)], axis=1) if n else w


def _layout_w_in(w):
    a = w[:, 0:1280]
    mcq, mckv, mkpe = w[:, 1280:1472], w[:, 1472:1600], w[:, 1600:1632]
    b = jnp.concatenate([mckv, _pad_cols(mcq, 64), mkpe, mkpe, mkpe, mkpe], axis=1)
    c = w[:, 1632:2400]
    d = _pad_cols(w[:, 2400:3176], PROJ_D - 776)
    return jnp.concatenate([a, b, c, d], axis=1).astype(BF16)


def _layout_w_qb(w):
    w4 = w.reshape(MLA_Q_RANK, N_HEADS, MLA_NOPE + MLA_ROPE)
    return jnp.concatenate([w4[:, :, :MLA_NOPE].reshape(MLA_Q_RANK, -1),
                            w4[:, :, MLA_NOPE:].reshape(MLA_Q_RANK, -1)], axis=1).astype(BF16)


def _layout_w_kvb(w):
    w4 = w.reshape(MLA_KV_RANK, N_HEADS, MLA_NOPE + HEAD_DIM)
    return jnp.concatenate([w4[:, :, :MLA_NOPE].reshape(MLA_KV_RANK, -1),
                            w4[:, :, MLA_NOPE:].reshape(MLA_KV_RANK, -1)], axis=1).astype(BF16)


def _rope_tables(seq):
    rows = seq // GRID_W
    row = jnp.repeat(jnp.arange(rows, dtype=F32), GRID_W)
    col = jnp.tile(jnp.arange(GRID_W, dtype=F32), rows)
    freqs = ROPE_BASE ** (-jnp.arange(ROPE_PAIRS, dtype=F32) / ROPE_PAIRS)
    cos_l, sin_l = [], []
    for pos in (row, col):
        ang = pos[:, None] * freqs
        cos_l += [jnp.cos(ang), jnp.cos(ang)]
        sin_l += [-jnp.sin(ang), jnp.sin(ang)]
    return jnp.concatenate(cos_l, axis=1), jnp.concatenate(sin_l, axis=1)


def _hgrn_state_pack(st):
    b = st.shape[0]
    st_t = jnp.swapaxes(st, -1, -2)
    zero = jnp.zeros_like(st_t[:, :, 0])
    rows = [jnp.concatenate([st_t[:, :, h] if g == h else zero for g in range(N_HEADS)], axis=-1)
            for h in range(N_HEADS)]
    return jnp.concatenate(rows, axis=-2).reshape(b, 2, GROUP, GROUP)


def _hgrn_state_unpack(sb):
    blocks = [sb[..., h * HEAD_DIM:(h + 1) * HEAD_DIM, h * HEAD_DIM:(h + 1) * HEAD_DIM]
              for h in range(N_HEADS)]
    return jnp.swapaxes(jnp.stack(blocks, axis=-3), -1, -2)


def _ssd_state_pack(st):
    st_t = jnp.swapaxes(st, -1, -2)
    zero = jnp.zeros_like(st_t[:, :, 0])
    rows = [jnp.concatenate([st_t[:, :, h] if h // 2 == g else zero for h in range(N_HEADS)], axis=-1)
            for g in range(2)]
    return jnp.concatenate(rows, axis=-2)


def _ssd_state_unpack(sb):
    blocks = [sb[..., (h // 2) * SSD_STATE:(h // 2 + 1) * SSD_STATE, h * HEAD_DIM:(h + 1) * HEAD_DIM]
              for h in range(N_HEADS)]
    return jnp.swapaxes(jnp.stack(blocks, axis=-3), -1, -2)


def _tile_lanes(v, n):
    return jnp.tile(v.reshape(1, -1), (1, n))


def kernel(x_prompt, x_sample, cache_mla_ckv, cache_mla_kpe, cache_diff_k, cache_diff_v, state_hgrn, state_ssd, c, c_ctx, w_mod, b_mod, w_in, hgrn_lb, hgrn_norm, mla_q_norm, mla_w_qb, mla_kv_norm, mla_w_kvb, diff_lambda, diff_norm, ssd_conv_w, ssd_conv_b, ssd_dt_bias, ssd_a_log, ssd_d, ssd_norm, w_out, ln1_g, ln1_b, peer_wq, peer_keys, peer_u, peer_v, ln2_g, ln2_b):
    nb, seq, d = x_prompt.shape
    nlat, lseq, _ = x_sample.shape
    depth = w_in.shape[0]
    t_ctx = nb * seq
    x = jnp.concatenate([x_prompt.reshape(t_ctx, d), x_sample.reshape(nlat * lseq, d)], axis=0)

    cond8 = jnp.concatenate([c_ctx.reshape(1, d), c, jnp.zeros((8 - 1 - nlat, d), F32)], axis=0)
    mods = _mods(cond8, w_mod, b_mod)
    mods = mods[:, :1 + nlat].reshape(depth, 1 + nlat, 6, 1, d)

    cos32, sin32 = _rope_tables(lseq)
    cos128, sin128 = jnp.tile(cos32, (1, 4)), jnp.tile(sin32, (1, 4))
    cos256, sin256 = jnp.tile(cos32, (1, 8)), jnp.tile(sin32, (1, 8))

    carried = None
    for l in range(depth):
        mods_l = mods[l]
        pa, pb, pc, pd = _inproj(x, mods_l, _layout_w_in(w_in[l]), t_ctx, lseq)

        norm_hg = _tile_lanes(hgrn_norm[l], N_HEADS)
        mla_w = (mla_q_norm[l].reshape(1, -1), _layout_w_qb(mla_w_qb[l]),
                 mla_kv_norm[l].reshape(1, -1), _layout_w_kvb(mla_w_kvb[l]))
        lam_init = 0.8 - 0.6 * math.exp(-0.3 * l)
        norm_df = _tile_lanes(diff_norm[l], N_HEADS)
        past = cache_diff_k.shape[2]
        ssd_w = (ssd_conv_w[l], ssd_conv_b[l].reshape(1, -1),
                 _pad_cols(ssd_dt_bias[l].reshape(1, -1), 120), _pad_cols(ssd_a_log[l].reshape(1, -1), 120),
                 jnp.repeat(ssd_d[l], HEAD_DIM).reshape(1, -1), ssd_norm[l].reshape(1, -1))

        stack = (l, depth)
        ctx_out = _mixers(
            nb, "mixers_ctx",
            _mla_parts(pb, 0, nb, seq, *mla_w, None, stack),
            _diff_parts(pc, 0, nb, seq, diff_lambda[l], norm_df, lam_init, None, stack),
            _ssd_parts(pd, 0, nb, seq, *ssd_w, None, stack),
            _hgrn_parts(pa, 0, nb, seq, hgrn_lb, norm_hg, None, l, stack),
            carried=carried)
        (mla_ctx, new_ckv, new_kpe), (df_ctx, new_dk, new_dv), (ssd_ctx, ssd_fin), (hg_ctx, hg_fin) = ctx_out
        carried = [None, new_ckv, new_kpe, None, new_dk, new_dv, None, ssd_fin, None, hg_fin]
        ((mla_lat,),) = _mixers(
            nlat, "mla_lat",
            _mla_parts(pb, t_ctx, nlat, lseq, *mla_w,
                       (cache_mla_ckv[:, l], jnp.tile(cache_mla_kpe[:, l], (1, 1, 4)), cos128, sin128)))
        ((df_lat,),) = _mixers(
            nlat, "diffattn_lat",
            _diff_parts(pc, t_ctx, nlat, lseq, diff_lambda[l], norm_df, lam_init,
                        (cache_diff_k[:, l].reshape(nlat, past, GROUP),
                         cache_diff_v[:, l].reshape(nlat, past, GROUP), cos256, sin256)))
        ((ssd_lat, _),) = _mixers(nlat, "ssd_lat",
                                  _ssd_parts(pd, t_ctx, nlat, lseq, *ssd_w, _ssd_state_pack(state_ssd[:, l])))
        ((hg_lat, _),) = _mixers(nlat, "hgrn_lat",
                                 _hgrn_parts(pa, t_ctx, nlat, lseq, hgrn_lb, norm_hg,
                                             _hgrn_state_pack(state_hgrn[:, l]), l))

        parts_ctx = [a.reshape(t_ctx, GROUP) for a in (hg_ctx, mla_ctx, df_ctx, ssd_ctx)]
        parts_lat = [a.reshape(nlat * lseq, GROUP) for a in (hg_lat, mla_lat, df_lat, ssd_lat)]
        x = _outproj(parts_ctx, parts_lat, w_out[l].astype(BF16), x, mods_l,
                     ln1_g[l].reshape(1, d), ln1_b[l].reshape(1, d), t_ctx, lseq)

        keys = peer_keys[l].reshape(2 * PEER_HEADS, PEER_KEYS, PEER_HALF).astype(BF16)
        h_bf, a_idx, b_idx, gate = _router(x, mods_l, peer_wq[l].astype(BF16), keys, t_ctx, lseq)
        w_gate, u_bf, v_bf = _gates(a_idx, b_idx, gate, peer_u, peer_v, l)
        x = _experts(h_bf, u_bf, v_bf, w_gate, x, mods_l,
                     ln2_g[l].reshape(1, d), ln2_b[l].reshape(1, d), t_ctx, lseq)

    y_prompt = x[:t_ctx].reshape(nb, seq, d)
    y_sample = x[t_ctx:].reshape(nlat, lseq, d)
    return (y_prompt, y_sample, new_ckv, new_kpe,
            new_dk.reshape(nb, depth, seq, N_HEADS, 2, DIFF_DIM),
            new_dv.reshape(nb, depth, seq, N_HEADS, 2 * DIFF_DIM),
            _hgrn_state_unpack(hg_fin), _ssd_state_unpack(ssd_fin))
```

```python
import functools
import math

import jax
import jax.numpy as jnp
from jax import lax
from jax.experimental import pallas as pl
from jax.experimental.pallas import tpu as pltpu

F32 = jnp.float32
BF16 = jnp.bfloat16
I32 = jnp.int32

D_MODEL = 1024
GROUP = 256
N_HEADS = 4
HEAD_DIM = 64
HG_BLOCK = 16
HG_SLAB = 256
HG_UNROLL = 8
SSD_CHUNK = 128
SSD_STATE = 64
MLA_Q_RANK = 192
MLA_KV_RANK = 128
MLA_NOPE = 64
MLA_ROPE = 32
DIFF_DIM = 32
GRID_W = 64
ROPE_PAIRS = 8
ROPE_BASE = 10000.0
PEER_HEADS = 8
PEER_KEYS = 128
PEER_TOPK = 16
PEER_HALF = 64
N_EXPERTS = PEER_KEYS * PEER_KEYS
NORM_EPS = 1e-6
LN_EPS = 1e-5
DEPTH = 2
ALPHA = (2.0 * DEPTH) ** 0.25
LOG2_E = 1.4426950408889634

PROJ_A = 5 * GROUP
PROJ_B = 512
PROJ_C = 3 * GROUP
PROJ_D = 896
VMEM_LIMIT = 56 * 1024 * 1024


def _cparams(*sem):
    return pltpu.CompilerParams(dimension_semantics=sem, vmem_limit_bytes=VMEM_LIMIT)


def _sigmoid(x):
    return 1.0 / (1.0 + jnp.exp(-x))


def _silu(x):
    return x * _sigmoid(x)


def _softplus(x):
    return jnp.maximum(x, 0.0) + jnp.log(1.0 + jnp.exp(-jnp.abs(x)))


def _gelu_tanh(x):
    return 0.5 * x * (1.0 + jnp.tanh(math.sqrt(2.0 / math.pi) * (x + 0.044715 * (x * x * x))))


def _dot(a, b, precision=None):
    return jnp.dot(a, b, preferred_element_type=F32, precision=precision)


def _dot_nt(a, b, precision=None):
    return lax.dot_general(a, b, (((1,), (1,)), ((), ())), preferred_element_type=F32,
                           precision=precision)


def _dot_tn(a, b, precision=None):
    return lax.dot_general(a, b, (((0,), (0,)), ((), ())), preferred_element_type=F32,
                           precision=precision)


def _split3(x):
    hi = x.astype(BF16)
    rest = x - hi.astype(F32)
    mid = rest.astype(BF16)
    lo = (rest - mid.astype(F32)).astype(BF16)
    return hi, mid, lo


def _dot_sel(sel, x):
    sel = sel.astype(BF16)
    hi, mid, lo = _split3(x)
    return _dot(sel, hi) + _dot(sel, mid) + _dot(sel, lo)


def _dot_sel_r(x, sel):
    sel = sel.astype(BF16)
    hi, mid, lo = _split3(x)
    return _dot(hi, sel) + _dot(mid, sel) + _dot(lo, sel)


def _iota(shape, dim):
    return lax.broadcasted_iota(I32, shape, dim)


def _block_mask(rows, cols, rblk, cblk):
    return (_iota((rows, cols), 0) // rblk) == (_iota((rows, cols), 1) // cblk)


def _lane_group_mask(width, start, size):
    lane = _iota((1, width), 1)
    return (lane >= start) & (lane < start + size)


def _layernorm(v, g, b):
    mu = jnp.mean(v, axis=-1, keepdims=True)
    d = v - mu
    var = jnp.mean(d * d, axis=-1, keepdims=True)
    return d * lax.rsqrt(var + LN_EPS) * g + b


def _swap_halves16(x):
    width = x.shape[-1]
    lane = _iota(x.shape, x.ndim - 1)
    up = pltpu.roll(x, width - 8, x.ndim - 1)
    down = pltpu.roll(x, 8, x.ndim - 1)
    return jnp.where((lane % 16) < 8, up, down)


def _rope(x, cos, sin_signed):
    return x * cos + _swap_halves16(x) * sin_signed


def _mods_kernel(c_ref, w_ref, b_ref, o_ref):
    s = _silu(c_ref[...]).astype(BF16)
    o_ref[...] = _dot(s, w_ref[...].astype(BF16)) + b_ref[...]


def _mods(cond8, w_mod, b_mod):
    depth, d, n = w_mod.shape
    tn = 3072
    return pl.pallas_call(
        _mods_kernel,
        grid=(depth, n // tn),
        in_specs=[pl.BlockSpec((8, d), lambda l, j: (0, 0)),
                  pl.BlockSpec((None, d, tn), lambda l, j: (l, 0, j)),
                  pl.BlockSpec((None, 1, tn), lambda l, j: (l, 0, j))],
        out_specs=pl.BlockSpec((None, 8, tn), lambda l, j: (l, 0, j)),
        out_shape=jax.ShapeDtypeStruct((depth, 8, n), F32),
        compiler_params=_cparams("arbitrary", "arbitrary"),
        name="mods",
    )(cond8, w_mod, b_mod.reshape(depth, 1, n))


def _mod_row_map(tm, t_ctx, s_lat):
    def index_map(i, *_):
        start = i * tm
        return (jnp.where(start < t_ctx, 0, 1 + (start - t_ctx) // s_lat), 0, 0, 0)
    return index_map


def _inproj_kernel(x_ref, m_ref, w_ref, oa_ref, ob_ref, oc_ref, od_ref):
    h = (x_ref[...] * (1.0 + m_ref[1]) + m_ref[0]).astype(BF16)
    start = 0
    for o_ref in (oa_ref, ob_ref, oc_ref, od_ref):
        width = o_ref.shape[-1]
        o_ref[...] = _dot(h, w_ref[:, start:start + width])
        start += width


def _inproj(x, mods_l, w_in_p, t_ctx, s_lat):
    t, d = x.shape
    tm = 512
    widths = (PROJ_A, PROJ_B, PROJ_C, PROJ_D)
    return pl.pallas_call(
        _inproj_kernel,
        grid=(t // tm,),
        in_specs=[pl.BlockSpec((tm, d), lambda i: (i, 0)),
                  pl.BlockSpec((None, 6, 1, d), _mod_row_map(tm, t_ctx, s_lat)),
                  pl.BlockSpec(w_in_p.shape, lambda i: (0, 0))],
        out_specs=[pl.BlockSpec((tm, w), lambda i: (i, 0)) for w in widths],
        out_shape=[jax.ShapeDtypeStruct((t, w), F32) for w in widths],
        compiler_params=_cparams("arbitrary"),
        name="inproj",
    )(x, mods_l, w_in_p)


def _hgrn_kernel(layer, has_state, *refs):
    if has_state:
        (a_ref, lb_ref, norm_ref, s0_ref, o_ref, sfin_ref,
         q_scr, k_scr, bc_scr, dec_scr, qt_scr, kt_scr, st_scr, o_scr) = refs
    else:
        (a_ref, lb_ref, norm_ref, o_ref, sfin_ref,
         q_scr, k_scr, bc_scr, dec_scr, qt_scr, kt_scr, st_scr, o_scr) = refs
        s0_ref = None
    seq = a_ref.shape[0]
    c = HG_BLOCK
    nblk = seq // c
    slab = HG_SLAB
    nb = slab // c

    lbp = lb_ref[...]
    e = jnp.exp(lbp - jnp.max(lbp, axis=0, keepdims=True))
    p = e / jnp.sum(e, axis=0, keepdims=True)
    lower = jnp.sum(p[1:layer + 1], axis=0) if layer > 0 else jnp.zeros_like(p[0])

    q = _silu(a_ref[:, 0:GROUP])
    q_scr[...] = q
    srow = _iota((slab, slab), 0)
    scol = _iota((slab, slab), 1)
    same = (srow // c) == (scol // c)
    cum_op = (jnp.where(same & (scol <= srow), 1.0, 0.0), jnp.where(same & (scol >= srow), 1.0, 0.0))
    for d in range(2):
        lb = lower[d:d + 1]
        f = lb + (1.0 - lb) * _sigmoid(a_ref[:, (1 + d) * GROUP:(2 + d) * GROUP])
        k = 1.0 - f
        lf = jnp.log(f)
        for s0 in range(0, seq, slab):
            bc = _dot_sel(cum_op[d], lf[s0:s0 + slab])
            bc3 = bc.reshape(nb, c, GROUP)
            edge = bc3[:, c - 1:c, :] if d == 0 else bc3[:, 0:1, :]
            tot = jnp.broadcast_to(edge, (nb, c, GROUP)).reshape(slab, GROUP)
            bc_scr[d, s0:s0 + slab, :] = bc * LOG2_E
            k_scr[d, s0:s0 + slab, :] = (bc - jnp.log(k[s0:s0 + slab])) * LOG2_E
            dec_scr[d, s0:s0 + slab, :] = jnp.exp(tot)
            qt_scr[d, s0:s0 + slab, :] = (q[s0:s0 + slab] * jnp.exp(bc)).astype(BF16)
            kt_scr[d, s0:s0 + slab, :] = (k[s0:s0 + slab] * jnp.exp(tot - bc)).astype(BF16)
    if has_state:
        st_scr[...] = s0_ref[...]
    else:
        st_scr[...] = jnp.zeros_like(st_scr)

    bd_ones = _block_mask(GROUP, GROUP, HEAD_DIM, HEAD_DIM).astype(BF16)
    rib = _iota((1, c, GROUP), 1)

    def slab_step(i, carry):
        r0 = pl.multiple_of(i * slab, slab)
        q3 = q_scr[pl.ds(r0, slab), :].reshape(nb, c, GROUP)
        v3 = a_ref[pl.ds(r0, slab), 3 * GROUP:4 * GROUP].reshape(nb, c, GROUP)
        o3 = jnp.zeros((nb, c, GROUP), F32)
        for d in range(2):
            bc3 = bc_scr[d, pl.ds(r0, slab), :].reshape(nb, c, GROUP)
            c3 = k_scr[d, pl.ds(r0, slab), :].reshape(nb, c, GROUP)
            for j in range(c):
                keep = (rib >= j) if d == 0 else (rib <= j)
                dec = jnp.exp2(jnp.where(keep, bc3 - c3[:, j:j + 1, :], -jnp.inf))
                pj = (dec * q3).astype(BF16).reshape(slab, GROUP)
                srep = _dot(pj, bd_ones).reshape(nb, c, GROUP)
                o3 = o3 + srep * v3[:, j:j + 1, :]
        o_scr[0, pl.ds(r0, slab), :] = o3.reshape(slab, GROUP)
        return carry

    lax.fori_loop(0, seq // slab, slab_step, 0)

    bd_mask = _block_mask(GROUP, GROUP, HEAD_DIM, HEAD_DIM)

    def body(n, carry):
        rows = [[pl.multiple_of(((n * HG_UNROLL + u) if d == 0 else nblk - 1 - (n * HG_UNROLL + u)) * c, c)
                 for u in range(HG_UNROLL)] for d in range(2)]
        upd = [[_dot_tn(a_ref[pl.ds(r0, c), 3 * GROUP:4 * GROUP].astype(BF16), kt_scr[d, pl.ds(r0, c), :])
                for r0 in rows[d]] for d in range(2)]
        for d in range(2):
            st = st_scr[d]
            for u, r0 in enumerate(rows[d]):
                o_scr[1 + d, pl.ds(r0, c), :] = _dot_nt(qt_scr[d, pl.ds(r0, c), :], st.astype(BF16))
                st = st * dec_scr[d, pl.ds(r0, 1), :] + jnp.where(bd_mask, upd[d][u], 0.0)
            st_scr[d] = st
        return carry

    lax.fori_loop(0, nblk // HG_UNROLL, body, 0)

    o = o_scr[0] + o_scr[1] + o_scr[2]
    mean_op = jnp.where(_block_mask(GROUP, GROUP, HEAD_DIM, HEAD_DIM), 1.0 / HEAD_DIM, 0.0)
    ms = _dot_sel_r(o * o, mean_op)
    y = o * lax.rsqrt(ms + NORM_EPS) * norm_ref[...]
    o_ref[...] = y * _silu(a_ref[:, 4 * GROUP:5 * GROUP])
    sfin_ref[...] = st_scr[...]


def _layer_slot(nseq, stack, tail):
    zeros = (0,) * len(tail)
    if stack is None:
        return (pl.BlockSpec((None,) + tail, lambda b: (b,) + zeros),
                jax.ShapeDtypeStruct((nseq,) + tail, F32))
    layer, depth = stack
    return (pl.BlockSpec((None, None) + tail, lambda b: (b, layer) + zeros),
            jax.ShapeDtypeStruct((nseq, depth) + tail, F32))


def _hgrn_parts(proj_a, row0, nseq, seq, hgrn_lb, norm_t, s0, layer, stack=None):
    has_state = s0 is not None
    blk0 = row0 // seq
    in_specs = [pl.BlockSpec((seq, PROJ_A), lambda b: (blk0 + b, 0)),
                pl.BlockSpec(hgrn_lb.shape, lambda b: (0, 0, 0)),
                pl.BlockSpec((1, GROUP), lambda b: (0, 0))]
    args = [proj_a, hgrn_lb, norm_t]
    if has_state:
        in_specs.append(pl.BlockSpec((None, 2, GROUP, GROUP), lambda b: (b, 0, 0, 0)))
        args.append(s0)
    out_specs, out_shape = zip(_layer_slot(nseq, None, (seq, GROUP)),
                               _layer_slot(nseq, stack, (2, GROUP, GROUP)))
    out_specs, out_shape = list(out_specs), list(out_shape)
    scratch = [pltpu.VMEM((seq, GROUP), F32),
               pltpu.VMEM((2, seq, GROUP), F32),
               pltpu.VMEM((2, seq, GROUP), F32),
               pltpu.VMEM((2, seq, GROUP), F32),
               pltpu.VMEM((2, seq, GROUP), BF16),
               pltpu.VMEM((2, seq, GROUP), BF16),
               pltpu.VMEM((2, GROUP, GROUP), F32),
               pltpu.VMEM((3, seq, GROUP), F32)]
    return functools.partial(_hgrn_kernel, layer, has_state), in_specs, args, out_specs, out_shape, scratch


def _mla_kernel(latent, *refs):
    if latent:
        (b_ref, qn_ref, wq_ref, kvn_ref, wkv_ref, cckv_ref, ckpe_ref, cos_ref, sin_ref,
         o_ref) = refs
    else:
        (b_ref, qn_ref, wq_ref, kvn_ref, wkv_ref, o_ref, ckv_ref, kpe_ref) = refs
    seq = b_ref.shape[0]
    mckv = b_ref[:, 0:MLA_KV_RANK]
    mcq = b_ref[:, MLA_KV_RANK:MLA_KV_RANK + MLA_Q_RANK]
    kpe_t = b_ref[:, 384:512]

    cq = mcq * lax.rsqrt(jnp.mean(mcq * mcq, axis=-1, keepdims=True) + NORM_EPS) * qn_ref[...]
    qf = _dot(cq.astype(BF16), wq_ref[...])
    ckv = mckv * lax.rsqrt(jnp.mean(mckv * mckv, axis=-1, keepdims=True) + NORM_EPS) * kvn_ref[...]
    q_nope = qf[:, 0:N_HEADS * MLA_NOPE]
    q_rope = qf[:, N_HEADS * MLA_NOPE:]
    if latent:
        cos = cos_ref[...]
        sin = sin_ref[...]
        q_rope = _rope(q_rope, cos, sin)
        ckv_all = jnp.concatenate([cckv_ref[...], ckv], axis=0)
        kpe_all = jnp.concatenate([ckpe_ref[...], _rope(kpe_t, cos, sin)], axis=0)
    else:
        ckv_ref[...] = ckv
        kpe_ref[...] = kpe_t[:, 0:MLA_ROPE]
        ckv_all = ckv
        kpe_all = kpe_t
    kv = _dot(ckv_all.astype(BF16), wkv_ref[...])
    kcat = jnp.concatenate([kv[:, 0:GROUP], kpe_all], axis=1).astype(BF16)
    v = kv[:, GROUP:].astype(BF16)
    qcat = jnp.concatenate([q_nope, q_rope], axis=1)
    scale = (MLA_NOPE + MLA_ROPE) ** -0.5
    qb = min(seq, 256)
    width = qcat.shape[1]
    for r0 in range(0, seq, qb):
        qblk = qcat[r0:r0 + qb]
        acc = jnp.zeros((qb, GROUP), F32)
        for h in range(N_HEADS):
            hm = (_lane_group_mask(width, h * MLA_NOPE, MLA_NOPE)
                  | _lane_group_mask(width, N_HEADS * MLA_NOPE + h * MLA_ROPE, MLA_ROPE))
            s = _dot_nt(jnp.where(hm, qblk, 0.0).astype(BF16), kcat)
            e = jnp.exp2((s - jnp.max(s, axis=-1, keepdims=True)) * (scale * LOG2_E))
            z = jnp.sum(e, axis=-1, keepdims=True)
            oh = _dot(e.astype(BF16), v) / z
            acc = acc + jnp.where(_lane_group_mask(GROUP, h * HEAD_DIM, HEAD_DIM), oh, 0.0)
        o_ref[r0:r0 + qb, :] = acc


def _mla_parts(proj_b, row0, nseq, seq, q_norm, w_qb_p, kv_norm, w_kvb_p, latent_args, stack=None):
    latent = latent_args is not None
    blk0 = row0 // seq
    in_specs = [pl.BlockSpec((seq, PROJ_B), lambda b: (blk0 + b, 0)),
                pl.BlockSpec(q_norm.shape, lambda b: (0, 0)),
                pl.BlockSpec(w_qb_p.shape, lambda b: (0, 0)),
                pl.BlockSpec(kv_norm.shape, lambda b: (0, 0)),
                pl.BlockSpec(w_kvb_p.shape, lambda b: (0, 0))]
    args = [proj_b, q_norm, w_qb_p, kv_norm, w_kvb_p]
    out_specs = [pl.BlockSpec((None, seq, GROUP), lambda b: (b, 0, 0))]
    out_shape = [jax.ShapeDtypeStruct((nseq, seq, GROUP), F32)]
    if latent:
        cckv, ckpe_t, cos, sin = latent_args
        past = cckv.shape[1]
        in_specs += [pl.BlockSpec((None, past, MLA_KV_RANK), lambda b: (b, 0, 0)),
                     pl.BlockSpec((None, past, 128), lambda b: (b, 0, 0)),
                     pl.BlockSpec(cos.shape, lambda b: (0, 0)),
                     pl.BlockSpec(sin.shape, lambda b: (0, 0))]
        args += [cckv, ckpe_t, cos, sin]
    else:
        for width in (MLA_KV_RANK, MLA_ROPE):
            spec, shape = _layer_slot(nseq, stack, (seq, width))
            out_specs.append(spec)
            out_shape.append(shape)
    return functools.partial(_mla_kernel, latent), in_specs, args, out_specs, out_shape, []


def _diff_kernel(latent, lam_init, *refs):
    if latent:
        (c_ref, lam_ref, norm_ref, ck_ref, cv_ref, cos_ref, sin_ref, o_ref) = refs
    else:
        (c_ref, lam_ref, norm_ref, o_ref, k_ref, v_ref) = refs
    seq = c_ref.shape[0]
    dq = c_ref[:, 0:GROUP]
    dk = c_ref[:, GROUP:2 * GROUP]
    dv = c_ref[:, 2 * GROUP:3 * GROUP]
    if latent:
        cos = cos_ref[...]
        sin = sin_ref[...]
        dq = _rope(dq, cos, sin)
        k_all = jnp.concatenate([ck_ref[...], _rope(dk, cos, sin)], axis=0)
        v_all = jnp.concatenate([cv_ref[...], dv], axis=0)
    else:
        k_ref[...] = dk
        v_ref[...] = dv
        k_all = dk
        v_all = dv
    lv = lam_ref[...]
    lam = (jnp.exp(jnp.sum(lv[0:1] * lv[1:2], axis=-1, keepdims=True))
           - jnp.exp(jnp.sum(lv[2:3] * lv[3:4], axis=-1, keepdims=True)) + lam_init)
    k_bf = k_all.astype(BF16)
    v_bf = v_all.astype(BF16)
    scale = DIFF_DIM ** -0.5
    mean_op = jnp.where(_block_mask(GROUP, GROUP, HEAD_DIM, HEAD_DIM), 1.0 / HEAD_DIM, 0.0)
    qb = min(seq, 256)
    for r0 in range(0, seq, qb):
        qblk = dq[r0:r0 + qb]
        acc = jnp.zeros((qb, GROUP), F32)
        for h in range(N_HEADS):
            outs = []
            for comp in range(2):
                cm = _lane_group_mask(GROUP, h * HEAD_DIM + comp * DIFF_DIM, DIFF_DIM)
                s = _dot_nt(jnp.where(cm, qblk, 0.0).astype(BF16), k_bf)
                e = jnp.exp2((s - jnp.max(s, axis=-1, keepdims=True)) * (scale * LOG2_E))
                outs.append(_dot(e.astype(BF16), v_bf) / jnp.sum(e, axis=-1, keepdims=True))
            oh = outs[0] - lam * outs[1]
            acc = acc + jnp.where(_lane_group_mask(GROUP, h * HEAD_DIM, HEAD_DIM), oh, 0.0)
        ms = _dot_sel_r(acc * acc, mean_op)
        o_ref[r0:r0 + qb, :] = acc * lax.rsqrt(ms + NORM_EPS) * norm_ref[...] * (1.0 - lam_init)


def _diff_parts(proj_c, row0, nseq, seq, lam_p, norm_t, lam_init, latent_args, stack=None):
    latent = latent_args is not None
    blk0 = row0 // seq
    in_specs = [pl.BlockSpec((seq, PROJ_C), lambda b: (blk0 + b, 0)),
                pl.BlockSpec(lam_p.shape, lambda b: (0, 0)),
                pl.BlockSpec(norm_t.shape, lambda b: (0, 0))]
    args = [proj_c, lam_p, norm_t]
    out_specs = [pl.BlockSpec((None, seq, GROUP), lambda b: (b, 0, 0))]
    out_shape = [jax.ShapeDtypeStruct((nseq, seq, GROUP), F32)]
    if latent:
        ck, cv, cos, sin = latent_args
        past = ck.shape[1]
        in_specs += [pl.BlockSpec((None, past, GROUP), lambda b: (b, 0, 0)),
                     pl.BlockSpec((None, past, GROUP), lambda b: (b, 0, 0)),
                     pl.BlockSpec(cos.shape, lambda b: (0, 0)),
                     pl.BlockSpec(sin.shape, lambda b: (0, 0))]
        args += [ck, cv, cos, sin]
    else:
        for _ in range(2):
            spec, shape = _layer_slot(nseq, stack, (seq, GROUP))
            out_specs.append(spec)
            out_shape.append(shape)
    return functools.partial(_diff_kernel, latent, lam_init), in_specs, args, out_specs, out_shape, []


def _mixers(nseq, name, *parts, carried=None):
    n_in = [len(p[1]) for p in parts]
    n_out = [len(p[3]) for p in parts]
    n_scr = [len(p[5]) for p in parts]
    carried = [None] * sum(n_out) if carried is None else list(carried)
    kept = [(o, arr) for o, arr in enumerate(carried) if arr is not None]

    def body(*refs):
        ins = refs[:sum(n_in)]
        outs = refs[sum(n_in) + len(kept):sum(n_in) + len(kept) + sum(n_out)]
        scr = refs[sum(n_in) + len(kept) + sum(n_out):]
        i0 = o0 = s0 = 0
        for part, ni, no, ns in zip(parts, n_in, n_out, n_scr):
            part[0](*ins[i0:i0 + ni], *outs[o0:o0 + no], *scr[s0:s0 + ns])
            i0, o0, s0 = i0 + ni, o0 + no, s0 + ns

    res = pl.pallas_call(
        body,
        grid=(nseq,),
        in_specs=[x for p in parts for x in p[1]] + [pl.BlockSpec(memory_space=pl.ANY)] * len(kept),
        out_specs=[x for p in parts for x in p[3]],
        out_shape=[x for p in parts for x in p[4]],
        scratch_shapes=[x for p in parts for x in p[5]],
        input_output_aliases={sum(n_in) + k: o for k, (o, _) in enumerate(kept)},
        compiler_params=_cparams("arbitrary"),
        name=name,
    )(*[x for p in parts for x in p[2]], *[arr for _, arr in kept])
    out, o0 = [], 0
    for no in n_out:
        out.append(tuple(res[o0:o0 + no]))
        o0 += no
    return out


def _ssd_kernel(has_state, *refs):
    if has_state:
        (d_ref, cw_ref, cb_ref, dtb_ref, alog_ref, dskip_ref, norm_ref, s0_ref,
         o_ref, sfin_ref, xs_scr, bm_scr, cm_scr, xdt_scr, a_scr, st_scr, yf_scr, yb_scr) = refs
    else:
        (d_ref, cw_ref, cb_ref, dtb_ref, alog_ref, dskip_ref, norm_ref,
         o_ref, sfin_ref, xs_scr, bm_scr, cm_scr, xdt_scr, a_scr, st_scr, yf_scr, yb_scr) = refs
    seq = d_ref.shape[0]
    c = SSD_CHUNK
    nchunk = seq // c
    ngrp = 2 * SSD_STATE

    xin = d_ref[:, GROUP:GROUP + 512]
    rows = _iota(xin.shape, 0)
    prev = jnp.where(rows == 0, 0.0, pltpu.roll(xin, 1, 0))
    nxt = jnp.where(rows == seq - 1, 0.0, pltpu.roll(xin, seq - 1, 0))
    cw = cw_ref[...]
    xbc = _silu(cw[0:1] * prev + cw[1:2] * xin + cw[2:3] * nxt + cb_ref[...])
    xs = xbc[:, 0:GROUP]
    xs_scr[...] = xs
    bm_scr[...] = xbc[:, GROUP:GROUP + ngrp]
    cm_scr[...] = xbc[:, GROUP + ngrp:GROUP + 2 * ngrp]
    dt = _softplus(d_ref[:, GROUP + 512:GROUP + 640] + dtb_ref[...])
    a_scr[...] = dt * (-jnp.exp(alog_ref[...]))
    erow = _iota((128, GROUP), 0)
    ehead = _iota((128, GROUP), 1) // HEAD_DIM
    expand = tuple((erow == 4 * d + ehead).astype(F32) for d in range(2))
    for d in range(2):
        xdt_scr[d] = xs * _dot_sel_r(dt, expand[d])
    if has_state:
        st_scr[...] = s0_ref[...]
    else:
        st_scr[...] = jnp.zeros_like(st_scr)

    row = _iota((c, c), 0)
    col = _iota((c, c), 1)
    tri = ((col <= row).astype(F32), (col >= row).astype(F32))
    keep = (col <= row, col >= row)
    grp_lane = _iota((1, ngrp), 1) // SSD_STATE
    valid = (_iota((ngrp, GROUP), 0) // SSD_STATE) == (_iota((ngrp, GROUP), 1) // (2 * HEAD_DIM))

    def chunk_step(d, r0, out_scr):
        a_c = a_scr[pl.ds(r0, c), :]
        bm_c = bm_scr[pl.ds(r0, c), :]
        cm_c = cm_scr[pl.ds(r0, c), :].astype(BF16)
        xdt_c = xdt_scr[d, pl.ds(r0, c), :]
        acum = _dot_sel(tri[d], a_c)
        acum_t = acum.T
        acum_rep = _dot_sel_r(acum, expand[d])
        bm2 = jnp.concatenate([jnp.where(grp_lane == g, bm_c, 0.0) for g in range(2)], axis=0)
        cb = _dot_nt(cm_c, bm2.astype(BF16))
        scores = []
        xparts = []
        for h in range(N_HEADS):
            lane = 4 * d + h
            seg = jnp.exp(jnp.where(keep[d], acum[:, lane:lane + 1] - acum_t[lane:lane + 1, :], -jnp.inf))
            g = h // 2
            scores.append((cb[:, g * c:(g + 1) * c] * seg).astype(BF16))
            xparts.append(jnp.where(_lane_group_mask(GROUP, h * HEAD_DIM, HEAD_DIM), xdt_c, 0.0))
        y = _dot(jnp.concatenate(scores, axis=1), jnp.concatenate(xparts, axis=0).astype(BF16))
        st = st_scr[d]
        y = y + _dot(cm_c, st.astype(BF16)) * jnp.exp(acum_rep)
        out_scr[pl.ds(r0, c), :] = y
        edge = acum_rep[c - 1:c] if d == 0 else acum_rep[0:1]
        xt = (xdt_c * jnp.exp(edge - acum_rep)).astype(BF16)
        upd = _dot_tn(bm_c.astype(BF16), xt)
        st_scr[d] = st * jnp.exp(edge) + jnp.where(valid, upd, 0.0)

    def body(n, carry):
        chunk_step(0, pl.multiple_of(n * c, c), yf_scr)
        chunk_step(1, pl.multiple_of((nchunk - 1 - n) * c, c), yb_scr)
        return carry

    lax.fori_loop(0, nchunk, body, 0)

    y = yf_scr[...] + yb_scr[...] + dskip_ref[...] * xs_scr[...]
    y = y * _silu(d_ref[:, 0:GROUP])
    o_ref[...] = y * lax.rsqrt(jnp.mean(y * y, axis=-1, keepdims=True) + NORM_EPS) * norm_ref[...]
    sfin_ref[...] = st_scr[...]


def _ssd_parts(proj_d, row0, nseq, seq, conv_w, conv_b, dt_bias_p, a_log_p, d_rep, norm, s0, stack=None):
    has_state = s0 is not None
    blk0 = row0 // seq
    ngrp = 2 * SSD_STATE
    small = [conv_w, conv_b, dt_bias_p, a_log_p, d_rep, norm]
    in_specs = ([pl.BlockSpec((seq, PROJ_D), lambda b: (blk0 + b, 0))]
                + [pl.BlockSpec(s.shape, lambda b: (0, 0)) for s in small])
    args = [proj_d] + small
    if has_state:
        in_specs.append(pl.BlockSpec((None, 2, ngrp, GROUP), lambda b: (b, 0, 0, 0)))
        args.append(s0)
    out_specs, out_shape = zip(_layer_slot(nseq, None, (seq, GROUP)),
                               _layer_slot(nseq, stack, (2, ngrp, GROUP)))
    out_specs, out_shape = list(out_specs), list(out_shape)
    scratch = [pltpu.VMEM((seq, GROUP), F32),
               pltpu.VMEM((seq, ngrp), F32),
               pltpu.VMEM((seq, ngrp), F32),
               pltpu.VMEM((2, seq, GROUP), F32),
               pltpu.VMEM((seq, 128), F32),
               pltpu.VMEM((2, ngrp, GROUP), F32),
               pltpu.VMEM((seq, GROUP), F32),
               pltpu.VMEM((seq, GROUP), F32)]
    return functools.partial(_ssd_kernel, has_state), in_specs, args, out_specs, out_shape, scratch


def _outproj_kernel(n_ctx_tiles, *refs):
    ctx_refs, lat_refs = refs[0:4], refs[4:8]
    w_ref, x_ref, m_ref, g_ref, b_ref, o_ref = refs[8:]
    is_ctx = pl.program_id(0) < n_ctx_tiles
    mixed = None
    for i, (c_ref, l_ref) in enumerate(zip(ctx_refs, lat_refs)):
        part = jnp.where(is_ctx, c_ref[...], l_ref[...]).astype(BF16)
        term = _dot(part, w_ref[i * GROUP:(i + 1) * GROUP, :])
        mixed = term if mixed is None else mixed + term
    o_ref[...] = _layernorm(ALPHA * x_ref[...] + m_ref[2] * mixed, g_ref[...], b_ref[...])


def _outproj(parts_ctx, parts_lat, w_out_bf, x, mods_l, ln_g, ln_b, t_ctx, s_lat):
    t, d = x.shape
    tm = 1024
    n_ctx = t_ctx // tm
    n_lat = (t - t_ctx) // tm
    ctx_spec = pl.BlockSpec((tm, GROUP), lambda i: (jnp.minimum(i, n_ctx - 1), 0))
    lat_spec = pl.BlockSpec((tm, GROUP), lambda i: (jnp.clip(i - n_ctx, 0, n_lat - 1), 0))
    return pl.pallas_call(
        functools.partial(_outproj_kernel, n_ctx),
        grid=(t // tm,),
        in_specs=[ctx_spec] * 4 + [lat_spec] * 4
        + [pl.BlockSpec(w_out_bf.shape, lambda i: (0, 0)),
           pl.BlockSpec((tm, d), lambda i: (i, 0)),
           pl.BlockSpec((None, 6, 1, d), _mod_row_map(tm, t_ctx, s_lat)),
           pl.BlockSpec((1, d), lambda i: (0, 0)),
           pl.BlockSpec((1, d), lambda i: (0, 0))],
        out_specs=pl.BlockSpec((tm, d), lambda i: (i, 0)),
        out_shape=jax.ShapeDtypeStruct((t, d), F32),
        compiler_params=_cparams("arbitrary"),
        name="outproj_ln",
    )(*parts_ctx, *parts_lat, w_out_bf, x, mods_l, ln_g, ln_b)


def _top_rows(s, k, extra=()):
    r = s.shape[0]
    rid = _iota(s.shape, 0).astype(F32)
    vals, ids = [], []
    picked = [[] for _ in extra]
    for _ in range(k):
        m = jnp.max(s, axis=0, keepdims=True)
        cand = jnp.where(s == m, rid, float(r))
        i = jnp.min(cand, axis=0, keepdims=True)
        hit = cand == i
        vals.append(m)
        ids.append(i)
        for lst, arr in zip(picked, extra):
            lst.append(jnp.max(jnp.where(hit, arr, -1.0), axis=0, keepdims=True))
        s = jnp.where(hit, -jnp.inf, s)
    cat = lambda xs: jnp.concatenate(xs, axis=0)
    return cat(vals), cat(ids), [cat(p) for p in picked]


def _router_kernel(x_ref, m_ref, wq_ref, keys_ref, h_ref, a_ref, b_ref, g_ref):
    tm = x_ref.shape[0]
    hb = (x_ref[...] * (1.0 + m_ref[4]) + m_ref[3]).astype(BF16)
    h_ref[...] = hb
    qt = _dot(hb, wq_ref[...]).T.astype(BF16)
    k = PEER_TOPK
    code_rows, g_rows = [], []
    for head in range(PEER_HEADS):
        tv, ti = [], []
        for half in range(2):
            g = 2 * head + half
            sc = _dot(keys_ref[g], qt[g * PEER_HALF:(g + 1) * PEER_HALF])
            v, i, _ = _top_rows(sc, k)
            tv.append(v)
            ti.append(i)
        cs = [tv[0][0:1] + tv[1]]
        ca = [jnp.broadcast_to(ti[0][0:1], (k, tm))]
        cb = [ti[1]]
        for k1 in range(1, 4):
            cs.append(tv[0][k1:k1 + 1] + tv[1][0:8])
            ca.append(jnp.broadcast_to(ti[0][k1:k1 + 1], (8, tm)))
            cb.append(ti[1][0:8])
        low = _iota((8, tm), 0) < 4
        v2_dup = jnp.where(low, tv[1][0:8], pltpu.roll(tv[1][0:8], 4, 0))
        i2_dup = jnp.where(low, ti[1][0:8], pltpu.roll(ti[1][0:8], 4, 0))
        for k1 in (4, 6):
            cs.append(jnp.where(low, tv[0][k1:k1 + 1], tv[0][k1 + 1:k1 + 2]) + v2_dup)
            ca.append(jnp.where(low, ti[0][k1:k1 + 1], ti[0][k1 + 1:k1 + 2]))
            cb.append(i2_dup)
        cs.append(tv[0][8:16] + tv[1][0:1])
        ca.append(ti[0][8:16])
        cb.append(jnp.broadcast_to(ti[1][0:1], (8, tm)))
        code = jnp.concatenate(ca, axis=0) * float(PEER_KEYS) + jnp.concatenate(cb, axis=0)
        best, _, (sel_code,) = _top_rows(jnp.concatenate(cs, axis=0), k, extra=(code,))
        e = jnp.exp(best - best[0:1])
        g_rows.append(e / jnp.sum(e, axis=0, keepdims=True))
        code_rows.append(sel_code)
    codes = jnp.concatenate(code_rows, axis=0)
    key1 = jnp.floor(codes * (1.0 / PEER_KEYS))
    a_ref[...] = key1.T.astype(I32)
    b_ref[...] = (codes - key1 * float(PEER_KEYS)).T.astype(I32)
    g_ref[...] = jnp.concatenate(g_rows, axis=0).T


def _router(x, mods_l, wq_bf, keys, t_ctx, s_lat):
    t, d = x.shape
    tm = 256
    nslot = PEER_HEADS * PEER_TOPK
    return pl.pallas_call(
        _router_kernel,
        grid=(t // tm,),
        in_specs=[pl.BlockSpec((tm, d), lambda i: (i, 0)),
                  pl.BlockSpec((None, 6, 1, d), _mod_row_map(tm, t_ctx, s_lat)),
                  pl.BlockSpec(wq_bf.shape, lambda i: (0, 0)),
                  pl.BlockSpec(keys.shape, lambda i: (0, 0, 0))],
        out_specs=[pl.BlockSpec((tm, d), lambda i: (i, 0)),
                   pl.BlockSpec((tm, nslot), lambda i: (i, 0)),
                   pl.BlockSpec((tm, nslot), lambda i: (i, 0)),
                   pl.BlockSpec((tm, nslot), lambda i: (i, 0))],
        out_shape=[jax.ShapeDtypeStruct((t, d), BF16),
                   jax.ShapeDtypeStruct((t, nslot), I32),
                   jax.ShapeDtypeStruct((t, nslot), I32),
                   jax.ShapeDtypeStruct((t, nslot), F32)],
        compiler_params=_cparams("arbitrary"),
        name="peer_router",
    )(x, mods_l, wq_bf, keys)


def _gates_kernel(a_ref, b_ref, g_ref, u_ref, v_ref, o_ref, ub_ref, vb_ref):
    tm = a_ref.shape[0]
    n = PEER_KEYS
    sub = 16
    ub_ref[...] = u_ref[...].astype(BF16)
    vb_ref[...] = v_ref[...].astype(BF16)
    key = _iota((sub, n, a_ref.shape[2]), 1).astype(F32).astype(BF16)
    zero = jnp.zeros((), BF16)
    for t0 in range(0, tm, sub):
        a = a_ref[t0:t0 + sub].astype(F32).astype(BF16)
        b = b_ref[t0:t0 + sub].astype(F32).astype(BF16)
        g = g_ref[t0:t0 + sub].astype(BF16)
        onehot_a = jnp.where(key == a, jnp.ones((), BF16), zero)
        gated_b = jnp.where(key == b, g, zero)
        w = lax.dot_general(onehot_a, gated_b, (((2,), (2,)), ((0,), (0,))),
                            preferred_element_type=F32)
        w_t = jnp.swapaxes(w.astype(BF16), 0, 1)
        for r in range(n):
            o_ref[t0:t0 + sub, r * n:(r + 1) * n] = w_t[r]


def _gates(a_idx, b_idx, gate, peer_u, peer_v, layer):
    t, nslot = a_idx.shape
    d = peer_u.shape[-1]
    tm = 192
    steps = t // tm
    te = N_EXPERTS // steps
    spec = pl.BlockSpec((tm, 1, nslot), lambda i: (i, 0, 0))
    tab_in = pl.BlockSpec((None, te, d), lambda i: (layer, i, 0))
    tab_out = pl.BlockSpec((te, d), lambda i: (i, 0))
    return pl.pallas_call(
        _gates_kernel,
        grid=(steps,),
        in_specs=[spec, spec, spec, tab_in, tab_in],
        out_specs=[pl.BlockSpec((tm, N_EXPERTS), lambda i: (i, 0)), tab_out, tab_out],
        out_shape=[jax.ShapeDtypeStruct((t, N_EXPERTS), BF16),
                   jax.ShapeDtypeStruct((N_EXPERTS, d), BF16),
                   jax.ShapeDtypeStruct((N_EXPERTS, d), BF16)],
        compiler_params=_cparams("arbitrary"),
        name="peer_gates",
    )(a_idx.reshape(t, 1, nslot), b_idx.reshape(t, 1, nslot), gate.reshape(t, 1, nslot), peer_u, peer_v)


def _experts_kernel(h_ref, u_ref, v_ref, w_ref, x_ref, m_ref, g_ref, b_ref, o_ref, acc_ref):
    j = pl.program_id(1)

    @pl.when(j == 0)
    def _():
        acc_ref[...] = jnp.zeros_like(acc_ref)

    act = _gelu_tanh(_dot_nt(h_ref[...], u_ref[...]))
    acc_ref[...] += _dot((act * w_ref[...].astype(F32)).astype(BF16), v_ref[...])

    @pl.when(j == pl.num_programs(1) - 1)
    def _():
        o_ref[...] = _layernorm(ALPHA * x_ref[...] + m_ref[5] * acc_ref[...], g_ref[...], b_ref[...])


def _experts(h_bf, u_bf, v_bf, w_gate, x, mods_l, ln_g, ln_b, t_ctx, s_lat):
    t, d = x.shape
    tm, te = 1024, 1024
    return pl.pallas_call(
        _experts_kernel,
        grid=(t // tm, N_EXPERTS // te),
        in_specs=[pl.BlockSpec((tm, d), lambda i, j: (i, 0)),
                  pl.BlockSpec((te, d), lambda i, j: (j, 0)),
                  pl.BlockSpec((te, d), lambda i, j: (j, 0)),
                  pl.BlockSpec((tm, te), lambda i, j: (i, j)),
                  pl.BlockSpec((tm, d), lambda i, j: (i, 0)),
                  pl.BlockSpec((None, 6, 1, d), _mod_row_map(tm, t_ctx, s_lat)),
                  pl.BlockSpec((1, d), lambda i, j: (0, 0)),
                  pl.BlockSpec((1, d), lambda i, j: (0, 0))],
        out_specs=pl.BlockSpec((tm, d), lambda i, j: (i, 0)),
        out_shape=jax.ShapeDtypeStruct((t, d), F32),
        scratch_shapes=[pltpu.VMEM((tm, d), F32)],
        compiler_params=_cparams("arbitrary", "arbitrary"),
        name="peer_experts",
    )(h_bf, u_bf, v_bf, w_gate, x, mods_l, ln_g, ln_b)


def _pad_cols(w, n):
    return jnp.concatenate([w, jnp.zeros((w.shape[0], n), w.dtype)], axis=1) if n else w


def _layout_w_in(w):
    a = w[:, 0:1280]
    mcq, mckv, mkpe = w[:, 1280:1472], w[:, 1472:1600], w[:, 1600:1632]
    b = jnp.concatenate([mckv, _pad_cols(mcq, 64), mkpe, mkpe, mkpe, mkpe], axis=1)
    c = w[:, 1632:2400]
    d = _pad_cols(w[:, 2400:3176], PROJ_D - 776)
    return jnp.concatenate([a, b, c, d], axis=1).astype(BF16)


def _layout_w_qb(w):
    w4 = w.reshape(MLA_Q_RANK, N_HEADS, MLA_NOPE + MLA_ROPE)
    return jnp.concatenate([w4[:, :, :MLA_NOPE].reshape(MLA_Q_RANK, -1),
                            w4[:, :, MLA_NOPE:].reshape(MLA_Q_RANK, -1)], axis=1).astype(BF16)


def _layout_w_kvb(w):
    w4 = w.reshape(MLA_KV_RANK, N_HEADS, MLA_NOPE + HEAD_DIM)
    return jnp.concatenate([w4[:, :, :MLA_NOPE].reshape(MLA_KV_RANK, -1),
                            w4[:, :, MLA_NOPE:].reshape(MLA_KV_RANK, -1)], axis=1).astype(BF16)


def _rope_tables(seq):
    rows = seq // GRID_W
    row = jnp.repeat(jnp.arange(rows, dtype=F32), GRID_W)
    col = jnp.tile(jnp.arange(GRID_W, dtype=F32), rows)
    freqs = ROPE_BASE ** (-jnp.arange(ROPE_PAIRS, dtype=F32) / ROPE_PAIRS)
    cos_l, sin_l = [], []
    for pos in (row, col):
        ang = pos[:, None] * freqs
        cos_l += [jnp.cos(ang), jnp.cos(ang)]
        sin_l += [-jnp.sin(ang), jnp.sin(ang)]
    return jnp.concatenate(cos_l, axis=1), jnp.concatenate(sin_l, axis=1)


def _hgrn_state_pack(st):
    b = st.shape[0]
    st_t = jnp.swapaxes(st, -1, -2)
    zero = jnp.zeros_like(st_t[:, :, 0])
    rows = [jnp.concatenate([st_t[:, :, h] if g == h else zero for g in range(N_HEADS)], axis=-1)
            for h in range(N_HEADS)]
    return jnp.concatenate(rows, axis=-2).reshape(b, 2, GROUP, GROUP)


def _hgrn_state_unpack(sb):
    blocks = [sb[..., h * HEAD_DIM:(h + 1) * HEAD_DIM, h * HEAD_DIM:(h + 1) * HEAD_DIM]
              for h in range(N_HEADS)]
    return jnp.swapaxes(jnp.stack(blocks, axis=-3), -1, -2)


def _ssd_state_pack(st):
    st_t = jnp.swapaxes(st, -1, -2)
    zero = jnp.zeros_like(st_t[:, :, 0])
    rows = [jnp.concatenate([st_t[:, :, h] if h // 2 == g else zero for h in range(N_HEADS)], axis=-1)
            for g in range(2)]
    return jnp.concatenate(rows, axis=-2)


def _ssd_state_unpack(sb):
    blocks = [sb[..., (h // 2) * SSD_STATE:(h // 2 + 1) * SSD_STATE, h * HEAD_DIM:(h + 1) * HEAD_DIM]
              for h in range(N_HEADS)]
    return jnp.swapaxes(jnp.stack(blocks, axis=-3), -1, -2)


def _tile_lanes(v, n):
    return jnp.tile(v.reshape(1, -1), (1, n))


def kernel(x_prompt, x_sample, cache_mla_ckv, cache_mla_kpe, cache_diff_k, cache_diff_v, state_hgrn, state_ssd, c, c_ctx, w_mod, b_mod, w_in, hgrn_lb, hgrn_norm, mla_q_norm, mla_w_qb, mla_kv_norm, mla_w_kvb, diff_lambda, diff_norm, ssd_conv_w, ssd_conv_b, ssd_dt_bias, ssd_a_log, ssd_d, ssd_norm, w_out, ln1_g, ln1_b, peer_wq, peer_keys, peer_u, peer_v, ln2_g, ln2_b):
    nb, seq, d = x_prompt.shape
    nlat, lseq, _ = x_sample.shape
    depth = w_in.shape[0]
    t_ctx = nb * seq
    x = jnp.concatenate([x_prompt.reshape(t_ctx, d), x_sample.reshape(nlat * lseq, d)], axis=0)

    cond8 = jnp.concatenate([c_ctx.reshape(1, d), c, jnp.zeros((8 - 1 - nlat, d), F32)], axis=0)
    mods = _mods(cond8, w_mod, b_mod)
    mods = mods[:, :1 + nlat].reshape(depth, 1 + nlat, 6, 1, d)

    cos32, sin32 = _rope_tables(lseq)
    cos128, sin128 = jnp.tile(cos32, (1, 4)), jnp.tile(sin32, (1, 4))
    cos256, sin256 = jnp.tile(cos32, (1, 8)), jnp.tile(sin32, (1, 8))

    carried = None
    for l in range(depth):
        mods_l = mods[l]
        pa, pb, pc, pd = _inproj(x, mods_l, _layout_w_in(w_in[l]), t_ctx, lseq)

        norm_hg = _tile_lanes(hgrn_norm[l], N_HEADS)
        mla_w = (mla_q_norm[l].reshape(1, -1), _layout_w_qb(mla_w_qb[l]),
                 mla_kv_norm[l].reshape(1, -1), _layout_w_kvb(mla_w_kvb[l]))
        lam_init = 0.8 - 0.6 * math.exp(-0.3 * l)
        norm_df = _tile_lanes(diff_norm[l], N_HEADS)
        past = cache_diff_k.shape[2]
        ssd_w = (ssd_conv_w[l], ssd_conv_b[l].reshape(1, -1),
                 _pad_cols(ssd_dt_bias[l].reshape(1, -1), 120), _pad_cols(ssd_a_log[l].reshape(1, -1), 120),
                 jnp.repeat(ssd_d[l], HEAD_DIM).reshape(1, -1), ssd_norm[l].reshape(1, -1))

        stack = (l, depth)
        ctx_out = _mixers(
            nb, "mixers_ctx",
            _mla_parts(pb, 0, nb, seq, *mla_w, None, stack),
            _diff_parts(pc, 0, nb, seq, diff_lambda[l], norm_df, lam_init, None, stack),
            _ssd_parts(pd, 0, nb, seq, *ssd_w, None, stack),
            _hgrn_parts(pa, 0, nb, seq, hgrn_lb, norm_hg, None, l, stack),
            carried=carried)
        (mla_ctx, new_ckv, new_kpe), (df_ctx, new_dk, new_dv), (ssd_ctx, ssd_fin), (hg_ctx, hg_fin) = ctx_out
        carried = [None, new_ckv, new_kpe, None, new_dk, new_dv, None, ssd_fin, None, hg_fin]
        ((mla_lat,),) = _mixers(
            nlat, "mla_lat",
            _mla_parts(pb, t_ctx, nlat, lseq, *mla_w,
                       (cache_mla_ckv[:, l], jnp.tile(cache_mla_kpe[:, l], (1, 1, 4)), cos128, sin128)))
        ((df_lat,),) = _mixers(
            nlat, "diffattn_lat",
            _diff_parts(pc, t_ctx, nlat, lseq, diff_lambda[l], norm_df, lam_init,
                        (cache_diff_k[:, l].reshape(nlat, past, GROUP),
                         cache_diff_v[:, l].reshape(nlat, past, GROUP), cos256, sin256)))
        ((ssd_lat, _),) = _mixers(nlat, "ssd_lat",
                                  _ssd_parts(pd, t_ctx, nlat, lseq, *ssd_w, _ssd_state_pack(state_ssd[:, l])))
        ((hg_lat, _),) = _mixers(nlat, "hgrn_lat",
                                 _hgrn_parts(pa, t_ctx, nlat, lseq, hgrn_lb, norm_hg,
                                             _hgrn_state_pack(state_hgrn[:, l]), l))

        parts_ctx = [a.reshape(t_ctx, GROUP) for a in (hg_ctx, mla_ctx, df_ctx, ssd_ctx)]
        parts_lat = [a.reshape(nlat * lseq, GROUP) for a in (hg_lat, mla_lat, df_lat, ssd_lat)]
        x = _outproj(parts_ctx, parts_lat, w_out[l].astype(BF16), x, mods_l,
                     ln1_g[l].reshape(1, d), ln1_b[l].reshape(1, d), t_ctx, lseq)

        keys = peer_keys[l].reshape(2 * PEER_HEADS, PEER_KEYS, PEER_HALF).astype(BF16)
        h_bf, a_idx, b_idx, gate = _router(x, mods_l, peer_wq[l].astype(BF16), keys, t_ctx, lseq)
        w_gate, u_bf, v_bf = _gates(a_idx, b_idx, gate, peer_u, peer_v, l)
        x = _experts(h_bf, u_bf, v_bf, w_gate, x, mods_l,
                     ln2_g[l].reshape(1, d), ln2_b[l].reshape(1, d), t_ctx, lseq)

    y_prompt = x[:t_ctx].reshape(nb, seq, d)
    y_sample = x[t_ctx:].reshape(nlat, lseq, d)
    return (y_prompt, y_sample, new_ckv, new_kpe,
            new_dk.reshape(nb, depth, seq, N_HEADS, 2, DIFF_DIM),
            new_dv.reshape(nb, depth, seq, N_HEADS, 2 * DIFF_DIM),
            _hgrn_state_unpack(hg_fin), _ssd_state_unpack(ssd_fin))
```

```python
import functools
import math

import jax
import jax.numpy as jnp
from jax import lax
from jax.experimental import pallas as pl
from jax.experimental.pallas import tpu as pltpu

F32 = jnp.float32
BF16 = jnp.bfloat16
I32 = jnp.int32

D_MODEL = 1024
GROUP = 256
N_HEADS = 4
HEAD_DIM = 64
HG_BLOCK = 16
HG_SLAB = 256
HG_UNROLL = 8
SSD_CHUNK = 128
SSD_STATE = 64
MLA_Q_RANK = 192
MLA_KV_RANK = 128
MLA_NOPE = 64
MLA_ROPE = 32
DIFF_DIM = 32
GRID_W = 64
ROPE_PAIRS = 8
ROPE_BASE = 10000.0
PEER_HEADS = 8
PEER_KEYS = 128
PEER_TOPK = 16
PEER_HALF = 64
N_EXPERTS = PEER_KEYS * PEER_KEYS
NORM_EPS = 1e-6
LN_EPS = 1e-5
DEPTH = 2
ALPHA = (2.0 * DEPTH) ** 0.25
LOG2_E = 1.4426950408889634

PROJ_A = 5 * GROUP
PROJ_B = 512
PROJ_C = 3 * GROUP
PROJ_D = 896
VMEM_LIMIT = 56 * 1024 * 1024


def _cparams(*sem):
    return pltpu.CompilerParams(dimension_semantics=sem, vmem_limit_bytes=VMEM_LIMIT)


def _sigmoid(x):
    return 1.0 / (1.0 + jnp.exp(-x))


def _silu(x):
    return x * _sigmoid(x)


def _softplus(x):
    return jnp.maximum(x, 0.0) + jnp.log(1.0 + jnp.exp(-jnp.abs(x)))


def _gelu_tanh(x):
    return 0.5 * x * (1.0 + jnp.tanh(math.sqrt(2.0 / math.pi) * (x + 0.044715 * (x * x * x))))


def _dot(a, b, precision=None):
    return jnp.dot(a, b, preferred_element_type=F32, precision=precision)


def _dot_nt(a, b, precision=None):
    return lax.dot_general(a, b, (((1,), (1,)), ((), ())), preferred_element_type=F32,
                           precision=precision)


def _dot_tn(a, b, precision=None):
    return lax.dot_general(a, b, (((0,), (0,)), ((), ())), preferred_element_type=F32,
                           precision=precision)


def _split3(x):
    hi = x.astype(BF16)
    rest = x - hi.astype(F32)
    mid = rest.astype(BF16)
    lo = (rest - mid.astype(F32)).astype(BF16)
    return hi, mid, lo


def _dot_sel(sel, x):
    sel = sel.astype(BF16)
    hi, mid, lo = _split3(x)
    return _dot(sel, hi) + _dot(sel, mid) + _dot(sel, lo)


def _dot_sel_r(x, sel):
    sel = sel.astype(BF16)
    hi, mid, lo = _split3(x)
    return _dot(hi, sel) + _dot(mid, sel) + _dot(lo, sel)


def _iota(shape, dim):
    return lax.broadcasted_iota(I32, shape, dim)


def _block_mask(rows, cols, rblk, cblk):
    return (_iota((rows, cols), 0) // rblk) == (_iota((rows, cols), 1) // cblk)


def _lane_group_mask(width, start, size):
    lane = _iota((1, width), 1)
    return (lane >= start) & (lane < start + size)


def _layernorm(v, g, b):
    mu = jnp.mean(v, axis=-1, keepdims=True)
    d = v - mu
    var = jnp.mean(d * d, axis=-1, keepdims=True)
    return d * lax.rsqrt(var + LN_EPS) * g + b


def _swap_halves16(x):
    width = x.shape[-1]
    lane = _iota(x.shape, x.ndim - 1)
    up = pltpu.roll(x, width - 8, x.ndim - 1)
    down = pltpu.roll(x, 8, x.ndim - 1)
    return jnp.where((lane % 16) < 8, up, down)


def _rope(x, cos, sin_signed):
    return x * cos + _swap_halves16(x) * sin_signed


def _mods_kernel(c_ref, w_ref, b_ref, o_ref):
    s = _silu(c_ref[...]).astype(BF16)
    o_ref[...] = _dot(s, w_ref[...].astype(BF16)) + b_ref[...]


def _mods(cond8, w_mod, b_mod):
    depth, d, n = w_mod.shape
    tn = 3072
    return pl.pallas_call(
        _mods_kernel,
        grid=(depth, n // tn),
        in_specs=[pl.BlockSpec((8, d), lambda l, j: (0, 0)),
                  pl.BlockSpec((None, d, tn), lambda l, j: (l, 0, j)),
                  pl.BlockSpec((None, 1, tn), lambda l, j: (l, 0, j))],
        out_specs=pl.BlockSpec((None, 8, tn), lambda l, j: (l, 0, j)),
        out_shape=jax.ShapeDtypeStruct((depth, 8, n), F32),
        compiler_params=_cparams("arbitrary", "arbitrary"),
        name="mods",
    )(cond8, w_mod, b_mod.reshape(depth, 1, n))


def _mod_row_map(tm, t_ctx, s_lat):
    def index_map(i, *_):
        start = i * tm
        return (jnp.where(start < t_ctx, 0, 1 + (start - t_ctx) // s_lat), 0, 0, 0)
    return index_map


def _inproj_kernel(x_ref, m_ref, w_ref, oa_ref, ob_ref, oc_ref, od_ref):
    h = (x_ref[...] * (1.0 + m_ref[1]) + m_ref[0]).astype(BF16)
    start = 0
    for o_ref in (oa_ref, ob_ref, oc_ref, od_ref):
        width = o_ref.shape[-1]
        o_ref[...] = _dot(h, w_ref[:, start:start + width])
        start += width


def _inproj(x, mods_l, w_in_p, t_ctx, s_lat):
    t, d = x.shape
    tm = 512
    widths = (PROJ_A, PROJ_B, PROJ_C, PROJ_D)
    return pl.pallas_call(
        _inproj_kernel,
        grid=(t // tm,),
        in_specs=[pl.BlockSpec((tm, d), lambda i: (i, 0)),
                  pl.BlockSpec((None, 6, 1, d), _mod_row_map(tm, t_ctx, s_lat)),
                  pl.BlockSpec(w_in_p.shape, lambda i: (0, 0))],
        out_specs=[pl.BlockSpec((tm, w), lambda i: (i, 0)) for w in widths],
        out_shape=[jax.ShapeDtypeStruct((t, w), F32) for w in widths],
        compiler_params=_cparams("arbitrary"),
        name="inproj",
    )(x, mods_l, w_in_p)


def _hgrn_kernel(layer, has_state, *refs):
    if has_state:
        (a_ref, lb_ref, norm_ref, s0_ref, o_ref, sfin_ref,
         q_scr, k_scr, bc_scr, dec_scr, qt_scr, kt_scr, st_scr, o_scr) = refs
    else:
        (a_ref, lb_ref, norm_ref, o_ref, sfin_ref,
         q_scr, k_scr, bc_scr, dec_scr, qt_scr, kt_scr, st_scr, o_scr) = refs
        s0_ref = None
    seq = a_ref.shape[0]
    c = HG_BLOCK
    nblk = seq // c
    slab = HG_SLAB
    nb = slab // c

    lbp = lb_ref[...]
    e = jnp.exp(lbp - jnp.max(lbp, axis=0, keepdims=True))
    p = e / jnp.sum(e, axis=0, keepdims=True)
    lower = jnp.sum(p[1:layer + 1], axis=0) if layer > 0 else jnp.zeros_like(p[0])

    q = _silu(a_ref[:, 0:GROUP])
    q_scr[...] = q
    srow = _iota((slab, slab), 0)
    scol = _iota((slab, slab), 1)
    same = (srow // c) == (scol // c)
    cum_op = (jnp.where(same & (scol <= srow), 1.0, 0.0), jnp.where(same & (scol >= srow), 1.0, 0.0))
    for d in range(2):
        lb = lower[d:d + 1]
        f = lb + (1.0 - lb) * _sigmoid(a_ref[:, (1 + d) * GROUP:(2 + d) * GROUP])
        k = 1.0 - f
        lf = jnp.log(f)
        for s0 in range(0, seq, slab):
            bc = _dot_sel(cum_op[d], lf[s0:s0 + slab])
            bc3 = bc.reshape(nb, c, GROUP)
            edge = bc3[:, c - 1:c, :] if d == 0 else bc3[:, 0:1, :]
            tot = jnp.broadcast_to(edge, (nb, c, GROUP)).reshape(slab, GROUP)
            bc_scr[d, s0:s0 + slab, :] = bc * LOG2_E
            k_scr[d, s0:s0 + slab, :] = (bc - jnp.log(k[s0:s0 + slab])) * LOG2_E
            dec_scr[d, s0:s0 + slab, :] = jnp.exp(tot)
            qt_scr[d, s0:s0 + slab, :] = (q[s0:s0 + slab] * jnp.exp(bc)).astype(BF16)
            kt_scr[d, s0:s0 + slab, :] = (k[s0:s0 + slab] * jnp.exp(tot - bc)).astype(BF16)
    if has_state:
        st_scr[...] = s0_ref[...]
    else:
        st_scr[...] = jnp.zeros_like(st_scr)

    bd_ones = _block_mask(GROUP, GROUP, HEAD_DIM, HEAD_DIM).astype(BF16)
    rib = _iota((1, c, GROUP), 1)

    def slab_step(i, carry):
        r0 = pl.multiple_of(i * slab, slab)
        q3 = q_scr[pl.ds(r0, slab), :].reshape(nb, c, GROUP)
        v3 = a_ref[pl.ds(r0, slab), 3 * GROUP:4 * GROUP].reshape(nb, c, GROUP)
        o3 = jnp.zeros((nb, c, GROUP), F32)
        for d in range(2):
            bc3 = bc_scr[d, pl.ds(r0, slab), :].reshape(nb, c, GROUP)
            c3 = k_scr[d, pl.ds(r0, slab), :].reshape(nb, c, GROUP)
            for j in range(c):
                keep = (rib >= j) if d == 0 else (rib <= j)
                dec = jnp.exp2(jnp.where(keep, bc3 - c3[:, j:j + 1, :], -jnp.inf))
                pj = (dec * q3).astype(BF16).reshape(slab, GROUP)
                srep = _dot(pj, bd_ones).reshape(nb, c, GROUP)
                o3 = o3 + srep * v3[:, j:j + 1, :]
        o_scr[0, pl.ds(r0, slab), :] = o3.reshape(slab, GROUP)
        return carry

    lax.fori_loop(0, seq // slab, slab_step, 0)

    bd_mask = _block_mask(GROUP, GROUP, HEAD_DIM, HEAD_DIM)

    def body(n, carry):
        rows = [[pl.multiple_of(((n * HG_UNROLL + u) if d == 0 else nblk - 1 - (n * HG_UNROLL + u)) * c, c)
                 for u in range(HG_UNROLL)] for d in range(2)]
        upd = [[_dot_tn(a_ref[pl.ds(r0, c), 3 * GROUP:4 * GROUP].astype(BF16), kt_scr[d, pl.ds(r0, c), :])
                for r0 in rows[d]] for d in range(2)]
        for d in range(2):
            st = st_scr[d]
            for u, r0 in enumerate(rows[d]):
                o_scr[1 + d, pl.ds(r0, c), :] = _dot_nt(qt_scr[d, pl.ds(r0, c), :], st.astype(BF16))
                st = st * dec_scr[d, pl.ds(r0, 1), :] + jnp.where(bd_mask, upd[d][u], 0.0)
            st_scr[d] = st
        return carry

    lax.fori_loop(0, nblk // HG_UNROLL, body, 0)

    o = o_scr[0] + o_scr[1] + o_scr[2]
    mean_op = jnp.where(_block_mask(GROUP, GROUP, HEAD_DIM, HEAD_DIM), 1.0 / HEAD_DIM, 0.0)
    ms = _dot_sel_r(o * o, mean_op)
    y = o * lax.rsqrt(ms + NORM_EPS) * norm_ref[...]
    o_ref[...] = y * _silu(a_ref[:, 4 * GROUP:5 * GROUP])
    sfin_ref[...] = st_scr[...]


def _layer_slot(nseq, stack, tail):
    zeros = (0,) * len(tail)
    if stack is None:
        return (pl.BlockSpec((None,) + tail, lambda b: (b,) + zeros),
                jax.ShapeDtypeStruct((nseq,) + tail, F32))
    layer, depth = stack
    return (pl.BlockSpec((None, None) + tail, lambda b: (b, layer) + zeros),
            jax.ShapeDtypeStruct((nseq, depth) + tail, F32))


def _hgrn_parts(proj_a, row0, nseq, seq, hgrn_lb, norm_t, s0, layer, stack=None):
    has_state = s0 is not None
    blk0 = row0 // seq
    in_specs = [pl.BlockSpec((seq, PROJ_A), lambda b: (blk0 + b, 0)),
                pl.BlockSpec(hgrn_lb.shape, lambda b: (0, 0, 0)),
                pl.BlockSpec((1, GROUP), lambda b: (0, 0))]
    args = [proj_a, hgrn_lb, norm_t]
    if has_state:
        in_specs.append(pl.BlockSpec((None, 2, GROUP, GROUP), lambda b: (b, 0, 0, 0)))
        args.append(s0)
    out_specs, out_shape = zip(_layer_slot(nseq, None, (seq, GROUP)),
                               _layer_slot(nseq, stack, (2, GROUP, GROUP)))
    out_specs, out_shape = list(out_specs), list(out_shape)
    scratch = [pltpu.VMEM((seq, GROUP), F32),
               pltpu.VMEM((2, seq, GROUP), F32),
               pltpu.VMEM((2, seq, GROUP), F32),
               pltpu.VMEM((2, seq, GROUP), F32),
               pltpu.VMEM((2, seq, GROUP), BF16),
               pltpu.VMEM((2, seq, GROUP), BF16),
               pltpu.VMEM((2, GROUP, GROUP), F32),
               pltpu.VMEM((3, seq, GROUP), F32)]
    return functools.partial(_hgrn_kernel, layer, has_state), in_specs, args, out_specs, out_shape, scratch


def _mla_kernel(latent, *refs):
    if latent:
        (b_ref, qn_ref, wq_ref, kvn_ref, wkv_ref, cckv_ref, ckpe_ref, cos_ref, sin_ref,
         o_ref) = refs
    else:
        (b_ref, qn_ref, wq_ref, kvn_ref, wkv_ref, o_ref, ckv_ref, kpe_ref) = refs
    seq = b_ref.shape[0]
    mckv = b_ref[:, 0:MLA_KV_RANK]
    mcq = b_ref[:, MLA_KV_RANK:MLA_KV_RANK + MLA_Q_RANK]
    kpe_t = b_ref[:, 384:512]

    cq = mcq * lax.rsqrt(jnp.mean(mcq * mcq, axis=-1, keepdims=True) + NORM_EPS) * qn_ref[...]
    qf = _dot(cq.astype(BF16), wq_ref[...])
    ckv = mckv * lax.rsqrt(jnp.mean(mckv * mckv, axis=-1, keepdims=True) + NORM_EPS) * kvn_ref[...]
    q_nope = qf[:, 0:N_HEADS * MLA_NOPE]
    q_rope = qf[:, N_HEADS * MLA_NOPE:]
    if latent:
        cos = cos_ref[...]
        sin = sin_ref[...]
        q_rope = _rope(q_rope, cos, sin)
        ckv_all = jnp.concatenate([cckv_ref[...], ckv], axis=0)
        kpe_all = jnp.concatenate([ckpe_ref[...], _rope(kpe_t, cos, sin)], axis=0)
    else:
        ckv_ref[...] = ckv
        kpe_ref[...] = kpe_t[:, 0:MLA_ROPE]
        ckv_all = ckv
        kpe_all = kpe_t
    kv = _dot(ckv_all.astype(BF16), wkv_ref[...])
    kcat = jnp.concatenate([kv[:, 0:GROUP], kpe_all], axis=1).astype(BF16)
    v = kv[:, GROUP:].astype(BF16)
    qcat = jnp.concatenate([q_nope, q_rope], axis=1)
    scale = (MLA_NOPE + MLA_ROPE) ** -0.5
    qb = min(seq, 256)
    width = qcat.shape[1]
    for r0 in range(0, seq, qb):
        qblk = qcat[r0:r0 + qb]
        acc = jnp.zeros((qb, GROUP), F32)
        for h in range(N_HEADS):
            hm = (_lane_group_mask(width, h * MLA_NOPE, MLA_NOPE)
                  | _lane_group_mask(width, N_HEADS * MLA_NOPE + h * MLA_ROPE, MLA_ROPE))
            s = _dot_nt(jnp.where(hm, qblk, 0.0).astype(BF16), kcat)
            e = jnp.exp2((s - jnp.max(s, axis=-1, keepdims=True)) * (scale * LOG2_E))
            z = jnp.sum(e, axis=-1, keepdims=True)
            oh = _dot(e.astype(BF16), v) / z
            acc = acc + jnp.where(_lane_group_mask(GROUP, h * HEAD_DIM, HEAD_DIM), oh, 0.0)
        o_ref[r0:r0 + qb, :] = acc


def _mla_parts(proj_b, row0, nseq, seq, q_norm, w_qb_p, kv_norm, w_kvb_p, latent_args, stack=None):
    latent = latent_args is not None
    blk0 = row0 // seq
    in_specs = [pl.BlockSpec((seq, PROJ_B), lambda b: (blk0 + b, 0)),
                pl.BlockSpec(q_norm.shape, lambda b: (0, 0)),
                pl.BlockSpec(w_qb_p.shape, lambda b: (0, 0)),
                pl.BlockSpec(kv_norm.shape, lambda b: (0, 0)),
                pl.BlockSpec(w_kvb_p.shape, lambda b: (0, 0))]
    args = [proj_b, q_norm, w_qb_p, kv_norm, w_kvb_p]
    out_specs = [pl.BlockSpec((None, seq, GROUP), lambda b: (b, 0, 0))]
    out_shape = [jax.ShapeDtypeStruct((nseq, seq, GROUP), F32)]
    if latent:
        cckv, ckpe_t, cos, sin = latent_args
        past = cckv.shape[1]
        in_specs += [pl.BlockSpec((None, past, MLA_KV_RANK), lambda b: (b, 0, 0)),
                     pl.BlockSpec((None, past, 128), lambda b: (b, 0, 0)),
                     pl.BlockSpec(cos.shape, lambda b: (0, 0)),
                     pl.BlockSpec(sin.shape, lambda b: (0, 0))]
        args += [cckv, ckpe_t, cos, sin]
    else:
        for width in (MLA_KV_RANK, MLA_ROPE):
            spec, shape = _layer_slot(nseq, stack, (seq, width))
            out_specs.append(spec)
            out_shape.append(shape)
    return functools.partial(_mla_kernel, latent), in_specs, args, out_specs, out_shape, []


def _diff_kernel(latent, lam_init, *refs):
    if latent:
        (c_ref, lam_ref, norm_ref, ck_ref, cv_ref, cos_ref, sin_ref, o_ref) = refs
    else:
        (c_ref, lam_ref, norm_ref, o_ref, k_ref, v_ref) = refs
    seq = c_ref.shape[0]
    dq = c_ref[:, 0:GROUP]
    dk = c_ref[:, GROUP:2 * GROUP]
    dv = c_ref[:, 2 * GROUP:3 * GROUP]
    if latent:
        cos = cos_ref[...]
        sin = sin_ref[...]
        dq = _rope(dq, cos, sin)
        k_all = jnp.concatenate([ck_ref[...], _rope(dk, cos, sin)], axis=0)
        v_all = jnp.concatenate([cv_ref[...], dv], axis=0)
    else:
        k_ref[...] = dk
        v_ref[...] = dv
        k_all = dk
        v_all = dv
    lv = lam_ref[...]
    lam = (jnp.exp(jnp.sum(lv[0:1] * lv[1:2], axis=-1, keepdims=True))
           - jnp.exp(jnp.sum(lv[2:3] * lv[3:4], axis=-1, keepdims=True)) + lam_init)
    k_bf = k_all.astype(BF16)
    v_bf = v_all.astype(BF16)
    scale = DIFF_DIM ** -0.5
    mean_op = jnp.where(_block_mask(GROUP, GROUP, HEAD_DIM, HEAD_DIM), 1.0 / HEAD_DIM, 0.0)
    qb = min(seq, 256)
    for r0 in range(0, seq, qb):
        qblk = dq[r0:r0 + qb]
        acc = jnp.zeros((qb, GROUP), F32)
        for h in range(N_HEADS):
            outs = []
            for comp in range(2):
                cm = _lane_group_mask(GROUP, h * HEAD_DIM + comp * DIFF_DIM, DIFF_DIM)
                s = _dot_nt(jnp.where(cm, qblk, 0.0).astype(BF16), k_bf)
                e = jnp.exp2((s - jnp.max(s, axis=-1, keepdims=True)) * (scale * LOG2_E))
                outs.append(_dot(e.astype(BF16), v_bf) / jnp.sum(e, axis=-1, keepdims=True))
            oh = outs[0] - lam * outs[1]
            acc = acc + jnp.where(_lane_group_mask(GROUP, h * HEAD_DIM, HEAD_DIM), oh, 0.0)
        ms = _dot_sel_r(acc * acc, mean_op)
        o_ref[r0:r0 + qb, :] = acc * lax.rsqrt(ms + NORM_EPS) * norm_ref[...] * (1.0 - lam_init)


def _diff_parts(proj_c, row0, nseq, seq, lam_p, norm_t, lam_init, latent_args, stack=None):
    latent = latent_args is not None
    blk0 = row0 // seq
    in_specs = [pl.BlockSpec((seq, PROJ_C), lambda b: (blk0 + b, 0)),
                pl.BlockSpec(lam_p.shape, lambda b: (0, 0)),
                pl.BlockSpec(norm_t.shape, lambda b: (0, 0))]
    args = [proj_c, lam_p, norm_t]
    out_specs = [pl.BlockSpec((None, seq, GROUP), lambda b: (b, 0, 0))]
    out_shape = [jax.ShapeDtypeStruct((nseq, seq, GROUP), F32)]
    if latent:
        ck, cv, cos, sin = latent_args
        past = ck.shape[1]
        in_specs += [pl.BlockSpec((None, past, GROUP), lambda b: (b, 0, 0)),
                     pl.BlockSpec((None, past, GROUP), lambda b: (b, 0, 0)),
                     pl.BlockSpec(cos.shape, lambda b: (0, 0)),
                     pl.BlockSpec(sin.shape, lambda b: (0, 0))]
        args += [ck, cv, cos, sin]
    else:
        for _ in range(2):
            spec, shape = _layer_slot(nseq, stack, (seq, GROUP))
            out_specs.append(spec)
            out_shape.append(shape)
    return functools.partial(_diff_kernel, latent, lam_init), in_specs, args, out_specs, out_shape, []


def _mixers(nseq, name, *parts, carried=None):
    n_in = [len(p[1]) for p in parts]
    n_out = [len(p[3]) for p in parts]
    n_scr = [len(p[5]) for p in parts]
    carried = [None] * sum(n_out) if carried is None else list(carried)
    kept = [(o, arr) for o, arr in enumerate(carried) if arr is not None]

    def body(*refs):
        ins = refs[:sum(n_in)]
        outs = refs[sum(n_in) + len(kept):sum(n_in) + len(kept) + sum(n_out)]
        scr = refs[sum(n_in) + len(kept) + sum(n_out):]
        i0 = o0 = s0 = 0
        for part, ni, no, ns in zip(parts, n_in, n_out, n_scr):
            part[0](*ins[i0:i0 + ni], *outs[o0:o0 + no], *scr[s0:s0 + ns])
            i0, o0, s0 = i0 + ni, o0 + no, s0 + ns

    res = pl.pallas_call(
        body,
        grid=(nseq,),
        in_specs=[x for p in parts for x in p[1]] + [pl.BlockSpec(memory_space=pl.ANY)] * len(kept),
        out_specs=[x for p in parts for x in p[3]],
        out_shape=[x for p in parts for x in p[4]],
        scratch_shapes=[x for p in parts for x in p[5]],
        input_output_aliases={sum(n_in) + k: o for k, (o, _) in enumerate(kept)},
        compiler_params=_cparams("arbitrary"),
        name=name,
    )(*[x for p in parts for x in p[2]], *[arr for _, arr in kept])
    out, o0 = [], 0
    for no in n_out:
        out.append(tuple(res[o0:o0 + no]))
        o0 += no
    return out


def _ssd_kernel(has_state, *refs):
    if has_state:
        (d_ref, cw_ref, cb_ref, dtb_ref, alog_ref, dskip_ref, norm_ref, s0_ref,
         o_ref, sfin_ref, xs_scr, bm_scr, cm_scr, xdt_scr, a_scr, st_scr, yf_scr, yb_scr) = refs
    else:
        (d_ref, cw_ref, cb_ref, dtb_ref, alog_ref, dskip_ref, norm_ref,
         o_ref, sfin_ref, xs_scr, bm_scr, cm_scr, xdt_scr, a_scr, st_scr, yf_scr, yb_scr) = refs
    seq = d_ref.shape[0]
    c = SSD_CHUNK
    nchunk = seq // c
    ngrp = 2 * SSD_STATE

    xin = d_ref[:, GROUP:GROUP + 512]
    rows = _iota(xin.shape, 0)
    prev = jnp.where(rows == 0, 0.0, pltpu.roll(xin, 1, 0))
    nxt = jnp.where(rows == seq - 1, 0.0, pltpu.roll(xin, seq - 1, 0))
    cw = cw_ref[...]
    xbc = _silu(cw[0:1] * prev + cw[1:2] * xin + cw[2:3] * nxt + cb_ref[...])
    xs = xbc[:, 0:GROUP]
    xs_scr[...] = xs
    bm_scr[...] = xbc[:, GROUP:GROUP + ngrp]
    cm_scr[...] = xbc[:, GROUP + ngrp:GROUP + 2 * ngrp]
    dt = _softplus(d_ref[:, GROUP + 512:GROUP + 640] + dtb_ref[...])
    a_scr[...] = dt * (-jnp.exp(alog_ref[...]))
    erow = _iota((128, GROUP), 0)
    ehead = _iota((128, GROUP), 1) // HEAD_DIM
    expand = tuple((erow == 4 * d + ehead).astype(F32) for d in range(2))
    for d in range(2):
        xdt_scr[d] = xs * _dot_sel_r(dt, expand[d])
    if has_state:
        st_scr[...] = s0_ref[...]
    else:
        st_scr[...] = jnp.zeros_like(st_scr)

    row = _iota((c, c), 0)
    col = _iota((c, c), 1)
    tri = ((col <= row).astype(F32), (col >= row).astype(F32))
    keep = (col <= row, col >= row)
    grp_lane = _iota((1, ngrp), 1) // SSD_STATE
    valid = (_iota((ngrp, GROUP), 0) // SSD_STATE) == (_iota((ngrp, GROUP), 1) // (2 * HEAD_DIM))

    def chunk_step(d, r0, out_scr):
        a_c = a_scr[pl.ds(r0, c), :]
        bm_c = bm_scr[pl.ds(r0, c), :]
        cm_c = cm_scr[pl.ds(r0, c), :].astype(BF16)
        xdt_c = xdt_scr[d, pl.ds(r0, c), :]
        acum = _dot_sel(tri[d], a_c)
        acum_t = acum.T
        acum_rep = _dot_sel_r(acum, expand[d])
        bm2 = jnp.concatenate([jnp.where(grp_lane == g, bm_c, 0.0) for g in range(2)], axis=0)
        cb = _dot_nt(cm_c, bm2.astype(BF16))
        scores = []
        xparts = []
        for h in range(N_HEADS):
            lane = 4 * d + h
            seg = jnp.exp(jnp.where(keep[d], acum[:, lane:lane + 1] - acum_t[lane:lane + 1, :], -jnp.inf))
            g = h // 2
            scores.append((cb[:, g * c:(g + 1) * c] * seg).astype(BF16))
            xparts.append(jnp.where(_lane_group_mask(GROUP, h * HEAD_DIM, HEAD_DIM), xdt_c, 0.0))
        y = _dot(jnp.concatenate(scores, axis=1), jnp.concatenate(xparts, axis=0).astype(BF16))
        st = st_scr[d]
        y = y + _dot(cm_c, st.astype(BF16)) * jnp.exp(acum_rep)
        out_scr[pl.ds(r0, c), :] = y
        edge = acum_rep[c - 1:c] if d == 0 else acum_rep[0:1]
        xt = (xdt_c * jnp.exp(edge - acum_rep)).astype(BF16)
        upd = _dot_tn(bm_c.astype(BF16), xt)
        st_scr[d] = st * jnp.exp(edge) + jnp.where(valid, upd, 0.0)

    def body(n, carry):
        chunk_step(0, pl.multiple_of(n * c, c), yf_scr)
        chunk_step(1, pl.multiple_of((nchunk - 1 - n) * c, c), yb_scr)
        return carry

    lax.fori_loop(0, nchunk, body, 0)

    y = yf_scr[...] + yb_scr[...] + dskip_ref[...] * xs_scr[...]
    y = y * _silu(d_ref[:, 0:GROUP])
    o_ref[...] = y * lax.rsqrt(jnp.mean(y * y, axis=-1, keepdims=True) + NORM_EPS) * norm_ref[...]
    sfin_ref[...] = st_scr[...]


def _ssd_parts(proj_d, row0, nseq, seq, conv_w, conv_b, dt_bias_p, a_log_p, d_rep, norm, s0, stack=None):
    has_state = s0 is not None
    blk0 = row0 // seq
    ngrp = 2 * SSD_STATE
    small = [conv_w, conv_b, dt_bias_p, a_log_p, d_rep, norm]
    in_specs = ([pl.BlockSpec((seq, PROJ_D), lambda b: (blk0 + b, 0))]
                + [pl.BlockSpec(s.shape, lambda b: (0, 0)) for s in small])
    args = [proj_d] + small
    if has_state:
        in_specs.append(pl.BlockSpec((None, 2, ngrp, GROUP), lambda b: (b, 0, 0, 0)))
        args.append(s0)
    out_specs, out_shape = zip(_layer_slot(nseq, None, (seq, GROUP)),
                               _layer_slot(nseq, stack, (2, ngrp, GROUP)))
    out_specs, out_shape = list(out_specs), list(out_shape)
    scratch = [pltpu.VMEM((seq, GROUP), F32),
               pltpu.VMEM((seq, ngrp), F32),
               pltpu.VMEM((seq, ngrp), F32),
               pltpu.VMEM((2, seq, GROUP), F32),
               pltpu.VMEM((seq, 128), F32),
               pltpu.VMEM((2, ngrp, GROUP), F32),
               pltpu.VMEM((seq, GROUP), F32),
               pltpu.VMEM((seq, GROUP), F32)]
    return functools.partial(_ssd_kernel, has_state), in_specs, args, out_specs, out_shape, scratch


def _outproj_kernel(n_ctx_tiles, *refs):
    ctx_refs, lat_refs = refs[0:4], refs[4:8]
    w_ref, x_ref, m_ref, g_ref, b_ref, o_ref = refs[8:]
    is_ctx = pl.program_id(0) < n_ctx_tiles
    mixed = None
    for i, (c_ref, l_ref) in enumerate(zip(ctx_refs, lat_refs)):
        part = jnp.where(is_ctx, c_ref[...], l_ref[...]).astype(BF16)
        term = _dot(part, w_ref[i * GROUP:(i + 1) * GROUP, :])
        mixed = term if mixed is None else mixed + term
    o_ref[...] = _layernorm(ALPHA * x_ref[...] + m_ref[2] * mixed, g_ref[...], b_ref[...])


def _outproj(parts_ctx, parts_lat, w_out_bf, x, mods_l, ln_g, ln_b, t_ctx, s_lat):
    t, d = x.shape
    tm = 1024
    n_ctx = t_ctx // tm
    n_lat = (t - t_ctx) // tm
    ctx_spec = pl.BlockSpec((tm, GROUP), lambda i: (jnp.minimum(i, n_ctx - 1), 0))
    lat_spec = pl.BlockSpec((tm, GROUP), lambda i: (jnp.clip(i - n_ctx, 0, n_lat - 1), 0))
    return pl.pallas_call(
        functools.partial(_outproj_kernel, n_ctx),
        grid=(t // tm,),
        in_specs=[ctx_spec] * 4 + [lat_spec] * 4
        + [pl.BlockSpec(w_out_bf.shape, lambda i: (0, 0)),
           pl.BlockSpec((tm, d), lambda i: (i, 0)),
           pl.BlockSpec((None, 6, 1, d), _mod_row_map(tm, t_ctx, s_lat)),
           pl.BlockSpec((1, d), lambda i: (0, 0)),
           pl.BlockSpec((1, d), lambda i: (0, 0))],
        out_specs=pl.BlockSpec((tm, d), lambda i: (i, 0)),
        out_shape=jax.ShapeDtypeStruct((t, d), F32),
        compiler_params=_cparams("arbitrary"),
        name="outproj_ln",
    )(*parts_ctx, *parts_lat, w_out_bf, x, mods_l, ln_g, ln_b)


def _top_rows(s, k, extra=()):
    r = s.shape[0]
    rid = _iota(s.shape, 0).astype(F32)
    vals, ids = [], []
    picked = [[] for _ in extra]
    for _ in range(k):
        m = jnp.max(s, axis=0, keepdims=True)
        cand = jnp.where(s == m, rid, float(r))
        i = jnp.min(cand, axis=0, keepdims=True)
        hit = cand == i
        vals.append(m)
        ids.append(i)
        for lst, arr in zip(picked, extra):
            lst.append(jnp.max(jnp.where(hit, arr, -1.0), axis=0, keepdims=True))
        s = jnp.where(hit, -jnp.inf, s)
    cat = lambda xs: jnp.concatenate(xs, axis=0)
    return cat(vals), cat(ids), [cat(p) for p in picked]


def _router_kernel(x_ref, m_ref, wq_ref, keys_ref, h_ref, a_ref, b_ref, g_ref):
    tm = x_ref.shape[0]
    hb = (x_ref[...] * (1.0 + m_ref[4]) + m_ref[3]).astype(BF16)
    h_ref[...] = hb
    qt = _dot(hb, wq_ref[...]).T.astype(BF16)
    k = PEER_TOPK
    code_rows, g_rows = [], []
    for head in range(PEER_HEADS):
        tv, ti = [], []
        for half in range(2):
            g = 2 * head + half
            sc = _dot(keys_ref[g], qt[g * PEER_HALF:(g + 1) * PEER_HALF])
            v, i, _ = _top_rows(sc, k)
            tv.append(v)
            ti.append(i)
        cs = [tv[0][0:1] + tv[1]]
        ca = [jnp.broadcast_to(ti[0][0:1], (k, tm))]
        cb = [ti[1]]
        for k1 in range(1, 4):
            cs.append(tv[0][k1:k1 + 1] + tv[1][0:8])
            ca.append(jnp.broadcast_to(ti[0][k1:k1 + 1], (8, tm)))
            cb.append(ti[1][0:8])
        low = _iota((8, tm), 0) < 4
        v2_dup = jnp.where(low, tv[1][0:8], pltpu.roll(tv[1][0:8], 4, 0))
        i2_dup = jnp.where(low, ti[1][0:8], pltpu.roll(ti[1][0:8], 4, 0))
        for k1 in (4, 6):
            cs.append(jnp.where(low, tv[0][k1:k1 + 1], tv[0][k1 + 1:k1 + 2]) + v2_dup)
            ca.append(jnp.where(low, ti[0][k1:k1 + 1], ti[0][k1 + 1:k1 + 2]))
            cb.append(i2_dup)
        cs.append(tv[0][8:16] + tv[1][0:1])
        ca.append(ti[0][8:16])
        cb.append(jnp.broadcast_to(ti[1][0:1], (8, tm)))
        code = jnp.concatenate(ca, axis=0) * float(PEER_KEYS) + jnp.concatenate(cb, axis=0)
        best, _, (sel_code,) = _top_rows(jnp.concatenate(cs, axis=0), k, extra=(code,))
        e = jnp.exp(best - best[0:1])
        g_rows.append(e / jnp.sum(e, axis=0, keepdims=True))
        code_rows.append(sel_code)
    codes = jnp.concatenate(code_rows, axis=0)
    key1 = jnp.floor(codes * (1.0 / PEER_KEYS))
    a_ref[...] = key1.T.astype(I32)
    b_ref[...] = (codes - key1 * float(PEER_KEYS)).T.astype(I32)
    g_ref[...] = jnp.concatenate(g_rows, axis=0).T


def _router(x, mods_l, wq_bf, keys, t_ctx, s_lat):
    t, d = x.shape
    tm = 256
    nslot = PEER_HEADS * PEER_TOPK
    return pl.pallas_call(
        _router_kernel,
        grid=(t // tm,),
        in_specs=[pl.BlockSpec((tm, d), lambda i: (i, 0)),
                  pl.BlockSpec((None, 6, 1, d), _mod_row_map(tm, t_ctx, s_lat)),
                  pl.BlockSpec(wq_bf.shape, lambda i: (0, 0)),
                  pl.BlockSpec(keys.shape, lambda i: (0, 0, 0))],
        out_specs=[pl.BlockSpec((tm, d), lambda i: (i, 0)),
                   pl.BlockSpec((tm, nslot), lambda i: (i, 0)),
                   pl.BlockSpec((tm, nslot), lambda i: (i, 0)),
                   pl.BlockSpec((tm, nslot), lambda i: (i, 0))],
        out_shape=[jax.ShapeDtypeStruct((t, d), BF16),
                   jax.ShapeDtypeStruct((t, nslot), I32),
                   jax.ShapeDtypeStruct((t, nslot), I32),
                   jax.ShapeDtypeStruct((t, nslot), F32)],
        compiler_params=_cparams("arbitrary"),
        name="peer_router",
    )(x, mods_l, wq_bf, keys)


def _gates_kernel(a_ref, b_ref, g_ref, u_ref, v_ref, o_ref, ub_ref, vb_ref):
    tm = a_ref.shape[0]
    n = PEER_KEYS
    sub = 16
    ub_ref[...] = u_ref[...].astype(BF16)
    vb_ref[...] = v_ref[...].astype(BF16)
    key = _iota((sub, n, a_ref.shape[2]), 1).astype(F32).astype(BF16)
    zero = jnp.zeros((), BF16)
    for t0 in range(0, tm, sub):
        a = a_ref[t0:t0 + sub].astype(F32).astype(BF16)
        b = b_ref[t0:t0 + sub].astype(F32).astype(BF16)
        g = g_ref[t0:t0 + sub].astype(BF16)
        onehot_a = jnp.where(key == a, jnp.ones((), BF16), zero)
        gated_b = jnp.where(key == b, g, zero)
        w = lax.dot_general(onehot_a, gated_b, (((2,), (2,)), ((0,), (0,))),
                            preferred_element_type=F32)
        w_t = jnp.swapaxes(w.astype(BF16), 0, 1)
        for r in range(n):
            o_ref[t0:t0 + sub, r * n:(r + 1) * n] = w_t[r]


def _gates(a_idx, b_idx, gate, peer_u, peer_v, layer):
    t, nslot = a_idx.shape
    d = peer_u.shape[-1]
    tm = 192
    steps = t // tm
    te = N_EXPERTS // steps
    spec = pl.BlockSpec((tm, 1, nslot), lambda i: (i, 0, 0))
    tab_in = pl.BlockSpec((None, te, d), lambda i: (layer, i, 0))
    tab_out = pl.BlockSpec((te, d), lambda i: (i, 0))
    return pl.pallas_call(
        _gates_kernel,
        grid=(steps,),
        in_specs=[spec, spec, spec, tab_in, tab_in],
        out_specs=[pl.BlockSpec((tm, N_EXPERTS), lambda i: (i, 0)), tab_out, tab_out],
        out_shape=[jax.ShapeDtypeStruct((t, N_EXPERTS), BF16),
                   jax.ShapeDtypeStruct((N_EXPERTS, d), BF16),
                   jax.ShapeDtypeStruct((N_EXPERTS, d), BF16)],
        compiler_params=_cparams("arbitrary"),
        name="peer_gates",
    )(a_idx.reshape(t, 1, nslot), b_idx.reshape(t, 1, nslot), gate.reshape(t, 1, nslot), peer_u, peer_v)


def _experts_kernel(h_ref, u_ref, v_ref, w_ref, x_ref, m_ref, g_ref, b_ref, o_ref, acc_ref):
    j = pl.program_id(1)

    @pl.when(j == 0)
    def _():
        acc_ref[...] = jnp.zeros_like(acc_ref)

    act = _gelu_tanh(_dot_nt(h_ref[...], u_ref[...]))
    acc_ref[...] += _dot((act * w_ref[...].astype(F32)).astype(BF16), v_ref[...])

    @pl.when(j == pl.num_programs(1) - 1)
    def _():
        o_ref[...] = _layernorm(ALPHA * x_ref[...] + m_ref[5] * acc_ref[...], g_ref[...], b_ref[...])


def _experts(h_bf, u_bf, v_bf, w_gate, x, mods_l, ln_g, ln_b, t_ctx, s_lat):
    t, d = x.shape
    tm, te = 1024, 1024
    return pl.pallas_call(
        _experts_kernel,
        grid=(t // tm, N_EXPERTS // te),
        in_specs=[pl.BlockSpec((tm, d), lambda i, j: (i, 0)),
                  pl.BlockSpec((te, d), lambda i, j: (j, 0)),
                  pl.BlockSpec((te, d), lambda i, j: (j, 0)),
                  pl.BlockSpec((tm, te), lambda i, j: (i, j)),
                  pl.BlockSpec((tm, d), lambda i, j: (i, 0)),
                  pl.BlockSpec((None, 6, 1, d), _mod_row_map(tm, t_ctx, s_lat)),
                  pl.BlockSpec((1, d), lambda i, j: (0, 0)),
                  pl.BlockSpec((1, d), lambda i, j: (0, 0))],
        out_specs=pl.BlockSpec((tm, d), lambda i, j: (i, 0)),
        out_shape=jax.ShapeDtypeStruct((t, d), F32),
        scratch_shapes=[pltpu.VMEM((tm, d), F32)],
        compiler_params=_cparams("arbitrary", "arbitrary"),
        name="peer_experts",
    )(h_bf, u_bf, v_bf, w_gate, x, mods_l, ln_g, ln_b)


def _pad_cols(w, n):
    return jnp.concatenate([w, jnp.zeros((w.shape[0], n), w.dtype)], axis=1) if n else w


def _layout_w_in(w):
    a = w[:, 0:1280]
    mcq, mckv, mkpe = w[:, 1280:1472], w[:, 1472:1600], w[:, 1600:1632]
    b = jnp.concatenate([mckv, _pad_cols(mcq, 64), mkpe, mkpe, mkpe, mkpe], axis=1)
    c = w[:, 1632:2400]
    d = _pad_cols(w[:, 2400:3176], PROJ_D - 776)
    return jnp.concatenate([a, b, c, d], axis=1).astype(BF16)


def _layout_w_qb(w):
    w4 = w.reshape(MLA_Q_RANK, N_HEADS, MLA_NOPE + MLA_ROPE)
    return jnp.concatenate([w4[:, :, :MLA_NOPE].reshape(MLA_Q_RANK, -1),
                            w4[:, :, MLA_NOPE:].reshape(MLA_Q_RANK, -1)], axis=1).astype(BF16)


def _layout_w_kvb(w):
    w4 = w.reshape(MLA_KV_RANK, N_HEADS, MLA_NOPE + HEAD_DIM)
    return jnp.concatenate([w4[:, :, :MLA_NOPE].reshape(MLA_KV_RANK, -1),
                            w4[:, :, MLA_NOPE:].reshape(MLA_KV_RANK, -1)], axis=1).astype(BF16)


def _rope_tables(seq):
    rows = seq // GRID_W
    row = jnp.repeat(jnp.arange(rows, dtype=F32), GRID_W)
    col = jnp.tile(jnp.arange(GRID_W, dtype=F32), rows)
    freqs = ROPE_BASE ** (-jnp.arange(ROPE_PAIRS, dtype=F32) / ROPE_PAIRS)
    cos_l, sin_l = [], []
    for pos in (row, col):
        ang = pos[:, None] * freqs
        cos_l += [jnp.cos(ang), jnp.cos(ang)]
        sin_l += [-jnp.sin(ang), jnp.sin(ang)]
    return jnp.concatenate(cos_l, axis=1), jnp.concatenate(sin_l, axis=1)


def _hgrn_state_pack(st):
    b = st.shape[0]
    st_t = jnp.swapaxes(st, -1, -2)
    zero = jnp.zeros_like(st_t[:, :, 0])
    rows = [jnp.concatenate([st_t[:, :, h] if g == h else zero for g in range(N_HEADS)], axis=-1)
            for h in range(N_HEADS)]
    return jnp.concatenate(rows, axis=-2).reshape(b, 2, GROUP, GROUP)


def _hgrn_state_unpack(sb):
    blocks = [sb[..., h * HEAD_DIM:(h + 1) * HEAD_DIM, h * HEAD_DIM:(h + 1) * HEAD_DIM]
              for h in range(N_HEADS)]
    return jnp.swapaxes(jnp.stack(blocks, axis=-3), -1, -2)


def _ssd_state_pack(st):
    st_t = jnp.swapaxes(st, -1, -2)
    zero = jnp.zeros_like(st_t[:, :, 0])
    rows = [jnp.concatenate([st_t[:, :, h] if h // 2 == g else zero for h in range(N_HEADS)], axis=-1)
            for g in range(2)]
    return jnp.concatenate(rows, axis=-2)


def _ssd_state_unpack(sb):
    blocks = [sb[..., (h // 2) * SSD_STATE:(h // 2 + 1) * SSD_STATE, h * HEAD_DIM:(h + 1) * HEAD_DIM]
              for h in range(N_HEADS)]
    return jnp.swapaxes(jnp.stack(blocks, axis=-3), -1, -2)


def _tile_lanes(v, n):
    return jnp.tile(v.reshape(1, -1), (1, n))


def kernel(x_prompt, x_sample, cache_mla_ckv, cache_mla_kpe, cache_diff_k, cache_diff_v, state_hgrn, state_ssd, c, c_ctx, w_mod, b_mod, w_in, hgrn_lb, hgrn_norm, mla_q_norm, mla_w_qb, mla_kv_norm, mla_w_kvb, diff_lambda, diff_norm, ssd_conv_w, ssd_conv_b, ssd_dt_bias, ssd_a_log, ssd_d, ssd_norm, w_out, ln1_g, ln1_b, peer_wq, peer_keys, peer_u, peer_v, ln2_g, ln2_b):
    nb, seq, d = x_prompt.shape
    nlat, lseq, _ = x_sample.shape
    depth = w_in.shape[0]
    t_ctx = nb * seq
    x = jnp.concatenate([x_prompt.reshape(t_ctx, d), x_sample.reshape(nlat * lseq, d)], axis=0)

    cond8 = jnp.concatenate([c_ctx.reshape(1, d), c, jnp.zeros((8 - 1 - nlat, d), F32)], axis=0)
    mods = _mods(cond8, w_mod, b_mod)
    mods = mods[:, :1 + nlat].reshape(depth, 1 + nlat, 6, 1, d)

    cos32, sin32 = _rope_tables(lseq)
    cos128, sin128 = jnp.tile(cos32, (1, 4)), jnp.tile(sin32, (1, 4))
    cos256, sin256 = jnp.tile(cos32, (1, 8)), jnp.tile(sin32, (1, 8))

    carried = [None if tail is None else jnp.zeros((nb, depth) + tail, F32)
               for tail in (None, (seq, MLA_KV_RANK), (seq, MLA_ROPE), None, (seq, GROUP), (seq, GROUP),
                            None, (2, 2 * SSD_STATE, GROUP), None, (2, GROUP, GROUP))]
    for l in range(depth):
        mods_l = mods[l]
        pa, pb, pc, pd = _inproj(x, mods_l, _layout_w_in(w_in[l]), t_ctx, lseq)

        norm_hg = _tile_lanes(hgrn_norm[l], N_HEADS)
        mla_w = (mla_q_norm[l].reshape(1, -1), _layout_w_qb(mla_w_qb[l]),
                 mla_kv_norm[l].reshape(1, -1), _layout_w_kvb(mla_w_kvb[l]))
        lam_init = 0.8 - 0.6 * math.exp(-0.3 * l)
        norm_df = _tile_lanes(diff_norm[l], N_HEADS)
        past = cache_diff_k.shape[2]
        ssd_w = (ssd_conv_w[l], ssd_conv_b[l].reshape(1, -1),
                 _pad_cols(ssd_dt_bias[l].reshape(1, -1), 120), _pad_cols(ssd_a_log[l].reshape(1, -1), 120),
                 jnp.repeat(ssd_d[l], HEAD_DIM).reshape(1, -1), ssd_norm[l].reshape(1, -1))

        stack = (l, depth)
        ctx_out = _mixers(
            nb, "mixers_ctx",
            _mla_parts(pb, 0, nb, seq, *mla_w, None, stack),
            _diff_parts(pc, 0, nb, seq, diff_lambda[l], norm_df, lam_init, None, stack),
            _ssd_parts(pd, 0, nb, seq, *ssd_w, None, stack),
            _hgrn_parts(pa, 0, nb, seq, hgrn_lb, norm_hg, None, l, stack),
            carried=carried)
        (mla_ctx, new_ckv, new_kpe), (df_ctx, new_dk, new_dv), (ssd_ctx, ssd_fin), (hg_ctx, hg_fin) = ctx_out
        carried = [None, new_ckv, new_kpe, None, new_dk, new_dv, None, ssd_fin, None, hg_fin]
        ((mla_lat,),) = _mixers(
            nlat, "mla_lat",
            _mla_parts(pb, t_ctx, nlat, lseq, *mla_w,
                       (cache_mla_ckv[:, l], jnp.tile(cache_mla_kpe[:, l], (1, 1, 4)), cos128, sin128)))
        ((df_lat,),) = _mixers(
            nlat, "diffattn_lat",
            _diff_parts(pc, t_ctx, nlat, lseq, diff_lambda[l], norm_df, lam_init,
                        (cache_diff_k[:, l].reshape(nlat, past, GROUP),
                         cache_diff_v[:, l].reshape(nlat, past, GROUP), cos256, sin256)))
        ((ssd_lat, _),) = _mixers(nlat, "ssd_lat",
                                  _ssd_parts(pd, t_ctx, nlat, lseq, *ssd_w, _ssd_state_pack(state_ssd[:, l])))
        ((hg_lat, _),) = _mixers(nlat, "hgrn_lat",
                                 _hgrn_parts(pa, t_ctx, nlat, lseq, hgrn_lb, norm_hg,
                                             _hgrn_state_pack(state_hgrn[:, l]), l))

        parts_ctx = [a.reshape(t_ctx, GROUP) for a in (hg_ctx, mla_ctx, df_ctx, ssd_ctx)]
        parts_lat = [a.reshape(nlat * lseq, GROUP) for a in (hg_lat, mla_lat, df_lat, ssd_lat)]
        x = _outproj(parts_ctx, parts_lat, w_out[l].astype(BF16), x, mods_l,
                     ln1_g[l].reshape(1, d), ln1_b[l].reshape(1, d), t_ctx, lseq)

        keys = peer_keys[l].reshape(2 * PEER_HEADS, PEER_KEYS, PEER_HALF).astype(BF16)
        h_bf, a_idx, b_idx, gate = _router(x, mods_l, peer_wq[l].astype(BF16), keys, t_ctx, lseq)
        w_gate, u_bf, v_bf = _gates(a_idx, b_idx, gate, peer_u, peer_v, l)
        x = _experts(h_bf, u_bf, v_bf, w_gate, x, mods_l,
                     ln2_g[l].reshape(1, d), ln2_b[l].reshape(1, d), t_ctx, lseq)

    y_prompt = x[:t_ctx].reshape(nb, seq, d)
    y_sample = x[t_ctx:].reshape(nlat, lseq, d)
    return (y_prompt, y_sample, new_ckv, new_kpe,
            new_dk.reshape(nb, depth, seq, N_HEADS, 2, DIFF_DIM),
            new_dv.reshape(nb, depth, seq, N_HEADS, 2 * DIFF_DIM),
            _hgrn_state_unpack(hg_fin), _ssd_state_unpack(ssd_fin))
```

```python
import functools
import math

import jax
import jax.numpy as jnp
from jax import lax
from jax.experimental import pallas as pl
from jax.experimental.pallas import tpu as pltpu

F32 = jnp.float32
BF16 = jnp.bfloat16
I32 = jnp.int32

D_MODEL = 1024
GROUP = 256
N_HEADS = 4
HEAD_DIM = 64
HG_BLOCK = 16
HG_SLAB = 256
HG_UNROLL = 8
SSD_CHUNK = 128
SSD_STATE = 64
MLA_Q_RANK = 192
MLA_KV_RANK = 128
MLA_NOPE = 64
MLA_ROPE = 32
DIFF_DIM = 32
GRID_W = 64
ROPE_PAIRS = 8
ROPE_BASE = 10000.0
PEER_HEADS = 8
PEER_KEYS = 128
PEER_TOPK = 16
PEER_HALF = 64
N_EXPERTS = PEER_KEYS * PEER_KEYS
NORM_EPS = 1e-6
LN_EPS = 1e-5
DEPTH = 2
ALPHA = (2.0 * DEPTH) ** 0.25
LOG2_E = 1.4426950408889634

PROJ_A = 5 * GROUP
PROJ_B = 512
PROJ_C = 3 * GROUP
PROJ_D = 896
VMEM_LIMIT = 56 * 1024 * 1024


def _cparams(*sem):
    return pltpu.CompilerParams(dimension_semantics=sem, vmem_limit_bytes=VMEM_LIMIT)


def _sigmoid(x):
    return 1.0 / (1.0 + jnp.exp(-x))


def _silu(x):
    return x * _sigmoid(x)


def _softplus(x):
    return jnp.maximum(x, 0.0) + jnp.log(1.0 + jnp.exp(-jnp.abs(x)))


def _gelu_tanh(x):
    return 0.5 * x * (1.0 + jnp.tanh(math.sqrt(2.0 / math.pi) * (x + 0.044715 * (x * x * x))))


def _dot(a, b, precision=None):
    return jnp.dot(a, b, preferred_element_type=F32, precision=precision)


def _dot_nt(a, b, precision=None):
    return lax.dot_general(a, b, (((1,), (1,)), ((), ())), preferred_element_type=F32,
                           precision=precision)


def _dot_tn(a, b, precision=None):
    return lax.dot_general(a, b, (((0,), (0,)), ((), ())), preferred_element_type=F32,
                           precision=precision)


def _split3(x):
    hi = x.astype(BF16)
    rest = x - hi.astype(F32)
    mid = rest.astype(BF16)
    lo = (rest - mid.astype(F32)).astype(BF16)
    return hi, mid, lo


def _dot_sel(sel, x):
    sel = sel.astype(BF16)
    hi, mid, lo = _split3(x)
    return _dot(sel, hi) + _dot(sel, mid) + _dot(sel, lo)


def _dot_sel_r(x, sel):
    sel = sel.astype(BF16)
    hi, mid, lo = _split3(x)
    return _dot(hi, sel) + _dot(mid, sel) + _dot(lo, sel)


def _iota(shape, dim):
    return lax.broadcasted_iota(I32, shape, dim)


def _block_mask(rows, cols, rblk, cblk):
    return (_iota((rows, cols), 0) // rblk) == (_iota((rows, cols), 1) // cblk)


def _lane_group_mask(width, start, size):
    lane = _iota((1, width), 1)
    return (lane >= start) & (lane < start + size)


def _layernorm(v, g, b):
    mu = jnp.mean(v, axis=-1, keepdims=True)
    d = v - mu
    var = jnp.mean(d * d, axis=-1, keepdims=True)
    return d * lax.rsqrt(var + LN_EPS) * g + b


def _swap_halves16(x):
    width = x.shape[-1]
    lane = _iota(x.shape, x.ndim - 1)
    up = pltpu.roll(x, width - 8, x.ndim - 1)
    down = pltpu.roll(x, 8, x.ndim - 1)
    return jnp.where((lane % 16) < 8, up, down)


def _rope(x, cos, sin_signed):
    return x * cos + _swap_halves16(x) * sin_signed


def _mods_kernel(c_ref, w_ref, b_ref, o_ref):
    s = _silu(c_ref[...]).astype(BF16)
    o_ref[...] = _dot(s, w_ref[...].astype(BF16)) + b_ref[...]


def _mods(cond8, w_mod, b_mod):
    depth, d, n = w_mod.shape
    tn = 3072
    return pl.pallas_call(
        _mods_kernel,
        grid=(depth, n // tn),
        in_specs=[pl.BlockSpec((8, d), lambda l, j: (0, 0)),
                  pl.BlockSpec((None, d, tn), lambda l, j: (l, 0, j)),
                  pl.BlockSpec((None, 1, tn), lambda l, j: (l, 0, j))],
        out_specs=pl.BlockSpec((None, 8, tn), lambda l, j: (l, 0, j)),
        out_shape=jax.ShapeDtypeStruct((depth, 8, n), F32),
        compiler_params=_cparams("arbitrary", "arbitrary"),
        name="mods",
    )(cond8, w_mod, b_mod.reshape(depth, 1, n))


def _mod_row_map(tm, t_ctx, s_lat):
    def index_map(i, *_):
        start = i * tm
        return (jnp.where(start < t_ctx, 0, 1 + (start - t_ctx) // s_lat), 0, 0, 0)
    return index_map


def _inproj_kernel(x_ref, m_ref, w_ref, oa_ref, ob_ref, oc_ref, od_ref):
    h = (x_ref[...] * (1.0 + m_ref[1]) + m_ref[0]).astype(BF16)
    start = 0
    for o_ref in (oa_ref, ob_ref, oc_ref, od_ref):
        width = o_ref.shape[-1]
        o_ref[...] = _dot(h, w_ref[:, start:start + width])
        start += width


def _inproj(x, mods_l, w_in_p, t_ctx, s_lat):
    t, d = x.shape
    tm = 512
    widths = (PROJ_A, PROJ_B, PROJ_C, PROJ_D)
    return pl.pallas_call(
        _inproj_kernel,
        grid=(t // tm,),
        in_specs=[pl.BlockSpec((tm, d), lambda i: (i, 0)),
                  pl.BlockSpec((None, 6, 1, d), _mod_row_map(tm, t_ctx, s_lat)),
                  pl.BlockSpec(w_in_p.shape, lambda i: (0, 0))],
        out_specs=[pl.BlockSpec((tm, w), lambda i: (i, 0)) for w in widths],
        out_shape=[jax.ShapeDtypeStruct((t, w), F32) for w in widths],
        compiler_params=_cparams("arbitrary"),
        name="inproj",
    )(x, mods_l, w_in_p)


def _hgrn_kernel(layer, has_state, *refs):
    if has_state:
        (a_ref, lb_ref, norm_ref, s0_ref, o_ref, sfin_ref,
         q_scr, k_scr, bc_scr, dec_scr, qt_scr, kt_scr, st_scr, o_scr) = refs
    else:
        (a_ref, lb_ref, norm_ref, o_ref, sfin_ref,
         q_scr, k_scr, bc_scr, dec_scr, qt_scr, kt_scr, st_scr, o_scr) = refs
        s0_ref = None
    seq = a_ref.shape[0]
    c = HG_BLOCK
    nblk = seq // c
    slab = HG_SLAB
    nb = slab // c

    lbp = lb_ref[...]
    e = jnp.exp(lbp - jnp.max(lbp, axis=0, keepdims=True))
    p = e / jnp.sum(e, axis=0, keepdims=True)
    lower = jnp.sum(p[1:layer + 1], axis=0) if layer > 0 else jnp.zeros_like(p[0])

    q = _silu(a_ref[:, 0:GROUP])
    q_scr[...] = q
    srow = _iota((slab, slab), 0)
    scol = _iota((slab, slab), 1)
    same = (srow // c) == (scol // c)
    cum_op = (jnp.where(same & (scol <= srow), 1.0, 0.0), jnp.where(same & (scol >= srow), 1.0, 0.0))
    for d in range(2):
        lb = lower[d:d + 1]
        f = lb + (1.0 - lb) * _sigmoid(a_ref[:, (1 + d) * GROUP:(2 + d) * GROUP])
        k = 1.0 - f
        lf = jnp.log(f)
        for s0 in range(0, seq, slab):
            bc = _dot_sel(cum_op[d], lf[s0:s0 + slab])
            bc3 = bc.reshape(nb, c, GROUP)
            edge = bc3[:, c - 1:c, :] if d == 0 else bc3[:, 0:1, :]
            tot = jnp.broadcast_to(edge, (nb, c, GROUP)).reshape(slab, GROUP)
            bc_scr[d, s0:s0 + slab, :] = bc * LOG2_E
            k_scr[d, s0:s0 + slab, :] = (bc - jnp.log(k[s0:s0 + slab])) * LOG2_E
            dec_scr[d, s0:s0 + slab, :] = jnp.exp(tot)
            qt_scr[d, s0:s0 + slab, :] = (q[s0:s0 + slab] * jnp.exp(bc)).astype(BF16)
            kt_scr[d, s0:s0 + slab, :] = (k[s0:s0 + slab] * jnp.exp(tot - bc)).astype(BF16)
    if has_state:
        st_scr[...] = s0_ref[...]
    else:
        st_scr[...] = jnp.zeros_like(st_scr)

    bd_ones = _block_mask(GROUP, GROUP, HEAD_DIM, HEAD_DIM).astype(BF16)
    rib = _iota((1, c, GROUP), 1)

    def slab_step(i, carry):
        r0 = pl.multiple_of(i * slab, slab)
        q3 = q_scr[pl.ds(r0, slab), :].reshape(nb, c, GROUP)
        v3 = a_ref[pl.ds(r0, slab), 3 * GROUP:4 * GROUP].reshape(nb, c, GROUP)
        o3 = jnp.zeros((nb, c, GROUP), F32)
        for d in range(2):
            bc3 = bc_scr[d, pl.ds(r0, slab), :].reshape(nb, c, GROUP)
            c3 = k_scr[d, pl.ds(r0, slab), :].reshape(nb, c, GROUP)
            for j in range(c):
                keep = (rib >= j) if d == 0 else (rib <= j)
                dec = jnp.exp2(jnp.where(keep, bc3 - c3[:, j:j + 1, :], -jnp.inf))
                pj = (dec * q3).astype(BF16).reshape(slab, GROUP)
                srep = _dot(pj, bd_ones).reshape(nb, c, GROUP)
                o3 = o3 + srep * v3[:, j:j + 1, :]
        o_scr[0, pl.ds(r0, slab), :] = o3.reshape(slab, GROUP)
        return carry

    lax.fori_loop(0, seq // slab, slab_step, 0)

    bd_mask = _block_mask(GROUP, GROUP, HEAD_DIM, HEAD_DIM)

    def body(n, carry):
        rows = [[pl.multiple_of(((n * HG_UNROLL + u) if d == 0 else nblk - 1 - (n * HG_UNROLL + u)) * c, c)
                 for u in range(HG_UNROLL)] for d in range(2)]
        upd = [[_dot_tn(a_ref[pl.ds(r0, c), 3 * GROUP:4 * GROUP].astype(BF16), kt_scr[d, pl.ds(r0, c), :])
                for r0 in rows[d]] for d in range(2)]
        for d in range(2):
            st = st_scr[d]
            for u, r0 in enumerate(rows[d]):
                o_scr[1 + d, pl.ds(r0, c), :] = _dot_nt(qt_scr[d, pl.ds(r0, c), :], st.astype(BF16))
                st = st * dec_scr[d, pl.ds(r0, 1), :] + jnp.where(bd_mask, upd[d][u], 0.0)
            st_scr[d] = st
        return carry

    lax.fori_loop(0, nblk // HG_UNROLL, body, 0)

    o = o_scr[0] + o_scr[1] + o_scr[2]
    mean_op = jnp.where(_block_mask(GROUP, GROUP, HEAD_DIM, HEAD_DIM), 1.0 / HEAD_DIM, 0.0)
    ms = _dot_sel_r(o * o, mean_op)
    y = o * lax.rsqrt(ms + NORM_EPS) * norm_ref[...]
    o_ref[...] = y * _silu(a_ref[:, 4 * GROUP:5 * GROUP])
    sfin_ref[...] = st_scr[...]


def _layer_slot(nseq, stack, tail):
    zeros = (0,) * len(tail)
    if stack is None:
        return (pl.BlockSpec((None,) + tail, lambda b: (b,) + zeros),
                jax.ShapeDtypeStruct((nseq,) + tail, F32))
    layer, depth, creating = stack
    shape = jax.ShapeDtypeStruct((nseq, depth) + tail, F32)
    if creating:
        return pl.BlockSpec((None, depth) + tail, lambda b: (b, 0) + zeros), shape
    return pl.BlockSpec((None, None) + tail, lambda b: (b, layer) + zeros), shape


def _hgrn_parts(proj_a, row0, nseq, seq, hgrn_lb, norm_t, s0, layer, stack=None):
    has_state = s0 is not None
    blk0 = row0 // seq
    in_specs = [pl.BlockSpec((seq, PROJ_A), lambda b: (blk0 + b, 0)),
                pl.BlockSpec(hgrn_lb.shape, lambda b: (0, 0, 0)),
                pl.BlockSpec((1, GROUP), lambda b: (0, 0))]
    args = [proj_a, hgrn_lb, norm_t]
    if has_state:
        in_specs.append(pl.BlockSpec((None, 2, GROUP, GROUP), lambda b: (b, 0, 0, 0)))
        args.append(s0)
    out_specs, out_shape = zip(_layer_slot(nseq, None, (seq, GROUP)),
                               _layer_slot(nseq, stack, (2, GROUP, GROUP)))
    out_specs, out_shape = list(out_specs), list(out_shape)
    scratch = [pltpu.VMEM((seq, GROUP), F32),
               pltpu.VMEM((2, seq, GROUP), F32),
               pltpu.VMEM((2, seq, GROUP), F32),
               pltpu.VMEM((2, seq, GROUP), F32),
               pltpu.VMEM((2, seq, GROUP), BF16),
               pltpu.VMEM((2, seq, GROUP), BF16),
               pltpu.VMEM((2, GROUP, GROUP), F32),
               pltpu.VMEM((3, seq, GROUP), F32)]
    return functools.partial(_hgrn_kernel, layer, has_state), in_specs, args, out_specs, out_shape, scratch


def _mla_kernel(latent, *refs):
    if latent:
        (b_ref, qn_ref, wq_ref, kvn_ref, wkv_ref, cckv_ref, ckpe_ref, cos_ref, sin_ref,
         o_ref) = refs
    else:
        (b_ref, qn_ref, wq_ref, kvn_ref, wkv_ref, o_ref, ckv_ref, kpe_ref) = refs
    seq = b_ref.shape[0]
    mckv = b_ref[:, 0:MLA_KV_RANK]
    mcq = b_ref[:, MLA_KV_RANK:MLA_KV_RANK + MLA_Q_RANK]
    kpe_t = b_ref[:, 384:512]

    cq = mcq * lax.rsqrt(jnp.mean(mcq * mcq, axis=-1, keepdims=True) + NORM_EPS) * qn_ref[...]
    qf = _dot(cq.astype(BF16), wq_ref[...])
    ckv = mckv * lax.rsqrt(jnp.mean(mckv * mckv, axis=-1, keepdims=True) + NORM_EPS) * kvn_ref[...]
    q_nope = qf[:, 0:N_HEADS * MLA_NOPE]
    q_rope = qf[:, N_HEADS * MLA_NOPE:]
    if latent:
        cos = cos_ref[...]
        sin = sin_ref[...]
        q_rope = _rope(q_rope, cos, sin)
        ckv_all = jnp.concatenate([cckv_ref[...], ckv], axis=0)
        kpe_all = jnp.concatenate([ckpe_ref[...], _rope(kpe_t, cos, sin)], axis=0)
    else:
        ckv_ref[...] = ckv
        kpe_ref[...] = kpe_t[:, 0:MLA_ROPE]
        ckv_all = ckv
        kpe_all = kpe_t
    kv = _dot(ckv_all.astype(BF16), wkv_ref[...])
    kcat = jnp.concatenate([kv[:, 0:GROUP], kpe_all], axis=1).astype(BF16)
    v = kv[:, GROUP:].astype(BF16)
    qcat = jnp.concatenate([q_nope, q_rope], axis=1)
    scale = (MLA_NOPE + MLA_ROPE) ** -0.5
    qb = min(seq, 256)
    width = qcat.shape[1]
    for r0 in range(0, seq, qb):
        qblk = qcat[r0:r0 + qb]
        acc = jnp.zeros((qb, GROUP), F32)
        for h in range(N_HEADS):
            hm = (_lane_group_mask(width, h * MLA_NOPE, MLA_NOPE)
                  | _lane_group_mask(width, N_HEADS * MLA_NOPE + h * MLA_ROPE, MLA_ROPE))
            s = _dot_nt(jnp.where(hm, qblk, 0.0).astype(BF16), kcat)
            e = jnp.exp2((s - jnp.max(s, axis=-1, keepdims=True)) * (scale * LOG2_E))
            z = jnp.sum(e, axis=-1, keepdims=True)
            oh = _dot(e.astype(BF16), v) / z
            acc = acc + jnp.where(_lane_group_mask(GROUP, h * HEAD_DIM, HEAD_DIM), oh, 0.0)
        o_ref[r0:r0 + qb, :] = acc


def _mla_parts(proj_b, row0, nseq, seq, q_norm, w_qb_p, kv_norm, w_kvb_p, latent_args, stack=None):
    latent = latent_args is not None
    blk0 = row0 // seq
    in_specs = [pl.BlockSpec((seq, PROJ_B), lambda b: (blk0 + b, 0)),
                pl.BlockSpec(q_norm.shape, lambda b: (0, 0)),
                pl.BlockSpec(w_qb_p.shape, lambda b: (0, 0)),
                pl.BlockSpec(kv_norm.shape, lambda b: (0, 0)),
                pl.BlockSpec(w_kvb_p.shape, lambda b: (0, 0))]
    args = [proj_b, q_norm, w_qb_p, kv_norm, w_kvb_p]
    out_specs = [pl.BlockSpec((None, seq, GROUP), lambda b: (b, 0, 0))]
    out_shape = [jax.ShapeDtypeStruct((nseq, seq, GROUP), F32)]
    if latent:
        cckv, ckpe_t, cos, sin = latent_args
        past = cckv.shape[1]
        in_specs += [pl.BlockSpec((None, past, MLA_KV_RANK), lambda b: (b, 0, 0)),
                     pl.BlockSpec((None, past, 128), lambda b: (b, 0, 0)),
                     pl.BlockSpec(cos.shape, lambda b: (0, 0)),
                     pl.BlockSpec(sin.shape, lambda b: (0, 0))]
        args += [cckv, ckpe_t, cos, sin]
    else:
        for width in (MLA_KV_RANK, MLA_ROPE):
            spec, shape = _layer_slot(nseq, stack, (seq, width))
            out_specs.append(spec)
            out_shape.append(shape)
    return functools.partial(_mla_kernel, latent), in_specs, args, out_specs, out_shape, []


def _diff_kernel(latent, lam_init, *refs):
    if latent:
        (c_ref, lam_ref, norm_ref, ck_ref, cv_ref, cos_ref, sin_ref, o_ref) = refs
    else:
        (c_ref, lam_ref, norm_ref, o_ref, k_ref, v_ref) = refs
    seq = c_ref.shape[0]
    dq = c_ref[:, 0:GROUP]
    dk = c_ref[:, GROUP:2 * GROUP]
    dv = c_ref[:, 2 * GROUP:3 * GROUP]
    if latent:
        cos = cos_ref[...]
        sin = sin_ref[...]
        dq = _rope(dq, cos, sin)
        k_all = jnp.concatenate([ck_ref[...], _rope(dk, cos, sin)], axis=0)
        v_all = jnp.concatenate([cv_ref[...], dv], axis=0)
    else:
        k_ref[...] = dk
        v_ref[...] = dv
        k_all = dk
        v_all = dv
    lv = lam_ref[...]
    lam = (jnp.exp(jnp.sum(lv[0:1] * lv[1:2], axis=-1, keepdims=True))
           - jnp.exp(jnp.sum(lv[2:3] * lv[3:4], axis=-1, keepdims=True)) + lam_init)
    k_bf = k_all.astype(BF16)
    v_bf = v_all.astype(BF16)
    scale = DIFF_DIM ** -0.5
    mean_op = jnp.where(_block_mask(GROUP, GROUP, HEAD_DIM, HEAD_DIM), 1.0 / HEAD_DIM, 0.0)
    qb = min(seq, 256)
    for r0 in range(0, seq, qb):
        qblk = dq[r0:r0 + qb]
        acc = jnp.zeros((qb, GROUP), F32)
        for h in range(N_HEADS):
            outs = []
            for comp in range(2):
                cm = _lane_group_mask(GROUP, h * HEAD_DIM + comp * DIFF_DIM, DIFF_DIM)
                s = _dot_nt(jnp.where(cm, qblk, 0.0).astype(BF16), k_bf)
                e = jnp.exp2((s - jnp.max(s, axis=-1, keepdims=True)) * (scale * LOG2_E))
                outs.append(_dot(e.astype(BF16), v_bf) / jnp.sum(e, axis=-1, keepdims=True))
            oh = outs[0] - lam * outs[1]
            acc = acc + jnp.where(_lane_group_mask(GROUP, h * HEAD_DIM, HEAD_DIM), oh, 0.0)
        ms = _dot_sel_r(acc * acc, mean_op)
        o_ref[r0:r0 + qb, :] = acc * lax.rsqrt(ms + NORM_EPS) * norm_ref[...] * (1.0 - lam_init)


def _diff_parts(proj_c, row0, nseq, seq, lam_p, norm_t, lam_init, latent_args, stack=None):
    latent = latent_args is not None
    blk0 = row0 // seq
    in_specs = [pl.BlockSpec((seq, PROJ_C), lambda b: (blk0 + b, 0)),
                pl.BlockSpec(lam_p.shape, lambda b: (0, 0)),
                pl.BlockSpec(norm_t.shape, lambda b: (0, 0))]
    args = [proj_c, lam_p, norm_t]
    out_specs = [pl.BlockSpec((None, seq, GROUP), lambda b: (b, 0, 0))]
    out_shape = [jax.ShapeDtypeStruct((nseq, seq, GROUP), F32)]
    if latent:
        ck, cv, cos, sin = latent_args
        past = ck.shape[1]
        in_specs += [pl.BlockSpec((None, past, GROUP), lambda b: (b, 0, 0)),
                     pl.BlockSpec((None, past, GROUP), lambda b: (b, 0, 0)),
                     pl.BlockSpec(cos.shape, lambda b: (0, 0)),
                     pl.BlockSpec(sin.shape, lambda b: (0, 0))]
        args += [ck, cv, cos, sin]
    else:
        for _ in range(2):
            spec, shape = _layer_slot(nseq, stack, (seq, GROUP))
            out_specs.append(spec)
            out_shape.append(shape)
    return functools.partial(_diff_kernel, latent, lam_init), in_specs, args, out_specs, out_shape, []


def _mixers(nseq, name, *parts, carried=None, creating=None):
    n_in = [len(p[1]) for p in parts]
    n_out = [len(p[3]) for p in parts]
    n_scr = [len(p[5]) for p in parts]
    carried = [None] * sum(n_out) if carried is None else list(carried)
    kept = [(o, arr) for o, arr in enumerate(carried) if arr is not None]

    def body(*refs):
        ins = refs[:sum(n_in)]
        outs = list(refs[sum(n_in) + len(kept):sum(n_in) + len(kept) + sum(n_out)])
        scr = refs[sum(n_in) + len(kept) + sum(n_out):]
        if creating is not None:
            layer, stacked = creating
            for o in stacked:
                full = outs[o]
                for m in range(full.shape[0]):
                    if m != layer:
                        full[m] = jnp.zeros(full.shape[1:], full.dtype)
                outs[o] = full.at[layer]
        i0 = o0 = s0 = 0
        for part, ni, no, ns in zip(parts, n_in, n_out, n_scr):
            part[0](*ins[i0:i0 + ni], *outs[o0:o0 + no], *scr[s0:s0 + ns])
            i0, o0, s0 = i0 + ni, o0 + no, s0 + ns

    res = pl.pallas_call(
        body,
        grid=(nseq,),
        in_specs=[x for p in parts for x in p[1]] + [pl.BlockSpec(memory_space=pl.ANY)] * len(kept),
        out_specs=[x for p in parts for x in p[3]],
        out_shape=[x for p in parts for x in p[4]],
        scratch_shapes=[x for p in parts for x in p[5]],
        input_output_aliases={sum(n_in) + k: o for k, (o, _) in enumerate(kept)},
        compiler_params=_cparams("arbitrary"),
        name=name,
    )(*[x for p in parts for x in p[2]], *[arr for _, arr in kept])
    out, o0 = [], 0
    for no in n_out:
        out.append(tuple(res[o0:o0 + no]))
        o0 += no
    return out


def _ssd_kernel(has_state, *refs):
    if has_state:
        (d_ref, cw_ref, cb_ref, dtb_ref, alog_ref, dskip_ref, norm_ref, s0_ref,
         o_ref, sfin_ref, xs_scr, bm_scr, cm_scr, xdt_scr, a_scr, st_scr, yf_scr, yb_scr) = refs
    else:
        (d_ref, cw_ref, cb_ref, dtb_ref, alog_ref, dskip_ref, norm_ref,
         o_ref, sfin_ref, xs_scr, bm_scr, cm_scr, xdt_scr, a_scr, st_scr, yf_scr, yb_scr) = refs
    seq = d_ref.shape[0]
    c = SSD_CHUNK
    nchunk = seq // c
    ngrp = 2 * SSD_STATE

    xin = d_ref[:, GROUP:GROUP + 512]
    rows = _iota(xin.shape, 0)
    prev = jnp.where(rows == 0, 0.0, pltpu.roll(xin, 1, 0))
    nxt = jnp.where(rows == seq - 1, 0.0, pltpu.roll(xin, seq - 1, 0))
    cw = cw_ref[...]
    xbc = _silu(cw[0:1] * prev + cw[1:2] * xin + cw[2:3] * nxt + cb_ref[...])
    xs = xbc[:, 0:GROUP]
    xs_scr[...] = xs
    bm_scr[...] = xbc[:, GROUP:GROUP + ngrp]
    cm_scr[...] = xbc[:, GROUP + ngrp:GROUP + 2 * ngrp]
    dt = _softplus(d_ref[:, GROUP + 512:GROUP + 640] + dtb_ref[...])
    a_scr[...] = dt * (-jnp.exp(alog_ref[...]))
    erow = _iota((128, GROUP), 0)
    ehead = _iota((128, GROUP), 1) // HEAD_DIM
    expand = tuple((erow == 4 * d + ehead).astype(F32) for d in range(2))
    for d in range(2):
        xdt_scr[d] = xs * _dot_sel_r(dt, expand[d])
    if has_state:
        st_scr[...] = s0_ref[...]
    else:
        st_scr[...] = jnp.zeros_like(st_scr)

    row = _iota((c, c), 0)
    col = _iota((c, c), 1)
    tri = ((col <= row).astype(F32), (col >= row).astype(F32))
    keep = (col <= row, col >= row)
    grp_lane = _iota((1, ngrp), 1) // SSD_STATE
    valid = (_iota((ngrp, GROUP), 0) // SSD_STATE) == (_iota((ngrp, GROUP), 1) // (2 * HEAD_DIM))

    def chunk_step(d, r0, out_scr):
        a_c = a_scr[pl.ds(r0, c), :]
        bm_c = bm_scr[pl.ds(r0, c), :]
        cm_c = cm_scr[pl.ds(r0, c), :].astype(BF16)
        xdt_c = xdt_scr[d, pl.ds(r0, c), :]
        acum = _dot_sel(tri[d], a_c)
        acum_t = acum.T
        acum_rep = _dot_sel_r(acum, expand[d])
        bm2 = jnp.concatenate([jnp.where(grp_lane == g, bm_c, 0.0) for g in range(2)], axis=0)
        cb = _dot_nt(cm_c, bm2.astype(BF16))
        scores = []
        xparts = []
        for h in range(N_HEADS):
            lane = 4 * d + h
            seg = jnp.exp(jnp.where(keep[d], acum[:, lane:lane + 1] - acum_t[lane:lane + 1, :], -jnp.inf))
            g = h // 2
            scores.append((cb[:, g * c:(g + 1) * c] * seg).astype(BF16))
            xparts.append(jnp.where(_lane_group_mask(GROUP, h * HEAD_DIM, HEAD_DIM), xdt_c, 0.0))
        y = _dot(jnp.concatenate(scores, axis=1), jnp.concatenate(xparts, axis=0).astype(BF16))
        st = st_scr[d]
        y = y + _dot(cm_c, st.astype(BF16)) * jnp.exp(acum_rep)
        out_scr[pl.ds(r0, c), :] = y
        edge = acum_rep[c - 1:c] if d == 0 else acum_rep[0:1]
        xt = (xdt_c * jnp.exp(edge - acum_rep)).astype(BF16)
        upd = _dot_tn(bm_c.astype(BF16), xt)
        st_scr[d] = st * jnp.exp(edge) + jnp.where(valid, upd, 0.0)

    def body(n, carry):
        chunk_step(0, pl.multiple_of(n * c, c), yf_scr)
        chunk_step(1, pl.multiple_of((nchunk - 1 - n) * c, c), yb_scr)
        return carry

    lax.fori_loop(0, nchunk, body, 0)

    y = yf_scr[...] + yb_scr[...] + dskip_ref[...] * xs_scr[...]
    y = y * _silu(d_ref[:, 0:GROUP])
    o_ref[...] = y * lax.rsqrt(jnp.mean(y * y, axis=-1, keepdims=True) + NORM_EPS) * norm_ref[...]
    sfin_ref[...] = st_scr[...]


def _ssd_parts(proj_d, row0, nseq, seq, conv_w, conv_b, dt_bias_p, a_log_p, d_rep, norm, s0, stack=None):
    has_state = s0 is not None
    blk0 = row0 // seq
    ngrp = 2 * SSD_STATE
    small = [conv_w, conv_b, dt_bias_p, a_log_p, d_rep, norm]
    in_specs = ([pl.BlockSpec((seq, PROJ_D), lambda b: (blk0 + b, 0))]
                + [pl.BlockSpec(s.shape, lambda b: (0, 0)) for s in small])
    args = [proj_d] + small
    if has_state:
        in_specs.append(pl.BlockSpec((None, 2, ngrp, GROUP), lambda b: (b, 0, 0, 0)))
        args.append(s0)
    out_specs, out_shape = zip(_layer_slot(nseq, None, (seq, GROUP)),
                               _layer_slot(nseq, stack, (2, ngrp, GROUP)))
    out_specs, out_shape = list(out_specs), list(out_shape)
    scratch = [pltpu.VMEM((seq, GROUP), F32),
               pltpu.VMEM((seq, ngrp), F32),
               pltpu.VMEM((seq, ngrp), F32),
               pltpu.VMEM((2, seq, GROUP), F32),
               pltpu.VMEM((seq, 128), F32),
               pltpu.VMEM((2, ngrp, GROUP), F32),
               pltpu.VMEM((seq, GROUP), F32),
               pltpu.VMEM((seq, GROUP), F32)]
    return functools.partial(_ssd_kernel, has_state), in_specs, args, out_specs, out_shape, scratch


def _outproj_kernel(n_ctx_tiles, *refs):
    ctx_refs, lat_refs = refs[0:4], refs[4:8]
    w_ref, x_ref, m_ref, g_ref, b_ref, o_ref = refs[8:]
    is_ctx = pl.program_id(0) < n_ctx_tiles
    mixed = None
    for i, (c_ref, l_ref) in enumerate(zip(ctx_refs, lat_refs)):
        part = jnp.where(is_ctx, c_ref[...], l_ref[...]).astype(BF16)
        term = _dot(part, w_ref[i * GROUP:(i + 1) * GROUP, :])
        mixed = term if mixed is None else mixed + term
    o_ref[...] = _layernorm(ALPHA * x_ref[...] + m_ref[2] * mixed, g_ref[...], b_ref[...])


def _outproj(parts_ctx, parts_lat, w_out_bf, x, mods_l, ln_g, ln_b, t_ctx, s_lat):
    t, d = x.shape
    tm = 1024
    n_ctx = t_ctx // tm
    n_lat = (t - t_ctx) // tm
    ctx_spec = pl.BlockSpec((tm, GROUP), lambda i: (jnp.minimum(i, n_ctx - 1), 0))
    lat_spec = pl.BlockSpec((tm, GROUP), lambda i: (jnp.clip(i - n_ctx, 0, n_lat - 1), 0))
    return pl.pallas_call(
        functools.partial(_outproj_kernel, n_ctx),
        grid=(t // tm,),
        in_specs=[ctx_spec] * 4 + [lat_spec] * 4
        + [pl.BlockSpec(w_out_bf.shape, lambda i: (0, 0)),
           pl.BlockSpec((tm, d), lambda i: (i, 0)),
           pl.BlockSpec((None, 6, 1, d), _mod_row_map(tm, t_ctx, s_lat)),
           pl.BlockSpec((1, d), lambda i: (0, 0)),
           pl.BlockSpec((1, d), lambda i: (0, 0))],
        out_specs=pl.BlockSpec((tm, d), lambda i: (i, 0)),
        out_shape=jax.ShapeDtypeStruct((t, d), F32),
        compiler_params=_cparams("arbitrary"),
        name="outproj_ln",
    )(*parts_ctx, *parts_lat, w_out_bf, x, mods_l, ln_g, ln_b)


def _top_rows(s, k, extra=()):
    r = s.shape[0]
    rid = _iota(s.shape, 0).astype(F32)
    vals, ids = [], []
    picked = [[] for _ in extra]
    for _ in range(k):
        m = jnp.max(s, axis=0, keepdims=True)
        cand = jnp.where(s == m, rid, float(r))
        i = jnp.min(cand, axis=0, keepdims=True)
        hit = cand == i
        vals.append(m)
        ids.append(i)
        for lst, arr in zip(picked, extra):
            lst.append(jnp.max(jnp.where(hit, arr, -1.0), axis=0, keepdims=True))
        s = jnp.where(hit, -jnp.inf, s)
    cat = lambda xs: jnp.concatenate(xs, axis=0)
    return cat(vals), cat(ids), [cat(p) for p in picked]


def _router_kernel(x_ref, m_ref, wq_ref, keys_ref, h_ref, a_ref, b_ref, g_ref):
    tm = x_ref.shape[0]
    hb = (x_ref[...] * (1.0 + m_ref[4]) + m_ref[3]).astype(BF16)
    h_ref[...] = hb
    qt = _dot(hb, wq_ref[...]).T.astype(BF16)
    k = PEER_TOPK
    code_rows, g_rows = [], []
    for head in range(PEER_HEADS):
        tv, ti = [], []
        for half in range(2):
            g = 2 * head + half
            sc = _dot(keys_ref[g], qt[g * PEER_HALF:(g + 1) * PEER_HALF])
            v, i, _ = _top_rows(sc, k)
            tv.append(v)
            ti.append(i)
        cs = [tv[0][0:1] + tv[1]]
        ca = [jnp.broadcast_to(ti[0][0:1], (k, tm))]
        cb = [ti[1]]
        for k1 in range(1, 4):
            cs.append(tv[0][k1:k1 + 1] + tv[1][0:8])
            ca.append(jnp.broadcast_to(ti[0][k1:k1 + 1], (8, tm)))
            cb.append(ti[1][0:8])
        low = _iota((8, tm), 0) < 4
        v2_dup = jnp.where(low, tv[1][0:8], pltpu.roll(tv[1][0:8], 4, 0))
        i2_dup = jnp.where(low, ti[1][0:8], pltpu.roll(ti[1][0:8], 4, 0))
        for k1 in (4, 6):
            cs.append(jnp.where(low, tv[0][k1:k1 + 1], tv[0][k1 + 1:k1 + 2]) + v2_dup)
            ca.append(jnp.where(low, ti[0][k1:k1 + 1], ti[0][k1 + 1:k1 + 2]))
            cb.append(i2_dup)
        cs.append(tv[0][8:16] + tv[1][0:1])
        ca.append(ti[0][8:16])
        cb.append(jnp.broadcast_to(ti[1][0:1], (8, tm)))
        code = jnp.concatenate(ca, axis=0) * float(PEER_KEYS) + jnp.concatenate(cb, axis=0)
        best, _, (sel_code,) = _top_rows(jnp.concatenate(cs, axis=0), k, extra=(code,))
        e = jnp.exp(best - best[0:1])
        g_rows.append(e / jnp.sum(e, axis=0, keepdims=True))
        code_rows.append(sel_code)
    codes = jnp.concatenate(code_rows, axis=0)
    key1 = jnp.floor(codes * (1.0 / PEER_KEYS))
    a_ref[...] = key1.T.astype(I32)
    b_ref[...] = (codes - key1 * float(PEER_KEYS)).T.astype(I32)
    g_ref[...] = jnp.concatenate(g_rows, axis=0).T


def _router(x, mods_l, wq_bf, keys, t_ctx, s_lat):
    t, d = x.shape
    tm = 256
    nslot = PEER_HEADS * PEER_TOPK
    return pl.pallas_call(
        _router_kernel,
        grid=(t // tm,),
        in_specs=[pl.BlockSpec((tm, d), lambda i: (i, 0)),
                  pl.BlockSpec((None, 6, 1, d), _mod_row_map(tm, t_ctx, s_lat)),
                  pl.BlockSpec(wq_bf.shape, lambda i: (0, 0)),
                  pl.BlockSpec(keys.shape, lambda i: (0, 0, 0))],
        out_specs=[pl.BlockSpec((tm, d), lambda i: (i, 0)),
                   pl.BlockSpec((tm, nslot), lambda i: (i, 0)),
                   pl.BlockSpec((tm, nslot), lambda i: (i, 0)),
                   pl.BlockSpec((tm, nslot), lambda i: (i, 0))],
        out_shape=[jax.ShapeDtypeStruct((t, d), BF16),
                   jax.ShapeDtypeStruct((t, nslot), I32),
                   jax.ShapeDtypeStruct((t, nslot), I32),
                   jax.ShapeDtypeStruct((t, nslot), F32)],
        compiler_params=_cparams("arbitrary"),
        name="peer_router",
    )(x, mods_l, wq_bf, keys)


def _gates_kernel(a_ref, b_ref, g_ref, u_ref, v_ref, o_ref, ub_ref, vb_ref):
    tm = a_ref.shape[0]
    n = PEER_KEYS
    sub = 16
    ub_ref[...] = u_ref[...].astype(BF16)
    vb_ref[...] = v_ref[...].astype(BF16)
    key = _iota((sub, n, a_ref.shape[2]), 1).astype(F32).astype(BF16)
    zero = jnp.zeros((), BF16)
    for t0 in range(0, tm, sub):
        a = a_ref[t0:t0 + sub].astype(F32).astype(BF16)
        b = b_ref[t0:t0 + sub].astype(F32).astype(BF16)
        g = g_ref[t0:t0 + sub].astype(BF16)
        onehot_a = jnp.where(key == a, jnp.ones((), BF16), zero)
        gated_b = jnp.where(key == b, g, zero)
        w = lax.dot_general(onehot_a, gated_b, (((2,), (2,)), ((0,), (0,))),
                            preferred_element_type=F32)
        w_t = jnp.swapaxes(w.astype(BF16), 0, 1)
        for r in range(n):
            o_ref[t0:t0 + sub, r * n:(r + 1) * n] = w_t[r]


def _gates(a_idx, b_idx, gate, peer_u, peer_v, layer):
    t, nslot = a_idx.shape
    d = peer_u.shape[-1]
    tm = 192
    steps = t // tm
    te = N_EXPERTS // steps
    spec = pl.BlockSpec((tm, 1, nslot), lambda i: (i, 0, 0))
    tab_in = pl.BlockSpec((None, te, d), lambda i: (layer, i, 0))
    tab_out = pl.BlockSpec((te, d), lambda i: (i, 0))
    return pl.pallas_call(
        _gates_kernel,
        grid=(steps,),
        in_specs=[spec, spec, spec, tab_in, tab_in],
        out_specs=[pl.BlockSpec((tm, N_EXPERTS), lambda i: (i, 0)), tab_out, tab_out],
        out_shape=[jax.ShapeDtypeStruct((t, N_EXPERTS), BF16),
                   jax.ShapeDtypeStruct((N_EXPERTS, d), BF16),
                   jax.ShapeDtypeStruct((N_EXPERTS, d), BF16)],
        compiler_params=_cparams("arbitrary"),
        name="peer_gates",
    )(a_idx.reshape(t, 1, nslot), b_idx.reshape(t, 1, nslot), gate.reshape(t, 1, nslot), peer_u, peer_v)


def _experts_kernel(h_ref, u_ref, v_ref, w_ref, x_ref, m_ref, g_ref, b_ref, o_ref, acc_ref):
    j = pl.program_id(1)

    @pl.when(j == 0)
    def _():
        acc_ref[...] = jnp.zeros_like(acc_ref)

    act = _gelu_tanh(_dot_nt(h_ref[...], u_ref[...]))
    acc_ref[...] += _dot((act * w_ref[...].astype(F32)).astype(BF16), v_ref[...])

    @pl.when(j == pl.num_programs(1) - 1)
    def _():
        o_ref[...] = _layernorm(ALPHA * x_ref[...] + m_ref[5] * acc_ref[...], g_ref[...], b_ref[...])


def _experts(h_bf, u_bf, v_bf, w_gate, x, mods_l, ln_g, ln_b, t_ctx, s_lat):
    t, d = x.shape
    tm, te = 1024, 1024
    return pl.pallas_call(
        _experts_kernel,
        grid=(t // tm, N_EXPERTS // te),
        in_specs=[pl.BlockSpec((tm, d), lambda i, j: (i, 0)),
                  pl.BlockSpec((te, d), lambda i, j: (j, 0)),
                  pl.BlockSpec((te, d), lambda i, j: (j, 0)),
                  pl.BlockSpec((tm, te), lambda i, j: (i, j)),
                  pl.BlockSpec((tm, d), lambda i, j: (i, 0)),
                  pl.BlockSpec((None, 6, 1, d), _mod_row_map(tm, t_ctx, s_lat)),
                  pl.BlockSpec((1, d), lambda i, j: (0, 0)),
                  pl.BlockSpec((1, d), lambda i, j: (0, 0))],
        out_specs=pl.BlockSpec((tm, d), lambda i, j: (i, 0)),
        out_shape=jax.ShapeDtypeStruct((t, d), F32),
        scratch_shapes=[pltpu.VMEM((tm, d), F32)],
        compiler_params=_cparams("arbitrary", "arbitrary"),
        name="peer_experts",
    )(h_bf, u_bf, v_bf, w_gate, x, mods_l, ln_g, ln_b)


def _pad_cols(w, n):
    return jnp.concatenate([w, jnp.zeros((w.shape[0], n), w.dtype)], axis=1) if n else w


def _layout_w_in(w):
    a = w[:, 0:1280]
    mcq, mckv, mkpe = w[:, 1280:1472], w[:, 1472:1600], w[:, 1600:1632]
    b = jnp.concatenate([mckv, _pad_cols(mcq, 64), mkpe, mkpe, mkpe, mkpe], axis=1)
    c = w[:, 1632:2400]
    d = _pad_cols(w[:, 2400:3176], PROJ_D - 776)
    return jnp.concatenate([a, b, c, d], axis=1).astype(BF16)


def _layout_w_qb(w):
    w4 = w.reshape(MLA_Q_RANK, N_HEADS, MLA_NOPE + MLA_ROPE)
    return jnp.concatenate([w4[:, :, :MLA_NOPE].reshape(MLA_Q_RANK, -1),
                            w4[:, :, MLA_NOPE:].reshape(MLA_Q_RANK, -1)], axis=1).astype(BF16)


def _layout_w_kvb(w):
    w4 = w.reshape(MLA_KV_RANK, N_HEADS, MLA_NOPE + HEAD_DIM)
    return jnp.concatenate([w4[:, :, :MLA_NOPE].reshape(MLA_KV_RANK, -1),
                            w4[:, :, MLA_NOPE:].reshape(MLA_KV_RANK, -1)], axis=1).astype(BF16)


def _rope_tables(seq):
    rows = seq // GRID_W
    row = jnp.repeat(jnp.arange(rows, dtype=F32), GRID_W)
    col = jnp.tile(jnp.arange(GRID_W, dtype=F32), rows)
    freqs = ROPE_BASE ** (-jnp.arange(ROPE_PAIRS, dtype=F32) / ROPE_PAIRS)
    cos_l, sin_l = [], []
    for pos in (row, col):
        ang = pos[:, None] * freqs
        cos_l += [jnp.cos(ang), jnp.cos(ang)]
        sin_l += [-jnp.sin(ang), jnp.sin(ang)]
    return jnp.concatenate(cos_l, axis=1), jnp.concatenate(sin_l, axis=1)


def _hgrn_state_pack(st):
    b = st.shape[0]
    st_t = jnp.swapaxes(st, -1, -2)
    zero = jnp.zeros_like(st_t[:, :, 0])
    rows = [jnp.concatenate([st_t[:, :, h] if g == h else zero for g in range(N_HEADS)], axis=-1)
            for h in range(N_HEADS)]
    return jnp.concatenate(rows, axis=-2).reshape(b, 2, GROUP, GROUP)


def _hgrn_state_unpack(sb):
    blocks = [sb[..., h * HEAD_DIM:(h + 1) * HEAD_DIM, h * HEAD_DIM:(h + 1) * HEAD_DIM]
              for h in range(N_HEADS)]
    return jnp.swapaxes(jnp.stack(blocks, axis=-3), -1, -2)


def _ssd_state_pack(st):
    st_t = jnp.swapaxes(st, -1, -2)
    zero = jnp.zeros_like(st_t[:, :, 0])
    rows = [jnp.concatenate([st_t[:, :, h] if h // 2 == g else zero for h in range(N_HEADS)], axis=-1)
            for g in range(2)]
    return jnp.concatenate(rows, axis=-2)


def _ssd_state_unpack(sb):
    blocks = [sb[..., (h // 2) * SSD_STATE:(h // 2 + 1) * SSD_STATE, h * HEAD_DIM:(h + 1) * HEAD_DIM]
              for h in range(N_HEADS)]
    return jnp.swapaxes(jnp.stack(blocks, axis=-3), -1, -2)


def _tile_lanes(v, n):
    return jnp.tile(v.reshape(1, -1), (1, n))


def kernel(x_prompt, x_sample, cache_mla_ckv, cache_mla_kpe, cache_diff_k, cache_diff_v, state_hgrn, state_ssd, c, c_ctx, w_mod, b_mod, w_in, hgrn_lb, hgrn_norm, mla_q_norm, mla_w_qb, mla_kv_norm, mla_w_kvb, diff_lambda, diff_norm, ssd_conv_w, ssd_conv_b, ssd_dt_bias, ssd_a_log, ssd_d, ssd_norm, w_out, ln1_g, ln1_b, peer_wq, peer_keys, peer_u, peer_v, ln2_g, ln2_b):
    nb, seq, d = x_prompt.shape
    nlat, lseq, _ = x_sample.shape
    depth = w_in.shape[0]
    t_ctx = nb * seq
    x = jnp.concatenate([x_prompt.reshape(t_ctx, d), x_sample.reshape(nlat * lseq, d)], axis=0)

    cond8 = jnp.concatenate([c_ctx.reshape(1, d), c, jnp.zeros((8 - 1 - nlat, d), F32)], axis=0)
    mods = _mods(cond8, w_mod, b_mod)
    mods = mods[:, :1 + nlat].reshape(depth, 1 + nlat, 6, 1, d)

    cos32, sin32 = _rope_tables(lseq)
    cos128, sin128 = jnp.tile(cos32, (1, 4)), jnp.tile(sin32, (1, 4))
    cos256, sin256 = jnp.tile(cos32, (1, 8)), jnp.tile(sin32, (1, 8))

    carried = None
    for l in range(depth):
        mods_l = mods[l]
        pa, pb, pc, pd = _inproj(x, mods_l, _layout_w_in(w_in[l]), t_ctx, lseq)

        norm_hg = _tile_lanes(hgrn_norm[l], N_HEADS)
        mla_w = (mla_q_norm[l].reshape(1, -1), _layout_w_qb(mla_w_qb[l]),
                 mla_kv_norm[l].reshape(1, -1), _layout_w_kvb(mla_w_kvb[l]))
        lam_init = 0.8 - 0.6 * math.exp(-0.3 * l)
        norm_df = _tile_lanes(diff_norm[l], N_HEADS)
        past = cache_diff_k.shape[2]
        ssd_w = (ssd_conv_w[l], ssd_conv_b[l].reshape(1, -1),
                 _pad_cols(ssd_dt_bias[l].reshape(1, -1), 120), _pad_cols(ssd_a_log[l].reshape(1, -1), 120),
                 jnp.repeat(ssd_d[l], HEAD_DIM).reshape(1, -1), ssd_norm[l].reshape(1, -1))

        stack = (l, depth, carried is None)
        ctx_out = _mixers(
            nb, "mixers_ctx",
            _mla_parts(pb, 0, nb, seq, *mla_w, None, stack),
            _diff_parts(pc, 0, nb, seq, diff_lambda[l], norm_df, lam_init, None, stack),
            _ssd_parts(pd, 0, nb, seq, *ssd_w, None, stack),
            _hgrn_parts(pa, 0, nb, seq, hgrn_lb, norm_hg, None, l, stack),
            carried=carried, creating=(l, (1, 2, 4, 5, 7, 9)) if carried is None else None)
        (mla_ctx, new_ckv, new_kpe), (df_ctx, new_dk, new_dv), (ssd_ctx, ssd_fin), (hg_ctx, hg_fin) = ctx_out
        carried = [None, new_ckv, new_kpe, None, new_dk, new_dv, None, ssd_fin, None, hg_fin]
        ((mla_lat,),) = _mixers(
            nlat, "mla_lat",
            _mla_parts(pb, t_ctx, nlat, lseq, *mla_w,
                       (cache_mla_ckv[:, l], jnp.tile(cache_mla_kpe[:, l], (1, 1, 4)), cos128, sin128)))
        ((df_lat,),) = _mixers(
            nlat, "diffattn_lat",
            _diff_parts(pc, t_ctx, nlat, lseq, diff_lambda[l], norm_df, lam_init,
                        (cache_diff_k[:, l].reshape(nlat, past, GROUP),
                         cache_diff_v[:, l].reshape(nlat, past, GROUP), cos256, sin256)))
        ((ssd_lat, _),) = _mixers(nlat, "ssd_lat",
                                  _ssd_parts(pd, t_ctx, nlat, lseq, *ssd_w, _ssd_state_pack(state_ssd[:, l])))
        ((hg_lat, _),) = _mixers(nlat, "hgrn_lat",
                                 _hgrn_parts(pa, t_ctx, nlat, lseq, hgrn_lb, norm_hg,
                                             _hgrn_state_pack(state_hgrn[:, l]), l))

        parts_ctx = [a.reshape(t_ctx, GROUP) for a in (hg_ctx, mla_ctx, df_ctx, ssd_ctx)]
        parts_lat = [a.reshape(nlat * lseq, GROUP) for a in (hg_lat, mla_lat, df_lat, ssd_lat)]
        x = _outproj(parts_ctx, parts_lat, w_out[l].astype(BF16), x, mods_l,
                     ln1_g[l].reshape(1, d), ln1_b[l].reshape(1, d), t_ctx, lseq)

        keys = peer_keys[l].reshape(2 * PEER_HEADS, PEER_KEYS, PEER_HALF).astype(BF16)
        h_bf, a_idx, b_idx, gate = _router(x, mods_l, peer_wq[l].astype(BF16), keys, t_ctx, lseq)
        w_gate, u_bf, v_bf = _gates(a_idx, b_idx, gate, peer_u, peer_v, l)
        x = _experts(h_bf, u_bf, v_bf, w_gate, x, mods_l,
                     ln2_g[l].reshape(1, d), ln2_b[l].reshape(1, d), t_ctx, lseq)

    y_prompt = x[:t_ctx].reshape(nb, seq, d)
    y_sample = x[t_ctx:].reshape(nlat, lseq, d)
    return (y_prompt, y_sample, new_ckv, new_kpe,
            new_dk.reshape(nb, depth, seq, N_HEADS, 2, DIFF_DIM),
            new_dv.reshape(nb, depth, seq, N_HEADS, 2 * DIFF_DIM),
            _hgrn_state_unpack(hg_fin), _ssd_state_unpack(ssd_fin))
```

```python
import functools
import math

import jax
import jax.numpy as jnp
from jax import lax
from jax.experimental import pallas as pl
from jax.experimental.pallas import tpu as pltpu

F32 = jnp.float32
BF16 = jnp.bfloat16
I32 = jnp.int32

D_MODEL = 1024
GROUP = 256
N_HEADS = 4
HEAD_DIM = 64
HG_BLOCK = 16
HG_SLAB = 256
HG_UNROLL = 16
SSD_CHUNK = 128
SSD_STATE = 64
MLA_Q_RANK = 192
MLA_KV_RANK = 128
MLA_NOPE = 64
MLA_ROPE = 32
DIFF_DIM = 32
GRID_W = 64
ROPE_PAIRS = 8
ROPE_BASE = 10000.0
PEER_HEADS = 8
PEER_KEYS = 128
PEER_TOPK = 16
PEER_HALF = 64
N_EXPERTS = PEER_KEYS * PEER_KEYS
NORM_EPS = 1e-6
LN_EPS = 1e-5
DEPTH = 2
ALPHA = (2.0 * DEPTH) ** 0.25
LOG2_E = 1.4426950408889634

PROJ_A = 5 * GROUP
PROJ_B = 512
PROJ_C = 3 * GROUP
PROJ_D = 896
VMEM_LIMIT = 56 * 1024 * 1024


def _cparams(*sem):
    return pltpu.CompilerParams(dimension_semantics=sem, vmem_limit_bytes=VMEM_LIMIT)


def _sigmoid(x):
    return 1.0 / (1.0 + jnp.exp(-x))


def _silu(x):
    return x * _sigmoid(x)


def _softplus(x):
    return jnp.maximum(x, 0.0) + jnp.log(1.0 + jnp.exp(-jnp.abs(x)))


def _gelu_tanh(x):
    return 0.5 * x * (1.0 + jnp.tanh(math.sqrt(2.0 / math.pi) * (x + 0.044715 * (x * x * x))))


def _dot(a, b):
    return jnp.dot(a, b, preferred_element_type=F32)


def _dot_nt(a, b):
    return lax.dot_general(a, b, (((1,), (1,)), ((), ())), preferred_element_type=F32)


def _dot_tn(a, b):
    return lax.dot_general(a, b, (((0,), (0,)), ((), ())), preferred_element_type=F32)


def _split3(x):
    hi = x.astype(BF16)
    rest = x - hi.astype(F32)
    mid = rest.astype(BF16)
    lo = (rest - mid.astype(F32)).astype(BF16)
    return hi, mid, lo


def _dot_sel(sel, x):
    sel = sel.astype(BF16)
    hi, mid, lo = _split3(x)
    return _dot(sel, hi) + _dot(sel, mid) + _dot(sel, lo)


def _dot_sel_r(x, sel):
    sel = sel.astype(BF16)
    hi, mid, lo = _split3(x)
    return _dot(hi, sel) + _dot(mid, sel) + _dot(lo, sel)


def _iota(shape, dim):
    return lax.broadcasted_iota(I32, shape, dim)


def _block_mask(rows, cols, rblk, cblk):
    return (_iota((rows, cols), 0) // rblk) == (_iota((rows, cols), 1) // cblk)


def _lane_group_mask(width, start, size):
    lane = _iota((1, width), 1)
    return (lane >= start) & (lane < start + size)


def _layernorm(v, g, b):
    mu = jnp.mean(v, axis=-1, keepdims=True)
    d = v - mu
    var = jnp.mean(d * d, axis=-1, keepdims=True)
    return d * lax.rsqrt(var + LN_EPS) * g + b


def _swap_halves16(x):
    width = x.shape[-1]
    lane = _iota(x.shape, x.ndim - 1)
    up = pltpu.roll(x, width - 8, x.ndim - 1)
    down = pltpu.roll(x, 8, x.ndim - 1)
    return jnp.where((lane % 16) < 8, up, down)


def _rope(x, cos, sin_signed):
    return x * cos + _swap_halves16(x) * sin_signed


def _mods_kernel(c_ref, w_ref, b_ref, o_ref):
    s = _silu(c_ref[...]).astype(BF16)
    o_ref[...] = _dot(s, w_ref[...].astype(BF16)) + b_ref[...]


def _mods(cond8, w_mod, b_mod):
    depth, d, n = w_mod.shape
    tn = 3072
    return pl.pallas_call(
        _mods_kernel,
        grid=(depth, n // tn),
        in_specs=[pl.BlockSpec((8, d), lambda l, j: (0, 0)),
                  pl.BlockSpec((None, d, tn), lambda l, j: (l, 0, j)),
                  pl.BlockSpec((None, 1, tn), lambda l, j: (l, 0, j))],
        out_specs=pl.BlockSpec((None, 8, tn), lambda l, j: (l, 0, j)),
        out_shape=jax.ShapeDtypeStruct((depth, 8, n), F32),
        compiler_params=_cparams("arbitrary", "arbitrary"),
        name="mods",
    )(cond8, w_mod, b_mod.reshape(depth, 1, n))


def _mod_row_map(tm, t_ctx, s_lat):
    def index_map(i, *_):
        start = i * tm
        return (jnp.where(start < t_ctx, 0, 1 + (start - t_ctx) // s_lat), 0, 0, 0)
    return index_map


def _inproj_kernel(x_ref, m_ref, w_ref, oa_ref, ob_ref, oc_ref, od_ref):
    h = (x_ref[...] * (1.0 + m_ref[1]) + m_ref[0]).astype(BF16)
    start = 0
    for o_ref in (oa_ref, ob_ref, oc_ref, od_ref):
        width = o_ref.shape[-1]
        o_ref[...] = _dot(h, w_ref[:, start:start + width])
        start += width


def _inproj(x, mods_l, w_in_p, t_ctx, s_lat):
    t, d = x.shape
    tm = 512
    widths = (PROJ_A, PROJ_B, PROJ_C, PROJ_D)
    return pl.pallas_call(
        _inproj_kernel,
        grid=(t // tm,),
        in_specs=[pl.BlockSpec((tm, d), lambda i: (i, 0)),
                  pl.BlockSpec((None, 6, 1, d), _mod_row_map(tm, t_ctx, s_lat)),
                  pl.BlockSpec(w_in_p.shape, lambda i: (0, 0))],
        out_specs=[pl.BlockSpec((tm, w), lambda i: (i, 0)) for w in widths],
        out_shape=[jax.ShapeDtypeStruct((t, w), F32) for w in widths],
        compiler_params=_cparams("arbitrary"),
        name="inproj",
    )(x, mods_l, w_in_p)


def _hgrn_kernel(layer, has_state, *refs):
    if has_state:
        (a_ref, lb_ref, norm_ref, s0_ref, o_ref, sfin_ref,
         q_scr, k_scr, bc_scr, dec_scr, qt_scr, kt_scr, st_scr, o_scr) = refs
    else:
        (a_ref, lb_ref, norm_ref, o_ref, sfin_ref,
         q_scr, k_scr, bc_scr, dec_scr, qt_scr, kt_scr, st_scr, o_scr) = refs
        s0_ref = None
    seq = a_ref.shape[0]
    c = HG_BLOCK
    nblk = seq // c
    slab = HG_SLAB
    nb = slab // c

    lbp = lb_ref[...]
    e = jnp.exp(lbp - jnp.max(lbp, axis=0, keepdims=True))
    p = e / jnp.sum(e, axis=0, keepdims=True)
    lower = jnp.sum(p[1:layer + 1], axis=0) if layer > 0 else jnp.zeros_like(p[0])

    q = _silu(a_ref[:, 0:GROUP])
    q_scr[...] = q
    srow = _iota((slab, slab), 0)
    scol = _iota((slab, slab), 1)
    same = (srow // c) == (scol // c)
    cum_op = (jnp.where(same & (scol <= srow), 1.0, 0.0), jnp.where(same & (scol >= srow), 1.0, 0.0))
    for d in range(2):
        lb = lower[d:d + 1]
        f = lb + (1.0 - lb) * _sigmoid(a_ref[:, (1 + d) * GROUP:(2 + d) * GROUP])
        k = 1.0 - f
        lf = jnp.log(f)
        for s0 in range(0, seq, slab):
            bc = _dot_sel(cum_op[d], lf[s0:s0 + slab])
            bc3 = bc.reshape(nb, c, GROUP)
            edge = bc3[:, c - 1:c, :] if d == 0 else bc3[:, 0:1, :]
            tot = jnp.broadcast_to(edge, (nb, c, GROUP)).reshape(slab, GROUP)
            bc_scr[d, s0:s0 + slab, :] = bc * LOG2_E
            k_scr[d, s0:s0 + slab, :] = (bc - jnp.log(k[s0:s0 + slab])) * LOG2_E
            dec_scr[d, s0:s0 + slab, :] = jnp.exp(tot)
            qt_scr[d, s0:s0 + slab, :] = (q[s0:s0 + slab] * jnp.exp(bc)).astype(BF16)
            kt_scr[d, s0:s0 + slab, :] = (k[s0:s0 + slab] * jnp.exp(tot - bc)).astype(BF16)
    if has_state:
        st_scr[...] = s0_ref[...]
    else:
        st_scr[...] = jnp.zeros_like(st_scr)

    bd_ones = _block_mask(GROUP, GROUP, HEAD_DIM, HEAD_DIM).astype(BF16)
    rib = _iota((1, c, GROUP), 1)

    def slab_step(i, carry):
        r0 = pl.multiple_of(i * slab, slab)
        q3 = q_scr[pl.ds(r0, slab), :].reshape(nb, c, GROUP)
        v3 = a_ref[pl.ds(r0, slab), 3 * GROUP:4 * GROUP].reshape(nb, c, GROUP)
        o3 = jnp.zeros((nb, c, GROUP), F32)
        for d in range(2):
            bc3 = bc_scr[d, pl.ds(r0, slab), :].reshape(nb, c, GROUP)
            c3 = k_scr[d, pl.ds(r0, slab), :].reshape(nb, c, GROUP)
            for j in range(c):
                keep = (rib >= j) if d == 0 else (rib <= j)
                dec = jnp.exp2(jnp.where(keep, bc3 - c3[:, j:j + 1, :], -jnp.inf))
                pj = (dec * q3).astype(BF16).reshape(slab, GROUP)
                srep = _dot(pj, bd_ones).reshape(nb, c, GROUP)
                o3 = o3 + srep * v3[:, j:j + 1, :]
        o_scr[0, pl.ds(r0, slab), :] = o3.reshape(slab, GROUP)
        return carry

    lax.fori_loop(0, seq // slab, slab_step, 0)

    bd_mask = _block_mask(GROUP, GROUP, HEAD_DIM, HEAD_DIM)

    def body(n, carry):
        rows = [[pl.multiple_of(((n * HG_UNROLL + u) if d == 0 else nblk - 1 - (n * HG_UNROLL + u)) * c, c)
                 for u in range(HG_UNROLL)] for d in range(2)]
        upd = [[_dot_tn(a_ref[pl.ds(r0, c), 3 * GROUP:4 * GROUP].astype(BF16), kt_scr[d, pl.ds(r0, c), :])
                for r0 in rows[d]] for d in range(2)]
        for d in range(2):
            st = st_scr[d]
            for u, r0 in enumerate(rows[d]):
                o_scr[1 + d, pl.ds(r0, c), :] = _dot_nt(qt_scr[d, pl.ds(r0, c), :], st.astype(BF16))
                st = st * dec_scr[d, pl.ds(r0, 1), :] + jnp.where(bd_mask, upd[d][u], 0.0)
            st_scr[d] = st
        return carry

    lax.fori_loop(0, nblk // HG_UNROLL, body, 0)

    o = o_scr[0] + o_scr[1] + o_scr[2]
    mean_op = jnp.where(_block_mask(GROUP, GROUP, HEAD_DIM, HEAD_DIM), 1.0 / HEAD_DIM, 0.0)
    ms = _dot_sel_r(o * o, mean_op)
    y = o * lax.rsqrt(ms + NORM_EPS) * norm_ref[...]
    o_ref[...] = y * _silu(a_ref[:, 4 * GROUP:5 * GROUP])
    sfin_ref[...] = st_scr[...]


def _layer_slot(nseq, stack, tail):
    zeros = (0,) * len(tail)
    if stack is None:
        return (pl.BlockSpec((None,) + tail, lambda b: (b,) + zeros),
                jax.ShapeDtypeStruct((nseq,) + tail, F32))
    layer, depth, creating = stack
    shape = jax.ShapeDtypeStruct((nseq, depth) + tail, F32)
    if creating:
        return pl.BlockSpec((None, depth) + tail, lambda b: (b, 0) + zeros), shape
    return pl.BlockSpec((None, None) + tail, lambda b: (b, layer) + zeros), shape


def _hgrn_parts(proj_a, row0, nseq, seq, hgrn_lb, norm_t, s0, layer, stack=None):
    has_state = s0 is not None
    blk0 = row0 // seq
    in_specs = [pl.BlockSpec((seq, PROJ_A), lambda b: (blk0 + b, 0)),
                pl.BlockSpec(hgrn_lb.shape, lambda b: (0, 0, 0)),
                pl.BlockSpec((1, GROUP), lambda b: (0, 0))]
    args = [proj_a, hgrn_lb, norm_t]
    if has_state:
        in_specs.append(pl.BlockSpec((None, 2, GROUP, GROUP), lambda b: (b, 0, 0, 0)))
        args.append(s0)
    out_specs, out_shape = zip(_layer_slot(nseq, None, (seq, GROUP)),
                               _layer_slot(nseq, stack, (2, GROUP, GROUP)))
    out_specs, out_shape = list(out_specs), list(out_shape)
    scratch = [pltpu.VMEM((seq, GROUP), F32),
               pltpu.VMEM((2, seq, GROUP), F32),
               pltpu.VMEM((2, seq, GROUP), F32),
               pltpu.VMEM((2, seq, GROUP), F32),
               pltpu.VMEM((2, seq, GROUP), BF16),
               pltpu.VMEM((2, seq, GROUP), BF16),
               pltpu.VMEM((2, GROUP, GROUP), F32),
               pltpu.VMEM((3, seq, GROUP), F32)]
    return functools.partial(_hgrn_kernel, layer, has_state), in_specs, args, out_specs, out_shape, scratch


def _mla_kernel(latent, *refs):
    if latent:
        (b_ref, qn_ref, wq_ref, kvn_ref, wkv_ref, cckv_ref, ckpe_ref, cos_ref, sin_ref,
         o_ref) = refs
    else:
        (b_ref, qn_ref, wq_ref, kvn_ref, wkv_ref, o_ref, ckv_ref, kpe_ref) = refs
    seq = b_ref.shape[0]
    mckv = b_ref[:, 0:MLA_KV_RANK]
    mcq = b_ref[:, MLA_KV_RANK:MLA_KV_RANK + MLA_Q_RANK]
    kpe_t = b_ref[:, 384:512]

    cq = mcq * lax.rsqrt(jnp.mean(mcq * mcq, axis=-1, keepdims=True) + NORM_EPS) * qn_ref[...]
    qf = _dot(cq.astype(BF16), wq_ref[...])
    ckv = mckv * lax.rsqrt(jnp.mean(mckv * mckv, axis=-1, keepdims=True) + NORM_EPS) * kvn_ref[...]
    q_nope = qf[:, 0:N_HEADS * MLA_NOPE]
    q_rope = qf[:, N_HEADS * MLA_NOPE:]
    if latent:
        cos = cos_ref[...]
        sin = sin_ref[...]
        q_rope = _rope(q_rope, cos, sin)
        ckv_all = jnp.concatenate([cckv_ref[...], ckv], axis=0)
        kpe_all = jnp.concatenate([ckpe_ref[...], _rope(kpe_t, cos, sin)], axis=0)
    else:
        ckv_ref[...] = ckv
        kpe_ref[...] = kpe_t[:, 0:MLA_ROPE]
        ckv_all = ckv
        kpe_all = kpe_t
    kv = _dot(ckv_all.astype(BF16), wkv_ref[...])
    kcat = jnp.concatenate([kv[:, 0:GROUP], kpe_all], axis=1).astype(BF16)
    v = kv[:, GROUP:].astype(BF16)
    qcat = jnp.concatenate([q_nope, q_rope], axis=1)
    scale = (MLA_NOPE + MLA_ROPE) ** -0.5
    qb = min(seq, 256)
    width = qcat.shape[1]
    for r0 in range(0, seq, qb):
        qblk = qcat[r0:r0 + qb]
        acc = jnp.zeros((qb, GROUP), F32)
        for h in range(N_HEADS):
            hm = (_lane_group_mask(width, h * MLA_NOPE, MLA_NOPE)
                  | _lane_group_mask(width, N_HEADS * MLA_NOPE + h * MLA_ROPE, MLA_ROPE))
            s = _dot_nt(jnp.where(hm, qblk, 0.0).astype(BF16), kcat)
            e = jnp.exp2((s - jnp.max(s, axis=-1, keepdims=True)) * (scale * LOG2_E))
            z = jnp.sum(e, axis=-1, keepdims=True)
            oh = _dot(e.astype(BF16), v) / z
            acc = acc + jnp.where(_lane_group_mask(GROUP, h * HEAD_DIM, HEAD_DIM), oh, 0.0)
        o_ref[r0:r0 + qb, :] = acc


def _mla_parts(proj_b, row0, nseq, seq, q_norm, w_qb_p, kv_norm, w_kvb_p, latent_args, stack=None):
    latent = latent_args is not None
    blk0 = row0 // seq
    in_specs = [pl.BlockSpec((seq, PROJ_B), lambda b: (blk0 + b, 0)),
                pl.BlockSpec(q_norm.shape, lambda b: (0, 0)),
                pl.BlockSpec(w_qb_p.shape, lambda b: (0, 0)),
                pl.BlockSpec(kv_norm.shape, lambda b: (0, 0)),
                pl.BlockSpec(w_kvb_p.shape, lambda b: (0, 0))]
    args = [proj_b, q_norm, w_qb_p, kv_norm, w_kvb_p]
    out_specs = [pl.BlockSpec((None, seq, GROUP), lambda b: (b, 0, 0))]
    out_shape = [jax.ShapeDtypeStruct((nseq, seq, GROUP), F32)]
    if latent:
        cckv, ckpe_t, cos, sin = latent_args
        past = cckv.shape[1]
        in_specs += [pl.BlockSpec((None, past, MLA_KV_RANK), lambda b: (b, 0, 0)),
                     pl.BlockSpec((None, past, 128), lambda b: (b, 0, 0)),
                     pl.BlockSpec(cos.shape, lambda b: (0, 0)),
                     pl.BlockSpec(sin.shape, lambda b: (0, 0))]
        args += [cckv, ckpe_t, cos, sin]
    else:
        for width in (MLA_KV_RANK, MLA_ROPE):
            spec, shape = _layer_slot(nseq, stack, (seq, width))
            out_specs.append(spec)
            out_shape.append(shape)
    return functools.partial(_mla_kernel, latent), in_specs, args, out_specs, out_shape, []


def _diff_kernel(latent, lam_init, *refs):
    if latent:
        (c_ref, lam_ref, norm_ref, ck_ref, cv_ref, cos_ref, sin_ref, o_ref) = refs
    else:
        (c_ref, lam_ref, norm_ref, o_ref, k_ref, v_ref) = refs
    seq = c_ref.shape[0]
    dq = c_ref[:, 0:GROUP]
    dk = c_ref[:, GROUP:2 * GROUP]
    dv = c_ref[:, 2 * GROUP:3 * GROUP]
    if latent:
        cos = cos_ref[...]
        sin = sin_ref[...]
        dq = _rope(dq, cos, sin)
        k_all = jnp.concatenate([ck_ref[...], _rope(dk, cos, sin)], axis=0)
        v_all = jnp.concatenate([cv_ref[...], dv], axis=0)
    else:
        k_ref[...] = dk
        v_ref[...] = dv
        k_all = dk
        v_all = dv
    lv = lam_ref[...]
    lam = (jnp.exp(jnp.sum(lv[0:1] * lv[1:2], axis=-1, keepdims=True))
           - jnp.exp(jnp.sum(lv[2:3] * lv[3:4], axis=-1, keepdims=True)) + lam_init)
    k_bf = k_all.astype(BF16)
    v_bf = v_all.astype(BF16)
    scale = DIFF_DIM ** -0.5
    mean_op = jnp.where(_block_mask(GROUP, GROUP, HEAD_DIM, HEAD_DIM), 1.0 / HEAD_DIM, 0.0)
    qb = min(seq, 256)
    for r0 in range(0, seq, qb):
        qblk = dq[r0:r0 + qb]
        acc = jnp.zeros((qb, GROUP), F32)
        for h in range(N_HEADS):
            outs = []
            for comp in range(2):
                cm = _lane_group_mask(GROUP, h * HEAD_DIM + comp * DIFF_DIM, DIFF_DIM)
                s = _dot_nt(jnp.where(cm, qblk, 0.0).astype(BF16), k_bf)
                e = jnp.exp2((s - jnp.max(s, axis=-1, keepdims=True)) * (scale * LOG2_E))
                outs.append(_dot(e.astype(BF16), v_bf) / jnp.sum(e, axis=-1, keepdims=True))
            oh = outs[0] - lam * outs[1]
            acc = acc + jnp.where(_lane_group_mask(GROUP, h * HEAD_DIM, HEAD_DIM), oh, 0.0)
        ms = _dot_sel_r(acc * acc, mean_op)
        o_ref[r0:r0 + qb, :] = acc * lax.rsqrt(ms + NORM_EPS) * norm_ref[...] * (1.0 - lam_init)


def _diff_parts(proj_c, row0, nseq, seq, lam_p, norm_t, lam_init, latent_args, stack=None):
    latent = latent_args is not None
    blk0 = row0 // seq
    in_specs = [pl.BlockSpec((seq, PROJ_C), lambda b: (blk0 + b, 0)),
                pl.BlockSpec(lam_p.shape, lambda b: (0, 0)),
                pl.BlockSpec(norm_t.shape, lambda b: (0, 0))]
    args = [proj_c, lam_p, norm_t]
    out_specs = [pl.BlockSpec((None, seq, GROUP), lambda b: (b, 0, 0))]
    out_shape = [jax.ShapeDtypeStruct((nseq, seq, GROUP), F32)]
    if latent:
        ck, cv, cos, sin = latent_args
        past = ck.shape[1]
        in_specs += [pl.BlockSpec((None, past, GROUP), lambda b: (b, 0, 0)),
                     pl.BlockSpec((None, past, GROUP), lambda b: (b, 0, 0)),
                     pl.BlockSpec(cos.shape, lambda b: (0, 0)),
                     pl.BlockSpec(sin.shape, lambda b: (0, 0))]
        args += [ck, cv, cos, sin]
    else:
        for _ in range(2):
            spec, shape = _layer_slot(nseq, stack, (seq, GROUP))
            out_specs.append(spec)
            out_shape.append(shape)
    return functools.partial(_diff_kernel, latent, lam_init), in_specs, args, out_specs, out_shape, []


def _mixers(nseq, name, *parts, carried=None, creating=None):
    n_in = [len(p[1]) for p in parts]
    n_out = [len(p[3]) for p in parts]
    n_scr = [len(p[5]) for p in parts]
    carried = [None] * sum(n_out) if carried is None else list(carried)
    kept = [(o, arr) for o, arr in enumerate(carried) if arr is not None]

    def body(*refs):
        ins = refs[:sum(n_in)]
        outs = list(refs[sum(n_in) + len(kept):sum(n_in) + len(kept) + sum(n_out)])
        scr = refs[sum(n_in) + len(kept) + sum(n_out):]
        if creating is not None:
            layer, stacked = creating
            for o in stacked:
                full = outs[o]
                for m in range(full.shape[0]):
                    if m != layer:
                        full[m] = jnp.zeros(full.shape[1:], full.dtype)
                outs[o] = full.at[layer]
        i0 = o0 = s0 = 0
        for part, ni, no, ns in zip(parts, n_in, n_out, n_scr):
            part[0](*ins[i0:i0 + ni], *outs[o0:o0 + no], *scr[s0:s0 + ns])
            i0, o0, s0 = i0 + ni, o0 + no, s0 + ns

    res = pl.pallas_call(
        body,
        grid=(nseq,),
        in_specs=[x for p in parts for x in p[1]] + [pl.BlockSpec(memory_space=pl.ANY)] * len(kept),
        out_specs=[x for p in parts for x in p[3]],
        out_shape=[x for p in parts for x in p[4]],
        scratch_shapes=[x for p in parts for x in p[5]],
        input_output_aliases={sum(n_in) + k: o for k, (o, _) in enumerate(kept)},
        compiler_params=_cparams("arbitrary"),
        name=name,
    )(*[x for p in parts for x in p[2]], *[arr for _, arr in kept])
    out, o0 = [], 0
    for no in n_out:
        out.append(tuple(res[o0:o0 + no]))
        o0 += no
    return out


def _ssd_kernel(has_state, *refs):
    if has_state:
        (d_ref, cw_ref, cb_ref, dtb_ref, alog_ref, dskip_ref, norm_ref, s0_ref,
         o_ref, sfin_ref, xs_scr, bm_scr, cm_scr, xdt_scr, a_scr, st_scr, yf_scr, yb_scr) = refs
    else:
        (d_ref, cw_ref, cb_ref, dtb_ref, alog_ref, dskip_ref, norm_ref,
         o_ref, sfin_ref, xs_scr, bm_scr, cm_scr, xdt_scr, a_scr, st_scr, yf_scr, yb_scr) = refs
    seq = d_ref.shape[0]
    c = SSD_CHUNK
    nchunk = seq // c
    ngrp = 2 * SSD_STATE

    xin = d_ref[:, GROUP:GROUP + 512]
    rows = _iota(xin.shape, 0)
    prev = jnp.where(rows == 0, 0.0, pltpu.roll(xin, 1, 0))
    nxt = jnp.where(rows == seq - 1, 0.0, pltpu.roll(xin, seq - 1, 0))
    cw = cw_ref[...]
    xbc = _silu(cw[0:1] * prev + cw[1:2] * xin + cw[2:3] * nxt + cb_ref[...])
    xs = xbc[:, 0:GROUP]
    xs_scr[...] = xs
    bm_scr[...] = xbc[:, GROUP:GROUP + ngrp]
    cm_scr[...] = xbc[:, GROUP + ngrp:GROUP + 2 * ngrp]
    dt = _softplus(d_ref[:, GROUP + 512:GROUP + 640] + dtb_ref[...])
    a_scr[...] = dt * (-jnp.exp(alog_ref[...]))
    erow = _iota((128, GROUP), 0)
    ehead = _iota((128, GROUP), 1) // HEAD_DIM
    expand = tuple((erow == 4 * d + ehead).astype(F32) for d in range(2))
    for d in range(2):
        xdt_scr[d] = xs * _dot_sel_r(dt, expand[d])
    if has_state:
        st_scr[...] = s0_ref[...]
    else:
        st_scr[...] = jnp.zeros_like(st_scr)

    row = _iota((c, c), 0)
    col = _iota((c, c), 1)
    tri = ((col <= row).astype(F32), (col >= row).astype(F32))
    keep = (col <= row, col >= row)
    grp_lane = _iota((1, ngrp), 1) // SSD_STATE
    valid = (_iota((ngrp, GROUP), 0) // SSD_STATE) == (_iota((ngrp, GROUP), 1) // (2 * HEAD_DIM))

    def chunk_step(d, r0, out_scr):
        a_c = a_scr[pl.ds(r0, c), :]
        bm_c = bm_scr[pl.ds(r0, c), :]
        cm_c = cm_scr[pl.ds(r0, c), :].astype(BF16)
        xdt_c = xdt_scr[d, pl.ds(r0, c), :]
        acum = _dot_sel(tri[d], a_c)
        acum_t = acum.T
        acum_rep = _dot_sel_r(acum, expand[d])
        bm2 = jnp.concatenate([jnp.where(grp_lane == g, bm_c, 0.0) for g in range(2)], axis=0)
        cb = _dot_nt(cm_c, bm2.astype(BF16))
        scores = []
        xparts = []
        for h in range(N_HEADS):
            lane = 4 * d + h
            seg = jnp.exp(jnp.where(keep[d], acum[:, lane:lane + 1] - acum_t[lane:lane + 1, :], -jnp.inf))
            g = h // 2
            scores.append((cb[:, g * c:(g + 1) * c] * seg).astype(BF16))
            xparts.append(jnp.where(_lane_group_mask(GROUP, h * HEAD_DIM, HEAD_DIM), xdt_c, 0.0))
        y = _dot(jnp.concatenate(scores, axis=1), jnp.concatenate(xparts, axis=0).astype(BF16))
        st = st_scr[d]
        y = y + _dot(cm_c, st.astype(BF16)) * jnp.exp(acum_rep)
        out_scr[pl.ds(r0, c), :] = y
        edge = acum_rep[c - 1:c] if d == 0 else acum_rep[0:1]
        xt = (xdt_c * jnp.exp(edge - acum_rep)).astype(BF16)
        upd = _dot_tn(bm_c.astype(BF16), xt)
        st_scr[d] = st * jnp.exp(edge) + jnp.where(valid, upd, 0.0)

    def body(n, carry):
        chunk_step(0, pl.multiple_of(n * c, c), yf_scr)
        chunk_step(1, pl.multiple_of((nchunk - 1 - n) * c, c), yb_scr)
        return carry

    lax.fori_loop(0, nchunk, body, 0)

    y = yf_scr[...] + yb_scr[...] + dskip_ref[...] * xs_scr[...]
    y = y * _silu(d_ref[:, 0:GROUP])
    o_ref[...] = y * lax.rsqrt(jnp.mean(y * y, axis=-1, keepdims=True) + NORM_EPS) * norm_ref[...]
    sfin_ref[...] = st_scr[...]


def _ssd_parts(proj_d, row0, nseq, seq, conv_w, conv_b, dt_bias_p, a_log_p, d_rep, norm, s0, stack=None):
    has_state = s0 is not None
    blk0 = row0 // seq
    ngrp = 2 * SSD_STATE
    small = [conv_w, conv_b, dt_bias_p, a_log_p, d_rep, norm]
    in_specs = ([pl.BlockSpec((seq, PROJ_D), lambda b: (blk0 + b, 0))]
                + [pl.BlockSpec(s.shape, lambda b: (0, 0)) for s in small])
    args = [proj_d] + small
    if has_state:
        in_specs.append(pl.BlockSpec((None, 2, ngrp, GROUP), lambda b: (b, 0, 0, 0)))
        args.append(s0)
    out_specs, out_shape = zip(_layer_slot(nseq, None, (seq, GROUP)),
                               _layer_slot(nseq, stack, (2, ngrp, GROUP)))
    out_specs, out_shape = list(out_specs), list(out_shape)
    scratch = [pltpu.VMEM((seq, GROUP), F32),
               pltpu.VMEM((seq, ngrp), F32),
               pltpu.VMEM((seq, ngrp), F32),
               pltpu.VMEM((2, seq, GROUP), F32),
               pltpu.VMEM((seq, 128), F32),
               pltpu.VMEM((2, ngrp, GROUP), F32),
               pltpu.VMEM((seq, GROUP), F32),
               pltpu.VMEM((seq, GROUP), F32)]
    return functools.partial(_ssd_kernel, has_state), in_specs, args, out_specs, out_shape, scratch


def _outproj_kernel(n_ctx_tiles, *refs):
    ctx_refs, lat_refs = refs[0:4], refs[4:8]
    w_ref, x_ref, m_ref, g_ref, b_ref, o_ref = refs[8:]
    is_ctx = pl.program_id(0) < n_ctx_tiles
    mixed = None
    for i, (c_ref, l_ref) in enumerate(zip(ctx_refs, lat_refs)):
        part = jnp.where(is_ctx, c_ref[...], l_ref[...]).astype(BF16)
        term = _dot(part, w_ref[i * GROUP:(i + 1) * GROUP, :])
        mixed = term if mixed is None else mixed + term
    o_ref[...] = _layernorm(ALPHA * x_ref[...] + m_ref[2] * mixed, g_ref[...], b_ref[...])


def _outproj(parts_ctx, parts_lat, w_out_bf, x, mods_l, ln_g, ln_b, t_ctx, s_lat):
    t, d = x.shape
    tm = 1024
    n_ctx = t_ctx // tm
    n_lat = (t - t_ctx) // tm
    ctx_spec = pl.BlockSpec((tm, GROUP), lambda i: (jnp.minimum(i, n_ctx - 1), 0))
    lat_spec = pl.BlockSpec((tm, GROUP), lambda i: (jnp.clip(i - n_ctx, 0, n_lat - 1), 0))
    return pl.pallas_call(
        functools.partial(_outproj_kernel, n_ctx),
        grid=(t // tm,),
        in_specs=[ctx_spec] * 4 + [lat_spec] * 4
        + [pl.BlockSpec(w_out_bf.shape, lambda i: (0, 0)),
           pl.BlockSpec((tm, d), lambda i: (i, 0)),
           pl.BlockSpec((None, 6, 1, d), _mod_row_map(tm, t_ctx, s_lat)),
           pl.BlockSpec((1, d), lambda i: (0, 0)),
           pl.BlockSpec((1, d), lambda i: (0, 0))],
        out_specs=pl.BlockSpec((tm, d), lambda i: (i, 0)),
        out_shape=jax.ShapeDtypeStruct((t, d), F32),
        compiler_params=_cparams("arbitrary"),
        name="outproj_ln",
    )(*parts_ctx, *parts_lat, w_out_bf, x, mods_l, ln_g, ln_b)


def _top_rows(s, k, extra=()):
    r = s.shape[0]
    rid = _iota(s.shape, 0).astype(F32)
    vals, ids = [], []
    picked = [[] for _ in extra]
    for _ in range(k):
        m = jnp.max(s, axis=0, keepdims=True)
        cand = jnp.where(s == m, rid, float(r))
        i = jnp.min(cand, axis=0, keepdims=True)
        hit = cand == i
        vals.append(m)
        ids.append(i)
        for lst, arr in zip(picked, extra):
            lst.append(jnp.max(jnp.where(hit, arr, -1.0), axis=0, keepdims=True))
        s = jnp.where(hit, -jnp.inf, s)
    cat = lambda xs: jnp.concatenate(xs, axis=0)
    return cat(vals), cat(ids), [cat(p) for p in picked]


def _router_kernel(x_ref, m_ref, wq_ref, keys_ref, h_ref, a_ref, b_ref, g_ref):
    tm = x_ref.shape[0]
    hb = (x_ref[...] * (1.0 + m_ref[4]) + m_ref[3]).astype(BF16)
    h_ref[...] = hb
    qt = _dot(hb, wq_ref[...]).T.astype(BF16)
    k = PEER_TOPK
    code_rows, g_rows = [], []
    for head in range(PEER_HEADS):
        tv, ti = [], []
        for half in range(2):
            g = 2 * head + half
            sc = _dot(keys_ref[g], qt[g * PEER_HALF:(g + 1) * PEER_HALF])
            v, i, _ = _top_rows(sc, k)
            tv.append(v)
            ti.append(i)
        cs = [tv[0][0:1] + tv[1]]
        ca = [jnp.broadcast_to(ti[0][0:1], (k, tm))]
        cb = [ti[1]]
        for k1 in range(1, 4):
            cs.append(tv[0][k1:k1 + 1] + tv[1][0:8])
            ca.append(jnp.broadcast_to(ti[0][k1:k1 + 1], (8, tm)))
            cb.append(ti[1][0:8])
        low = _iota((8, tm), 0) < 4
        v2_dup = jnp.where(low, tv[1][0:8], pltpu.roll(tv[1][0:8], 4, 0))
        i2_dup = jnp.where(low, ti[1][0:8], pltpu.roll(ti[1][0:8], 4, 0))
        for k1 in (4, 6):
            cs.append(jnp.where(low, tv[0][k1:k1 + 1], tv[0][k1 + 1:k1 + 2]) + v2_dup)
            ca.append(jnp.where(low, ti[0][k1:k1 + 1], ti[0][k1 + 1:k1 + 2]))
            cb.append(i2_dup)
        cs.append(tv[0][8:16] + tv[1][0:1])
        ca.append(ti[0][8:16])
        cb.append(jnp.broadcast_to(ti[1][0:1], (8, tm)))
        code = jnp.concatenate(ca, axis=0) * float(PEER_KEYS) + jnp.concatenate(cb, axis=0)
        best, _, (sel_code,) = _top_rows(jnp.concatenate(cs, axis=0), k, extra=(code,))
        e = jnp.exp(best - best[0:1])
        g_rows.append(e / jnp.sum(e, axis=0, keepdims=True))
        code_rows.append(sel_code)
    codes = jnp.concatenate(code_rows, axis=0)
    key1 = jnp.floor(codes * (1.0 / PEER_KEYS))
    a_ref[...] = key1.T.astype(I32)
    b_ref[...] = (codes - key1 * float(PEER_KEYS)).T.astype(I32)
    g_ref[...] = jnp.concatenate(g_rows, axis=0).T


def _router(x, mods_l, wq_bf, keys, t_ctx, s_lat):
    t, d = x.shape
    tm = 256
    nslot = PEER_HEADS * PEER_TOPK
    return pl.pallas_call(
        _router_kernel,
        grid=(t // tm,),
        in_specs=[pl.BlockSpec((tm, d), lambda i: (i, 0)),
                  pl.BlockSpec((None, 6, 1, d), _mod_row_map(tm, t_ctx, s_lat)),
                  pl.BlockSpec(wq_bf.shape, lambda i: (0, 0)),
                  pl.BlockSpec(keys.shape, lambda i: (0, 0, 0))],
        out_specs=[pl.BlockSpec((tm, d), lambda i: (i, 0)),
                   pl.BlockSpec((tm, nslot), lambda i: (i, 0)),
                   pl.BlockSpec((tm, nslot), lambda i: (i, 0)),
                   pl.BlockSpec((tm, nslot), lambda i: (i, 0))],
        out_shape=[jax.ShapeDtypeStruct((t, d), BF16),
                   jax.ShapeDtypeStruct((t, nslot), I32),
                   jax.ShapeDtypeStruct((t, nslot), I32),
                   jax.ShapeDtypeStruct((t, nslot), F32)],
        compiler_params=_cparams("arbitrary"),
        name="peer_router",
    )(x, mods_l, wq_bf, keys)


def _gates_kernel(a_ref, b_ref, g_ref, u_ref, v_ref, o_ref, ub_ref, vb_ref):
    tm = a_ref.shape[0]
    n = PEER_KEYS
    sub = 16
    ub_ref[...] = u_ref[...].astype(BF16)
    vb_ref[...] = v_ref[...].astype(BF16)
    key = _iota((sub, n, a_ref.shape[2]), 1).astype(F32).astype(BF16)
    zero = jnp.zeros((), BF16)
    for t0 in range(0, tm, sub):
        a = a_ref[t0:t0 + sub].astype(F32).astype(BF16)
        b = b_ref[t0:t0 + sub].astype(F32).astype(BF16)
        g = g_ref[t0:t0 + sub].astype(BF16)
        onehot_a = jnp.where(key == a, jnp.ones((), BF16), zero)
        gated_b = jnp.where(key == b, g, zero)
        w = lax.dot_general(onehot_a, gated_b, (((2,), (2,)), ((0,), (0,))),
                            preferred_element_type=F32)
        w_t = jnp.swapaxes(w.astype(BF16), 0, 1)
        for r in range(n):
            o_ref[t0:t0 + sub, r * n:(r + 1) * n] = w_t[r]


def _gates(a_idx, b_idx, gate, peer_u, peer_v, layer):
    t, nslot = a_idx.shape
    d = peer_u.shape[-1]
    tm = 192
    steps = t // tm
    te = N_EXPERTS // steps
    spec = pl.BlockSpec((tm, 1, nslot), lambda i: (i, 0, 0))
    tab_in = pl.BlockSpec((None, te, d), lambda i: (layer, i, 0))
    tab_out = pl.BlockSpec((te, d), lambda i: (i, 0))
    return pl.pallas_call(
        _gates_kernel,
        grid=(steps,),
        in_specs=[spec, spec, spec, tab_in, tab_in],
        out_specs=[pl.BlockSpec((tm, N_EXPERTS), lambda i: (i, 0)), tab_out, tab_out],
        out_shape=[jax.ShapeDtypeStruct((t, N_EXPERTS), BF16),
                   jax.ShapeDtypeStruct((N_EXPERTS, d), BF16),
                   jax.ShapeDtypeStruct((N_EXPERTS, d), BF16)],
        compiler_params=_cparams("arbitrary"),
        name="peer_gates",
    )(a_idx.reshape(t, 1, nslot), b_idx.reshape(t, 1, nslot), gate.reshape(t, 1, nslot), peer_u, peer_v)


def _experts_kernel(h_ref, u_ref, v_ref, w_ref, x_ref, m_ref, g_ref, b_ref, o_ref, acc_ref):
    j = pl.program_id(1)

    @pl.when(j == 0)
    def _():
        acc_ref[...] = jnp.zeros_like(acc_ref)

    act = _gelu_tanh(_dot_nt(h_ref[...], u_ref[...]))
    acc_ref[...] += _dot((act * w_ref[...].astype(F32)).astype(BF16), v_ref[...])

    @pl.when(j == pl.num_programs(1) - 1)
    def _():
        o_ref[...] = _layernorm(ALPHA * x_ref[...] + m_ref[5] * acc_ref[...], g_ref[...], b_ref[...])


def _experts(h_bf, u_bf, v_bf, w_gate, x, mods_l, ln_g, ln_b, t_ctx, s_lat):
    t, d = x.shape
    tm, te = 1024, 1024
    return pl.pallas_call(
        _experts_kernel,
        grid=(t // tm, N_EXPERTS // te),
        in_specs=[pl.BlockSpec((tm, d), lambda i, j: (i, 0)),
                  pl.BlockSpec((te, d), lambda i, j: (j, 0)),
                  pl.BlockSpec((te, d), lambda i, j: (j, 0)),
                  pl.BlockSpec((tm, te), lambda i, j: (i, j)),
                  pl.BlockSpec((tm, d), lambda i, j: (i, 0)),
                  pl.BlockSpec((None, 6, 1, d), _mod_row_map(tm, t_ctx, s_lat)),
                  pl.BlockSpec((1, d), lambda i, j: (0, 0)),
                  pl.BlockSpec((1, d), lambda i, j: (0, 0))],
        out_specs=pl.BlockSpec((tm, d), lambda i, j: (i, 0)),
        out_shape=jax.ShapeDtypeStruct((t, d), F32),
        scratch_shapes=[pltpu.VMEM((tm, d), F32)],
        compiler_params=_cparams("arbitrary", "arbitrary"),
        name="peer_experts",
    )(h_bf, u_bf, v_bf, w_gate, x, mods_l, ln_g, ln_b)


def _pad_cols(w, n):
    return jnp.concatenate([w, jnp.zeros((w.shape[0], n), w.dtype)], axis=1) if n else w


def _layout_w_in(w):
    a = w[:, 0:1280]
    mcq, mckv, mkpe = w[:, 1280:1472], w[:, 1472:1600], w[:, 1600:1632]
    b = jnp.concatenate([mckv, _pad_cols(mcq, 64), mkpe, mkpe, mkpe, mkpe], axis=1)
    c = w[:, 1632:2400]
    d = _pad_cols(w[:, 2400:3176], PROJ_D - 776)
    return jnp.concatenate([a, b, c, d], axis=1).astype(BF16)


def _layout_w_qb(w):
    w4 = w.reshape(MLA_Q_RANK, N_HEADS, MLA_NOPE + MLA_ROPE)
    return jnp.concatenate([w4[:, :, :MLA_NOPE].reshape(MLA_Q_RANK, -1),
                            w4[:, :, MLA_NOPE:].reshape(MLA_Q_RANK, -1)], axis=1).astype(BF16)


def _layout_w_kvb(w):
    w4 = w.reshape(MLA_KV_RANK, N_HEADS, MLA_NOPE + HEAD_DIM)
    return jnp.concatenate([w4[:, :, :MLA_NOPE].reshape(MLA_KV_RANK, -1),
                            w4[:, :, MLA_NOPE:].reshape(MLA_KV_RANK, -1)], axis=1).astype(BF16)


def _rope_tables(seq):
    rows = seq // GRID_W
    row = jnp.repeat(jnp.arange(rows, dtype=F32), GRID_W)
    col = jnp.tile(jnp.arange(GRID_W, dtype=F32), rows)
    freqs = ROPE_BASE ** (-jnp.arange(ROPE_PAIRS, dtype=F32) / ROPE_PAIRS)
    cos_l, sin_l = [], []
    for pos in (row, col):
        ang = pos[:, None] * freqs
        cos_l += [jnp.cos(ang), jnp.cos(ang)]
        sin_l += [-jnp.sin(ang), jnp.sin(ang)]
    return jnp.concatenate(cos_l, axis=1), jnp.concatenate(sin_l, axis=1)


def _hgrn_state_pack(st):
    b = st.shape[0]
    st_t = jnp.swapaxes(st, -1, -2)
    zero = jnp.zeros_like(st_t[:, :, 0])
    rows = [jnp.concatenate([st_t[:, :, h] if g == h else zero for g in range(N_HEADS)], axis=-1)
            for h in range(N_HEADS)]
    return jnp.concatenate(rows, axis=-2).reshape(b, 2, GROUP, GROUP)


def _hgrn_state_unpack(sb):
    blocks = [sb[..., h * HEAD_DIM:(h + 1) * HEAD_DIM, h * HEAD_DIM:(h + 1) * HEAD_DIM]
              for h in range(N_HEADS)]
    return jnp.swapaxes(jnp.stack(blocks, axis=-3), -1, -2)


def _ssd_state_pack(st):
    st_t = jnp.swapaxes(st, -1, -2)
    zero = jnp.zeros_like(st_t[:, :, 0])
    rows = [jnp.concatenate([st_t[:, :, h] if h // 2 == g else zero for h in range(N_HEADS)], axis=-1)
            for g in range(2)]
    return jnp.concatenate(rows, axis=-2)


def _ssd_state_unpack(sb):
    blocks = [sb[..., (h // 2) * SSD_STATE:(h // 2 + 1) * SSD_STATE, h * HEAD_DIM:(h + 1) * HEAD_DIM]
              for h in range(N_HEADS)]
    return jnp.swapaxes(jnp.stack(blocks, axis=-3), -1, -2)


def _tile_lanes(v, n):
    return jnp.tile(v.reshape(1, -1), (1, n))


def kernel(x_prompt, x_sample, cache_mla_ckv, cache_mla_kpe, cache_diff_k, cache_diff_v, state_hgrn, state_ssd, c, c_ctx, w_mod, b_mod, w_in, hgrn_lb, hgrn_norm, mla_q_norm, mla_w_qb, mla_kv_norm, mla_w_kvb, diff_lambda, diff_norm, ssd_conv_w, ssd_conv_b, ssd_dt_bias, ssd_a_log, ssd_d, ssd_norm, w_out, ln1_g, ln1_b, peer_wq, peer_keys, peer_u, peer_v, ln2_g, ln2_b):
    nb, seq, d = x_prompt.shape
    nlat, lseq, _ = x_sample.shape
    depth = w_in.shape[0]
    t_ctx = nb * seq
    x = jnp.concatenate([x_prompt.reshape(t_ctx, d), x_sample.reshape(nlat * lseq, d)], axis=0)

    cond8 = jnp.concatenate([c_ctx.reshape(1, d), c, jnp.zeros((8 - 1 - nlat, d), F32)], axis=0)
    mods = _mods(cond8, w_mod, b_mod)
    mods = mods[:, :1 + nlat].reshape(depth, 1 + nlat, 6, 1, d)

    cos32, sin32 = _rope_tables(lseq)
    cos128, sin128 = jnp.tile(cos32, (1, 4)), jnp.tile(sin32, (1, 4))
    cos256, sin256 = jnp.tile(cos32, (1, 8)), jnp.tile(sin32, (1, 8))

    carried = None
    for l in range(depth):
        mods_l = mods[l]
        pa, pb, pc, pd = _inproj(x, mods_l, _layout_w_in(w_in[l]), t_ctx, lseq)

        norm_hg = _tile_lanes(hgrn_norm[l], N_HEADS)
        mla_w = (mla_q_norm[l].reshape(1, -1), _layout_w_qb(mla_w_qb[l]),
                 mla_kv_norm[l].reshape(1, -1), _layout_w_kvb(mla_w_kvb[l]))
        lam_init = 0.8 - 0.6 * math.exp(-0.3 * l)
        norm_df = _tile_lanes(diff_norm[l], N_HEADS)
        past = cache_diff_k.shape[2]
        ssd_w = (ssd_conv_w[l], ssd_conv_b[l].reshape(1, -1),
                 _pad_cols(ssd_dt_bias[l].reshape(1, -1), 120), _pad_cols(ssd_a_log[l].reshape(1, -1), 120),
                 jnp.repeat(ssd_d[l], HEAD_DIM).reshape(1, -1), ssd_norm[l].reshape(1, -1))

        stack = (l, depth, carried is None)
        ctx_out = _mixers(
            nb, "mixers_ctx",
            _mla_parts(pb, 0, nb, seq, *mla_w, None, stack),
            _diff_parts(pc, 0, nb, seq, diff_lambda[l], norm_df, lam_init, None, stack),
            _ssd_parts(pd, 0, nb, seq, *ssd_w, None, stack),
            _hgrn_parts(pa, 0, nb, seq, hgrn_lb, norm_hg, None, l, stack),
            carried=carried, creating=(l, (1, 2, 4, 5, 7, 9)) if carried is None else None)
        (mla_ctx, new_ckv, new_kpe), (df_ctx, new_dk, new_dv), (ssd_ctx, ssd_fin), (hg_ctx, hg_fin) = ctx_out
        carried = [None, new_ckv, new_kpe, None, new_dk, new_dv, None, ssd_fin, None, hg_fin]
        ((mla_lat,),) = _mixers(
            nlat, "mla_lat",
            _mla_parts(pb, t_ctx, nlat, lseq, *mla_w,
                       (cache_mla_ckv[:, l], jnp.tile(cache_mla_kpe[:, l], (1, 1, 4)), cos128, sin128)))
        ((df_lat,),) = _mixers(
            nlat, "diffattn_lat",
            _diff_parts(pc, t_ctx, nlat, lseq, diff_lambda[l], norm_df, lam_init,
                        (cache_diff_k[:, l].reshape(nlat, past, GROUP),
                         cache_diff_v[:, l].reshape(nlat, past, GROUP), cos256, sin256)))
        ((ssd_lat, _),) = _mixers(nlat, "ssd_lat",
                                  _ssd_parts(pd, t_ctx, nlat, lseq, *ssd_w, _ssd_state_pack(state_ssd[:, l])))
        ((hg_lat, _),) = _mixers(nlat, "hgrn_lat",
                                 _hgrn_parts(pa, t_ctx, nlat, lseq, hgrn_lb, norm_hg,
                                             _hgrn_state_pack(state_hgrn[:, l]), l))

        parts_ctx = [a.reshape(t_ctx, GROUP) for a in (hg_ctx, mla_ctx, df_ctx, ssd_ctx)]
        parts_lat = [a.reshape(nlat * lseq, GROUP) for a in (hg_lat, mla_lat, df_lat, ssd_lat)]
        x = _outproj(parts_ctx, parts_lat, w_out[l].astype(BF16), x, mods_l,
                     ln1_g[l].reshape(1, d), ln1_b[l].reshape(1, d), t_ctx, lseq)

        keys = peer_keys[l].reshape(2 * PEER_HEADS, PEER_KEYS, PEER_HALF).astype(BF16)
        h_bf, a_idx, b_idx, gate = _router(x, mods_l, peer_wq[l].astype(BF16), keys, t_ctx, lseq)
        w_gate, u_bf, v_bf = _gates(a_idx, b_idx, gate, peer_u, peer_v, l)
        x = _experts(h_bf, u_bf, v_bf, w_gate, x, mods_l,
                     ln2_g[l].reshape(1, d), ln2_b[l].reshape(1, d), t_ctx, lseq)

    y_prompt = x[:t_ctx].reshape(nb, seq, d)
    y_sample = x[t_ctx:].reshape(nlat, lseq, d)
    return (y_prompt, y_sample, new_ckv, new_kpe,
            new_dk.reshape(nb, depth, seq, N_HEADS, 2, DIFF_DIM),
            new_dv.reshape(nb, depth, seq, N_HEADS, 2 * DIFF_DIM),
            _hgrn_state_unpack(hg_fin), _ssd_state_unpack(ssd_fin))
```

```python
import functools
import math

import jax
import jax.numpy as jnp
from jax import lax
from jax.experimental import pallas as pl
from jax.experimental.pallas import tpu as pltpu

F32 = jnp.float32
BF16 = jnp.bfloat16
I32 = jnp.int32

D_MODEL = 1024
GROUP = 256
N_HEADS = 4
HEAD_DIM = 64
HG_BLOCK = 16
HG_SLAB = 256
HG_UNROLL = 16
SSD_CHUNK = 128
SSD_STATE = 64
MLA_Q_RANK = 192
MLA_KV_RANK = 128
MLA_NOPE = 64
MLA_ROPE = 32
DIFF_DIM = 32
GRID_W = 64
ROPE_PAIRS = 8
ROPE_BASE = 10000.0
PEER_HEADS = 8
PEER_KEYS = 128
PEER_TOPK = 16
PEER_HALF = 64
N_EXPERTS = PEER_KEYS * PEER_KEYS
NORM_EPS = 1e-6
LN_EPS = 1e-5
DEPTH = 2
ALPHA = (2.0 * DEPTH) ** 0.25
LOG2_E = 1.4426950408889634

PROJ_A = 5 * GROUP
PROJ_B = 512
PROJ_C = 3 * GROUP
PROJ_D = 896
VMEM_LIMIT = 56 * 1024 * 1024


def _cparams(*sem):
    return pltpu.CompilerParams(dimension_semantics=sem, vmem_limit_bytes=VMEM_LIMIT)


def _sigmoid(x):
    return 1.0 / (1.0 + jnp.exp(-x))


def _silu(x):
    return x * _sigmoid(x)


def _softplus(x):
    return jnp.maximum(x, 0.0) + jnp.log(1.0 + jnp.exp(-jnp.abs(x)))


def _gelu_tanh(x):
    return 0.5 * x * (1.0 + jnp.tanh(math.sqrt(2.0 / math.pi) * (x + 0.044715 * (x * x * x))))


def _dot(a, b):
    return jnp.dot(a, b, preferred_element_type=F32)


def _dot_nt(a, b):
    return lax.dot_general(a, b, (((1,), (1,)), ((), ())), preferred_element_type=F32)


def _dot_tn(a, b):
    return lax.dot_general(a, b, (((0,), (0,)), ((), ())), preferred_element_type=F32)


def _split3(x):
    hi = x.astype(BF16)
    rest = x - hi.astype(F32)
    mid = rest.astype(BF16)
    lo = (rest - mid.astype(F32)).astype(BF16)
    return hi, mid, lo


def _dot_sel(sel, x):
    sel = sel.astype(BF16)
    hi, mid, lo = _split3(x)
    return _dot(sel, hi) + _dot(sel, mid) + _dot(sel, lo)


def _dot_sel_r(x, sel):
    sel = sel.astype(BF16)
    hi, mid, lo = _split3(x)
    return _dot(hi, sel) + _dot(mid, sel) + _dot(lo, sel)


def _iota(shape, dim):
    return lax.broadcasted_iota(I32, shape, dim)


def _block_mask(rows, cols, rblk, cblk):
    return (_iota((rows, cols), 0) // rblk) == (_iota((rows, cols), 1) // cblk)


def _lane_group_mask(width, start, size):
    lane = _iota((1, width), 1)
    return (lane >= start) & (lane < start + size)


def _layernorm(v, g, b):
    mu = jnp.mean(v, axis=-1, keepdims=True)
    d = v - mu
    var = jnp.mean(d * d, axis=-1, keepdims=True)
    return d * lax.rsqrt(var + LN_EPS) * g + b


def _swap_halves16(x):
    width = x.shape[-1]
    lane = _iota(x.shape, x.ndim - 1)
    up = pltpu.roll(x, width - 8, x.ndim - 1)
    down = pltpu.roll(x, 8, x.ndim - 1)
    return jnp.where((lane % 16) < 8, up, down)


def _rope(x, cos, sin_signed):
    return x * cos + _swap_halves16(x) * sin_signed


def _mods_kernel(c_ref, w_ref, b_ref, o_ref):
    s = _silu(c_ref[...]).astype(BF16)
    o_ref[...] = _dot(s, w_ref[...].astype(BF16)) + b_ref[...]


def _mods(cond8, w_mod, b_mod):
    depth, d, n = w_mod.shape
    tn = 3072
    return pl.pallas_call(
        _mods_kernel,
        grid=(depth, n // tn),
        in_specs=[pl.BlockSpec((8, d), lambda l, j: (0, 0)),
                  pl.BlockSpec((None, d, tn), lambda l, j: (l, 0, j)),
                  pl.BlockSpec((None, 1, tn), lambda l, j: (l, 0, j))],
        out_specs=pl.BlockSpec((None, 8, tn), lambda l, j: (l, 0, j)),
        out_shape=jax.ShapeDtypeStruct((depth, 8, n), F32),
        compiler_params=_cparams("arbitrary", "arbitrary"),
        name="mods",
    )(cond8, w_mod, b_mod.reshape(depth, 1, n))


def _mod_row_map(tm, t_ctx, s_lat):
    def index_map(i, *_):
        start = i * tm
        return (jnp.where(start < t_ctx, 0, 1 + (start - t_ctx) // s_lat), 0, 0, 0)
    return index_map


def _inproj_kernel(x_ref, m_ref, w_ref, oa_ref, ob_ref, oc_ref, od_ref):
    h = (x_ref[...] * (1.0 + m_ref[1]) + m_ref[0]).astype(BF16)
    start = 0
    for o_ref in (oa_ref, ob_ref, oc_ref, od_ref):
        width = o_ref.shape[-1]
        o_ref[...] = _dot(h, w_ref[:, start:start + width])
        start += width


def _inproj(x, mods_l, w_in_p, t_ctx, s_lat):
    t, d = x.shape
    tm = 512
    widths = (PROJ_A, PROJ_B, PROJ_C, PROJ_D)
    return pl.pallas_call(
        _inproj_kernel,
        grid=(t // tm,),
        in_specs=[pl.BlockSpec((tm, d), lambda i: (i, 0)),
                  pl.BlockSpec((None, 6, 1, d), _mod_row_map(tm, t_ctx, s_lat)),
                  pl.BlockSpec(w_in_p.shape, lambda i: (0, 0))],
        out_specs=[pl.BlockSpec((tm, w), lambda i: (i, 0)) for w in widths],
        out_shape=[jax.ShapeDtypeStruct((t, w), F32) for w in widths],
        compiler_params=_cparams("arbitrary"),
        name="inproj",
    )(x, mods_l, w_in_p)


def _hgrn_kernel(layer, has_state, *refs):
    if has_state:
        (a_ref, lb_ref, norm_ref, s0_ref, o_ref, sfin_ref,
         q_scr, k_scr, bc_scr, dec_scr, qt_scr, kt_scr, st_scr, o_scr) = refs
    else:
        (a_ref, lb_ref, norm_ref, o_ref, sfin_ref,
         q_scr, k_scr, bc_scr, dec_scr, qt_scr, kt_scr, st_scr, o_scr) = refs
        s0_ref = None
    seq = a_ref.shape[0]
    c = HG_BLOCK
    nblk = seq // c
    slab = HG_SLAB
    nb = slab // c

    lbp = lb_ref[...]
    e = jnp.exp(lbp - jnp.max(lbp, axis=0, keepdims=True))
    p = e / jnp.sum(e, axis=0, keepdims=True)
    lower = jnp.sum(p[1:layer + 1], axis=0) if layer > 0 else jnp.zeros_like(p[0])

    q = _silu(a_ref[:, 0:GROUP])
    q_scr[...] = q
    srow = _iota((slab, slab), 0)
    scol = _iota((slab, slab), 1)
    same = (srow // c) == (scol // c)
    cum_op = (jnp.where(same & (scol <= srow), 1.0, 0.0), jnp.where(same & (scol >= srow), 1.0, 0.0))
    for d in range(2):
        lb = lower[d:d + 1]
        f = lb + (1.0 - lb) * _sigmoid(a_ref[:, (1 + d) * GROUP:(2 + d) * GROUP])
        k = 1.0 - f
        lf = jnp.log(f)
        for s0 in range(0, seq, slab):
            bc = _dot_sel(cum_op[d], lf[s0:s0 + slab])
            bc3 = bc.reshape(nb, c, GROUP)
            edge = bc3[:, c - 1:c, :] if d == 0 else bc3[:, 0:1, :]
            tot = jnp.broadcast_to(edge, (nb, c, GROUP)).reshape(slab, GROUP)
            bc_scr[d, s0:s0 + slab, :] = bc * LOG2_E
            k_scr[d, s0:s0 + slab, :] = (bc - jnp.log(k[s0:s0 + slab])) * LOG2_E
            dec_scr[d, s0:s0 + slab, :] = jnp.exp(tot)
            qt_scr[d, s0:s0 + slab, :] = (q[s0:s0 + slab] * jnp.exp(bc)).astype(BF16)
            kt_scr[d, s0:s0 + slab, :] = (k[s0:s0 + slab] * jnp.exp(tot - bc)).astype(BF16)
    if has_state:
        st_scr[...] = s0_ref[...]
    else:
        st_scr[...] = jnp.zeros_like(st_scr)

    bd_ones = _block_mask(GROUP, GROUP, HEAD_DIM, HEAD_DIM).astype(BF16)
    rib = _iota((1, c, GROUP), 1)

    def slab_step(i, carry):
        r0 = pl.multiple_of(i * slab, slab)
        q3 = q_scr[pl.ds(r0, slab), :].reshape(nb, c, GROUP)
        v3 = a_ref[pl.ds(r0, slab), 3 * GROUP:4 * GROUP].reshape(nb, c, GROUP)
        o3 = jnp.zeros((nb, c, GROUP), F32)
        for d in range(2):
            bc3 = bc_scr[d, pl.ds(r0, slab), :].reshape(nb, c, GROUP)
            c3 = k_scr[d, pl.ds(r0, slab), :].reshape(nb, c, GROUP)
            for j in range(c):
                keep = (rib >= j) if d == 0 else (rib <= j)
                dec = jnp.exp2(jnp.where(keep, bc3 - c3[:, j:j + 1, :], -jnp.inf))
                pj = (dec * q3).astype(BF16).reshape(slab, GROUP)
                srep = _dot(pj, bd_ones).reshape(nb, c, GROUP)
                o3 = o3 + srep * v3[:, j:j + 1, :]
        o_scr[0, pl.ds(r0, slab), :] = o3.reshape(slab, GROUP)
        return carry

    lax.fori_loop(0, seq // slab, slab_step, 0)

    bd_mask = _block_mask(GROUP, GROUP, HEAD_DIM, HEAD_DIM)

    def body(n, carry):
        rows = [[pl.multiple_of(((n * HG_UNROLL + u) if d == 0 else nblk - 1 - (n * HG_UNROLL + u)) * c, c)
                 for u in range(HG_UNROLL)] for d in range(2)]
        upd = [[_dot_tn(a_ref[pl.ds(r0, c), 3 * GROUP:4 * GROUP].astype(BF16), kt_scr[d, pl.ds(r0, c), :])
                for r0 in rows[d]] for d in range(2)]
        for d in range(2):
            st = st_scr[d]
            for u, r0 in enumerate(rows[d]):
                o_scr[1 + d, pl.ds(r0, c), :] = _dot_nt(qt_scr[d, pl.ds(r0, c), :], st.astype(BF16))
                st = st * dec_scr[d, pl.ds(r0, 1), :] + jnp.where(bd_mask, upd[d][u], 0.0)
            st_scr[d] = st
        return carry

    lax.fori_loop(0, nblk // HG_UNROLL, body, 0)

    o = o_scr[0] + o_scr[1] + o_scr[2]
    mean_op = jnp.where(_block_mask(GROUP, GROUP, HEAD_DIM, HEAD_DIM), 1.0 / HEAD_DIM, 0.0)
    ms = _dot_sel_r(o * o, mean_op)
    y = o * lax.rsqrt(ms + NORM_EPS) * norm_ref[...]
    o_ref[...] = y * _silu(a_ref[:, 4 * GROUP:5 * GROUP])
    sfin_ref[...] = st_scr[...]


def _layer_slot(nseq, stack, tail):
    zeros = (0,) * len(tail)
    if stack is None:
        return (pl.BlockSpec((None,) + tail, lambda b: (b,) + zeros),
                jax.ShapeDtypeStruct((nseq,) + tail, F32))
    layer, depth, creating = stack
    shape = jax.ShapeDtypeStruct((nseq, depth) + tail, F32)
    if creating:
        return pl.BlockSpec((None, depth) + tail, lambda b: (b, 0) + zeros), shape
    return pl.BlockSpec((None, None) + tail, lambda b: (b, layer) + zeros), shape


def _hgrn_parts(proj_a, row0, nseq, seq, hgrn_lb, norm_t, s0, layer, stack=None):
    has_state = s0 is not None
    blk0 = row0 // seq
    in_specs = [pl.BlockSpec((seq, PROJ_A), lambda b: (blk0 + b, 0)),
                pl.BlockSpec(hgrn_lb.shape, lambda b: (0, 0, 0)),
                pl.BlockSpec((1, GROUP), lambda b: (0, 0))]
    args = [proj_a, hgrn_lb, norm_t]
    if has_state:
        in_specs.append(pl.BlockSpec((None, 2, GROUP, GROUP), lambda b: (b, 0, 0, 0)))
        args.append(s0)
    out_specs, out_shape = zip(_layer_slot(nseq, None, (seq, GROUP)),
                               _layer_slot(nseq, stack, (2, GROUP, GROUP)))
    out_specs, out_shape = list(out_specs), list(out_shape)
    scratch = [pltpu.VMEM((seq, GROUP), F32),
               pltpu.VMEM((2, seq, GROUP), F32),
               pltpu.VMEM((2, seq, GROUP), F32),
               pltpu.VMEM((2, seq, GROUP), F32),
               pltpu.VMEM((2, seq, GROUP), BF16),
               pltpu.VMEM((2, seq, GROUP), BF16),
               pltpu.VMEM((2, GROUP, GROUP), F32),
               pltpu.VMEM((3, seq, GROUP), F32)]
    return functools.partial(_hgrn_kernel, layer, has_state), in_specs, args, out_specs, out_shape, scratch


def _mla_kernel(latent, *refs):
    if latent:
        (b_ref, qn_ref, wq_ref, kvn_ref, wkv_ref, cckv_ref, ckpe_ref, cos_ref, sin_ref,
         o_ref) = refs
    else:
        (b_ref, qn_ref, wq_ref, kvn_ref, wkv_ref, o_ref, ckv_ref, kpe_ref) = refs
    seq = b_ref.shape[0]
    mckv = b_ref[:, 0:MLA_KV_RANK]
    mcq = b_ref[:, MLA_KV_RANK:MLA_KV_RANK + MLA_Q_RANK]
    kpe_t = b_ref[:, 384:512]

    cq = mcq * lax.rsqrt(jnp.mean(mcq * mcq, axis=-1, keepdims=True) + NORM_EPS) * qn_ref[...]
    qf = _dot(cq.astype(BF16), wq_ref[...])
    ckv = mckv * lax.rsqrt(jnp.mean(mckv * mckv, axis=-1, keepdims=True) + NORM_EPS) * kvn_ref[...]
    q_nope = qf[:, 0:N_HEADS * MLA_NOPE]
    q_rope = qf[:, N_HEADS * MLA_NOPE:]
    if latent:
        cos = cos_ref[...]
        sin = sin_ref[...]
        q_rope = _rope(q_rope, cos, sin)
        ckv_all = jnp.concatenate([cckv_ref[...], ckv], axis=0)
        kpe_all = jnp.concatenate([ckpe_ref[...], _rope(kpe_t, cos, sin)], axis=0)
    else:
        ckv_ref[...] = ckv
        kpe_ref[...] = kpe_t[:, 0:MLA_ROPE]
        ckv_all = ckv
        kpe_all = kpe_t
    kv = _dot(ckv_all.astype(BF16), wkv_ref[...])
    kcat = jnp.concatenate([kv[:, 0:GROUP], kpe_all], axis=1).astype(BF16)
    v = kv[:, GROUP:].astype(BF16)
    qcat = jnp.concatenate([q_nope, q_rope], axis=1)
    scale = (MLA_NOPE + MLA_ROPE) ** -0.5
    qb = min(seq, 256)
    width = qcat.shape[1]
    for r0 in range(0, seq, qb):
        qblk = qcat[r0:r0 + qb]
        acc = jnp.zeros((qb, GROUP), F32)
        for h in range(N_HEADS):
            hm = (_lane_group_mask(width, h * MLA_NOPE, MLA_NOPE)
                  | _lane_group_mask(width, N_HEADS * MLA_NOPE + h * MLA_ROPE, MLA_ROPE))
            s = _dot_nt(jnp.where(hm, qblk, 0.0).astype(BF16), kcat)
            e = jnp.exp2((s - jnp.max(s, axis=-1, keepdims=True)) * (scale * LOG2_E))
            z = jnp.sum(e, axis=-1, keepdims=True)
            oh = _dot(e.astype(BF16), v) / z
            acc = acc + jnp.where(_lane_group_mask(GROUP, h * HEAD_DIM, HEAD_DIM), oh, 0.0)
        o_ref[r0:r0 + qb, :] = acc


def _mla_parts(proj_b, row0, nseq, seq, q_norm, w_qb_p, kv_norm, w_kvb_p, latent_args, stack=None):
    latent = latent_args is not None
    blk0 = row0 // seq
    in_specs = [pl.BlockSpec((seq, PROJ_B), lambda b: (blk0 + b, 0)),
                pl.BlockSpec(q_norm.shape, lambda b: (0, 0)),
                pl.BlockSpec(w_qb_p.shape, lambda b: (0, 0)),
                pl.BlockSpec(kv_norm.shape, lambda b: (0, 0)),
                pl.BlockSpec(w_kvb_p.shape, lambda b: (0, 0))]
    args = [proj_b, q_norm, w_qb_p, kv_norm, w_kvb_p]
    out_specs = [pl.BlockSpec((None, seq, GROUP), lambda b: (b, 0, 0))]
    out_shape = [jax.ShapeDtypeStruct((nseq, seq, GROUP), F32)]
    if latent:
        cckv, ckpe_t, cos, sin = latent_args
        past = cckv.shape[1]
        in_specs += [pl.BlockSpec((None, past, MLA_KV_RANK), lambda b: (b, 0, 0)),
                     pl.BlockSpec((None, past, 128), lambda b: (b, 0, 0)),
                     pl.BlockSpec(cos.shape, lambda b: (0, 0)),
                     pl.BlockSpec(sin.shape, lambda b: (0, 0))]
        args += [cckv, ckpe_t, cos, sin]
    else:
        for width in (MLA_KV_RANK, MLA_ROPE):
            spec, shape = _layer_slot(nseq, stack, (seq, width))
            out_specs.append(spec)
            out_shape.append(shape)
    return functools.partial(_mla_kernel, latent), in_specs, args, out_specs, out_shape, []


def _diff_kernel(latent, lam_init, *refs):
    if latent:
        (c_ref, lam_ref, norm_ref, ck_ref, cv_ref, cos_ref, sin_ref, o_ref) = refs
    else:
        (c_ref, lam_ref, norm_ref, o_ref, k_ref, v_ref) = refs
    seq = c_ref.shape[0]
    dq = c_ref[:, 0:GROUP]
    dk = c_ref[:, GROUP:2 * GROUP]
    dv = c_ref[:, 2 * GROUP:3 * GROUP]
    if latent:
        cos = cos_ref[...]
        sin = sin_ref[...]
        dq = _rope(dq, cos, sin)
        k_all = jnp.concatenate([ck_ref[...], _rope(dk, cos, sin)], axis=0)
        v_all = jnp.concatenate([cv_ref[...], dv], axis=0)
    else:
        k_ref[...] = dk
        v_ref[...] = dv
        k_all = dk
        v_all = dv
    lv = lam_ref[...]
    lam = (jnp.exp(jnp.sum(lv[0:1] * lv[1:2], axis=-1, keepdims=True))
           - jnp.exp(jnp.sum(lv[2:3] * lv[3:4], axis=-1, keepdims=True)) + lam_init)
    k_bf = k_all.astype(BF16)
    v_bf = v_all.astype(BF16)
    scale = DIFF_DIM ** -0.5
    mean_op = jnp.where(_block_mask(GROUP, GROUP, HEAD_DIM, HEAD_DIM), 1.0 / HEAD_DIM, 0.0)
    qb = min(seq, 256)
    for r0 in range(0, seq, qb):
        qblk = dq[r0:r0 + qb]
        acc = jnp.zeros((qb, GROUP), F32)
        for h in range(N_HEADS):
            outs = []
            for comp in range(2):
                cm = _lane_group_mask(GROUP, h * HEAD_DIM + comp * DIFF_DIM, DIFF_DIM)
                s = _dot_nt(jnp.where(cm, qblk, 0.0).astype(BF16), k_bf)
                e = jnp.exp2((s - jnp.max(s, axis=-1, keepdims=True)) * (scale * LOG2_E))
                outs.append(_dot(e.astype(BF16), v_bf) / jnp.sum(e, axis=-1, keepdims=True))
            oh = outs[0] - lam * outs[1]
            acc = acc + jnp.where(_lane_group_mask(GROUP, h * HEAD_DIM, HEAD_DIM), oh, 0.0)
        ms = _dot_sel_r(acc * acc, mean_op)
        o_ref[r0:r0 + qb, :] = acc * lax.rsqrt(ms + NORM_EPS) * norm_ref[...] * (1.0 - lam_init)


def _diff_parts(proj_c, row0, nseq, seq, lam_p, norm_t, lam_init, latent_args, stack=None):
    latent = latent_args is not None
    blk0 = row0 // seq
    in_specs = [pl.BlockSpec((seq, PROJ_C), lambda b: (blk0 + b, 0)),
                pl.BlockSpec(lam_p.shape, lambda b: (0, 0)),
                pl.BlockSpec(norm_t.shape, lambda b: (0, 0))]
    args = [proj_c, lam_p, norm_t]
    out_specs = [pl.BlockSpec((None, seq, GROUP), lambda b: (b, 0, 0))]
    out_shape = [jax.ShapeDtypeStruct((nseq, seq, GROUP), F32)]
    if latent:
        ck, cv, cos, sin = latent_args
        past = ck.shape[1]
        in_specs += [pl.BlockSpec((None, past, GROUP), lambda b: (b, 0, 0)),
                     pl.BlockSpec((None, past, GROUP), lambda b: (b, 0, 0)),
                     pl.BlockSpec(cos.shape, lambda b: (0, 0)),
                     pl.BlockSpec(sin.shape, lambda b: (0, 0))]
        args += [ck, cv, cos, sin]
    else:
        for _ in range(2):
            spec, shape = _layer_slot(nseq, stack, (seq, GROUP))
            out_specs.append(spec)
            out_shape.append(shape)
    return functools.partial(_diff_kernel, latent, lam_init), in_specs, args, out_specs, out_shape, []


def _mixers(nseq, name, *parts, carried=None, creating=None):
    n_in = [len(p[1]) for p in parts]
    n_out = [len(p[3]) for p in parts]
    n_scr = [len(p[5]) for p in parts]
    carried = [None] * sum(n_out) if carried is None else list(carried)
    kept = [(o, arr) for o, arr in enumerate(carried) if arr is not None]

    def body(*refs):
        ins = refs[:sum(n_in)]
        outs = list(refs[sum(n_in) + len(kept):sum(n_in) + len(kept) + sum(n_out)])
        scr = refs[sum(n_in) + len(kept) + sum(n_out):]
        if creating is not None:
            layer, stacked = creating
            for o in stacked:
                full = outs[o]
                for m in range(full.shape[0]):
                    if m != layer:
                        full[m] = jnp.zeros(full.shape[1:], full.dtype)
                outs[o] = full.at[layer]
        i0 = o0 = s0 = 0
        for part, ni, no, ns in zip(parts, n_in, n_out, n_scr):
            part[0](*ins[i0:i0 + ni], *outs[o0:o0 + no], *scr[s0:s0 + ns])
            i0, o0, s0 = i0 + ni, o0 + no, s0 + ns

    res = pl.pallas_call(
        body,
        grid=(nseq,),
        in_specs=[x for p in parts for x in p[1]] + [pl.BlockSpec(memory_space=pl.ANY)] * len(kept),
        out_specs=[x for p in parts for x in p[3]],
        out_shape=[x for p in parts for x in p[4]],
        scratch_shapes=[x for p in parts for x in p[5]],
        input_output_aliases={sum(n_in) + k: o for k, (o, _) in enumerate(kept)},
        compiler_params=_cparams("arbitrary"),
        name=name,
    )(*[x for p in parts for x in p[2]], *[arr for _, arr in kept])
    out, o0 = [], 0
    for no in n_out:
        out.append(tuple(res[o0:o0 + no]))
        o0 += no
    return out


def _ssd_kernel(has_state, *refs):
    if has_state:
        (d_ref, cw_ref, cb_ref, dtb_ref, alog_ref, dskip_ref, norm_ref, s0_ref,
         o_ref, sfin_ref, xs_scr, bm_scr, cm_scr, xdt_scr, a_scr, st_scr, yf_scr, yb_scr, arep_scr) = refs
    else:
        (d_ref, cw_ref, cb_ref, dtb_ref, alog_ref, dskip_ref, norm_ref,
         o_ref, sfin_ref, xs_scr, bm_scr, cm_scr, xdt_scr, a_scr, st_scr, yf_scr, yb_scr, arep_scr) = refs
    seq = d_ref.shape[0]
    c = SSD_CHUNK
    nchunk = seq // c
    ngrp = 2 * SSD_STATE

    xin = d_ref[:, GROUP:GROUP + 512]
    rows = _iota(xin.shape, 0)
    prev = jnp.where(rows == 0, 0.0, pltpu.roll(xin, 1, 0))
    nxt = jnp.where(rows == seq - 1, 0.0, pltpu.roll(xin, seq - 1, 0))
    cw = cw_ref[...]
    xbc = _silu(cw[0:1] * prev + cw[1:2] * xin + cw[2:3] * nxt + cb_ref[...])
    xs = xbc[:, 0:GROUP]
    xs_scr[...] = xs
    bm_scr[...] = xbc[:, GROUP:GROUP + ngrp]
    cm_scr[...] = xbc[:, GROUP + ngrp:GROUP + 2 * ngrp]
    dt = _softplus(d_ref[:, GROUP + 512:GROUP + 640] + dtb_ref[...])
    neg_a = -jnp.exp(alog_ref[...])
    a_scr[...] = dt * neg_a
    erow = _iota((128, GROUP), 0)
    ehead = _iota((128, GROUP), 1) // HEAD_DIM
    expand = tuple((erow == 4 * d + ehead).astype(F32) for d in range(2))
    for d in range(2):
        dt_rep = _dot_sel_r(dt, expand[d])
        xdt_scr[d] = xs * dt_rep
        arep_scr[d] = dt_rep * _dot_sel_r(jnp.broadcast_to(neg_a, (8, 128)), expand[d])[0:1]
    if has_state:
        st_scr[...] = s0_ref[...]
    else:
        st_scr[...] = jnp.zeros_like(st_scr)

    row = _iota((c, c), 0)
    col = _iota((c, c), 1)
    tri = ((col <= row).astype(F32), (col >= row).astype(F32))
    keep = (col <= row, col >= row)
    grp_lane = _iota((1, ngrp), 1) // SSD_STATE
    valid = (_iota((ngrp, GROUP), 0) // SSD_STATE) == (_iota((ngrp, GROUP), 1) // (2 * HEAD_DIM))

    def chunk_step(d, r0, out_scr):
        a_c = a_scr[pl.ds(r0, c), :]
        bm_c = bm_scr[pl.ds(r0, c), :]
        cm_c = cm_scr[pl.ds(r0, c), :].astype(BF16)
        xdt_c = xdt_scr[d, pl.ds(r0, c), :]
        acum = _dot_sel(tri[d], a_c)
        acum_t = acum.T
        acum_rep = _dot_sel(tri[d], arep_scr[d, pl.ds(r0, c), :])
        bm2 = jnp.concatenate([jnp.where(grp_lane == g, bm_c, 0.0) for g in range(2)], axis=0)
        cb = _dot_nt(cm_c, bm2.astype(BF16))
        scores = []
        xparts = []
        for h in range(N_HEADS):
            lane = 4 * d + h
            seg = jnp.exp(jnp.where(keep[d], acum[:, lane:lane + 1] - acum_t[lane:lane + 1, :], -jnp.inf))
            g = h // 2
            scores.append((cb[:, g * c:(g + 1) * c] * seg).astype(BF16))
            xparts.append(jnp.where(_lane_group_mask(GROUP, h * HEAD_DIM, HEAD_DIM), xdt_c, 0.0))
        y = _dot(jnp.concatenate(scores, axis=1), jnp.concatenate(xparts, axis=0).astype(BF16))
        st = st_scr[d]
        y = y + _dot(cm_c, st.astype(BF16)) * jnp.exp(acum_rep)
        out_scr[pl.ds(r0, c), :] = y
        edge = acum_rep[c - 1:c] if d == 0 else acum_rep[0:1]
        xt = (xdt_c * jnp.exp(edge - acum_rep)).astype(BF16)
        upd = _dot_tn(bm_c.astype(BF16), xt)
        st_scr[d] = st * jnp.exp(edge) + jnp.where(valid, upd, 0.0)

    def body(n, carry):
        chunk_step(0, pl.multiple_of(n * c, c), yf_scr)
        chunk_step(1, pl.multiple_of((nchunk - 1 - n) * c, c), yb_scr)
        return carry

    lax.fori_loop(0, nchunk, body, 0)

    y = yf_scr[...] + yb_scr[...] + dskip_ref[...] * xs_scr[...]
    y = y * _silu(d_ref[:, 0:GROUP])
    o_ref[...] = y * lax.rsqrt(jnp.mean(y * y, axis=-1, keepdims=True) + NORM_EPS) * norm_ref[...]
    sfin_ref[...] = st_scr[...]


def _ssd_parts(proj_d, row0, nseq, seq, conv_w, conv_b, dt_bias_p, a_log_p, d_rep, norm, s0, stack=None):
    has_state = s0 is not None
    blk0 = row0 // seq
    ngrp = 2 * SSD_STATE
    small = [conv_w, conv_b, dt_bias_p, a_log_p, d_rep, norm]
    in_specs = ([pl.BlockSpec((seq, PROJ_D), lambda b: (blk0 + b, 0))]
                + [pl.BlockSpec(s.shape, lambda b: (0, 0)) for s in small])
    args = [proj_d] + small
    if has_state:
        in_specs.append(pl.BlockSpec((None, 2, ngrp, GROUP), lambda b: (b, 0, 0, 0)))
        args.append(s0)
    out_specs, out_shape = zip(_layer_slot(nseq, None, (seq, GROUP)),
                               _layer_slot(nseq, stack, (2, ngrp, GROUP)))
    out_specs, out_shape = list(out_specs), list(out_shape)
    scratch = [pltpu.VMEM((seq, GROUP), F32),
               pltpu.VMEM((seq, ngrp), F32),
               pltpu.VMEM((seq, ngrp), F32),
               pltpu.VMEM((2, seq, GROUP), F32),
               pltpu.VMEM((seq, 128), F32),
               pltpu.VMEM((2, ngrp, GROUP), F32),
               pltpu.VMEM((seq, GROUP), F32),
               pltpu.VMEM((seq, GROUP), F32),
               pltpu.VMEM((2, seq, GROUP), F32)]
    return functools.partial(_ssd_kernel, has_state), in_specs, args, out_specs, out_shape, scratch


def _outproj_kernel(n_ctx_tiles, *refs):
    ctx_refs, lat_refs = refs[0:4], refs[4:8]
    w_ref, x_ref, m_ref, g_ref, b_ref, o_ref = refs[8:]
    is_ctx = pl.program_id(0) < n_ctx_tiles
    mixed = None
    for i, (c_ref, l_ref) in enumerate(zip(ctx_refs, lat_refs)):
        part = jnp.where(is_ctx, c_ref[...], l_ref[...]).astype(BF16)
        term = _dot(part, w_ref[i * GROUP:(i + 1) * GROUP, :])
        mixed = term if mixed is None else mixed + term
    o_ref[...] = _layernorm(ALPHA * x_ref[...] + m_ref[2] * mixed, g_ref[...], b_ref[...])


def _outproj(parts_ctx, parts_lat, w_out_bf, x, mods_l, ln_g, ln_b, t_ctx, s_lat):
    t, d = x.shape
    tm = 1024
    n_ctx = t_ctx // tm
    n_lat = (t - t_ctx) // tm
    ctx_spec = pl.BlockSpec((tm, GROUP), lambda i: (jnp.minimum(i, n_ctx - 1), 0))
    lat_spec = pl.BlockSpec((tm, GROUP), lambda i: (jnp.clip(i - n_ctx, 0, n_lat - 1), 0))
    return pl.pallas_call(
        functools.partial(_outproj_kernel, n_ctx),
        grid=(t // tm,),
        in_specs=[ctx_spec] * 4 + [lat_spec] * 4
        + [pl.BlockSpec(w_out_bf.shape, lambda i: (0, 0)),
           pl.BlockSpec((tm, d), lambda i: (i, 0)),
           pl.BlockSpec((None, 6, 1, d), _mod_row_map(tm, t_ctx, s_lat)),
           pl.BlockSpec((1, d), lambda i: (0, 0)),
           pl.BlockSpec((1, d), lambda i: (0, 0))],
        out_specs=pl.BlockSpec((tm, d), lambda i: (i, 0)),
        out_shape=jax.ShapeDtypeStruct((t, d), F32),
        compiler_params=_cparams("arbitrary"),
        name="outproj_ln",
    )(*parts_ctx, *parts_lat, w_out_bf, x, mods_l, ln_g, ln_b)


def _top_rows(s, k, extra=()):
    r = s.shape[0]
    rid = _iota(s.shape, 0).astype(F32)
    vals, ids = [], []
    picked = [[] for _ in extra]
    for _ in range(k):
        m = jnp.max(s, axis=0, keepdims=True)
        cand = jnp.where(s == m, rid, float(r))
        i = jnp.min(cand, axis=0, keepdims=True)
        hit = cand == i
        vals.append(m)
        ids.append(i)
        for lst, arr in zip(picked, extra):
            lst.append(jnp.max(jnp.where(hit, arr, -1.0), axis=0, keepdims=True))
        s = jnp.where(hit, -jnp.inf, s)
    cat = lambda xs: jnp.concatenate(xs, axis=0)
    return cat(vals), cat(ids), [cat(p) for p in picked]


def _router_kernel(x_ref, m_ref, wq_ref, keys_ref, h_ref, a_ref, b_ref, g_ref):
    tm = x_ref.shape[0]
    hb = (x_ref[...] * (1.0 + m_ref[4]) + m_ref[3]).astype(BF16)
    h_ref[...] = hb
    qt = _dot(hb, wq_ref[...]).T.astype(BF16)
    k = PEER_TOPK
    code_rows, g_rows = [], []
    for head in range(PEER_HEADS):
        tv, ti = [], []
        for half in range(2):
            g = 2 * head + half
            sc = _dot(keys_ref[g], qt[g * PEER_HALF:(g + 1) * PEER_HALF])
            v, i, _ = _top_rows(sc, k)
            tv.append(v)
            ti.append(i)
        cs = [tv[0][0:1] + tv[1]]
        ca = [jnp.broadcast_to(ti[0][0:1], (k, tm))]
        cb = [ti[1]]
        for k1 in range(1, 4):
            cs.append(tv[0][k1:k1 + 1] + tv[1][0:8])
            ca.append(jnp.broadcast_to(ti[0][k1:k1 + 1], (8, tm)))
            cb.append(ti[1][0:8])
        low = _iota((8, tm), 0) < 4
        v2_dup = jnp.where(low, tv[1][0:8], pltpu.roll(tv[1][0:8], 4, 0))
        i2_dup = jnp.where(low, ti[1][0:8], pltpu.roll(ti[1][0:8], 4, 0))
        for k1 in (4, 6):
            cs.append(jnp.where(low, tv[0][k1:k1 + 1], tv[0][k1 + 1:k1 + 2]) + v2_dup)
            ca.append(jnp.where(low, ti[0][k1:k1 + 1], ti[0][k1 + 1:k1 + 2]))
            cb.append(i2_dup)
        cs.append(tv[0][8:16] + tv[1][0:1])
        ca.append(ti[0][8:16])
        cb.append(jnp.broadcast_to(ti[1][0:1], (8, tm)))
        code = jnp.concatenate(ca, axis=0) * float(PEER_KEYS) + jnp.concatenate(cb, axis=0)
        best, _, (sel_code,) = _top_rows(jnp.concatenate(cs, axis=0), k, extra=(code,))
        e = jnp.exp(best - best[0:1])
        g_rows.append(e / jnp.sum(e, axis=0, keepdims=True))
        code_rows.append(sel_code)
    codes = jnp.concatenate(code_rows, axis=0)
    key1 = jnp.floor(codes * (1.0 / PEER_KEYS))
    a_ref[...] = key1.T.astype(I32)
    b_ref[...] = (codes - key1 * float(PEER_KEYS)).T.astype(I32)
    g_ref[...] = jnp.concatenate(g_rows, axis=0).T


def _router(x, mods_l, wq_bf, keys, t_ctx, s_lat):
    t, d = x.shape
    tm = 256
    nslot = PEER_HEADS * PEER_TOPK
    return pl.pallas_call(
        _router_kernel,
        grid=(t // tm,),
        in_specs=[pl.BlockSpec((tm, d), lambda i: (i, 0)),
                  pl.BlockSpec((None, 6, 1, d), _mod_row_map(tm, t_ctx, s_lat)),
                  pl.BlockSpec(wq_bf.shape, lambda i: (0, 0)),
                  pl.BlockSpec(keys.shape, lambda i: (0, 0, 0))],
        out_specs=[pl.BlockSpec((tm, d), lambda i: (i, 0)),
                   pl.BlockSpec((tm, nslot), lambda i: (i, 0)),
                   pl.BlockSpec((tm, nslot), lambda i: (i, 0)),
                   pl.BlockSpec((tm, nslot), lambda i: (i, 0))],
        out_shape=[jax.ShapeDtypeStruct((t, d), BF16),
                   jax.ShapeDtypeStruct((t, nslot), I32),
                   jax.ShapeDtypeStruct((t, nslot), I32),
                   jax.ShapeDtypeStruct((t, nslot), F32)],
        compiler_params=_cparams("arbitrary"),
        name="peer_router",
    )(x, mods_l, wq_bf, keys)


def _gates_kernel(a_ref, b_ref, g_ref, u_ref, v_ref, o_ref, ub_ref, vb_ref):
    tm = a_ref.shape[0]
    n = PEER_KEYS
    sub = 16
    ub_ref[...] = u_ref[...].astype(BF16)
    vb_ref[...] = v_ref[...].astype(BF16)
    key = _iota((sub, n, a_ref.shape[2]), 1).astype(F32).astype(BF16)
    zero = jnp.zeros((), BF16)
    for t0 in range(0, tm, sub):
        a = a_ref[t0:t0 + sub].astype(F32).astype(BF16)
        b = b_ref[t0:t0 + sub].astype(F32).astype(BF16)
        g = g_ref[t0:t0 + sub].astype(BF16)
        onehot_a = jnp.where(key == a, jnp.ones((), BF16), zero)
        gated_b = jnp.where(key == b, g, zero)
        w = lax.dot_general(onehot_a, gated_b, (((2,), (2,)), ((0,), (0,))),
                            preferred_element_type=F32)
        w_t = jnp.swapaxes(w.astype(BF16), 0, 1)
        for r in range(n):
            o_ref[t0:t0 + sub, r * n:(r + 1) * n] = w_t[r]


def _gates(a_idx, b_idx, gate, peer_u, peer_v, layer):
    t, nslot = a_idx.shape
    d = peer_u.shape[-1]
    tm = 192
    steps = t // tm
    te = N_EXPERTS // steps
    spec = pl.BlockSpec((tm, 1, nslot), lambda i: (i, 0, 0))
    tab_in = pl.BlockSpec((None, te, d), lambda i: (layer, i, 0))
    tab_out = pl.BlockSpec((te, d), lambda i: (i, 0))
    return pl.pallas_call(
        _gates_kernel,
        grid=(steps,),
        in_specs=[spec, spec, spec, tab_in, tab_in],
        out_specs=[pl.BlockSpec((tm, N_EXPERTS), lambda i: (i, 0)), tab_out, tab_out],
        out_shape=[jax.ShapeDtypeStruct((t, N_EXPERTS), BF16),
                   jax.ShapeDtypeStruct((N_EXPERTS, d), BF16),
                   jax.ShapeDtypeStruct((N_EXPERTS, d), BF16)],
        compiler_params=_cparams("arbitrary"),
        name="peer_gates",
    )(a_idx.reshape(t, 1, nslot), b_idx.reshape(t, 1, nslot), gate.reshape(t, 1, nslot), peer_u, peer_v)


def _experts_kernel(h_ref, u_ref, v_ref, w_ref, x_ref, m_ref, g_ref, b_ref, o_ref, acc_ref):
    j = pl.program_id(1)

    @pl.when(j == 0)
    def _():
        acc_ref[...] = jnp.zeros_like(acc_ref)

    act = _gelu_tanh(_dot_nt(h_ref[...], u_ref[...]))
    acc_ref[...] += _dot((act * w_ref[...].astype(F32)).astype(BF16), v_ref[...])

    @pl.when(j == pl.num_programs(1) - 1)
    def _():
        o_ref[...] = _layernorm(ALPHA * x_ref[...] + m_ref[5] * acc_ref[...], g_ref[...], b_ref[...])


def _experts(h_bf, u_bf, v_bf, w_gate, x, mods_l, ln_g, ln_b, t_ctx, s_lat):
    t, d = x.shape
    tm, te = 1024, 1024
    return pl.pallas_call(
        _experts_kernel,
        grid=(t // tm, N_EXPERTS // te),
        in_specs=[pl.BlockSpec((tm, d), lambda i, j: (i, 0)),
                  pl.BlockSpec((te, d), lambda i, j: (j, 0)),
                  pl.BlockSpec((te, d), lambda i, j: (j, 0)),
                  pl.BlockSpec((tm, te), lambda i, j: (i, j)),
                  pl.BlockSpec((tm, d), lambda i, j: (i, 0)),
                  pl.BlockSpec((None, 6, 1, d), _mod_row_map(tm, t_ctx, s_lat)),
                  pl.BlockSpec((1, d), lambda i, j: (0, 0)),
                  pl.BlockSpec((1, d), lambda i, j: (0, 0))],
        out_specs=pl.BlockSpec((tm, d), lambda i, j: (i, 0)),
        out_shape=jax.ShapeDtypeStruct((t, d), F32),
        scratch_shapes=[pltpu.VMEM((tm, d), F32)],
        compiler_params=_cparams("arbitrary", "arbitrary"),
        name="peer_experts",
    )(h_bf, u_bf, v_bf, w_gate, x, mods_l, ln_g, ln_b)


def _pad_cols(w, n):
    return jnp.concatenate([w, jnp.zeros((w.shape[0], n), w.dtype)], axis=1) if n else w


def _layout_w_in(w):
    a = w[:, 0:1280]
    mcq, mckv, mkpe = w[:, 1280:1472], w[:, 1472:1600], w[:, 1600:1632]
    b = jnp.concatenate([mckv, _pad_cols(mcq, 64), mkpe, mkpe, mkpe, mkpe], axis=1)
    c = w[:, 1632:2400]
    d = _pad_cols(w[:, 2400:3176], PROJ_D - 776)
    return jnp.concatenate([a, b, c, d], axis=1).astype(BF16)


def _layout_w_qb(w):
    w4 = w.reshape(MLA_Q_RANK, N_HEADS, MLA_NOPE + MLA_ROPE)
    return jnp.concatenate([w4[:, :, :MLA_NOPE].reshape(MLA_Q_RANK, -1),
                            w4[:, :, MLA_NOPE:].reshape(MLA_Q_RANK, -1)], axis=1).astype(BF16)


def _layout_w_kvb(w):
    w4 = w.reshape(MLA_KV_RANK, N_HEADS, MLA_NOPE + HEAD_DIM)
    return jnp.concatenate([w4[:, :, :MLA_NOPE].reshape(MLA_KV_RANK, -1),
                            w4[:, :, MLA_NOPE:].reshape(MLA_KV_RANK, -1)], axis=1).astype(BF16)


def _rope_tables(seq):
    rows = seq // GRID_W
    row = jnp.repeat(jnp.arange(rows, dtype=F32), GRID_W)
    col = jnp.tile(jnp.arange(GRID_W, dtype=F32), rows)
    freqs = ROPE_BASE ** (-jnp.arange(ROPE_PAIRS, dtype=F32) / ROPE_PAIRS)
    cos_l, sin_l = [], []
    for pos in (row, col):
        ang = pos[:, None] * freqs
        cos_l += [jnp.cos(ang), jnp.cos(ang)]
        sin_l += [-jnp.sin(ang), jnp.sin(ang)]
    return jnp.concatenate(cos_l, axis=1), jnp.concatenate(sin_l, axis=1)


def _hgrn_state_pack(st):
    b = st.shape[0]
    st_t = jnp.swapaxes(st, -1, -2)
    zero = jnp.zeros_like(st_t[:, :, 0])
    rows = [jnp.concatenate([st_t[:, :, h] if g == h else zero for g in range(N_HEADS)], axis=-1)
            for h in range(N_HEADS)]
    return jnp.concatenate(rows, axis=-2).reshape(b, 2, GROUP, GROUP)


def _hgrn_state_unpack(sb):
    blocks = [sb[..., h * HEAD_DIM:(h + 1) * HEAD_DIM, h * HEAD_DIM:(h + 1) * HEAD_DIM]
              for h in range(N_HEADS)]
    return jnp.swapaxes(jnp.stack(blocks, axis=-3), -1, -2)


def _ssd_state_pack(st):
    st_t = jnp.swapaxes(st, -1, -2)
    zero = jnp.zeros_like(st_t[:, :, 0])
    rows = [jnp.concatenate([st_t[:, :, h] if h // 2 == g else zero for h in range(N_HEADS)], axis=-1)
            for g in range(2)]
    return jnp.concatenate(rows, axis=-2)


def _ssd_state_unpack(sb):
    blocks = [sb[..., (h // 2) * SSD_STATE:(h // 2 + 1) * SSD_STATE, h * HEAD_DIM:(h + 1) * HEAD_DIM]
              for h in range(N_HEADS)]
    return jnp.swapaxes(jnp.stack(blocks, axis=-3), -1, -2)


def _tile_lanes(v, n):
    return jnp.tile(v.reshape(1, -1), (1, n))


def kernel(x_prompt, x_sample, cache_mla_ckv, cache_mla_kpe, cache_diff_k, cache_diff_v, state_hgrn, state_ssd, c, c_ctx, w_mod, b_mod, w_in, hgrn_lb, hgrn_norm, mla_q_norm, mla_w_qb, mla_kv_norm, mla_w_kvb, diff_lambda, diff_norm, ssd_conv_w, ssd_conv_b, ssd_dt_bias, ssd_a_log, ssd_d, ssd_norm, w_out, ln1_g, ln1_b, peer_wq, peer_keys, peer_u, peer_v, ln2_g, ln2_b):
    nb, seq, d = x_prompt.shape
    nlat, lseq, _ = x_sample.shape
    depth = w_in.shape[0]
    t_ctx = nb * seq
    x = jnp.concatenate([x_prompt.reshape(t_ctx, d), x_sample.reshape(nlat * lseq, d)], axis=0)

    cond8 = jnp.concatenate([c_ctx.reshape(1, d), c, jnp.zeros((8 - 1 - nlat, d), F32)], axis=0)
    mods = _mods(cond8, w_mod, b_mod)
    mods = mods[:, :1 + nlat].reshape(depth, 1 + nlat, 6, 1, d)

    cos32, sin32 = _rope_tables(lseq)
    cos128, sin128 = jnp.tile(cos32, (1, 4)), jnp.tile(sin32, (1, 4))
    cos256, sin256 = jnp.tile(cos32, (1, 8)), jnp.tile(sin32, (1, 8))

    carried = None
    for l in range(depth):
        mods_l = mods[l]
        pa, pb, pc, pd = _inproj(x, mods_l, _layout_w_in(w_in[l]), t_ctx, lseq)

        norm_hg = _tile_lanes(hgrn_norm[l], N_HEADS)
        mla_w = (mla_q_norm[l].reshape(1, -1), _layout_w_qb(mla_w_qb[l]),
                 mla_kv_norm[l].reshape(1, -1), _layout_w_kvb(mla_w_kvb[l]))
        lam_init = 0.8 - 0.6 * math.exp(-0.3 * l)
        norm_df = _tile_lanes(diff_norm[l], N_HEADS)
        past = cache_diff_k.shape[2]
        ssd_w = (ssd_conv_w[l], ssd_conv_b[l].reshape(1, -1),
                 _pad_cols(ssd_dt_bias[l].reshape(1, -1), 120), _pad_cols(ssd_a_log[l].reshape(1, -1), 120),
                 jnp.repeat(ssd_d[l], HEAD_DIM).reshape(1, -1), ssd_norm[l].reshape(1, -1))

        stack = (l, depth, carried is None)
        ctx_out = _mixers(
            nb, "mixers_ctx",
            _mla_parts(pb, 0, nb, seq, *mla_w, None, stack),
            _diff_parts(pc, 0, nb, seq, diff_lambda[l], norm_df, lam_init, None, stack),
            _ssd_parts(pd, 0, nb, seq, *ssd_w, None, stack),
            _hgrn_parts(pa, 0, nb, seq, hgrn_lb, norm_hg, None, l, stack),
            carried=carried, creating=(l, (1, 2, 4, 5, 7, 9)) if carried is None else None)
        (mla_ctx, new_ckv, new_kpe), (df_ctx, new_dk, new_dv), (ssd_ctx, ssd_fin), (hg_ctx, hg_fin) = ctx_out
        carried = [None, new_ckv, new_kpe, None, new_dk, new_dv, None, ssd_fin, None, hg_fin]
        ((mla_lat,),) = _mixers(
            nlat, "mla_lat",
            _mla_parts(pb, t_ctx, nlat, lseq, *mla_w,
                       (cache_mla_ckv[:, l], jnp.tile(cache_mla_kpe[:, l], (1, 1, 4)), cos128, sin128)))
        ((df_lat,),) = _mixers(
            nlat, "diffattn_lat",
            _diff_parts(pc, t_ctx, nlat, lseq, diff_lambda[l], norm_df, lam_init,
                        (cache_diff_k[:, l].reshape(nlat, past, GROUP),
                         cache_diff_v[:, l].reshape(nlat, past, GROUP), cos256, sin256)))
        ((ssd_lat, _),) = _mixers(nlat, "ssd_lat",
                                  _ssd_parts(pd, t_ctx, nlat, lseq, *ssd_w, _ssd_state_pack(state_ssd[:, l])))
        ((hg_lat, _),) = _mixers(nlat, "hgrn_lat",
                                 _hgrn_parts(pa, t_ctx, nlat, lseq, hgrn_lb, norm_hg,
                                             _hgrn_state_pack(state_hgrn[:, l]), l))

        parts_ctx = [a.reshape(t_ctx, GROUP) for a in (hg_ctx, mla_ctx, df_ctx, ssd_ctx)]
        parts_lat = [a.reshape(nlat * lseq, GROUP) for a in (hg_lat, mla_lat, df_lat, ssd_lat)]
        x = _outproj(parts_ctx, parts_lat, w_out[l].astype(BF16), x, mods_l,
                     ln1_g[l].reshape(1, d), ln1_b[l].reshape(1, d), t_ctx, lseq)

        keys = peer_keys[l].reshape(2 * PEER_HEADS, PEER_KEYS, PEER_HALF).astype(BF16)
        h_bf, a_idx, b_idx, gate = _router(x, mods_l, peer_wq[l].astype(BF16), keys, t_ctx, lseq)
        w_gate, u_bf, v_bf = _gates(a_idx, b_idx, gate, peer_u, peer_v, l)
        x = _experts(h_bf, u_bf, v_bf, w_gate, x, mods_l,
                     ln2_g[l].reshape(1, d), ln2_b[l].reshape(1, d), t_ctx, lseq)

    y_prompt = x[:t_ctx].reshape(nb, seq, d)
    y_sample = x[t_ctx:].reshape(nlat, lseq, d)
    return (y_prompt, y_sample, new_ckv, new_kpe,
            new_dk.reshape(nb, depth, seq, N_HEADS, 2, DIFF_DIM),
            new_dv.reshape(nb, depth, seq, N_HEADS, 2 * DIFF_DIM),
            _hgrn_state_unpack(hg_fin), _ssd_state_unpack(ssd_fin))
```
